```python
import jax, jax.numpy as jnp
from jax import lax
import numpy as np

D_MODEL = 1024
BATCH = 16
SEQ = 256
DEPTH = 1
DEC_BATCH = 2
DEC_SEQ = 2048
PAST_LEN = 512

GRID_W = 64
MLA_HEADS = 8
QK_NOPE = 64
QK_ROPE = 32
V_HEAD = 64
Q_LORA = 256
KV_LORA = 256
ROPE_THETA = 10000.0
GDN_HEADS = 8
GDN_DK = 64
GDN_DV = 64
CONV_K = 5
CHUNK = 64
N_EXPERTS = 64
TOP_K = 8
N_GROUPS = 8
TOPK_GROUPS = 4
D_EXPERT = 256
D_SHARED = 256
ROUTED_SCALE = 2.5
EPS = 1e-6
Q_BLOCK = 128
TOK_BLOCK = 128

MLA_QK = QK_NOPE + QK_ROPE
MLA_WIDTH = MLA_HEADS * V_HEAD
GDN_QKV = GDN_HEADS * (2 * GDN_DK + GDN_DV)
GDN_WIDTH = GDN_HEADS * GDN_DV
IN_SIZES = (Q_LORA, KV_LORA, QK_ROPE, GDN_QKV, GDN_WIDTH, 2 * GDN_HEADS, 2 * GDN_HEADS, D_MODEL, D_MODEL)
D_IN = sum(IN_SIZES)

kernel_name = 'hybrid_mla_gdn_moe_diffusion_step'


def rms_norm(x, g):
    xf = x.astype(jnp.float32)
    y = xf * lax.rsqrt(jnp.mean(xf * xf, axis=-1, keepdims=True) + EPS)
    return (y * g.astype(jnp.float32)).astype(x.dtype)


def l2_normalize(x):
    return x * lax.rsqrt(jnp.sum(x * x, axis=-1, keepdims=True) + EPS)


def axial_rope(n_tokens):
    rows = n_tokens // GRID_W
    row = jnp.repeat(jnp.arange(rows, dtype=jnp.float32), GRID_W)
    col = jnp.tile(jnp.arange(GRID_W, dtype=jnp.float32), rows)
    n_freq = QK_ROPE // 4
    inv = ROPE_THETA ** (-jnp.arange(n_freq, dtype=jnp.float32) / n_freq)
    ang = jnp.stack([row[:, None] * inv, col[:, None] * inv], axis=1)
    return jnp.cos(ang), jnp.sin(ang)


def apply_rope(x, cos, sin):
    x5 = x.reshape(x.shape[:-1] + (2, 2, QK_ROPE // 4)).astype(jnp.float32)
    a, b = x5[..., 0, :], x5[..., 1, :]
    out = jnp.stack([a * cos - b * sin, b * cos + a * sin], axis=-2)
    return out.reshape(x.shape).astype(x.dtype)


def block_attention(q, k, v):
    b, lq, h, dqk = q.shape
    nb = lq // Q_BLOCK
    scale = dqk ** -0.5
    qb = jnp.moveaxis(q.reshape(b, nb, Q_BLOCK, h, dqk), 1, 0)

    def one(qi):
        s = jnp.einsum('bqhd,bkhd->bhqk', qi, k, preferred_element_type=jnp.float32) * scale
        p = jax.nn.softmax(s, axis=-1).astype(v.dtype)
        return jnp.einsum('bhqk,bkhd->bqhd', p, v)

    o = lax.map(one, qb)
    return jnp.moveaxis(o, 0, 1).reshape(b, lq, h, v.shape[-1])


def project_inputs(h, w_in):
    offs = np.cumsum(IN_SIZES)[:-1].tolist()
    return jnp.split(h @ w_in, offs, axis=-1)


def mla_queries(c_q, q_norm_g, w_uq):
    q = rms_norm(c_q, q_norm_g) @ w_uq
    q = q.reshape(q.shape[:-1] + (MLA_HEADS, MLA_QK))
    return q[..., :QK_NOPE], q[..., QK_NOPE:]


def mla_keys_values(ckv, krope, w_ukv):
    kv = (ckv @ w_ukv).reshape(ckv.shape[:-1] + (MLA_HEADS, QK_NOPE + V_HEAD))
    k_nope, v = kv[..., :QK_NOPE], kv[..., QK_NOPE:]
    k_rope = jnp.broadcast_to(krope[..., None, :], k_nope.shape[:-1] + (QK_ROPE,))
    return jnp.concatenate([k_nope, k_rope], axis=-1), v


def depthwise_conv(x, w):
    return lax.conv_general_dilated(x, w[:, None, :], window_strides=(1,),
                                    padding=[(CONV_K // 2, CONV_K // 2)],
                                    dimension_numbers=('NWC', 'WIO', 'NWC'),
                                    feature_group_count=x.shape[-1])


def gdn_prepare(qkv_raw, a_raw, b_raw, conv_w, a_log, dt_bias):
    bsz, n, _ = qkv_raw.shape
    qkv = jax.nn.silu(depthwise_conv(qkv_raw, conv_w)).astype(jnp.float32)
    nq = GDN_HEADS * GDN_DK
    q = qkv[..., :nq].reshape(bsz, n, GDN_HEADS, GDN_DK)
    k = qkv[..., nq:2 * nq].reshape(bsz, n, GDN_HEADS, GDN_DK)
    v = qkv[..., 2 * nq:].reshape(bsz, n, GDN_HEADS, GDN_DV)
    q = l2_normalize(q) * GDN_DK ** -0.5
    k = l2_normalize(k)
    a = a_raw.astype(jnp.float32).reshape(bsz, n, 2, GDN_HEADS)
    g = -jnp.exp(a_log.astype(jnp.float32)) * jax.nn.softplus(a + dt_bias.astype(jnp.float32))
    beta = jax.nn.sigmoid(b_raw.astype(jnp.float32).reshape(bsz, n, 2, GDN_HEADS))
    return q, k, v, g, beta


def chunk_gated_delta(q, k, v, g, beta, s0):
    bsz, n_tok, nh, _ = q.shape
    n = n_tok // CHUNK

    def chunks(t):
        t = jnp.moveaxis(t, 2, 1)
        return t.reshape((bsz, nh, n, CHUNK) + t.shape[3:])

    q, k, v, g, beta = (chunks(t) for t in (q, k, v, g, beta))
    gc = jnp.cumsum(g, axis=-1)
    idx = jnp.arange(CHUNK)
    incl = idx[:, None] >= idx[None, :]
    strict = idx[:, None] > idx[None, :]
    decay = jnp.exp(jnp.where(incl, gc[..., :, None] - gc[..., None, :], -jnp.inf))
    kb = k * beta[..., None]
    lmat = jnp.where(strict, jnp.einsum('bhncd,bhnsd->bhncs', kb, k) * decay, 0.0)
    eye = jnp.eye(CHUNK, dtype=jnp.float32)
    tmat = lax.linalg.triangular_solve(eye + lmat, jnp.broadcast_to(eye, lmat.shape), left_side=True, lower=True)
    u = tmat @ (v * beta[..., None])
    w = tmat @ (kb * jnp.exp(gc)[..., None])
    a_intra = jnp.einsum('bhncd,bhnsd->bhncs', q, k) * decay
    q_dec = q * jnp.exp(gc)[..., None]
    k_dec = k * jnp.exp(gc[..., -1:] - gc)[..., None]
    g_last = jnp.exp(gc[..., -1])

    def step(s, inp):
        u_c, w_c, a_c, qd_c, kd_c, gl_c = inp
        v_new = u_c - w_c @ s
        o = qd_c @ s + a_c @ v_new
        s = s * gl_c[..., None, None] + jnp.swapaxes(kd_c, -1, -2) @ v_new
        return s, o

    xs = tuple(jnp.moveaxis(t, 2, 0) for t in (u, w, a_intra, q_dec, k_dec, g_last))
    s_fin, o = lax.scan(step, s0, xs)
    o = jnp.moveaxis(o, 0, 2).reshape(bsz, nh, n_tok, -1)
    return jnp.moveaxis(o, 1, 2), s_fin


def bidirectional_gdn(q, k, v, g, beta, s0):
    o_f, s_f = chunk_gated_delta(q, k, v, g[:, :, 0], beta[:, :, 0], s0[:, 0])
    fl = lambda t: jnp.flip(t, axis=1)
    o_b, s_b = chunk_gated_delta(fl(q), fl(k), fl(v), fl(g[:, :, 1]), fl(beta[:, :, 1]), s0[:, 1])
    return o_f + fl(o_b), jnp.stack([s_f, s_b], axis=1)


def gdn_output(o, z, gdn_norm_g, dtype):
    bsz, n = z.shape[:2]
    o = rms_norm(o, gdn_norm_g) * jax.nn.silu(z.astype(jnp.float32).reshape(bsz, n, GDN_HEADS, GDN_DV))
    return o.reshape(bsz, n, GDN_WIDTH).astype(dtype)


def moe(h, p):
    bsz, n, d = h.shape
    tokens = h.reshape(-1, TOK_BLOCK, d)
    per_group = N_EXPERTS // N_GROUPS

    def one(t):
        s = jax.nn.sigmoid((t @ p['w_router']).astype(jnp.float32))
        sel = s + p['e_bias'].astype(jnp.float32)
        grp = lax.top_k(sel.reshape(-1, N_GROUPS, per_group), 2)[0].sum(-1)
        _, gidx = lax.top_k(grp, TOPK_GROUPS)
        gmask = jnp.any(gidx[:, :, None] == jnp.arange(N_GROUPS), axis=1)
        sel = jnp.where(jnp.repeat(gmask, per_group, axis=-1), sel, -jnp.inf)
        _, eidx = lax.top_k(sel, TOP_K)
        wk = jnp.take_along_axis(s, eidx, axis=-1)
        wk = wk / jnp.sum(wk, axis=-1, keepdims=True) * ROUTED_SCALE
        gates = jnp.einsum('tk,tke->te', wk, jax.nn.one_hot(eidx, N_EXPERTS, dtype=jnp.float32)).astype(t.dtype)
        hg = jnp.einsum('td,edf->tef', t, p['w_gate'])
        hu = jnp.einsum('td,edf->tef', t, p['w_up'])
        routed = jnp.einsum('tef,efd->td', jax.nn.silu(hg) * hu * gates[:, :, None], p['w_down'])
        shared = (jax.nn.silu(t @ p['ws_gate']) * (t @ p['ws_up'])) @ p['ws_down']
        return routed + shared

    return lax.map(one, tokens).reshape(bsz, n, d)


def mix_merge(o_mla, o_gdn, gate_a, gate_b, p):
    bsz, n = o_gdn.shape[:2]
    y_a = o_mla.reshape(bsz, n, MLA_WIDTH) @ p['w_oa']
    y_b = o_gdn @ p['w_ob']
    return (jax.nn.sigmoid(gate_a) * y_a + jax.nn.sigmoid(gate_b) * y_b) @ p['w_o']


def channel_sublayer(x, p, shift, scale, gate):
    h = rms_norm(x, p['g_pre_ffn']) * (1 + scale) + shift
    return x + gate * rms_norm(moe(h, p), p['g_post_ffn'])


def context_layer(x, p, mods):
    bsz = x.shape[0]
    h = rms_norm(x, p['g_pre_mix']) * (1 + mods[:, :, 1]) + mods[:, :, 0]
    c_q, ckv_raw, krope, qkv_raw, z, a_raw, b_raw, gate_a, gate_b = project_inputs(h, p['w_in'])
    ckv = rms_norm(ckv_raw, p['kv_norm_g'])
    q_nope, q_rope = mla_queries(c_q, p['q_norm_g'], p['w_uq'])
    k, v = mla_keys_values(ckv, krope, p['w_ukv'])
    o_mla = block_attention(jnp.concatenate([q_nope, q_rope], axis=-1), k, v)
    q_g, k_g, v_g, g_g, b_g = gdn_prepare(qkv_raw, a_raw, b_raw, p['conv_w'], p['a_log'], p['dt_bias'])
    s0 = jnp.zeros((bsz, 2, GDN_HEADS, GDN_DK, GDN_DV), jnp.float32)
    o_g, s_fin = bidirectional_gdn(q_g, k_g, v_g, g_g, b_g, s0)
    y = mix_merge(o_mla, gdn_output(o_g, z, p['gdn_norm_g'], x.dtype), gate_a, gate_b, p)
    x = x + mods[:, :, 2] * rms_norm(y, p['g_post_mix'])
    x = channel_sublayer(x, p, mods[:, :, 3], mods[:, :, 4], mods[:, :, 5])
    return x, ckv, krope, s_fin.astype(x.dtype)


def latent_layer(x, p, mods, ckv_ctx, krope_ctx, s_ctx):
    n = x.shape[1]
    h = rms_norm(x, p['g_pre_mix']) * (1 + mods[:, :, 1]) + mods[:, :, 0]
    c_q, ckv_raw, krope, qkv_raw, z, a_raw, b_raw, gate_a, gate_b = project_inputs(h, p['w_in'])
    cos, sin = axial_rope(n)
    ckv = rms_norm(ckv_raw, p['kv_norm_g'])
    q_nope, q_rope = mla_queries(c_q, p['q_norm_g'], p['w_uq'])
    q_rope = apply_rope(q_rope, cos[:, None], sin[:, None])
    k_lat, v_lat = mla_keys_values(ckv, apply_rope(krope, cos, sin), p['w_ukv'])
    k_ctx, v_ctx = mla_keys_values(ckv_ctx, krope_ctx, p['w_ukv'])
    o_mla = block_attention(jnp.concatenate([q_nope, q_rope], axis=-1),
                            jnp.concatenate([k_ctx, k_lat], axis=1),
                            jnp.concatenate([v_ctx, v_lat], axis=1))
    q_g, k_g, v_g, g_g, b_g = gdn_prepare(qkv_raw, a_raw, b_raw, p['conv_w'], p['a_log'], p['dt_bias'])
    o_g, _ = bidirectional_gdn(q_g, k_g, v_g, g_g, b_g, s_ctx.astype(jnp.float32))
    y = mix_merge(o_mla, gdn_output(o_g, z, p['gdn_norm_g'], x.dtype), gate_a, gate_b, p)
    x = x + mods[:, :, 2] * rms_norm(y, p['g_post_mix'])
    return channel_sublayer(x, p, mods[:, :, 3], mods[:, :, 4], mods[:, :, 5])


def setup_inputs(seed: int = 0) -> dict:
    key = jax.random.key(seed)
    ks = iter(jax.random.split(key, 48))
    f32 = jnp.float32
    D = D_MODEL

    def nrm(shape, scale):
        return jax.random.normal(next(ks), shape, f32) * scale

    def gain(shape):
        return 1.0 + nrm(shape, 0.05)

    dt = jnp.exp(jax.random.uniform(next(ks), (DEPTH, 2, GDN_HEADS), f32, minval=np.log(1e-3), maxval=np.log(1e-1)))
    return {
        'x_prompt': nrm((BATCH, SEQ, D), 1.0),
        'x_sample': nrm((DEC_BATCH, DEC_SEQ, D), 1.0),
        'cache_ckv': nrm((DEC_BATCH, DEPTH, PAST_LEN, KV_LORA), 1.0),
        'cache_krope': nrm((DEC_BATCH, DEPTH, PAST_LEN, QK_ROPE), 1.0),
        'state_delta': nrm((DEC_BATCH, DEPTH, 2, GDN_HEADS, GDN_DK, GDN_DV), 0.1),
        'c': nrm((DEC_BATCH, D), 1.0),
        'c_ctx': nrm((D,), 1.0),
        'w_ada': nrm((DEPTH, D, 6 * D), D ** -0.5),
        'b_ada': nrm((DEPTH, 6 * D), 0.02),
        'g_pre_mix': gain((DEPTH, D)),
        'g_post_mix': gain((DEPTH, D)),
        'g_pre_ffn': gain((DEPTH, D)),
        'g_post_ffn': gain((DEPTH, D)),
        'w_in': nrm((DEPTH, D, D_IN), D ** -0.5),
        'q_norm_g': gain((DEPTH, Q_LORA)),
        'kv_norm_g': gain((DEPTH, KV_LORA)),
        'w_uq': nrm((DEPTH, Q_LORA, MLA_HEADS * MLA_QK), Q_LORA ** -0.5),
        'w_ukv': nrm((DEPTH, KV_LORA, MLA_HEADS * (QK_NOPE + V_HEAD)), KV_LORA ** -0.5),
        'conv_w': nrm((DEPTH, CONV_K, GDN_QKV), CONV_K ** -0.5),
        'a_log': jnp.log(jax.random.uniform(next(ks), (DEPTH, 2, GDN_HEADS), f32, minval=1.0, maxval=16.0)),
        'dt_bias': dt + jnp.log(-jnp.expm1(-dt)),
        'gdn_norm_g': gain((DEPTH, GDN_DV)),
        'w_oa': nrm((DEPTH, MLA_WIDTH, D), MLA_WIDTH ** -0.5),
        'w_ob': nrm((DEPTH, GDN_WIDTH, D), GDN_WIDTH ** -0.5),
        'w_o': nrm((DEPTH, D, D), D ** -0.5),
        'w_router': nrm((DEPTH, D, N_EXPERTS), D ** -0.5),
        'e_bias': nrm((DEPTH, N_EXPERTS), 0.01),
        'w_gate': nrm((DEPTH, N_EXPERTS, D, D_EXPERT), D ** -0.5),
        'w_up': nrm((DEPTH, N_EXPERTS, D, D_EXPERT), D ** -0.5),
        'w_down': nrm((DEPTH, N_EXPERTS, D_EXPERT, D), D_EXPERT ** -0.5),
        'ws_gate': nrm((DEPTH, D, D_SHARED), D ** -0.5),
        'ws_up': nrm((DEPTH, D, D_SHARED), D ** -0.5),
        'ws_down': nrm((DEPTH, D_SHARED, D), D_SHARED ** -0.5),
    }


def reference(x_prompt, x_sample, cache_ckv, cache_krope, state_delta, c, c_ctx,
              w_ada, b_ada, g_pre_mix, g_post_mix, g_pre_ffn, g_post_ffn,
              w_in, q_norm_g, kv_norm_g, w_uq, w_ukv, conv_w, a_log, dt_bias, gdn_norm_g,
              w_oa, w_ob, w_o, w_router, e_bias, w_gate, w_up, w_down, ws_gate, ws_up, ws_down):
    y_p, y_s = x_prompt, x_sample
    ckv_list, krope_list, state_list = [], [], []
    for l in range(DEPTH):
        p = {
            'g_pre_mix': g_pre_mix[l], 'g_post_mix': g_post_mix[l],
            'g_pre_ffn': g_pre_ffn[l], 'g_post_ffn': g_post_ffn[l],
            'w_in': w_in[l], 'q_norm_g': q_norm_g[l], 'kv_norm_g': kv_norm_g[l],
            'w_uq': w_uq[l], 'w_ukv': w_ukv[l], 'conv_w': conv_w[l],
            'a_log': a_log[l], 'dt_bias': dt_bias[l], 'gdn_norm_g': gdn_norm_g[l],
            'w_oa': w_oa[l], 'w_ob': w_ob[l], 'w_o': w_o[l],
            'w_router': w_router[l], 'e_bias': e_bias[l],
            'w_gate': w_gate[l], 'w_up': w_up[l], 'w_down': w_down[l],
            'ws_gate': ws_gate[l], 'ws_up': ws_up[l], 'ws_down': ws_down[l],
        }
        mods_ctx = (jax.nn.silu(c_ctx) @ w_ada[l] + b_ada[l]).reshape(1, 1, 6, D_MODEL)
        mods_lat = (jax.nn.silu(c) @ w_ada[l] + b_ada[l]).reshape(-1, 1, 6, D_MODEL)
        y_p, ckv_l, krope_l, state_l = context_layer(y_p, p, mods_ctx)
        y_s = latent_layer(y_s, p, mods_lat, cache_ckv[:, l], cache_krope[:, l], state_delta[:, l])
        ckv_list.append(ckv_l)
        krope_list.append(krope_l)
        state_list.append(state_l)
    new_ckv = jnp.stack(ckv_list, axis=1)
    new_krope = jnp.stack(krope_list, axis=1)
    new_state = jnp.stack(state_list, axis=1)
    return (y_p, y_s, new_ckv, new_krope, new_state)
```

```python
import functools

import numpy as np
import jax
import jax.numpy as jnp
from jax import lax
from jax.experimental import pallas as pl
from jax.experimental.pallas import tpu as pltpu

F32 = jnp.float32
BF16 = jnp.bfloat16

D_MODEL = 1024
DEPTH = 1
GRID_W = 64
MLA_HEADS = 8
QK_NOPE = 64
QK_ROPE = 32
V_HEAD = 64
Q_LORA = 256
KV_LORA = 256
ROPE_THETA = 10000.0
GDN_HEADS = 8
GDN_DK = 64
GDN_DV = 64
CONV_K = 5
CHUNK = 64
N_EXPERTS = 64
TOP_K = 8
N_GROUPS = 8
TOPK_GROUPS = 4
D_EXPERT = 256
D_SHARED = 256
ROUTED_SCALE = 2.5
EPS = 1e-6

LANES = 128
MLA_QK = QK_NOPE + QK_ROPE
GDN_W = GDN_HEADS * GDN_DK
N_PAIRS = GDN_HEADS // 2
HEAD_PAD = LANES

_SEG = {}
_off = 0
for _name, _width in (("cq", Q_LORA), ("ckv", KV_LORA), ("kr", LANES), ("krs", LANES),
                      ("qkv", 3 * GDN_W), ("z", GDN_W), ("ab", LANES),
                      ("ga", D_MODEL), ("gb", D_MODEL)):
    _SEG[_name] = (_off, _off + _width)
    _off += _width
N_CAT = _off

ROW_TILE = 256
MOE_TILE = 1024
PREP_GROUP = 4
SCAN_SEQS = 2
VMEM_LIMIT = 56 * 1024 * 1024


def _dot(a, b):
    return jnp.dot(a, b, preferred_element_type=F32)


def _dot_nt(a, b):
    return lax.dot_general(a, b, (((1,), (1,)), ((), ())), preferred_element_type=F32)


def _dot_tn(a, b):
    return lax.dot_general(a, b, (((0,), (0,)), ((), ())), preferred_element_type=F32)


def _dot_hi(a, b):
    return jnp.dot(a, b, preferred_element_type=F32, precision=lax.Precision.HIGHEST)


def _rms(x, g):
    return x * lax.rsqrt(jnp.mean(x * x, axis=-1, keepdims=True) + EPS) * g


def _silu(x):
    return x * jax.nn.sigmoid(x)


def _cparams(sem):
    return pltpu.CompilerParams(dimension_semantics=sem, vmem_limit_bytes=VMEM_LIMIT)


def _mods_kernel(c_ref, w_ref, b_ref, o_ref):
    s = _silu(c_ref[...]).astype(BF16)
    o_ref[...] = _dot(s, w_ref[...].astype(BF16)) + b_ref[...]


def _mods(cond8, w_ada, b_ada):
    n = w_ada.shape[1]
    bn = 512
    return pl.pallas_call(
        _mods_kernel,
        out_shape=jax.ShapeDtypeStruct((8, n), F32),
        grid=(n // bn,),
        in_specs=[pl.BlockSpec((8, D_MODEL), lambda j: (0, 0)),
                  pl.BlockSpec((D_MODEL, bn), lambda j: (0, j)),
                  pl.BlockSpec((1, bn), lambda j: (0, j))],
        out_specs=pl.BlockSpec((8, bn), lambda j: (0, j)),
        compiler_params=_cparams(("parallel",)),
        name="mods",
    )(cond8, w_ada, b_ada)


def _proj_kernel(rope, x_ref, m_ref, gpre_ref, wcat_ref, qg_ref, kvg_ref, wuq_ref, wuqs_ref,
                 wuk_ref, wuv_ref, cos_ref, sin_ref,
                 q_ref, k_ref, v_ref, ckv_ref, kr_ref, qkv_ref, z_ref, ab_ref, ga_ref, gb_ref):
    m = m_ref[0]
    h = (_rms(x_ref[...], gpre_ref[...]) * (1.0 + m[1:2]) + m[0:1]).astype(BF16)

    def seg(name):
        a, b = _SEG[name]
        return _dot(h, wcat_ref[:, a:b])

    qkv_ref[...] = seg("qkv")
    z_ref[...] = seg("z")
    ab_ref[...] = seg("ab")
    ga_ref[...] = seg("ga")
    gb_ref[...] = seg("gb")

    qn = _rms(seg("cq"), qg_ref[...]).astype(BF16)
    ckv = _rms(seg("ckv"), kvg_ref[...])
    ckv_ref[...] = ckv
    ckv_b = ckv.astype(BF16)
    kr = seg("kr")
    kr_ref[...] = kr
    qm = _dot(qn, wuq_ref[...])
    kk = _dot(ckv_b, wuk_ref[...])
    v_ref[...] = _dot(ckv_b, wuv_ref[...]).astype(BF16)
    scale = MLA_QK ** -0.5
    if rope:
        cos = cos_ref[...]
        sin = sin_ref[...]
        qs = _dot(qn, wuqs_ref[...])
        kr = kr * cos + seg("krs") * sin
    for hd in range(MLA_HEADS):
        sl = slice(hd * HEAD_PAD, (hd + 1) * HEAD_PAD)
        qh = qm[:, sl]
        if rope:
            qh = qh * cos + qs[:, sl] * sin
        q_ref[:, sl] = (qh * scale).astype(BF16)
        k_ref[:, sl] = (kk[:, sl] + kr).astype(BF16)


def _proj(x, mods, mod_index, gpre, wts, rope_tabs):
    t = x.shape[0]
    tm = ROW_TILE
    rope = rope_tabs is not None
    if rope:
        cos, sin = rope_tabs
        n_rope_blocks = cos.shape[0] // tm
        rope_spec = pl.BlockSpec((tm, LANES), lambda i: (i % n_rope_blocks, 0))
    else:
        cos = sin = jnp.zeros((8, LANES), F32)
        rope_spec = pl.BlockSpec((8, LANES), lambda i: (0, 0))

    def full(a):
        return pl.BlockSpec(a.shape, lambda i: (0,) * a.ndim)

    def rows(w):
        return pl.BlockSpec((tm, w), lambda i: (i, 0))

    out_widths = (("q", MLA_HEADS * HEAD_PAD, BF16), ("k", MLA_HEADS * HEAD_PAD, BF16),
                  ("v", MLA_HEADS * V_HEAD, BF16), ("ckv", KV_LORA, F32), ("kr", LANES, F32),
                  ("qkv", 3 * GDN_W, F32), ("z", GDN_W, F32), ("ab", LANES, F32),
                  ("ga", D_MODEL, F32), ("gb", D_MODEL, F32))
    outs = pl.pallas_call(
        functools.partial(_proj_kernel, rope),
        out_shape=[jax.ShapeDtypeStruct((t, w), dt) for _, w, dt in out_widths],
        grid=(t // tm,),
        in_specs=[rows(D_MODEL),
                  pl.BlockSpec((1, 6, D_MODEL), lambda i: (mod_index(i), 0, 0)),
                  full(gpre), full(wts["wcat"]), full(wts["qg"]), full(wts["kvg"]),
                  full(wts["wuq"]), full(wts["wuqs"]), full(wts["wuk"]), full(wts["wuv"]),
                  rope_spec, rope_spec],
        out_specs=[rows(w) for _, w, _ in out_widths],
        compiler_params=_cparams(("parallel",)),
        name="proj_rope" if rope else "proj",
    )(x, mods, gpre, wts["wcat"], wts["qg"], wts["kvg"], wts["wuq"], wts["wuqs"],
      wts["wuk"], wts["wuv"], cos, sin)
    return {name: o for (name, _, _), o in zip(out_widths, outs)}


def _cache_kv_kernel(ckv_ref, kr_ref, wuk_ref, wuv_ref, k_ref, v_ref):
    c = ckv_ref[...].astype(BF16)
    kk = _dot(c, wuk_ref[...])
    v_ref[...] = _dot(c, wuv_ref[...]).astype(BF16)
    kr = kr_ref[...]
    for hd in range(MLA_HEADS):
        sl = slice(hd * HEAD_PAD, (hd + 1) * HEAD_PAD)
        k_ref[:, sl] = (kk[:, sl] + kr).astype(BF16)


def _cache_kv(ckv, kr128, wts):
    t = ckv.shape[0]
    tm = 512
    return pl.pallas_call(
        _cache_kv_kernel,
        out_shape=[jax.ShapeDtypeStruct((t, MLA_HEADS * HEAD_PAD), BF16),
                   jax.ShapeDtypeStruct((t, MLA_HEADS * V_HEAD), BF16)],
        grid=(t // tm,),
        in_specs=[pl.BlockSpec((tm, KV_LORA), lambda i: (i, 0)),
                  pl.BlockSpec((tm, LANES), lambda i: (i, 0)),
                  pl.BlockSpec(wts["wuk"].shape, lambda i: (0, 0)),
                  pl.BlockSpec(wts["wuv"].shape, lambda i: (0, 0))],
        out_specs=[pl.BlockSpec((tm, MLA_HEADS * HEAD_PAD), lambda i: (i, 0)),
                   pl.BlockSpec((tm, MLA_HEADS * V_HEAD), lambda i: (i, 0))],
        compiler_params=_cparams(("parallel",)),
        name="cache_kv",
    )(ckv, kr128, wts["wuk"], wts["wuv"])


def _attn_kernel(n_kv, q_ref, *refs):
    k_refs = refs[:n_kv]
    v_refs = refs[n_kv:2 * n_kv]
    o_ref = refs[2 * n_kv]
    lane = lax.broadcasted_iota(jnp.int32, (1, LANES), 1)
    low = lane < V_HEAD
    for pr in range(MLA_HEADS // 2):
        halves = []
        for hd in (2 * pr, 2 * pr + 1):
            sl = slice(hd * HEAD_PAD, (hd + 1) * HEAD_PAD)
            qh = q_ref[:, sl]
            scores = [_dot_nt(qh, kr[:, sl]) for kr in k_refs]
            mx = functools.reduce(jnp.maximum, [jnp.max(s, axis=-1, keepdims=True) for s in scores])
            ps = [jnp.exp(s - mx) for s in scores]
            den = functools.reduce(jnp.add, [jnp.sum(p, axis=-1, keepdims=True) for p in ps])
            vsl = slice(pr * LANES, (pr + 1) * LANES)
            acc = functools.reduce(jnp.add, [_dot(p.astype(BF16), vr[:, vsl]) for p, vr in zip(ps, v_refs)])
            halves.append(acc / den)
        o_ref[:, pr * LANES:(pr + 1) * LANES] = jnp.where(low, halves[0], halves[1]).astype(BF16)


def _attention(q, kvs, n_seq, seq_len, name):
    tq = ROW_TILE
    nq = seq_len // tq
    n_kv = len(kvs)
    in_specs = [pl.BlockSpec((tq, MLA_HEADS * HEAD_PAD), lambda b, j: (b * nq + j, 0))]
    in_specs += [pl.BlockSpec((rows, MLA_HEADS * HEAD_PAD), lambda b, j: (b, 0)) for _, _, rows in kvs]
    in_specs += [pl.BlockSpec((rows, MLA_HEADS * V_HEAD), lambda b, j: (b, 0)) for _, _, rows in kvs]
    return pl.pallas_call(
        functools.partial(_attn_kernel, n_kv),
        out_shape=jax.ShapeDtypeStruct((n_seq * seq_len, MLA_HEADS * V_HEAD), BF16),
        grid=(n_seq, nq),
        in_specs=in_specs,
        out_specs=pl.BlockSpec((tq, MLA_HEADS * V_HEAD), lambda b, j: (b * nq + j, 0)),
        compiler_params=_cparams(("parallel", "parallel")),
        name=name,
    )(q, *[k for k, _, _ in kvs], *[v for _, v, _ in kvs])


def _pair_masks():
    lane = lax.broadcasted_iota(jnp.int32, (1, LANES), 1)
    return lane < GDN_DK


def _stack(x, low):
    zero = jnp.zeros_like(x)
    return jnp.concatenate([jnp.where(low, x, zero), jnp.where(low, zero, x)], axis=0)


def _split3(x):
    hi = x.astype(BF16)
    r = x - hi.astype(F32)
    mid = r.astype(BF16)
    lo = (r - mid.astype(F32)).astype(BF16)
    return hi, mid, lo


def _gdn_prep_kernel(q_ref, k_ref, v_ref, cwq_ref, cwk_ref, cwv_ref, ab_ref, alog_ref, dtb_ref, e_ref,
                     uf_ref, ub_ref, wf_ref, wb_ref, af_ref, abk_ref, qdf_ref, qdb_ref, kdf_ref, kdb_ref,
                     glf_ref, glb_ref,
                     qn_s, kn_s, vn_s, gcb_s, gf_s):
    seq = q_ref.shape[0]
    n_chunks = seq // CHUNK
    low = _pair_masks()
    row = lax.broadcasted_iota(jnp.int32, (seq, 1), 0)
    lane = lax.broadcasted_iota(jnp.int32, (1, LANES), 1)

    def conv(x_ref, cw_ref):
        x = x_ref[...]
        acc = jnp.zeros_like(x)
        for j in range(CONV_K):
            sh = CONV_K // 2 - j
            xs = x if sh == 0 else pltpu.roll(x, sh % seq, axis=0)
            src = row - sh
            valid = (src >= 0) & (src < seq)
            acc = acc + jnp.where(valid, xs, 0.0) * cw_ref[j:j + 1, :]
        return _silu(acc)

    def l2n(x):
        sq = x * x
        s0 = jnp.sum(jnp.where(low, sq, 0.0), axis=-1, keepdims=True)
        s1 = jnp.sum(jnp.where(low, 0.0, sq), axis=-1, keepdims=True)
        return x * lax.rsqrt(jnp.where(low, s0, s1) + EPS)

    qn_s[...] = l2n(conv(q_ref, cwq_ref)) * (GDN_DK ** -0.5)
    kn_s[...] = l2n(conv(k_ref, cwk_ref))
    vn_s[...] = conv(v_ref, cwv_ref)

    a = ab_ref[...]
    xg = a + dtb_ref[...]
    softplus = jnp.maximum(xg, 0.0) + jnp.log(1.0 + jnp.exp(-jnp.abs(xg)))
    act = jnp.where(lane < 2 * GDN_HEADS, -jnp.exp(alog_ref[...]) * softplus, jax.nn.sigmoid(a))

    ti = lax.broadcasted_iota(jnp.int32, (CHUNK, CHUNK), 0)
    tj = lax.broadcasted_iota(jnp.int32, (CHUNK, CHUNK), 1)
    tri_lo = (tj <= ti).astype(F32)
    tri_up = (tj >= ti).astype(F32)
    for c in range(n_chunks):
        ac = act[c * CHUNK:(c + 1) * CHUNK]
        lo = _dot_hi(tri_lo, ac)
        up = _dot_hi(tri_up, ac)
        gcb_s[c * CHUNK:(c + 1) * CHUNK, :] = jnp.where(lane < GDN_HEADS, lo,
                                                        jnp.where(lane < 2 * GDN_HEADS, up, ac))
    gf_s[...] = _dot_hi(gcb_s[...], e_ref[0])

    ri = lax.broadcasted_iota(jnp.int32, (CHUNK, LANES), 0)
    cj = lax.broadcasted_iota(jnp.int32, (CHUNK, LANES), 1) % CHUNK
    eye = (ri == cj).astype(F32)
    ones_b = jnp.ones((CHUNK, LANES), BF16)
    lane0 = (lax.broadcasted_iota(jnp.int32, (1, LANES), 1) % CHUNK) == 0

    def pmm(x, y):
        return _dot(x.astype(BF16), _stack(y, low).astype(BF16))

    def pmm_split(x, y):
        xh, xm, _ = _split3(x)
        yh, ym, _ = _split3(_stack(y, low))
        return _dot(xh, yh) + (_dot(xh, ym) + _dot(xm, yh))

    dirs = ((0, True, uf_ref, wf_ref, af_ref, qdf_ref, kdf_ref, glf_ref),
            (1, False, ub_ref, wb_ref, abk_ref, qdb_ref, kdb_ref, glb_ref))

    def one_chunk(c):
        r0 = pl.multiple_of(c * CHUNK, CHUNK)
        rows = pl.ds(r0, CHUNK)
        qc = qn_s[rows, :]
        kc = kn_s[rows, :]
        vc = vn_s[rows, :]
        kst = _stack(kc, low).astype(BF16)
        kk = _dot_nt(kc.astype(BF16), kst)
        qk = _dot_nt(qc.astype(BF16), kst)
        for d, lower, u_ref, w_ref, a_ref, qd_ref, kd_ref, gl_ref in dirs:
            gc = gf_s[rows, d * LANES:(d + 1) * LANES]
            beta = gf_s[rows, (2 + d) * LANES:(3 + d) * LANES]
            incl = (ri >= cj) if lower else (ri <= cj)
            strict = (ri > cj) if lower else (ri < cj)
            gsel = _stack(jnp.where(lane0, gc, 0.0), low)
            gr = functools.reduce(jnp.add, [_dot_nt(ones_b, piece) for piece in _split3(gsel)])
            dm = jnp.exp(jnp.where(incl, gc - gr, -jnp.inf))
            lm = jnp.where(strict, beta * kk * dm, 0.0)
            a_ref[0, rows, :] = (qk * dm).astype(BF16)
            x = -lm
            t = eye + x
            for _ in range(5):
                x = pmm(x, x)
                t = t + pmm(t, x)
            resid = eye - pmm_split(eye + lm, t)
            t = t + pmm(t, resid)
            egc = jnp.exp(gc)
            u_ref[0, rows, :] = pmm(t, vc * beta)
            w_ref[0, rows, :] = pmm(t, kc * beta * egc).astype(BF16)
            qd_ref[0, rows, :] = (qc * egc).astype(BF16)
            gtot = gc[CHUNK - 1:CHUNK, :] if lower else gc[0:1, :]
            kd_ref[0, rows, :] = (kc * jnp.exp(gtot - gc)).astype(BF16)
            gl_ref[0, pl.ds(c, 1), :, :] = jnp.broadcast_to(jnp.exp(gtot), (1, 8, LANES))

    def group(it, carry):
        for cc in range(PREP_GROUP):
            one_chunk(it * PREP_GROUP + cc)
        return carry

    lax.fori_loop(0, n_chunks // PREP_GROUP, group, 0)


def _gdn_prep(qkv, ab, conv_w, alog128, dtb128, expand, n_seq, seq_len):
    n_chunks = seq_len // CHUNK
    col = lambda off: pl.BlockSpec((seq_len, LANES), lambda s, p: (s, off + p))
    cw = lambda off: pl.BlockSpec((CONV_K, LANES), lambda s, p: (0, off + p))
    vec = pl.BlockSpec((1, LANES), lambda s, p: (0, 0))
    big = lambda: pl.BlockSpec((1, seq_len, LANES), lambda s, p: (s, 0, p))
    glspec = lambda: pl.BlockSpec((1, n_chunks, 8, LANES), lambda s, p: (s, 0, 0, p))
    shp = lambda dt: jax.ShapeDtypeStruct((n_seq, seq_len, GDN_W), dt)
    glshp = jax.ShapeDtypeStruct((n_seq, n_chunks, 8, GDN_W), F32)
    return pl.pallas_call(
        _gdn_prep_kernel,
        out_shape=[shp(F32), shp(F32)] + [shp(BF16)] * 8 + [glshp, glshp],
        grid=(n_seq, N_PAIRS),
        in_specs=[col(0), col(N_PAIRS), col(2 * N_PAIRS), cw(0), cw(N_PAIRS), cw(2 * N_PAIRS),
                  pl.BlockSpec((seq_len, LANES), lambda s, p: (s, 0)), vec, vec,
                  pl.BlockSpec((1, LANES, 4 * LANES), lambda s, p: (p, 0, 0))],
        out_specs=[big() for _ in range(10)] + [glspec(), glspec()],
        scratch_shapes=[pltpu.VMEM((seq_len, LANES), F32)] * 4 + [pltpu.VMEM((seq_len, 4 * LANES), F32)],
        compiler_params=_cparams(("parallel", "parallel")),
        name="gdn_prep_%d" % seq_len,
    )(qkv, qkv, qkv, conv_w, conv_w, conv_w, ab, alog128, dtb128, expand)


def _gdn_scan_kernel(uf_ref, ub_ref, wf_ref, wb_ref, af_ref, abk_ref, qdf_ref, qdb_ref, kdf_ref, kdb_ref,
                     glf_ref, glb_ref, s0_ref, of_ref, ob_ref, sfin_ref, state):
    step = pl.program_id(1)
    n_steps = pl.num_programs(1)
    low = _pair_masks()
    chains = [(d, j, p) for d in range(2) for j in range(SCAN_SEQS) for p in range(N_PAIRS)]

    first = step == 0
    per_dir = ((uf_ref, wf_ref, af_ref, qdf_ref, kdf_ref, glf_ref, of_ref),
               (ub_ref, wb_ref, abk_ref, qdb_ref, kdb_ref, glb_ref, ob_ref))
    for idx, (d, j, p) in enumerate(chains):
        u_ref, w_ref, a_ref, qd_ref, kd_ref, gl_ref, o_ref = per_dir[d]
        sl = slice(p * LANES, (p + 1) * LANES)
        s = jnp.where(first, _stack(s0_ref[j, d, p], low), state[idx])
        sb = s.astype(BF16)
        vnew = u_ref[j, :, sl] - _dot(w_ref[j, :, sl], sb)
        vst = _stack(vnew, low).astype(BF16)
        o_ref[j, :, sl] = _dot(qd_ref[j, :, sl], sb) + _dot(a_ref[j, :, sl], vst)
        kst = _stack(kd_ref[j, :, sl], low)
        state[idx] = s * gl_ref[j, 0, 0:1, sl] + _dot_tn(kst, vst)

    @pl.when(step == n_steps - 1)
    def _():
        for idx, (d, j, p) in enumerate(chains):
            s = state[idx]
            sfin_ref[j, d, p] = s[:GDN_DK] + s[GDN_DK:]


def _gdn_scan(prep, s0, n_seq, seq_len):
    n_chunks = seq_len // CHUNK
    ns = SCAN_SEQS
    fwd = lambda: pl.BlockSpec((ns, CHUNK, GDN_W), lambda g, i: (g, i, 0))
    bwd = lambda: pl.BlockSpec((ns, CHUNK, GDN_W), lambda g, i: (g, n_chunks - 1 - i, 0))
    glf = pl.BlockSpec((ns, 1, 8, GDN_W), lambda g, i: (g, i, 0, 0))
    glb = pl.BlockSpec((ns, 1, 8, GDN_W), lambda g, i: (g, n_chunks - 1 - i, 0, 0))
    st = lambda: pl.BlockSpec((ns, 2, N_PAIRS, GDN_DK, LANES), lambda g, i: (g, 0, 0, 0, 0))
    oshape = jax.ShapeDtypeStruct((n_seq, seq_len, GDN_W), F32)
    return pl.pallas_call(
        _gdn_scan_kernel,
        out_shape=[oshape, oshape, jax.ShapeDtypeStruct((n_seq, 2, N_PAIRS, GDN_DK, LANES), F32)],
        grid=(n_seq // ns, n_chunks),
        in_specs=[fwd(), bwd()] * 5 + [glf, glb, st()],
        out_specs=[fwd(), bwd(), st()],
        scratch_shapes=[pltpu.VMEM((2 * ns * N_PAIRS, LANES, LANES), F32)],
        compiler_params=_cparams(("parallel", "arbitrary")),
        name="gdn_scan_%d" % seq_len,
    )(*prep, s0)


def _route(sel, s):
    per_group = N_EXPERTS // N_GROUPS
    ninf = -jnp.inf
    sub = lax.broadcasted_iota(jnp.int32, sel.shape, 1).astype(F32)
    gid = lax.broadcasted_iota(jnp.int32, (N_GROUPS, 1, sel.shape[2]), 0).astype(F32)
    m1 = jnp.max(sel, axis=1, keepdims=True)
    i1 = jnp.min(jnp.where(sel == m1, sub, float(per_group)), axis=1, keepdims=True)
    m2 = jnp.max(jnp.where(sub == i1, ninf, sel), axis=1, keepdims=True)
    work = m1 + m2
    gmask = jnp.zeros(work.shape, jnp.bool_)
    for _ in range(TOPK_GROUPS):
        m = jnp.max(work, axis=0, keepdims=True)
        idx = jnp.min(jnp.where(work == m, gid, float(N_GROUPS)), axis=0, keepdims=True)
        pick = gid == idx
        gmask = gmask | pick
        work = jnp.where(pick, ninf, work)
    work = jnp.where(gmask, sel, ninf)
    eid = gid * per_group + sub
    chosen = jnp.zeros(sel.shape, jnp.bool_)
    for _ in range(TOP_K):
        m = jnp.max(jnp.max(work, axis=1, keepdims=True), axis=0, keepdims=True)
        idx = jnp.min(jnp.min(jnp.where(work == m, eid, float(N_EXPERTS)), axis=1, keepdims=True),
                      axis=0, keepdims=True)
        pick = eid == idx
        chosen = chosen | pick
        work = jnp.where(pick, ninf, work)
    wk = jnp.where(chosen, s, 0.0)
    den = jnp.sum(jnp.sum(wk, axis=1, keepdims=True), axis=0, keepdims=True)
    return wk / den * ROUTED_SCALE


def _merge_kernel(x_ref, m_ref, omla_ref, of_ref, ob_ref, z_ref, ga_ref, gb_ref,
                  woa_ref, wob_ref, wo_ref, gpost_ref, gpre_ref, gdng_ref, wr_ref, eb_ref,
                  x1_ref, h2_ref, gates_ref):
    m = m_ref[0]
    low = _pair_masks()
    o = of_ref[...] + ob_ref[...]
    z = z_ref[...]
    parts = []
    for p in range(N_PAIRS):
        sl = slice(p * LANES, (p + 1) * LANES)
        op = o[:, sl]
        sq = op * op
        s0 = jnp.sum(jnp.where(low, sq, 0.0), axis=-1, keepdims=True)
        s1 = jnp.sum(jnp.where(low, 0.0, sq), axis=-1, keepdims=True)
        ms = jnp.where(low, s0, s1) * (1.0 / GDN_DV)
        parts.append(op * lax.rsqrt(ms + EPS) * gdng_ref[...] * _silu(z[:, sl]))
    og = jnp.concatenate(parts, axis=1).astype(BF16)
    ya = _dot(omla_ref[...], woa_ref[...])
    yb = _dot(og, wob_ref[...])
    mix = (jax.nn.sigmoid(ga_ref[...]) * ya + jax.nn.sigmoid(gb_ref[...]) * yb).astype(BF16)
    y = _dot(mix, wo_ref[...])
    x1 = x_ref[...] + m[2:3] * _rms(y, gpost_ref[...])
    x1_ref[...] = x1
    h2 = _rms(x1, gpre_ref[...]) * (1.0 + m[4:5]) + m[3:4]
    h2_ref[...] = h2.astype(BF16)
    logits = lax.dot_general(wr_ref[...], h2, (((1,), (1,)), ((), ())),
                             preferred_element_type=F32, precision=lax.Precision.HIGHEST)
    s = jax.nn.sigmoid(logits)
    sel = s + eb_ref[...]
    tm = s.shape[1]
    shape3 = (N_GROUPS, N_EXPERTS // N_GROUPS, tm)
    gates_t = _route(sel.reshape(shape3), s.reshape(shape3)).reshape(N_EXPERTS, tm)
    gates_t = jnp.concatenate([gates_t, jnp.zeros((LANES - N_EXPERTS, tm), F32)], axis=0)
    gates_ref[...] = gates_t.T


def _merge(x, mods, mod_index, omla, o_f, o_b, pr, wts):
    t = x.shape[0]
    tm = ROW_TILE

    def full(a):
        return pl.BlockSpec(a.shape, lambda i: (0,) * a.ndim)

    def rows(w):
        return pl.BlockSpec((tm, w), lambda i: (i, 0))

    names = ("woa", "wob", "wo", "gpost", "gpre2", "gdng", "wr_t", "eb")
    return pl.pallas_call(
        _merge_kernel,
        out_shape=[jax.ShapeDtypeStruct((t, D_MODEL), F32), jax.ShapeDtypeStruct((t, D_MODEL), BF16),
                   jax.ShapeDtypeStruct((t, LANES), F32)],
        grid=(t // tm,),
        in_specs=[rows(D_MODEL), pl.BlockSpec((1, 6, D_MODEL), lambda i: (mod_index(i), 0, 0)),
                  rows(MLA_HEADS * V_HEAD), rows(GDN_W), rows(GDN_W), rows(GDN_W),
                  rows(D_MODEL), rows(D_MODEL)] + [full(wts[n]) for n in names],
        out_specs=[rows(D_MODEL), rows(D_MODEL), rows(LANES)],
        compiler_params=_cparams(("parallel",)),
        name="merge",
    )(x, mods, omla, o_f, o_b, pr["z"], pr["ga"], pr["gb"], *[wts[n] for n in names])


def _moe_kernel(h_ref, gates_ref, wg_ref, wu_ref, wd_ref, wsg_ref, wsu_ref, wsd_ref,
                x1_ref, m_ref, gpost_ref, y_ref, acc_ref):
    e = pl.program_id(1)
    h = h_ref[...]

    @pl.when(e == 0)
    def _():
        sh = (_silu(_dot(h, wsg_ref[...])) * _dot(h, wsu_ref[...])).astype(BF16)
        acc_ref[...] = _dot(sh, wsd_ref[...])

    lane = lax.broadcasted_iota(jnp.int32, (1, LANES), 1)
    gcol = jnp.sum(jnp.where(lane == e, gates_ref[...], 0.0), axis=-1, keepdims=True)
    hg = _dot(h, wg_ref[0].astype(BF16))
    hu = _dot(h, wu_ref[0].astype(BF16))
    act = (_silu(hg) * hu * gcol).astype(BF16)
    acc_ref[...] += _dot(act, wd_ref[0].astype(BF16))

    @pl.when(e == pl.num_programs(1) - 1)
    def _():
        m = m_ref[0]
        y_ref[...] = x1_ref[...] + m[5:6] * _rms(acc_ref[...], gpost_ref[...])


def _moe(h2, gates, x1, mods, mod_index, wts):
    t = h2.shape[0]
    tm = MOE_TILE

    def full(a):
        return pl.BlockSpec(a.shape, lambda i, e: (0,) * a.ndim)

    return pl.pallas_call(
        _moe_kernel,
        out_shape=jax.ShapeDtypeStruct((t, D_MODEL), F32),
        grid=(t // tm, N_EXPERTS),
        in_specs=[pl.BlockSpec((tm, D_MODEL), lambda i, e: (i, 0)),
                  pl.BlockSpec((tm, LANES), lambda i, e: (i, 0)),
                  pl.BlockSpec((1, D_MODEL, D_EXPERT), lambda i, e: (e, 0, 0)),
                  pl.BlockSpec((1, D_MODEL, D_EXPERT), lambda i, e: (e, 0, 0)),
                  pl.BlockSpec((1, D_EXPERT, D_MODEL), lambda i, e: (e, 0, 0)),
                  full(wts["wsg"]), full(wts["wsu"]), full(wts["wsd"]),
                  pl.BlockSpec((tm, D_MODEL), lambda i, e: (i, 0)),
                  pl.BlockSpec((1, 6, D_MODEL), lambda i, e: (mod_index(i), 0, 0)),
                  full(wts["gpost2"])],
        out_specs=pl.BlockSpec((tm, D_MODEL), lambda i, e: (i, 0)),
        scratch_shapes=[pltpu.VMEM((tm, D_MODEL), F32)],
        compiler_params=_cparams(("parallel", "arbitrary")),
        name="moe",
    )(h2, gates, wts["w_gate"], wts["w_up"], wts["w_down"], wts["wsg"], wts["wsu"], wts["wsd"],
      x1, mods, wts["gpost2"])


def _rope_swap(w):
    nf = QK_ROPE // 4
    parts = [w[..., i * nf:(i + 1) * nf] for i in range(4)]
    return jnp.concatenate([parts[1], parts[0], parts[3], parts[2]], axis=-1)


def _head_block(nope, rope):
    lead = nope.shape[:-2] if nope is not None else rope.shape[:-2]
    nope = jnp.zeros(lead + (MLA_HEADS, QK_NOPE), F32) if nope is None else nope
    rope = jnp.zeros(lead + (MLA_HEADS, QK_ROPE), F32) if rope is None else rope
    pad = jnp.zeros(lead + (MLA_HEADS, HEAD_PAD - MLA_QK), F32)
    return jnp.concatenate([nope, rope, pad], axis=-1).reshape(lead + (MLA_HEADS * HEAD_PAD,))


def _rope_block(w):
    lead = w.shape[:-1]
    return jnp.concatenate([jnp.zeros(lead + (QK_NOPE,), F32), w,
                            jnp.zeros(lead + (HEAD_PAD - MLA_QK,), F32)], axis=-1)


def _rope_tables(n_tokens):
    rows = n_tokens // GRID_W
    row = np.repeat(np.arange(rows, dtype=np.float32), GRID_W)
    colv = np.tile(np.arange(GRID_W, dtype=np.float32), rows)
    nf = QK_ROPE // 4
    inv = jnp.asarray(ROPE_THETA, F32) ** (-jnp.arange(nf, dtype=F32) / nf)
    ang_r = jnp.asarray(row)[:, None] * inv
    ang_c = jnp.asarray(colv)[:, None] * inv
    cos32 = jnp.concatenate([jnp.cos(ang_r), jnp.cos(ang_r), jnp.cos(ang_c), jnp.cos(ang_c)], axis=-1)
    sin32 = jnp.concatenate([-jnp.sin(ang_r), jnp.sin(ang_r), -jnp.sin(ang_c), jnp.sin(ang_c)], axis=-1)
    ones = jnp.ones((n_tokens, QK_NOPE), F32)
    tail = jnp.zeros((n_tokens, HEAD_PAD - MLA_QK), F32)
    cos = jnp.concatenate([ones, cos32, tail], axis=-1)
    sin = jnp.concatenate([jnp.zeros((n_tokens, QK_NOPE), F32), sin32, tail], axis=-1)
    return cos, sin


def _expand_matrix():
    e = np.zeros((N_PAIRS, LANES, 4 * LANES), np.float32)
    for p in range(N_PAIRS):
        for blk in range(4):
            for hh in range(2):
                src = blk * GDN_HEADS + 2 * p + hh
                e[p, src, blk * LANES + hh * GDN_DK: blk * LANES + (hh + 1) * GDN_DK] = 1.0
    return jnp.asarray(e)


def _prepare_weights(w_in, q_norm_g, kv_norm_g, w_uq, w_ukv, w_oa, w_ob, w_o, g_post_mix, g_pre_ffn,
                     g_post_ffn, gdn_norm_g, w_router, e_bias, w_gate, w_up, w_down, ws_gate, ws_up, ws_down):
    offs = np.cumsum((Q_LORA, KV_LORA, QK_ROPE, 3 * GDN_W, GDN_W, 2 * GDN_HEADS, 2 * GDN_HEADS,
                      D_MODEL, D_MODEL))[:-1].tolist()
    cq, ckv, kr, qkv, z, a, b, ga, gb = jnp.split(w_in, offs, axis=-1)
    ab = jnp.concatenate([a, b, jnp.zeros((D_MODEL, LANES - 4 * GDN_HEADS), F32)], axis=-1)
    wcat = jnp.concatenate([cq, ckv, _rope_block(kr), _rope_block(_rope_swap(kr)), qkv, z, ab, ga, gb],
                           axis=-1).astype(BF16)
    uq = w_uq.reshape(Q_LORA, MLA_HEADS, MLA_QK)
    ukv = w_ukv.reshape(KV_LORA, MLA_HEADS, QK_NOPE + V_HEAD)
    return {
        "wcat": wcat,
        "qg": q_norm_g.reshape(1, Q_LORA), "kvg": kv_norm_g.reshape(1, KV_LORA),
        "wuq": _head_block(uq[..., :QK_NOPE], uq[..., QK_NOPE:]).astype(BF16),
        "wuqs": _head_block(None, _rope_swap(uq[..., QK_NOPE:])).astype(BF16),
        "wuk": _head_block(ukv[..., :QK_NOPE], None).astype(BF16),
        "wuv": ukv[..., QK_NOPE:].reshape(KV_LORA, MLA_HEADS * V_HEAD).astype(BF16),
        "woa": w_oa.astype(BF16), "wob": w_ob.astype(BF16), "wo": w_o.astype(BF16),
        "gpost": g_post_mix.reshape(1, D_MODEL), "gpre2": g_pre_ffn.reshape(1, D_MODEL),
        "gpost2": g_post_ffn.reshape(1, D_MODEL),
        "gdng": jnp.tile(gdn_norm_g.reshape(1, GDN_DV), (1, 2)),
        "wr_t": w_router.T, "eb": e_bias.reshape(N_EXPERTS, 1),
        "w_gate": w_gate, "w_up": w_up, "w_down": w_down,
        "wsg": ws_gate.astype(BF16), "wsu": ws_up.astype(BF16), "wsd": ws_down.astype(BF16),
    }


def _pad_lanes(v):
    v = v.reshape(1, -1)
    return jnp.concatenate([v, jnp.zeros((1, LANES - v.shape[1]), F32)], axis=-1)


def _layer_group(x, n_seq, seq_len, mods, mod_index, moe_mod_index, wts, gpre, conv_w, alog128, dtb128,
                 expand, rope_tabs, extra_kv, s0):
    pr = _proj(x, mods, mod_index, gpre, wts, rope_tabs)
    kvs = list(extra_kv) + [(pr["k"], pr["v"], seq_len)]
    omla = _attention(pr["q"], kvs, n_seq, seq_len, "attn_%d" % seq_len)
    prep = _gdn_prep(pr["qkv"], pr["ab"], conv_w, alog128, dtb128, expand, n_seq, seq_len)
    o_f, o_b, s_fin = _gdn_scan(prep, s0, n_seq, seq_len)
    t = n_seq * seq_len
    x1, h2, gates = _merge(x, mods, mod_index, omla, o_f.reshape(t, GDN_W), o_b.reshape(t, GDN_W), pr, wts)
    y = _moe(h2, gates, x1, mods, moe_mod_index, wts)
    return y, pr, s_fin


def _state_to_pairs(s):
    b = s.shape[0]
    s = s.reshape(b, 2, N_PAIRS, 2, GDN_DK, GDN_DV)
    return jnp.transpose(s, (0, 1, 2, 4, 3, 5)).reshape(b, 2, N_PAIRS, GDN_DK, 2 * GDN_DV)


def _pairs_to_state(s):
    b = s.shape[0]
    s = s.reshape(b, 2, N_PAIRS, GDN_DK, 2, GDN_DV)
    return jnp.transpose(s, (0, 1, 2, 4, 3, 5)).reshape(b, 2, GDN_HEADS, GDN_DK, GDN_DV)


def kernel(x_prompt, x_sample, cache_ckv, cache_krope, state_delta, c, c_ctx, w_ada, b_ada, g_pre_mix,
           g_post_mix, g_pre_ffn, g_post_ffn, w_in, q_norm_g, kv_norm_g, w_uq, w_ukv, conv_w, a_log,
           dt_bias, gdn_norm_g, w_oa, w_ob, w_o, w_router, e_bias, w_gate, w_up, w_down, ws_gate, ws_up,
           ws_down):
    batch, seq, _ = x_prompt.shape
    dec_batch, dec_seq, _ = x_sample.shape
    past = cache_ckv.shape[2]
    y_p = x_prompt.reshape(batch * seq, D_MODEL)
    y_s = x_sample.reshape(dec_batch * dec_seq, D_MODEL)
    expand = _expand_matrix()
    rope_tabs = _rope_tables(dec_seq)
    cond8 = jnp.concatenate([c_ctx[None], c, jnp.zeros((8 - 1 - dec_batch, D_MODEL), F32)], axis=0)
    ckv_out, krope_out, state_out = [], [], []
    for l in range(DEPTH):
        wts = _prepare_weights(w_in[l], q_norm_g[l], kv_norm_g[l], w_uq[l], w_ukv[l], w_oa[l], w_ob[l],
                               w_o[l], g_post_mix[l], g_pre_ffn[l], g_post_ffn[l], gdn_norm_g[l],
                               w_router[l], e_bias[l], w_gate[l], w_up[l], w_down[l], ws_gate[l],
                               ws_up[l], ws_down[l])
        gpre = g_pre_mix[l].reshape(1, D_MODEL)
        alog128 = _pad_lanes(a_log[l])
        dtb128 = _pad_lanes(dt_bias[l])
        mods = _mods(cond8, w_ada[l], b_ada[l].reshape(1, -1)).reshape(8, 6, D_MODEL)

        zero_state = jnp.zeros((batch, 2, N_PAIRS, GDN_DK, LANES), F32)
        y_p, pr_p, s_fin = _layer_group(
            y_p, batch, seq, mods, lambda i: 0, lambda i: 0, wts, gpre, conv_w[l], alog128, dtb128,
            expand, None, [], zero_state)
        ckv_out.append(pr_p["ckv"].reshape(batch, seq, KV_LORA))
        krope_out.append(pr_p["kr"][:, QK_NOPE:MLA_QK].reshape(batch, seq, QK_ROPE))
        state_out.append(_pairs_to_state(s_fin))

        kr_ctx = _rope_block(cache_krope[:, l].reshape(dec_batch * past, QK_ROPE))
        k_ctx, v_ctx = _cache_kv(cache_ckv[:, l].reshape(dec_batch * past, KV_LORA), kr_ctx, wts)
        tiles_per_seq = dec_seq // ROW_TILE
        moe_tiles_per_seq = dec_seq // MOE_TILE
        y_s, _, _ = _layer_group(
            y_s, dec_batch, dec_seq, mods, lambda i: 1 + i // tiles_per_seq,
            lambda i: 1 + i // moe_tiles_per_seq, wts, gpre, conv_w[l], alog128, dtb128, expand,
            rope_tabs, [(k_ctx, v_ctx, past)], _state_to_pairs(state_delta[:, l]))
    new_ckv = jnp.stack(ckv_out, axis=1)
    new_krope = jnp.stack(krope_out, axis=1)
    new_state = jnp.stack(state_out, axis=1)
    return (y_p.reshape(batch, seq, D_MODEL), y_s.reshape(dec_batch, dec_seq, D_MODEL),
            new_ckv, new_krope, new_state)
```

```python
import functools

import numpy as np
import jax
import jax.numpy as jnp
from jax import lax
from jax.experimental import pallas as pl
from jax.experimental.pallas import tpu as pltpu

F32 = jnp.float32
BF16 = jnp.bfloat16

D_MODEL = 1024
DEPTH = 1
GRID_W = 64
MLA_HEADS = 8
QK_NOPE = 64
QK_ROPE = 32
V_HEAD = 64
Q_LORA = 256
KV_LORA = 256
ROPE_THETA = 10000.0
GDN_HEADS = 8
GDN_DK = 64
GDN_DV = 64
CONV_K = 5
CHUNK = 64
N_EXPERTS = 64
TOP_K = 8
N_GROUPS = 8
TOPK_GROUPS = 4
D_EXPERT = 256
D_SHARED = 256
ROUTED_SCALE = 2.5
EPS = 1e-6

LANES = 128
MLA_QK = QK_NOPE + QK_ROPE
GDN_W = GDN_HEADS * GDN_DK
N_PAIRS = GDN_HEADS // 2
HEAD_PAD = LANES

_SEG = {}
_off = 0
for _name, _width in (("cq", Q_LORA), ("ckv", KV_LORA), ("kr", LANES), ("krs", LANES),
                      ("qkv", 3 * GDN_W), ("z", GDN_W), ("ab", LANES),
                      ("ga", D_MODEL), ("gb", D_MODEL)):
    _SEG[_name] = (_off, _off + _width)
    _off += _width
N_CAT = _off

ROW_TILE = 256
MOE_TILE = 1024
PREP_GROUP = 8
PREP_ROWS = 1024
SCAN_SEQS = 2
VMEM_LIMIT = 56 * 1024 * 1024


def _dot(a, b):
    return jnp.dot(a, b, preferred_element_type=F32)


def _dot_nt(a, b):
    return lax.dot_general(a, b, (((1,), (1,)), ((), ())), preferred_element_type=F32)


def _dot_tn(a, b):
    return lax.dot_general(a, b, (((0,), (0,)), ((), ())), preferred_element_type=F32)


def _dot_hi(a, b):
    return jnp.dot(a, b, preferred_element_type=F32, precision=lax.Precision.HIGHEST)


def _rms(x, g):
    return x * lax.rsqrt(jnp.mean(x * x, axis=-1, keepdims=True) + EPS) * g


def _silu(x):
    return x * jax.nn.sigmoid(x)


def _cparams(sem):
    return pltpu.CompilerParams(dimension_semantics=sem, vmem_limit_bytes=VMEM_LIMIT)


def _mods_kernel(c_ref, w_ref, b_ref, o_ref):
    s = _silu(c_ref[...]).astype(BF16)
    o_ref[...] = _dot(s, w_ref[...].astype(BF16)) + b_ref[...]


def _mods(cond8, w_ada, b_ada):
    n = w_ada.shape[1]
    bn = 512
    return pl.pallas_call(
        _mods_kernel,
        out_shape=jax.ShapeDtypeStruct((8, n), F32),
        grid=(n // bn,),
        in_specs=[pl.BlockSpec((8, D_MODEL), lambda j: (0, 0)),
                  pl.BlockSpec((D_MODEL, bn), lambda j: (0, j)),
                  pl.BlockSpec((1, bn), lambda j: (0, j))],
        out_specs=pl.BlockSpec((8, bn), lambda j: (0, j)),
        compiler_params=_cparams(("parallel",)),
        name="mods",
    )(cond8, w_ada, b_ada)


def _proj_kernel(rope, x_ref, m_ref, gpre_ref, wcat_ref, qg_ref, kvg_ref, wuq_ref, wuqs_ref,
                 wuk_ref, wuv_ref, cos_ref, sin_ref,
                 q_ref, k_ref, v_ref, ckv_ref, kr_ref, qkv_ref, z_ref, ab_ref, ga_ref, gb_ref):
    m = m_ref[0]
    h = (_rms(x_ref[...], gpre_ref[...]) * (1.0 + m[1:2]) + m[0:1]).astype(BF16)

    def seg(name):
        a, b = _SEG[name]
        return _dot(h, wcat_ref[:, a:b])

    qkv_ref[...] = seg("qkv")
    z_ref[...] = seg("z")
    ab_ref[...] = seg("ab")
    ga_ref[...] = seg("ga")
    gb_ref[...] = seg("gb")

    qn = _rms(seg("cq"), qg_ref[...]).astype(BF16)
    ckv = _rms(seg("ckv"), kvg_ref[...])
    ckv_ref[...] = ckv
    ckv_b = ckv.astype(BF16)
    kr = seg("kr")
    kr_ref[...] = kr
    qm = _dot(qn, wuq_ref[...])
    kk = _dot(ckv_b, wuk_ref[...])
    v_ref[...] = _dot(ckv_b, wuv_ref[...]).astype(BF16)
    scale = MLA_QK ** -0.5
    if rope:
        cos = cos_ref[...]
        sin = sin_ref[...]
        qs = _dot(qn, wuqs_ref[...])
        kr = kr * cos + seg("krs") * sin
    for hd in range(MLA_HEADS):
        sl = slice(hd * HEAD_PAD, (hd + 1) * HEAD_PAD)
        qh = qm[:, sl]
        if rope:
            qh = qh * cos + qs[:, sl] * sin
        q_ref[:, sl] = (qh * scale).astype(BF16)
        k_ref[:, sl] = (kk[:, sl] + kr).astype(BF16)


def _proj(x, mods, mod_index, gpre, wts, rope_tabs):
    t = x.shape[0]
    tm = ROW_TILE
    rope = rope_tabs is not None
    if rope:
        cos, sin = rope_tabs
        n_rope_blocks = cos.shape[0] // tm
        rope_spec = pl.BlockSpec((tm, LANES), lambda i: (i % n_rope_blocks, 0))
    else:
        cos = sin = jnp.zeros((8, LANES), F32)
        rope_spec = pl.BlockSpec((8, LANES), lambda i: (0, 0))

    def full(a):
        return pl.BlockSpec(a.shape, lambda i: (0,) * a.ndim)

    def rows(w):
        return pl.BlockSpec((tm, w), lambda i: (i, 0))

    out_widths = (("q", MLA_HEADS * HEAD_PAD, BF16), ("k", MLA_HEADS * HEAD_PAD, BF16),
                  ("v", MLA_HEADS * V_HEAD, BF16), ("ckv", KV_LORA, F32), ("kr", LANES, F32),
                  ("qkv", 3 * GDN_W, F32), ("z", GDN_W, F32), ("ab", LANES, F32),
                  ("ga", D_MODEL, F32), ("gb", D_MODEL, F32))
    outs = pl.pallas_call(
        functools.partial(_proj_kernel, rope),
        out_shape=[jax.ShapeDtypeStruct((t, w), dt) for _, w, dt in out_widths],
        grid=(t // tm,),
        in_specs=[rows(D_MODEL),
                  pl.BlockSpec((1, 6, D_MODEL), lambda i: (mod_index(i), 0, 0)),
                  full(gpre), full(wts["wcat"]), full(wts["qg"]), full(wts["kvg"]),
                  full(wts["wuq"]), full(wts["wuqs"]), full(wts["wuk"]), full(wts["wuv"]),
                  rope_spec, rope_spec],
        out_specs=[rows(w) for _, w, _ in out_widths],
        compiler_params=_cparams(("parallel",)),
        name="proj_rope" if rope else "proj",
    )(x, mods, gpre, wts["wcat"], wts["qg"], wts["kvg"], wts["wuq"], wts["wuqs"],
      wts["wuk"], wts["wuv"], cos, sin)
    return {name: o for (name, _, _), o in zip(out_widths, outs)}


def _cache_kv_kernel(ckv_ref, kr_ref, wuk_ref, wuv_ref, k_ref, v_ref):
    c = ckv_ref[...].astype(BF16)
    kk = _dot(c, wuk_ref[...])
    v_ref[...] = _dot(c, wuv_ref[...]).astype(BF16)
    kr = kr_ref[...]
    for hd in range(MLA_HEADS):
        sl = slice(hd * HEAD_PAD, (hd + 1) * HEAD_PAD)
        k_ref[:, sl] = (kk[:, sl] + kr).astype(BF16)


def _cache_kv(ckv, kr128, wts):
    t = ckv.shape[0]
    tm = 512
    return pl.pallas_call(
        _cache_kv_kernel,
        out_shape=[jax.ShapeDtypeStruct((t, MLA_HEADS * HEAD_PAD), BF16),
                   jax.ShapeDtypeStruct((t, MLA_HEADS * V_HEAD), BF16)],
        grid=(t // tm,),
        in_specs=[pl.BlockSpec((tm, KV_LORA), lambda i: (i, 0)),
                  pl.BlockSpec((tm, LANES), lambda i: (i, 0)),
                  pl.BlockSpec(wts["wuk"].shape, lambda i: (0, 0)),
                  pl.BlockSpec(wts["wuv"].shape, lambda i: (0, 0))],
        out_specs=[pl.BlockSpec((tm, MLA_HEADS * HEAD_PAD), lambda i: (i, 0)),
                   pl.BlockSpec((tm, MLA_HEADS * V_HEAD), lambda i: (i, 0))],
        compiler_params=_cparams(("parallel",)),
        name="cache_kv",
    )(ckv, kr128, wts["wuk"], wts["wuv"])


def _attn_kernel(n_kv, q_ref, *refs):
    k_refs = refs[:n_kv]
    v_refs = refs[n_kv:2 * n_kv]
    o_ref = refs[2 * n_kv]
    lane = lax.broadcasted_iota(jnp.int32, (1, LANES), 1)
    low = lane < V_HEAD
    for pr in range(MLA_HEADS // 2):
        halves = []
        for hd in (2 * pr, 2 * pr + 1):
            sl = slice(hd * HEAD_PAD, (hd + 1) * HEAD_PAD)
            qh = q_ref[:, sl]
            scores = [_dot_nt(qh, kr[:, sl]) for kr in k_refs]
            mx = functools.reduce(jnp.maximum, [jnp.max(s, axis=-1, keepdims=True) for s in scores])
            ps = [jnp.exp(s - mx) for s in scores]
            den = functools.reduce(jnp.add, [jnp.sum(p, axis=-1, keepdims=True) for p in ps])
            vsl = slice(pr * LANES, (pr + 1) * LANES)
            acc = functools.reduce(jnp.add, [_dot(p.astype(BF16), vr[:, vsl]) for p, vr in zip(ps, v_refs)])
            halves.append(acc / den)
        o_ref[:, pr * LANES:(pr + 1) * LANES] = jnp.where(low, halves[0], halves[1]).astype(BF16)


def _attention(q, kvs, n_seq, seq_len, name):
    tq = ROW_TILE
    nq = seq_len // tq
    n_kv = len(kvs)
    in_specs = [pl.BlockSpec((tq, MLA_HEADS * HEAD_PAD), lambda b, j: (b * nq + j, 0))]
    in_specs += [pl.BlockSpec((rows, MLA_HEADS * HEAD_PAD), lambda b, j: (b, 0)) for _, _, rows in kvs]
    in_specs += [pl.BlockSpec((rows, MLA_HEADS * V_HEAD), lambda b, j: (b, 0)) for _, _, rows in kvs]
    return pl.pallas_call(
        functools.partial(_attn_kernel, n_kv),
        out_shape=jax.ShapeDtypeStruct((n_seq * seq_len, MLA_HEADS * V_HEAD), BF16),
        grid=(n_seq, nq),
        in_specs=in_specs,
        out_specs=pl.BlockSpec((tq, MLA_HEADS * V_HEAD), lambda b, j: (b * nq + j, 0)),
        compiler_params=_cparams(("parallel", "parallel")),
        name=name,
    )(q, *[k for k, _, _ in kvs], *[v for _, v, _ in kvs])


def _pair_masks():
    lane = lax.broadcasted_iota(jnp.int32, (1, LANES), 1)
    return lane < GDN_DK


def _stack(x, low):
    zero = jnp.zeros_like(x)
    return jnp.concatenate([jnp.where(low, x, zero), jnp.where(low, zero, x)], axis=0)


def _split3(x):
    hi = x.astype(BF16)
    r = x - hi.astype(F32)
    mid = r.astype(BF16)
    lo = (r - mid.astype(F32)).astype(BF16)
    return hi, mid, lo


def _gdn_prep_kernel(seq_len, q_ref, k_ref, v_ref, cwq_ref, cwk_ref, cwv_ref, ab_ref, alog_ref, dtb_ref, e_ref,
                     uf_ref, ub_ref, wf_ref, wb_ref, af_ref, abk_ref, qdf_ref, qdb_ref, kdf_ref, kdb_ref,
                     glf_ref, glb_ref,
                     qn_s, kn_s, vn_s, gcb_s, gf_s):
    seq = q_ref.shape[0]
    n_chunks = seq // CHUNK
    low = _pair_masks()
    row = lax.broadcasted_iota(jnp.int32, (seq, 1), 0) % seq_len
    lane = lax.broadcasted_iota(jnp.int32, (1, LANES), 1)

    def conv(x_ref, cw_ref):
        x = x_ref[...]
        acc = jnp.zeros_like(x)
        for j in range(CONV_K):
            sh = CONV_K // 2 - j
            xs = x if sh == 0 else pltpu.roll(x, sh % seq, axis=0)
            src = row - sh
            valid = (src >= 0) & (src < seq_len)
            acc = acc + jnp.where(valid, xs, 0.0) * cw_ref[j:j + 1, :]
        return _silu(acc)

    def l2n(x):
        sq = x * x
        s0 = jnp.sum(jnp.where(low, sq, 0.0), axis=-1, keepdims=True)
        s1 = jnp.sum(jnp.where(low, 0.0, sq), axis=-1, keepdims=True)
        return x * lax.rsqrt(jnp.where(low, s0, s1) + EPS)

    qn_s[...] = l2n(conv(q_ref, cwq_ref)) * (GDN_DK ** -0.5)
    kn_s[...] = l2n(conv(k_ref, cwk_ref))
    vn_s[...] = conv(v_ref, cwv_ref)

    a = ab_ref[...]
    xg = a + dtb_ref[...]
    softplus = jnp.maximum(xg, 0.0) + jnp.log(1.0 + jnp.exp(-jnp.abs(xg)))
    act = jnp.where(lane < 2 * GDN_HEADS, -jnp.exp(alog_ref[...]) * softplus, jax.nn.sigmoid(a))

    ti = lax.broadcasted_iota(jnp.int32, (CHUNK, CHUNK), 0)
    tj = lax.broadcasted_iota(jnp.int32, (CHUNK, CHUNK), 1)
    tri_lo = (tj <= ti).astype(BF16)
    tri_up = (tj >= ti).astype(BF16)
    for c in range(n_chunks):
        ac = act[c * CHUNK:(c + 1) * CHUNK]
        pieces = _split3(ac)
        lo = functools.reduce(jnp.add, [_dot(tri_lo, pc) for pc in pieces])
        up = functools.reduce(jnp.add, [_dot(tri_up, pc) for pc in pieces])
        gcb_s[c * CHUNK:(c + 1) * CHUNK, :] = jnp.where(lane < GDN_HEADS, lo,
                                                        jnp.where(lane < 2 * GDN_HEADS, up, ac))
    expand = e_ref[0].astype(BF16)
    gf_s[...] = functools.reduce(jnp.add, [_dot(pc, expand) for pc in _split3(gcb_s[...])])

    ri = lax.broadcasted_iota(jnp.int32, (CHUNK, LANES), 0)
    cj = lax.broadcasted_iota(jnp.int32, (CHUNK, LANES), 1) % CHUNK
    eye = (ri == cj).astype(F32)
    ones_b = jnp.ones((CHUNK, LANES), BF16)
    lane0 = (lax.broadcasted_iota(jnp.int32, (1, LANES), 1) % CHUNK) == 0

    def pmm(x, y):
        return _dot(x.astype(BF16), _stack(y, low).astype(BF16))

    def pmm_split(x, y):
        xh, xm, _ = _split3(x)
        yh, ym, _ = _split3(_stack(y, low))
        return _dot(xh, yh) + (_dot(xh, ym) + _dot(xm, yh))

    out_refs = ((uf_ref, wf_ref, af_ref, qdf_ref, kdf_ref, glf_ref),
                (ub_ref, wb_ref, abk_ref, qdb_ref, kdb_ref, glb_ref))
    incl = (ri >= cj, ri <= cj)
    strict = (ri > cj, ri < cj)

    def group(it, carry):
        cs = [it * PREP_GROUP + cc for cc in range(PREP_GROUP)]
        rows = [pl.ds(pl.multiple_of(c * CHUNK, CHUNK), CHUNK) for c in cs]
        qc = [qn_s[r, :] for r in rows]
        kc = [kn_s[r, :] for r in rows]
        vc = [vn_s[r, :] for r in rows]
        kst = [_stack(k, low).astype(BF16) for k in kc]
        kk = [_dot_nt(k.astype(BF16), ks) for k, ks in zip(kc, kst)]
        qk = [_dot_nt(q.astype(BF16), ks) for q, ks in zip(qc, kst)]
        chains = [(ci, d) for ci in range(PREP_GROUP) for d in range(2)]
        gc = [gf_s[rows[ci], d * LANES:(d + 1) * LANES] for ci, d in chains]
        beta = [gf_s[rows[ci], (2 + d) * LANES:(3 + d) * LANES] for ci, d in chains]
        pieces = [_split3(_stack(jnp.where(lane0, g, 0.0), low)) for g in gc]
        gr = [functools.reduce(jnp.add, [_dot_nt(ones_b, pc) for pc in ps]) for ps in pieces]
        dm = [jnp.exp(jnp.where(incl[d], g - r, -jnp.inf)) for (ci, d), g, r in zip(chains, gc, gr)]
        lm = [jnp.where(strict[d], b * kk[ci] * m, 0.0) for (ci, d), b, m in zip(chains, beta, dm)]
        aint = [(qk[ci] * m).astype(BF16) for (ci, d), m in zip(chains, dm)]
        x = [-l for l in lm]
        t = [eye + xx for xx in x]
        for _ in range(5):
            x = [pmm(xx, xx) for xx in x]
            t = [tt + pmm(tt, xx) for tt, xx in zip(t, x)]
        resid = [eye - pmm_split(eye + l, tt) for l, tt in zip(lm, t)]
        t = [tt + pmm(tt, rr) for tt, rr in zip(t, resid)]
        egc = [jnp.exp(g) for g in gc]
        u = [pmm(tt, vc[ci] * b) for (ci, d), tt, b in zip(chains, t, beta)]
        w = [pmm(tt, kc[ci] * b * e).astype(BF16) for (ci, d), tt, b, e in zip(chains, t, beta, egc)]
        qd = [(qc[ci] * e).astype(BF16) for (ci, d), e in zip(chains, egc)]
        gtot = [g[CHUNK - 1:CHUNK, :] if d == 0 else g[0:1, :] for (ci, d), g in zip(chains, gc)]
        kd = [(kc[ci] * jnp.exp(gt - g)).astype(BF16) for (ci, d), gt, g in zip(chains, gtot, gc)]
        for n, (ci, d) in enumerate(chains):
            u_ref, w_ref, a_ref, qd_ref, kd_ref, gl_ref = out_refs[d]
            u_ref[0, rows[ci], :] = u[n]
            w_ref[0, rows[ci], :] = w[n]
            a_ref[0, rows[ci], :] = aint[n]
            qd_ref[0, rows[ci], :] = qd[n]
            kd_ref[0, rows[ci], :] = kd[n]
            gl_ref[0, pl.ds(cs[ci], 1), :, :] = jnp.broadcast_to(jnp.exp(gtot[n]), (1, 8, LANES))
        return carry

    lax.fori_loop(0, n_chunks // PREP_GROUP, group, 0)


def _gdn_prep(qkv, ab, conv_w, alog128, dtb128, expand, n_seq, seq_len):
    rb = max(seq_len, PREP_ROWS)
    nb = n_seq * seq_len // rb
    n_chunks = rb // CHUNK
    col = lambda off: pl.BlockSpec((rb, LANES), lambda s, p: (s, off + p))
    cw = lambda off: pl.BlockSpec((CONV_K, LANES), lambda s, p: (0, off + p))
    vec = pl.BlockSpec((1, LANES), lambda s, p: (0, 0))
    big = lambda: pl.BlockSpec((1, rb, LANES), lambda s, p: (s, 0, p))
    glspec = lambda: pl.BlockSpec((1, n_chunks, 8, LANES), lambda s, p: (s, 0, 0, p))
    shp = lambda dt: jax.ShapeDtypeStruct((nb, rb, GDN_W), dt)
    glshp = jax.ShapeDtypeStruct((nb, n_chunks, 8, GDN_W), F32)
    outs = pl.pallas_call(
        functools.partial(_gdn_prep_kernel, seq_len),
        out_shape=[shp(F32), shp(F32)] + [shp(BF16)] * 8 + [glshp, glshp],
        grid=(nb, N_PAIRS),
        in_specs=[col(0), col(N_PAIRS), col(2 * N_PAIRS), cw(0), cw(N_PAIRS), cw(2 * N_PAIRS),
                  pl.BlockSpec((rb, LANES), lambda s, p: (s, 0)), vec, vec,
                  pl.BlockSpec((1, LANES, 4 * LANES), lambda s, p: (p, 0, 0))],
        out_specs=[big() for _ in range(10)] + [glspec(), glspec()],
        scratch_shapes=[pltpu.VMEM((rb, LANES), F32)] * 4 + [pltpu.VMEM((rb, 4 * LANES), F32)],
        compiler_params=_cparams(("parallel", "parallel")),
        name="gdn_prep_%d" % seq_len,
    )(qkv, qkv, qkv, conv_w, conv_w, conv_w, ab, alog128, dtb128, expand)
    per_seq = [o.reshape(n_seq, seq_len, GDN_W) for o in outs[:10]]
    return per_seq + [o.reshape(n_seq, seq_len // CHUNK, 8, GDN_W) for o in outs[10:]]


def _gdn_scan_kernel(uf_ref, ub_ref, wf_ref, wb_ref, af_ref, abk_ref, qdf_ref, qdb_ref, kdf_ref, kdb_ref,
                     glf_ref, glb_ref, s0_ref, of_ref, ob_ref, sfin_ref, state):
    step = pl.program_id(1)
    n_steps = pl.num_programs(1)
    low = _pair_masks()
    chains = [(d, j, p) for d in range(2) for j in range(SCAN_SEQS) for p in range(N_PAIRS)]

    first = step == 0
    per_dir = ((uf_ref, wf_ref, af_ref, qdf_ref, kdf_ref, glf_ref, of_ref),
               (ub_ref, wb_ref, abk_ref, qdb_ref, kdb_ref, glb_ref, ob_ref))
    def rd(k, d, j, p):
        return per_dir[d][k][j, :, p * LANES:(p + 1) * LANES]

    s = [jnp.where(first, _stack(s0_ref[j, d, p], low), state[idx]) for idx, (d, j, p) in enumerate(chains)]
    sb = [x.astype(BF16) for x in s]
    ws = [_dot(rd(1, *c), b) for c, b in zip(chains, sb)]
    qs = [_dot(rd(3, *c), b) for c, b in zip(chains, sb)]
    vst = [_stack(rd(0, *c) - w, low).astype(BF16) for c, w in zip(chains, ws)]
    upd = [_dot_tn(_stack(rd(4, *c), low), v) for c, v in zip(chains, vst)]
    intra = [_dot(rd(2, *c), v) for c, v in zip(chains, vst)]
    for idx, (d, j, p) in enumerate(chains):
        sl = slice(p * LANES, (p + 1) * LANES)
        state[idx] = s[idx] * per_dir[d][5][j, 0, 0:1, sl] + upd[idx]
        per_dir[d][6][j, :, sl] = qs[idx] + intra[idx]

    @pl.when(step == n_steps - 1)
    def _():
        for idx, (d, j, p) in enumerate(chains):
            s = state[idx]
            sfin_ref[j, d, p] = s[:GDN_DK] + s[GDN_DK:]


def _gdn_scan(prep, s0, n_seq, seq_len):
    n_chunks = seq_len // CHUNK
    ns = SCAN_SEQS
    fwd = lambda: pl.BlockSpec((ns, CHUNK, GDN_W), lambda g, i: (g, i, 0))
    bwd = lambda: pl.BlockSpec((ns, CHUNK, GDN_W), lambda g, i: (g, n_chunks - 1 - i, 0))
    glf = pl.BlockSpec((ns, 1, 8, GDN_W), lambda g, i: (g, i, 0, 0))
    glb = pl.BlockSpec((ns, 1, 8, GDN_W), lambda g, i: (g, n_chunks - 1 - i, 0, 0))
    st = lambda: pl.BlockSpec((ns, 2, N_PAIRS, GDN_DK, LANES), lambda g, i: (g, 0, 0, 0, 0))
    oshape = jax.ShapeDtypeStruct((n_seq, seq_len, GDN_W), F32)
    return pl.pallas_call(
        _gdn_scan_kernel,
        out_shape=[oshape, oshape, jax.ShapeDtypeStruct((n_seq, 2, N_PAIRS, GDN_DK, LANES), F32)],
        grid=(n_seq // ns, n_chunks),
        in_specs=[fwd(), bwd()] * 5 + [glf, glb, st()],
        out_specs=[fwd(), bwd(), st()],
        scratch_shapes=[pltpu.VMEM((2 * ns * N_PAIRS, LANES, LANES), F32)],
        compiler_params=_cparams(("parallel", "arbitrary")),
        name="gdn_scan_%d" % seq_len,
    )(*prep, s0)


def _route(sel, s):
    per_group = N_EXPERTS // N_GROUPS
    ninf = -jnp.inf
    sub = lax.broadcasted_iota(jnp.int32, sel.shape, 1).astype(F32)
    gid = lax.broadcasted_iota(jnp.int32, (N_GROUPS, 1, sel.shape[2]), 0).astype(F32)
    m1 = jnp.max(sel, axis=1, keepdims=True)
    i1 = jnp.min(jnp.where(sel == m1, sub, float(per_group)), axis=1, keepdims=True)
    m2 = jnp.max(jnp.where(sub == i1, ninf, sel), axis=1, keepdims=True)
    work = m1 + m2
    gmask = jnp.zeros(work.shape, jnp.bool_)
    for _ in range(TOPK_GROUPS):
        m = jnp.max(work, axis=0, keepdims=True)
        idx = jnp.min(jnp.where(work == m, gid, float(N_GROUPS)), axis=0, keepdims=True)
        pick = gid == idx
        gmask = gmask | pick
        work = jnp.where(pick, ninf, work)
    work = jnp.where(gmask, sel, ninf)
    eid = gid * per_group + sub
    chosen = jnp.zeros(sel.shape, jnp.bool_)
    for _ in range(TOP_K):
        m = jnp.max(jnp.max(work, axis=1, keepdims=True), axis=0, keepdims=True)
        idx = jnp.min(jnp.min(jnp.where(work == m, eid, float(N_EXPERTS)), axis=1, keepdims=True),
                      axis=0, keepdims=True)
        pick = eid == idx
        chosen = chosen | pick
        work = jnp.where(pick, ninf, work)
    wk = jnp.where(chosen, s, 0.0)
    den = jnp.sum(jnp.sum(wk, axis=1, keepdims=True), axis=0, keepdims=True)
    return wk / den * ROUTED_SCALE


def _merge_kernel(x_ref, m_ref, omla_ref, of_ref, ob_ref, z_ref, ga_ref, gb_ref,
                  woa_ref, wob_ref, wo_ref, gpost_ref, gpre_ref, gdng_ref, wr_ref, eb_ref,
                  x1_ref, h2_ref, gates_ref):
    m = m_ref[0]
    low = _pair_masks()
    o = of_ref[...] + ob_ref[...]
    z = z_ref[...]
    parts = []
    for p in range(N_PAIRS):
        sl = slice(p * LANES, (p + 1) * LANES)
        op = o[:, sl]
        sq = op * op
        s0 = jnp.sum(jnp.where(low, sq, 0.0), axis=-1, keepdims=True)
        s1 = jnp.sum(jnp.where(low, 0.0, sq), axis=-1, keepdims=True)
        ms = jnp.where(low, s0, s1) * (1.0 / GDN_DV)
        parts.append(op * lax.rsqrt(ms + EPS) * gdng_ref[...] * _silu(z[:, sl]))
    og = jnp.concatenate(parts, axis=1).astype(BF16)
    ya = _dot(omla_ref[...], woa_ref[...])
    yb = _dot(og, wob_ref[...])
    mix = (jax.nn.sigmoid(ga_ref[...]) * ya + jax.nn.sigmoid(gb_ref[...]) * yb).astype(BF16)
    y = _dot(mix, wo_ref[...])
    x1 = x_ref[...] + m[2:3] * _rms(y, gpost_ref[...])
    x1_ref[...] = x1
    h2 = _rms(x1, gpre_ref[...]) * (1.0 + m[4:5]) + m[3:4]
    h2_ref[...] = h2.astype(BF16)
    logits = lax.dot_general(wr_ref[...], h2, (((1,), (1,)), ((), ())),
                             preferred_element_type=F32, precision=lax.Precision.HIGHEST)
    s = jax.nn.sigmoid(logits)
    sel = s + eb_ref[...]
    tm = s.shape[1]
    shape3 = (N_GROUPS, N_EXPERTS // N_GROUPS, tm)
    gates_t = _route(sel.reshape(shape3), s.reshape(shape3)).reshape(N_EXPERTS, tm)
    gates_t = jnp.concatenate([gates_t, jnp.zeros((LANES - N_EXPERTS, tm), F32)], axis=0)
    gates_ref[...] = gates_t.T


def _merge(x, mods, mod_index, omla, o_f, o_b, pr, wts):
    t = x.shape[0]
    tm = ROW_TILE

    def full(a):
        return pl.BlockSpec(a.shape, lambda i: (0,) * a.ndim)

    def rows(w):
        return pl.BlockSpec((tm, w), lambda i: (i, 0))

    names = ("woa", "wob", "wo", "gpost", "gpre2", "gdng", "wr_t", "eb")
    return pl.pallas_call(
        _merge_kernel,
        out_shape=[jax.ShapeDtypeStruct((t, D_MODEL), F32), jax.ShapeDtypeStruct((t, D_MODEL), BF16),
                   jax.ShapeDtypeStruct((t, LANES), F32)],
        grid=(t // tm,),
        in_specs=[rows(D_MODEL), pl.BlockSpec((1, 6, D_MODEL), lambda i: (mod_index(i), 0, 0)),
                  rows(MLA_HEADS * V_HEAD), rows(GDN_W), rows(GDN_W), rows(GDN_W),
                  rows(D_MODEL), rows(D_MODEL)] + [full(wts[n]) for n in names],
        out_specs=[rows(D_MODEL), rows(D_MODEL), rows(LANES)],
        compiler_params=_cparams(("parallel",)),
        name="merge",
    )(x, mods, omla, o_f, o_b, pr["z"], pr["ga"], pr["gb"], *[wts[n] for n in names])


def _moe_kernel(h_ref, gates_ref, wg_ref, wu_ref, wd_ref, wsg_ref, wsu_ref, wsd_ref,
                x1_ref, m_ref, gpost_ref, y_ref, acc_ref):
    e = pl.program_id(1)
    h = h_ref[...]

    @pl.when(e == 0)
    def _():
        sh = (_silu(_dot(h, wsg_ref[...])) * _dot(h, wsu_ref[...])).astype(BF16)
        acc_ref[...] = _dot(sh, wsd_ref[...])

    lane = lax.broadcasted_iota(jnp.int32, (1, LANES), 1)
    gcol = jnp.sum(jnp.where(lane == e, gates_ref[...], 0.0), axis=-1, keepdims=True)
    hg = _dot(h, wg_ref[0].astype(BF16))
    hu = _dot(h, wu_ref[0].astype(BF16))
    act = (_silu(hg) * hu * gcol).astype(BF16)
    acc_ref[...] += _dot(act, wd_ref[0].astype(BF16))

    @pl.when(e == pl.num_programs(1) - 1)
    def _():
        m = m_ref[0]
        y_ref[...] = x1_ref[...] + m[5:6] * _rms(acc_ref[...], gpost_ref[...])


def _moe(h2, gates, x1, mods, mod_index, wts):
    t = h2.shape[0]
    tm = MOE_TILE

    def full(a):
        return pl.BlockSpec(a.shape, lambda i, e: (0,) * a.ndim)

    return pl.pallas_call(
        _moe_kernel,
        out_shape=jax.ShapeDtypeStruct((t, D_MODEL), F32),
        grid=(t // tm, N_EXPERTS),
        in_specs=[pl.BlockSpec((tm, D_MODEL), lambda i, e: (i, 0)),
                  pl.BlockSpec((tm, LANES), lambda i, e: (i, 0)),
                  pl.BlockSpec((1, D_MODEL, D_EXPERT), lambda i, e: (e, 0, 0)),
                  pl.BlockSpec((1, D_MODEL, D_EXPERT), lambda i, e: (e, 0, 0)),
                  pl.BlockSpec((1, D_EXPERT, D_MODEL), lambda i, e: (e, 0, 0)),
                  full(wts["wsg"]), full(wts["wsu"]), full(wts["wsd"]),
                  pl.BlockSpec((tm, D_MODEL), lambda i, e: (i, 0)),
                  pl.BlockSpec((1, 6, D_MODEL), lambda i, e: (mod_index(i), 0, 0)),
                  full(wts["gpost2"])],
        out_specs=pl.BlockSpec((tm, D_MODEL), lambda i, e: (i, 0)),
        scratch_shapes=[pltpu.VMEM((tm, D_MODEL), F32)],
        compiler_params=_cparams(("parallel", "arbitrary")),
        name="moe",
    )(h2, gates, wts["w_gate"], wts["w_up"], wts["w_down"], wts["wsg"], wts["wsu"], wts["wsd"],
      x1, mods, wts["gpost2"])


def _rope_swap(w):
    nf = QK_ROPE // 4
    parts = [w[..., i * nf:(i + 1) * nf] for i in range(4)]
    return jnp.concatenate([parts[1], parts[0], parts[3], parts[2]], axis=-1)


def _head_block(nope, rope):
    lead = nope.shape[:-2] if nope is not None else rope.shape[:-2]
    nope = jnp.zeros(lead + (MLA_HEADS, QK_NOPE), F32) if nope is None else nope
    rope = jnp.zeros(lead + (MLA_HEADS, QK_ROPE), F32) if rope is None else rope
    pad = jnp.zeros(lead + (MLA_HEADS, HEAD_PAD - MLA_QK), F32)
    return jnp.concatenate([nope, rope, pad], axis=-1).reshape(lead + (MLA_HEADS * HEAD_PAD,))


def _rope_block(w):
    lead = w.shape[:-1]
    return jnp.concatenate([jnp.zeros(lead + (QK_NOPE,), F32), w,
                            jnp.zeros(lead + (HEAD_PAD - MLA_QK,), F32)], axis=-1)


def _rope_tables(n_tokens):
    rows = n_tokens // GRID_W
    row = np.repeat(np.arange(rows, dtype=np.float32), GRID_W)
    colv = np.tile(np.arange(GRID_W, dtype=np.float32), rows)
    nf = QK_ROPE // 4
    inv = jnp.asarray(ROPE_THETA, F32) ** (-jnp.arange(nf, dtype=F32) / nf)
    ang_r = jnp.asarray(row)[:, None] * inv
    ang_c = jnp.asarray(colv)[:, None] * inv
    cos32 = jnp.concatenate([jnp.cos(ang_r), jnp.cos(ang_r), jnp.cos(ang_c), jnp.cos(ang_c)], axis=-1)
    sin32 = jnp.concatenate([-jnp.sin(ang_r), jnp.sin(ang_r), -jnp.sin(ang_c), jnp.sin(ang_c)], axis=-1)
    ones = jnp.ones((n_tokens, QK_NOPE), F32)
    tail = jnp.zeros((n_tokens, HEAD_PAD - MLA_QK), F32)
    cos = jnp.concatenate([ones, cos32, tail], axis=-1)
    sin = jnp.concatenate([jnp.zeros((n_tokens, QK_NOPE), F32), sin32, tail], axis=-1)
    return cos, sin


def _expand_matrix():
    e = np.zeros((N_PAIRS, LANES, 4 * LANES), np.float32)
    for p in range(N_PAIRS):
        for blk in range(4):
            for hh in range(2):
                src = blk * GDN_HEADS + 2 * p + hh
                e[p, src, blk * LANES + hh * GDN_DK: blk * LANES + (hh + 1) * GDN_DK] = 1.0
    return jnp.asarray(e)


def _prepare_weights(w_in, q_norm_g, kv_norm_g, w_uq, w_ukv, w_oa, w_ob, w_o, g_post_mix, g_pre_ffn,
                     g_post_ffn, gdn_norm_g, w_router, e_bias, w_gate, w_up, w_down, ws_gate, ws_up, ws_down):
    offs = np.cumsum((Q_LORA, KV_LORA, QK_ROPE, 3 * GDN_W, GDN_W, 2 * GDN_HEADS, 2 * GDN_HEADS,
                      D_MODEL, D_MODEL))[:-1].tolist()
    cq, ckv, kr, qkv, z, a, b, ga, gb = jnp.split(w_in, offs, axis=-1)
    ab = jnp.concatenate([a, b, jnp.zeros((D_MODEL, LANES - 4 * GDN_HEADS), F32)], axis=-1)
    wcat = jnp.concatenate([cq, ckv, _rope_block(kr), _rope_block(_rope_swap(kr)), qkv, z, ab, ga, gb],
                           axis=-1).astype(BF16)
    uq = w_uq.reshape(Q_LORA, MLA_HEADS, MLA_QK)
    ukv = w_ukv.reshape(KV_LORA, MLA_HEADS, QK_NOPE + V_HEAD)
    return {
        "wcat": wcat,
        "qg": q_norm_g.reshape(1, Q_LORA), "kvg": kv_norm_g.reshape(1, KV_LORA),
        "wuq": _head_block(uq[..., :QK_NOPE], uq[..., QK_NOPE:]).astype(BF16),
        "wuqs": _head_block(None, _rope_swap(uq[..., QK_NOPE:])).astype(BF16),
        "wuk": _head_block(ukv[..., :QK_NOPE], None).astype(BF16),
        "wuv": ukv[..., QK_NOPE:].reshape(KV_LORA, MLA_HEADS * V_HEAD).astype(BF16),
        "woa": w_oa.astype(BF16), "wob": w_ob.astype(BF16), "wo": w_o.astype(BF16),
        "gpost": g_post_mix.reshape(1, D_MODEL), "gpre2": g_pre_ffn.reshape(1, D_MODEL),
        "gpost2": g_post_ffn.reshape(1, D_MODEL),
        "gdng": jnp.tile(gdn_norm_g.reshape(1, GDN_DV), (1, 2)),
        "wr_t": w_router.T, "eb": e_bias.reshape(N_EXPERTS, 1),
        "w_gate": w_gate, "w_up": w_up, "w_down": w_down,
        "wsg": ws_gate.astype(BF16), "wsu": ws_up.astype(BF16), "wsd": ws_down.astype(BF16),
    }


def _pad_lanes(v):
    v = v.reshape(1, -1)
    return jnp.concatenate([v, jnp.zeros((1, LANES - v.shape[1]), F32)], axis=-1)


def _layer_group(x, n_seq, seq_len, mods, mod_index, moe_mod_index, wts, gpre, conv_w, alog128, dtb128,
                 expand, rope_tabs, extra_kv, s0):
    pr = _proj(x, mods, mod_index, gpre, wts, rope_tabs)
    kvs = list(extra_kv) + [(pr["k"], pr["v"], seq_len)]
    omla = _attention(pr["q"], kvs, n_seq, seq_len, "attn_%d" % seq_len)
    prep = _gdn_prep(pr["qkv"], pr["ab"], conv_w, alog128, dtb128, expand, n_seq, seq_len)
    o_f, o_b, s_fin = _gdn_scan(prep, s0, n_seq, seq_len)
    t = n_seq * seq_len
    x1, h2, gates = _merge(x, mods, mod_index, omla, o_f.reshape(t, GDN_W), o_b.reshape(t, GDN_W), pr, wts)
    y = _moe(h2, gates, x1, mods, moe_mod_index, wts)
    return y, pr, s_fin


def _state_to_pairs(s):
    b = s.shape[0]
    s = s.reshape(b, 2, N_PAIRS, 2, GDN_DK, GDN_DV)
    return jnp.transpose(s, (0, 1, 2, 4, 3, 5)).reshape(b, 2, N_PAIRS, GDN_DK, 2 * GDN_DV)


def _pairs_to_state(s):
    b = s.shape[0]
    s = s.reshape(b, 2, N_PAIRS, GDN_DK, 2, GDN_DV)
    return jnp.transpose(s, (0, 1, 2, 4, 3, 5)).reshape(b, 2, GDN_HEADS, GDN_DK, GDN_DV)


def kernel(x_prompt, x_sample, cache_ckv, cache_krope, state_delta, c, c_ctx, w_ada, b_ada, g_pre_mix,
           g_post_mix, g_pre_ffn, g_post_ffn, w_in, q_norm_g, kv_norm_g, w_uq, w_ukv, conv_w, a_log,
           dt_bias, gdn_norm_g, w_oa, w_ob, w_o, w_router, e_bias, w_gate, w_up, w_down, ws_gate, ws_up,
           ws_down):
    batch, seq, _ = x_prompt.shape
    dec_batch, dec_seq, _ = x_sample.shape
    past = cache_ckv.shape[2]
    y_p = x_prompt.reshape(batch * seq, D_MODEL)
    y_s = x_sample.reshape(dec_batch * dec_seq, D_MODEL)
    expand = _expand_matrix()
    rope_tabs = _rope_tables(dec_seq)
    cond8 = jnp.concatenate([c_ctx[None], c, jnp.zeros((8 - 1 - dec_batch, D_MODEL), F32)], axis=0)
    ckv_out, krope_out, state_out = [], [], []
    for l in range(DEPTH):
        wts = _prepare_weights(w_in[l], q_norm_g[l], kv_norm_g[l], w_uq[l], w_ukv[l], w_oa[l], w_ob[l],
                               w_o[l], g_post_mix[l], g_pre_ffn[l], g_post_ffn[l], gdn_norm_g[l],
                               w_router[l], e_bias[l], w_gate[l], w_up[l], w_down[l], ws_gate[l],
                               ws_up[l], ws_down[l])
        gpre = g_pre_mix[l].reshape(1, D_MODEL)
        alog128 = _pad_lanes(a_log[l])
        dtb128 = _pad_lanes(dt_bias[l])
        mods = _mods(cond8, w_ada[l], b_ada[l].reshape(1, -1)).reshape(8, 6, D_MODEL)

        zero_state = jnp.zeros((batch, 2, N_PAIRS, GDN_DK, LANES), F32)
        y_p, pr_p, s_fin = _layer_group(
            y_p, batch, seq, mods, lambda i: 0, lambda i: 0, wts, gpre, conv_w[l], alog128, dtb128,
            expand, None, [], zero_state)
        ckv_out.append(pr_p["ckv"].reshape(batch, seq, KV_LORA))
        krope_out.append(pr_p["kr"][:, QK_NOPE:MLA_QK].reshape(batch, seq, QK_ROPE))
        state_out.append(_pairs_to_state(s_fin))

        kr_ctx = _rope_block(cache_krope[:, l].reshape(dec_batch * past, QK_ROPE))
        k_ctx, v_ctx = _cache_kv(cache_ckv[:, l].reshape(dec_batch * past, KV_LORA), kr_ctx, wts)
        tiles_per_seq = dec_seq // ROW_TILE
        moe_tiles_per_seq = dec_seq // MOE_TILE
        y_s, _, _ = _layer_group(
            y_s, dec_batch, dec_seq, mods, lambda i: 1 + i // tiles_per_seq,
            lambda i: 1 + i // moe_tiles_per_seq, wts, gpre, conv_w[l], alog128, dtb128, expand,
            rope_tabs, [(k_ctx, v_ctx, past)], _state_to_pairs(state_delta[:, l]))
    new_ckv = jnp.stack(ckv_out, axis=1)
    new_krope = jnp.stack(krope_out, axis=1)
    new_state = jnp.stack(state_out, axis=1)
    return (y_p.reshape(batch, seq, D_MODEL), y_s.reshape(dec_batch, dec_seq, D_MODEL),
            new_ckv, new_krope, new_state)
```

```python
import functools

import numpy as np
import jax
import jax.numpy as jnp
from jax import lax
from jax.experimental import pallas as pl
from jax.experimental.pallas import tpu as pltpu

F32 = jnp.float32
BF16 = jnp.bfloat16

D_MODEL = 1024
DEPTH = 1
GRID_W = 64
MLA_HEADS = 8
QK_NOPE = 64
QK_ROPE = 32
V_HEAD = 64
Q_LORA = 256
KV_LORA = 256
ROPE_THETA = 10000.0
GDN_HEADS = 8
GDN_DK = 64
GDN_DV = 64
CONV_K = 5
CHUNK = 64
N_EXPERTS = 64
TOP_K = 8
N_GROUPS = 8
TOPK_GROUPS = 4
D_EXPERT = 256
D_SHARED = 256
ROUTED_SCALE = 2.5
EPS = 1e-6

LANES = 128
MLA_QK = QK_NOPE + QK_ROPE
GDN_W = GDN_HEADS * GDN_DK
N_PAIRS = GDN_HEADS // 2
HEAD_PAD = LANES

_SEG = {}
_off = 0
for _name, _width in (("cq", Q_LORA), ("ckv", KV_LORA), ("kr", LANES), ("krs", LANES),
                      ("qkv", 3 * GDN_W), ("z", GDN_W), ("ab", LANES),
                      ("ga", D_MODEL), ("gb", D_MODEL)):
    _SEG[_name] = (_off, _off + _width)
    _off += _width
N_CAT = _off

ROW_TILE = 256
N_TOKENS = 8192
MOE_BLOCK = ROW_TILE
PIECE = 16
EXP_TILE = 256
PIECES_PER_TILE = EXP_TILE // PIECE
N_BLOCKS = N_TOKENS // MOE_BLOCK
STAGE_PIECES = -(-((MOE_BLOCK * TOP_K + N_EXPERTS * (PIECE - 1)) // PIECE) // PIECES_PER_TILE) * PIECES_PER_TILE
N_EXP_TILES = -(-((N_TOKENS * TOP_K + N_BLOCKS * N_EXPERTS * (PIECE - 1)) // PIECE
                  + N_EXPERTS * (PIECES_PER_TILE - 1)) // PIECES_PER_TILE)
DUMP_PIECE0 = N_EXP_TILES * PIECES_PER_TILE
DISPATCH_ROWS = (DUMP_PIECE0 + N_BLOCKS * PIECES_PER_TILE) * PIECE
DISPATCH_W = D_MODEL + LANES
PREP_GROUP = 8
PREP_ROWS = 1024
SCAN_SEQS = 2
VMEM_LIMIT = 56 * 1024 * 1024


def _dot(a, b):
    return jnp.dot(a, b, preferred_element_type=F32)


def _dot_nt(a, b):
    return lax.dot_general(a, b, (((1,), (1,)), ((), ())), preferred_element_type=F32)


def _dot_tn(a, b):
    return lax.dot_general(a, b, (((0,), (0,)), ((), ())), preferred_element_type=F32)


def _rms(x, g):
    return x * lax.rsqrt(jnp.mean(x * x, axis=-1, keepdims=True) + EPS) * g


def _silu(x):
    return x * jax.nn.sigmoid(x)


def _cparams(sem):
    return pltpu.CompilerParams(dimension_semantics=sem, vmem_limit_bytes=VMEM_LIMIT)


def _mods_kernel(c_ref, w_ref, b_ref, o_ref):
    s = _silu(c_ref[...]).astype(BF16)
    o_ref[...] = _dot(s, w_ref[...].astype(BF16)) + b_ref[...]


def _mods(cond8, w_ada, b_ada):
    n = w_ada.shape[1]
    bn = 512
    return pl.pallas_call(
        _mods_kernel,
        out_shape=jax.ShapeDtypeStruct((8, n), F32),
        grid=(n // bn,),
        in_specs=[pl.BlockSpec((8, D_MODEL), lambda j: (0, 0)),
                  pl.BlockSpec((D_MODEL, bn), lambda j: (0, j)),
                  pl.BlockSpec((1, bn), lambda j: (0, j))],
        out_specs=pl.BlockSpec((8, bn), lambda j: (0, j)),
        compiler_params=_cparams(("parallel",)),
        name="mods",
    )(cond8, w_ada, b_ada)


def _proj_kernel(rope, x_ref, m_ref, gpre_ref, wcat_ref, qg_ref, kvg_ref, wuq_ref, wuqs_ref,
                 wuk_ref, wuv_ref, cos_ref, sin_ref,
                 q_ref, k_ref, v_ref, ckv_ref, kr_ref, qkv_ref, z_ref, ab_ref, ga_ref, gb_ref):
    m = m_ref[0]
    h = (_rms(x_ref[...], gpre_ref[...]) * (1.0 + m[1:2]) + m[0:1]).astype(BF16)

    def seg(name):
        a, b = _SEG[name]
        return _dot(h, wcat_ref[:, a:b])

    qkv_ref[...] = seg("qkv")
    z_ref[...] = seg("z")
    ab_ref[...] = seg("ab")
    ga_ref[...] = seg("ga")
    gb_ref[...] = seg("gb")

    qn = _rms(seg("cq"), qg_ref[...]).astype(BF16)
    ckv = _rms(seg("ckv"), kvg_ref[...])
    ckv_ref[...] = ckv
    ckv_b = ckv.astype(BF16)
    kr = seg("kr")
    kr_ref[...] = kr
    qm = _dot(qn, wuq_ref[...])
    kk = _dot(ckv_b, wuk_ref[...])
    v_ref[...] = _dot(ckv_b, wuv_ref[...]).astype(BF16)
    scale = MLA_QK ** -0.5
    if rope:
        cos = cos_ref[...]
        sin = sin_ref[...]
        qs = _dot(qn, wuqs_ref[...])
        kr = kr * cos + seg("krs") * sin
    for hd in range(MLA_HEADS):
        sl = slice(hd * HEAD_PAD, (hd + 1) * HEAD_PAD)
        qh = qm[:, sl]
        if rope:
            qh = qh * cos + qs[:, sl] * sin
        q_ref[:, sl] = (qh * scale).astype(BF16)
        k_ref[:, sl] = (kk[:, sl] + kr).astype(BF16)


def _proj(x, mods, mod_index, gpre, wts, rope_tabs):
    t = x.shape[0]
    tm = ROW_TILE
    rope = rope_tabs is not None
    if rope:
        cos, sin = rope_tabs
        n_rope_blocks = cos.shape[0] // tm
        rope_spec = pl.BlockSpec((tm, LANES), lambda i: (i % n_rope_blocks, 0))
    else:
        cos = sin = jnp.zeros((8, LANES), F32)
        rope_spec = pl.BlockSpec((8, LANES), lambda i: (0, 0))

    def full(a):
        return pl.BlockSpec(a.shape, lambda i: (0,) * a.ndim)

    def rows(w):
        return pl.BlockSpec((tm, w), lambda i: (i, 0))

    out_widths = (("q", MLA_HEADS * HEAD_PAD, BF16), ("k", MLA_HEADS * HEAD_PAD, BF16),
                  ("v", MLA_HEADS * V_HEAD, BF16), ("ckv", KV_LORA, F32), ("kr", LANES, F32),
                  ("qkv", 3 * GDN_W, F32), ("z", GDN_W, F32), ("ab", LANES, F32),
                  ("ga", D_MODEL, F32), ("gb", D_MODEL, F32))
    outs = pl.pallas_call(
        functools.partial(_proj_kernel, rope),
        out_shape=[jax.ShapeDtypeStruct((t, w), dt) for _, w, dt in out_widths],
        grid=(t // tm,),
        in_specs=[rows(D_MODEL),
                  pl.BlockSpec((1, 6, D_MODEL), lambda i: (mod_index(i), 0, 0)),
                  full(gpre), full(wts["wcat"]), full(wts["qg"]), full(wts["kvg"]),
                  full(wts["wuq"]), full(wts["wuqs"]), full(wts["wuk"]), full(wts["wuv"]),
                  rope_spec, rope_spec],
        out_specs=[rows(w) for _, w, _ in out_widths],
        compiler_params=_cparams(("parallel",)),
        name="proj_rope" if rope else "proj",
    )(x, mods, gpre, wts["wcat"], wts["qg"], wts["kvg"], wts["wuq"], wts["wuqs"],
      wts["wuk"], wts["wuv"], cos, sin)
    return {name: o for (name, _, _), o in zip(out_widths, outs)}


def _cache_kv_kernel(ckv_ref, kr_ref, wuk_ref, wuv_ref, k_ref, v_ref):
    c = ckv_ref[...].astype(BF16)
    kk = _dot(c, wuk_ref[...])
    v_ref[...] = _dot(c, wuv_ref[...]).astype(BF16)
    kr = kr_ref[...]
    for hd in range(MLA_HEADS):
        sl = slice(hd * HEAD_PAD, (hd + 1) * HEAD_PAD)
        k_ref[:, sl] = (kk[:, sl] + kr).astype(BF16)


def _cache_kv(ckv, kr128, wts):
    t = ckv.shape[0]
    tm = 512
    return pl.pallas_call(
        _cache_kv_kernel,
        out_shape=[jax.ShapeDtypeStruct((t, MLA_HEADS * HEAD_PAD), BF16),
                   jax.ShapeDtypeStruct((t, MLA_HEADS * V_HEAD), BF16)],
        grid=(t // tm,),
        in_specs=[pl.BlockSpec((tm, KV_LORA), lambda i: (i, 0)),
                  pl.BlockSpec((tm, LANES), lambda i: (i, 0)),
                  pl.BlockSpec(wts["wuk"].shape, lambda i: (0, 0)),
                  pl.BlockSpec(wts["wuv"].shape, lambda i: (0, 0))],
        out_specs=[pl.BlockSpec((tm, MLA_HEADS * HEAD_PAD), lambda i: (i, 0)),
                   pl.BlockSpec((tm, MLA_HEADS * V_HEAD), lambda i: (i, 0))],
        compiler_params=_cparams(("parallel",)),
        name="cache_kv",
    )(ckv, kr128, wts["wuk"], wts["wuv"])


def _attn_kernel(n_kv, q_ref, *refs):
    k_refs = refs[:n_kv]
    v_refs = refs[n_kv:2 * n_kv]
    o_ref = refs[2 * n_kv]
    lane = lax.broadcasted_iota(jnp.int32, (1, LANES), 1)
    low = lane < V_HEAD
    for pr in range(MLA_HEADS // 2):
        halves = []
        for hd in (2 * pr, 2 * pr + 1):
            sl = slice(hd * HEAD_PAD, (hd + 1) * HEAD_PAD)
            qh = q_ref[:, sl]
            scores = [_dot_nt(qh, kr[:, sl]) for kr in k_refs]
            mx = functools.reduce(jnp.maximum, [jnp.max(s, axis=-1, keepdims=True) for s in scores])
            ps = [jnp.exp(s - mx) for s in scores]
            den = functools.reduce(jnp.add, [jnp.sum(p, axis=-1, keepdims=True) for p in ps])
            vsl = slice(pr * LANES, (pr + 1) * LANES)
            acc = functools.reduce(jnp.add, [_dot(p.astype(BF16), vr[:, vsl]) for p, vr in zip(ps, v_refs)])
            halves.append(acc / den)
        o_ref[:, pr * LANES:(pr + 1) * LANES] = jnp.where(low, halves[0], halves[1]).astype(BF16)


def _attention(q, kvs, n_seq, seq_len, name):
    tq = ROW_TILE
    nq = seq_len // tq
    n_kv = len(kvs)
    in_specs = [pl.BlockSpec((tq, MLA_HEADS * HEAD_PAD), lambda b, j: (b * nq + j, 0))]
    in_specs += [pl.BlockSpec((rows, MLA_HEADS * HEAD_PAD), lambda b, j: (b, 0)) for _, _, rows in kvs]
    in_specs += [pl.BlockSpec((rows, MLA_HEADS * V_HEAD), lambda b, j: (b, 0)) for _, _, rows in kvs]
    return pl.pallas_call(
        functools.partial(_attn_kernel, n_kv),
        out_shape=jax.ShapeDtypeStruct((n_seq * seq_len, MLA_HEADS * V_HEAD), BF16),
        grid=(n_seq, nq),
        in_specs=in_specs,
        out_specs=pl.BlockSpec((tq, MLA_HEADS * V_HEAD), lambda b, j: (b * nq + j, 0)),
        compiler_params=_cparams(("parallel", "parallel")),
        name=name,
    )(q, *[k for k, _, _ in kvs], *[v for _, v, _ in kvs])


def _pair_masks():
    lane = lax.broadcasted_iota(jnp.int32, (1, LANES), 1)
    return lane < GDN_DK


def _stack(x, low):
    zero = jnp.zeros_like(x)
    return jnp.concatenate([jnp.where(low, x, zero), jnp.where(low, zero, x)], axis=0)


def _split3(x):
    hi = x.astype(BF16)
    r = x - hi.astype(F32)
    mid = r.astype(BF16)
    lo = (r - mid.astype(F32)).astype(BF16)
    return hi, mid, lo


def _gdn_prep_kernel(seq_len, q_ref, k_ref, v_ref, cwq_ref, cwk_ref, cwv_ref, ab_ref, alog_ref, dtb_ref, e_ref,
                     uf_ref, ub_ref, wf_ref, wb_ref, af_ref, abk_ref, qdf_ref, qdb_ref, kdf_ref, kdb_ref,
                     glf_ref, glb_ref,
                     qn_s, kn_s, vn_s, gcb_s, gf_s):
    seq = q_ref.shape[0]
    n_chunks = seq // CHUNK
    low = _pair_masks()
    row = lax.broadcasted_iota(jnp.int32, (seq, 1), 0) % seq_len
    lane = lax.broadcasted_iota(jnp.int32, (1, LANES), 1)

    def conv(x_ref, cw_ref):
        x = x_ref[...]
        acc = jnp.zeros_like(x)
        for j in range(CONV_K):
            sh = CONV_K // 2 - j
            xs = x if sh == 0 else pltpu.roll(x, sh % seq, axis=0)
            src = row - sh
            valid = (src >= 0) & (src < seq_len)
            acc = acc + jnp.where(valid, xs, 0.0) * cw_ref[j:j + 1, :]
        return _silu(acc)

    def l2n(x):
        sq = x * x
        s0 = jnp.sum(jnp.where(low, sq, 0.0), axis=-1, keepdims=True)
        s1 = jnp.sum(jnp.where(low, 0.0, sq), axis=-1, keepdims=True)
        return x * lax.rsqrt(jnp.where(low, s0, s1) + EPS)

    qn_s[...] = l2n(conv(q_ref, cwq_ref)) * (GDN_DK ** -0.5)
    kn_s[...] = l2n(conv(k_ref, cwk_ref))
    vn_s[...] = conv(v_ref, cwv_ref)

    a = ab_ref[...]
    xg = a + dtb_ref[...]
    softplus = jnp.maximum(xg, 0.0) + jnp.log(1.0 + jnp.exp(-jnp.abs(xg)))
    act = jnp.where(lane < 2 * GDN_HEADS, -jnp.exp(alog_ref[...]) * softplus, jax.nn.sigmoid(a))

    ti = lax.broadcasted_iota(jnp.int32, (CHUNK, CHUNK), 0)
    tj = lax.broadcasted_iota(jnp.int32, (CHUNK, CHUNK), 1)
    tri_lo = (tj <= ti).astype(BF16)
    tri_up = (tj >= ti).astype(BF16)
    for c in range(n_chunks):
        ac = act[c * CHUNK:(c + 1) * CHUNK]
        pieces = _split3(ac)
        lo = functools.reduce(jnp.add, [_dot(tri_lo, pc) for pc in pieces])
        up = functools.reduce(jnp.add, [_dot(tri_up, pc) for pc in pieces])
        gcb_s[c * CHUNK:(c + 1) * CHUNK, :] = jnp.where(lane < GDN_HEADS, lo,
                                                        jnp.where(lane < 2 * GDN_HEADS, up, ac))
    expand = e_ref[0].astype(BF16)
    gf_s[...] = functools.reduce(jnp.add, [_dot(pc, expand) for pc in _split3(gcb_s[...])])

    ri = lax.broadcasted_iota(jnp.int32, (CHUNK, LANES), 0)
    cj = lax.broadcasted_iota(jnp.int32, (CHUNK, LANES), 1) % CHUNK
    eye = (ri == cj).astype(F32)
    ones_b = jnp.ones((CHUNK, LANES), BF16)
    lane0 = (lax.broadcasted_iota(jnp.int32, (1, LANES), 1) % CHUNK) == 0

    def pmm(x, y):
        return _dot(x.astype(BF16), _stack(y, low).astype(BF16))

    def pmm_split(x, y):
        xh, xm, _ = _split3(x)
        yh, ym, _ = _split3(_stack(y, low))
        return _dot(xh, yh) + (_dot(xh, ym) + _dot(xm, yh))

    out_refs = ((uf_ref, wf_ref, af_ref, qdf_ref, kdf_ref, glf_ref),
                (ub_ref, wb_ref, abk_ref, qdb_ref, kdb_ref, glb_ref))
    incl = (ri >= cj, ri <= cj)
    strict = (ri > cj, ri < cj)
    diag8 = (ri // 8) == (cj // 8)
    merge_masks = [((ri // (2 * s)) == (cj // (2 * s))) & ((ri // s) != (cj // s)) for s in (8, 16, 32)]

    def group(it, carry):
        cs = [it * PREP_GROUP + cc for cc in range(PREP_GROUP)]
        rows = [pl.ds(pl.multiple_of(c * CHUNK, CHUNK), CHUNK) for c in cs]
        qc = [qn_s[r, :] for r in rows]
        kc = [kn_s[r, :] for r in rows]
        vc = [vn_s[r, :] for r in rows]
        kst = [_stack(k, low).astype(BF16) for k in kc]
        kk = [_dot_nt(k.astype(BF16), ks) for k, ks in zip(kc, kst)]
        qk = [_dot_nt(q.astype(BF16), ks) for q, ks in zip(qc, kst)]
        chains = [(ci, d) for ci in range(PREP_GROUP) for d in range(2)]
        gc = [gf_s[rows[ci], d * LANES:(d + 1) * LANES] for ci, d in chains]
        beta = [gf_s[rows[ci], (2 + d) * LANES:(3 + d) * LANES] for ci, d in chains]
        pieces = [_split3(_stack(jnp.where(lane0, g, 0.0), low)) for g in gc]
        gr = [functools.reduce(jnp.add, [_dot_nt(ones_b, pc) for pc in ps]) for ps in pieces]
        dm = [jnp.exp(jnp.where(incl[d], g - r, -jnp.inf)) for (ci, d), g, r in zip(chains, gc, gr)]
        lm = [jnp.where(strict[d], b * kk[ci] * m, 0.0) for (ci, d), b, m in zip(chains, beta, dm)]
        aint = [(qk[ci] * m).astype(BF16) for (ci, d), m in zip(chains, dm)]
        x = [-jnp.where(diag8, l, 0.0) for l in lm]
        t = [eye + xx for xx in x]
        for _ in range(2):
            x = [pmm(xx, xx) for xx in x]
            t = [tt + pmm(tt, xx) for tt, xx in zip(t, x)]
        for off in merge_masks:
            tc = [pmm(tt, jnp.where(off, l, 0.0)) for tt, l in zip(t, lm)]
            t = [tt - pmm(a, tt) for tt, a in zip(t, tc)]
        resid = [eye - pmm_split(eye + l, tt) for l, tt in zip(lm, t)]
        t = [tt + pmm(tt, rr) for tt, rr in zip(t, resid)]
        egc = [jnp.exp(g) for g in gc]
        u = [pmm(tt, vc[ci] * b) for (ci, d), tt, b in zip(chains, t, beta)]
        w = [pmm(tt, kc[ci] * b * e).astype(BF16) for (ci, d), tt, b, e in zip(chains, t, beta, egc)]
        qd = [(qc[ci] * e).astype(BF16) for (ci, d), e in zip(chains, egc)]
        gtot = [g[CHUNK - 1:CHUNK, :] if d == 0 else g[0:1, :] for (ci, d), g in zip(chains, gc)]
        kd = [(kc[ci] * jnp.exp(gt - g)).astype(BF16) for (ci, d), gt, g in zip(chains, gtot, gc)]
        for n, (ci, d) in enumerate(chains):
            u_ref, w_ref, a_ref, qd_ref, kd_ref, gl_ref = out_refs[d]
            u_ref[0, rows[ci], :] = u[n]
            w_ref[0, rows[ci], :] = w[n]
            a_ref[0, rows[ci], :] = aint[n]
            qd_ref[0, rows[ci], :] = qd[n]
            kd_ref[0, rows[ci], :] = kd[n]
            gl_ref[0, pl.ds(cs[ci], 1), :, :] = jnp.broadcast_to(jnp.exp(gtot[n]), (1, 8, LANES))
        return carry

    lax.fori_loop(0, n_chunks // PREP_GROUP, group, 0)


def _gdn_prep(qkv, ab, conv_w, alog128, dtb128, expand, n_seq, seq_len):
    rb = max(seq_len, PREP_ROWS)
    nb = n_seq * seq_len // rb
    n_chunks = rb // CHUNK
    col = lambda off: pl.BlockSpec((rb, LANES), lambda s, p: (s, off + p))
    cw = lambda off: pl.BlockSpec((CONV_K, LANES), lambda s, p: (0, off + p))
    vec = pl.BlockSpec((1, LANES), lambda s, p: (0, 0))
    big = lambda: pl.BlockSpec((1, rb, LANES), lambda s, p: (s, 0, p))
    glspec = lambda: pl.BlockSpec((1, n_chunks, 8, LANES), lambda s, p: (s, 0, 0, p))
    shp = lambda dt: jax.ShapeDtypeStruct((nb, rb, GDN_W), dt)
    glshp = jax.ShapeDtypeStruct((nb, n_chunks, 8, GDN_W), F32)
    outs = pl.pallas_call(
        functools.partial(_gdn_prep_kernel, seq_len),
        out_shape=[shp(F32), shp(F32)] + [shp(BF16)] * 8 + [glshp, glshp],
        grid=(nb, N_PAIRS),
        in_specs=[col(0), col(N_PAIRS), col(2 * N_PAIRS), cw(0), cw(N_PAIRS), cw(2 * N_PAIRS),
                  pl.BlockSpec((rb, LANES), lambda s, p: (s, 0)), vec, vec,
                  pl.BlockSpec((1, LANES, 4 * LANES), lambda s, p: (p, 0, 0))],
        out_specs=[big() for _ in range(10)] + [glspec(), glspec()],
        scratch_shapes=[pltpu.VMEM((rb, LANES), F32)] * 4 + [pltpu.VMEM((rb, 4 * LANES), F32)],
        compiler_params=_cparams(("parallel", "parallel")),
        name="gdn_prep_%d" % seq_len,
    )(qkv, qkv, qkv, conv_w, conv_w, conv_w, ab, alog128, dtb128, expand)
    per_seq = [o.reshape(n_seq, seq_len, GDN_W) for o in outs[:10]]
    return per_seq + [o.reshape(n_seq, seq_len // CHUNK, 8, GDN_W) for o in outs[10:]]


def _gdn_scan_kernel(uf_ref, ub_ref, wf_ref, wb_ref, af_ref, abk_ref, qdf_ref, qdb_ref, kdf_ref, kdb_ref,
                     glf_ref, glb_ref, s0_ref, of_ref, ob_ref, sfin_ref, state):
    step = pl.program_id(1)
    n_steps = pl.num_programs(1)
    low = _pair_masks()
    chains = [(d, j, p) for d in range(2) for j in range(SCAN_SEQS) for p in range(N_PAIRS)]

    first = step == 0
    per_dir = ((uf_ref, wf_ref, af_ref, qdf_ref, kdf_ref, glf_ref, of_ref),
               (ub_ref, wb_ref, abk_ref, qdb_ref, kdb_ref, glb_ref, ob_ref))
    def rd(k, d, j, p):
        return per_dir[d][k][j, :, p * LANES:(p + 1) * LANES]

    s = [jnp.where(first, _stack(s0_ref[j, d, p], low), state[idx]) for idx, (d, j, p) in enumerate(chains)]
    sb = [x.astype(BF16) for x in s]
    ws = [_dot(rd(1, *c), b) for c, b in zip(chains, sb)]
    qs = [_dot(rd(3, *c), b) for c, b in zip(chains, sb)]
    vst = [_stack(rd(0, *c) - w, low).astype(BF16) for c, w in zip(chains, ws)]
    upd = [_dot_tn(_stack(rd(4, *c), low), v) for c, v in zip(chains, vst)]
    intra = [_dot(rd(2, *c), v) for c, v in zip(chains, vst)]
    for idx, (d, j, p) in enumerate(chains):
        sl = slice(p * LANES, (p + 1) * LANES)
        state[idx] = s[idx] * per_dir[d][5][j, 0, 0:1, sl] + upd[idx]
        per_dir[d][6][j, :, sl] = qs[idx] + intra[idx]

    @pl.when(step == n_steps - 1)
    def _():
        for idx, (d, j, p) in enumerate(chains):
            s = state[idx]
            sfin_ref[j, d, p] = s[:GDN_DK] + s[GDN_DK:]


def _gdn_scan(prep, s0, n_seq, seq_len):
    n_chunks = seq_len // CHUNK
    ns = SCAN_SEQS
    fwd = lambda: pl.BlockSpec((ns, CHUNK, GDN_W), lambda g, i: (g, i, 0))
    bwd = lambda: pl.BlockSpec((ns, CHUNK, GDN_W), lambda g, i: (g, n_chunks - 1 - i, 0))
    glf = pl.BlockSpec((ns, 1, 8, GDN_W), lambda g, i: (g, i, 0, 0))
    glb = pl.BlockSpec((ns, 1, 8, GDN_W), lambda g, i: (g, n_chunks - 1 - i, 0, 0))
    st = lambda: pl.BlockSpec((ns, 2, N_PAIRS, GDN_DK, LANES), lambda g, i: (g, 0, 0, 0, 0))
    oshape = jax.ShapeDtypeStruct((n_seq, seq_len, GDN_W), F32)
    return pl.pallas_call(
        _gdn_scan_kernel,
        out_shape=[oshape, oshape, jax.ShapeDtypeStruct((n_seq, 2, N_PAIRS, GDN_DK, LANES), F32)],
        grid=(n_seq // ns, n_chunks),
        in_specs=[fwd(), bwd()] * 5 + [glf, glb, st()],
        out_specs=[fwd(), bwd(), st()],
        scratch_shapes=[pltpu.VMEM((2 * ns * N_PAIRS, LANES, LANES), F32)],
        compiler_params=_cparams(("parallel", "arbitrary")),
        name="gdn_scan_%d" % seq_len,
    )(*prep, s0)


def _route(sel, s):
    per_group = N_EXPERTS // N_GROUPS
    ninf = -jnp.inf
    sub = lax.broadcasted_iota(jnp.int32, sel.shape, 1).astype(F32)
    gid = lax.broadcasted_iota(jnp.int32, (N_GROUPS, 1, sel.shape[2]), 0).astype(F32)
    m1 = jnp.max(sel, axis=1, keepdims=True)
    i1 = jnp.min(jnp.where(sel == m1, sub, float(per_group)), axis=1, keepdims=True)
    m2 = jnp.max(jnp.where(sub == i1, ninf, sel), axis=1, keepdims=True)
    work = m1 + m2
    gmask = jnp.zeros(work.shape, jnp.bool_)
    for _ in range(TOPK_GROUPS):
        m = jnp.max(work, axis=0, keepdims=True)
        idx = jnp.min(jnp.where(work == m, gid, float(N_GROUPS)), axis=0, keepdims=True)
        pick = gid == idx
        gmask = gmask | pick
        work = jnp.where(pick, ninf, work)
    work = jnp.where(gmask, sel, ninf)
    eid = gid * per_group + sub
    chosen = jnp.zeros(sel.shape, jnp.bool_)
    for _ in range(TOP_K):
        m = jnp.max(jnp.max(work, axis=1, keepdims=True), axis=0, keepdims=True)
        idx = jnp.min(jnp.min(jnp.where(work == m, eid, float(N_EXPERTS)), axis=1, keepdims=True),
                      axis=0, keepdims=True)
        pick = eid == idx
        chosen = chosen | pick
        work = jnp.where(pick, ninf, work)
    wk = jnp.where(chosen, s, 0.0)
    den = jnp.sum(jnp.sum(wk, axis=1, keepdims=True), axis=0, keepdims=True)
    return wk / den * ROUTED_SCALE


def _merge_kernel(x_ref, m_ref, omla_ref, of_ref, ob_ref, z_ref, ga_ref, gb_ref,
                  woa_ref, wob_ref, wo_ref, gpost_ref, gpre_ref, gdng_ref, wr_ref, eb_ref,
                  x1_ref, h2_ref, gates_ref, cnt_ref):
    m = m_ref[0]
    low = _pair_masks()
    o = of_ref[...] + ob_ref[...]
    z = z_ref[...]
    parts = []
    for p in range(N_PAIRS):
        sl = slice(p * LANES, (p + 1) * LANES)
        op = o[:, sl]
        sq = op * op
        s0 = jnp.sum(jnp.where(low, sq, 0.0), axis=-1, keepdims=True)
        s1 = jnp.sum(jnp.where(low, 0.0, sq), axis=-1, keepdims=True)
        ms = jnp.where(low, s0, s1) * (1.0 / GDN_DV)
        parts.append(op * lax.rsqrt(ms + EPS) * gdng_ref[...] * _silu(z[:, sl]))
    og = jnp.concatenate(parts, axis=1).astype(BF16)
    ya = _dot(omla_ref[...], woa_ref[...])
    yb = _dot(og, wob_ref[...])
    mix = (jax.nn.sigmoid(ga_ref[...]) * ya + jax.nn.sigmoid(gb_ref[...]) * yb).astype(BF16)
    y = _dot(mix, wo_ref[...])
    x1 = x_ref[...] + m[2:3] * _rms(y, gpost_ref[...])
    x1_ref[...] = x1
    h2 = _rms(x1, gpre_ref[...]) * (1.0 + m[4:5]) + m[3:4]
    h2_ref[...] = h2.astype(BF16)
    logits = lax.dot_general(wr_ref[...], h2, (((1,), (1,)), ((), ())),
                             preferred_element_type=F32, precision=lax.Precision.HIGHEST)
    s = jax.nn.sigmoid(logits)
    sel = s + eb_ref[...]
    tm = s.shape[1]
    shape3 = (N_GROUPS, N_EXPERTS // N_GROUPS, tm)
    gates_t = _route(sel.reshape(shape3), s.reshape(shape3)).reshape(N_EXPERTS, tm)
    gates_ref[...] = gates_t
    cnt_ref[0] = jnp.sum((gates_t > 0.0).astype(F32), axis=1, keepdims=True)


def _merge(x, mods, mod_index, omla, o_f, o_b, pr, wts):
    t = x.shape[0]
    tm = ROW_TILE

    def full(a):
        return pl.BlockSpec(a.shape, lambda i: (0,) * a.ndim)

    def rows(w):
        return pl.BlockSpec((tm, w), lambda i: (i, 0))

    names = ("woa", "wob", "wo", "gpost", "gpre2", "gdng", "wr_t", "eb")
    return pl.pallas_call(
        _merge_kernel,
        out_shape=[jax.ShapeDtypeStruct((t, D_MODEL), F32), jax.ShapeDtypeStruct((t, D_MODEL), BF16),
                   jax.ShapeDtypeStruct((N_EXPERTS, t), F32),
                   jax.ShapeDtypeStruct((t // tm, N_EXPERTS, 1), F32)],
        grid=(t // tm,),
        in_specs=[rows(D_MODEL), pl.BlockSpec((1, 6, D_MODEL), lambda i: (mod_index(i), 0, 0)),
                  rows(MLA_HEADS * V_HEAD), rows(GDN_W), rows(GDN_W), rows(GDN_W),
                  rows(D_MODEL), rows(D_MODEL)] + [full(wts[n]) for n in names],
        out_specs=[rows(D_MODEL), rows(D_MODEL), pl.BlockSpec((N_EXPERTS, tm), lambda i: (0, i)),
                   pl.BlockSpec((1, N_EXPERTS, 1), lambda i: (i, 0, 0))],
        compiler_params=_cparams(("parallel",)),
        name="merge",
    )(x, mods, omla, o_f, o_b, pr["z"], pr["ga"], pr["gb"], *[wts[n] for n in names])


def _dispatch_tables(cnt):
    nb = cnt.shape[0]
    ppt = PIECES_PER_TILE
    pc = (cnt + PIECE - 1) // PIECE
    seg_end = jnp.cumsum(pc, axis=1)
    seg = seg_end - pc
    nvalid = seg_end[:, -1]
    ngroups = (nvalid + ppt - 1) // ppt
    tp = jnp.sum(pc, axis=0)
    rp = ((tp + ppt - 1) // ppt) * ppt
    gs_end = jnp.cumsum(rp)
    gs = gs_end - rp
    blk_off = jnp.cumsum(pc, axis=0) - pc
    c = jnp.arange(STAGE_PIECES, dtype=jnp.int32)
    ce = jnp.sum((seg_end[:, None, :] <= c[None, :, None]).astype(jnp.int32), axis=-1)
    ce = jnp.minimum(ce, N_EXPERTS - 1)
    valid = c[None, :] < nvalid[:, None]
    take = lambda a: jnp.take_along_axis(a, ce, axis=1)
    within = c[None, :] - take(seg)
    cb = jnp.where(valid, within * PIECE, -(1 << 20))
    dump = DUMP_PIECE0 + jnp.arange(nb, dtype=jnp.int32)[:, None] * ppt + c[None, :] % ppt
    dst = jnp.where(valid, gs[ce] + take(blk_off) + within, dump)
    src = jnp.where(valid, dst, dst[:, :1])
    j = jnp.arange(N_EXP_TILES, dtype=jnp.int32)
    te = jnp.minimum(jnp.sum((gs_end[None, :] <= (j * ppt)[:, None]).astype(jnp.int32), axis=-1),
                     N_EXPERTS - 1)
    n_used = gs_end[-1] // ppt
    used = j < n_used
    tv = jnp.where(used, jnp.clip((tp[te] - (j * ppt - gs[te])) * PIECE, 0, EXP_TILE), 0)
    tin = jnp.where(used, j, n_used - 1)
    tout = jnp.where(used, j, N_EXP_TILES)
    i32 = lambda a: a.astype(jnp.int32)
    return {"ce": i32(ce), "cb": i32(cb), "dst": i32(dst), "src": i32(src), "ngroups": i32(ngroups),
            "te": i32(te), "tv": i32(tv), "tin": i32(tin), "tout": i32(tout)}


def _piece_onehot(ce_ref, cb_ref, rank_s, b, g, extra=None):
    sub = lax.broadcasted_iota(jnp.int32, (PIECE, 1), 0).astype(F32)
    ps, ex = [], []
    for cc in range(PIECES_PER_TILE):
        c = g * PIECES_PER_TILE + cc
        e = ce_ref[b, c]
        base = cb_ref[b, c].astype(F32)
        hit = rank_s[pl.ds(e, 1), :] == base + sub
        ps.append(jnp.where(hit, 1.0, 0.0).astype(BF16))
        if extra is not None:
            ex.append(jnp.sum(jnp.where(hit, extra[pl.ds(e, 1), :], 0.0), axis=-1, keepdims=True))
    p = jnp.concatenate(ps, axis=0)
    return (p, jnp.concatenate(ex, axis=0)) if extra is not None else p


def _block_ranks(gt):
    n = gt.shape[1]
    ti = lax.broadcasted_iota(jnp.int32, (n, n), 0)
    tj = lax.broadcasted_iota(jnp.int32, (n, n), 1)
    before = (ti < tj).astype(BF16)
    member = gt > 0.0
    rank = _dot(member.astype(BF16), before)
    return jnp.where(member, rank, -1.0)


def _moe_sort_kernel(ce_ref, cb_ref, dst_ref, ng_ref, h_ref, gt_ref, xg_ref, stage, rank_s, gate_s, sem):
    b = pl.program_id(0)
    n_groups = ng_ref[b]
    gt = gt_ref[...]
    rank_s[...] = _block_ranks(gt)
    gate_s[...] = gt
    lane = lax.broadcasted_iota(jnp.int32, (1, LANES), 1)

    def piece_copy(c):
        r0 = pl.multiple_of(c * PIECE, PIECE)
        d0 = pl.multiple_of(dst_ref[b, c] * PIECE, PIECE)
        return pltpu.make_async_copy(stage.at[pl.ds(r0, PIECE)], xg_ref.at[pl.ds(d0, PIECE)], sem)

    def group(g, carry):
        p, gcol = _piece_onehot(ce_ref, cb_ref, rank_s, b, g, gate_s)
        xs = _dot(p, h_ref[...]).astype(BF16)
        hi, mid, lo = (t.astype(F32) for t in _split3(gcol))
        gblk = jnp.where(lane == 0, hi, jnp.where(lane == 1, mid, jnp.where(lane == 2, lo, 0.0)))
        r0 = pl.multiple_of(g * EXP_TILE, EXP_TILE)
        stage[pl.ds(r0, EXP_TILE), :] = jnp.concatenate([xs, gblk.astype(BF16)], axis=1)
        for cc in range(PIECES_PER_TILE):
            piece_copy(g * PIECES_PER_TILE + cc).start()
        return carry

    lax.fori_loop(0, n_groups, group, 0)

    def wait_one(c, carry):
        piece_copy(c).wait()
        return carry

    lax.fori_loop(0, n_groups * PIECES_PER_TILE, wait_one, 0)


def _moe_sort(h2, gates_t, tabs):
    nb = h2.shape[0] // MOE_BLOCK
    grid_spec = pltpu.PrefetchScalarGridSpec(
        num_scalar_prefetch=4, grid=(nb,),
        in_specs=[pl.BlockSpec((MOE_BLOCK, D_MODEL), lambda b, *_: (b, 0)),
                  pl.BlockSpec((N_EXPERTS, MOE_BLOCK), lambda b, *_: (0, b))],
        out_specs=pl.BlockSpec(memory_space=pl.ANY),
        scratch_shapes=[pltpu.VMEM((STAGE_PIECES * PIECE, DISPATCH_W), BF16),
                        pltpu.VMEM((N_EXPERTS, MOE_BLOCK), F32), pltpu.VMEM((N_EXPERTS, MOE_BLOCK), F32),
                        pltpu.SemaphoreType.DMA])
    return pl.pallas_call(
        _moe_sort_kernel,
        out_shape=jax.ShapeDtypeStruct((DISPATCH_ROWS, DISPATCH_W), BF16),
        grid_spec=grid_spec,
        compiler_params=_cparams(("arbitrary",)),
        name="moe_sort",
    )(tabs["ce"], tabs["cb"], tabs["dst"], tabs["ngroups"], h2, gates_t)


def _moe_expert_kernel(te_ref, tv_ref, tin_ref, tout_ref, x_ref, wg_ref, wu_ref, wd_ref, y_ref):
    valid = tv_ref[pl.program_id(0)]

    @pl.when(valid > 0)
    def _():
        keep = lax.broadcasted_iota(jnp.int32, (EXP_TILE, 1), 0) < valid
        xrow = x_ref[...]
        x = jnp.where(keep, xrow[:, :D_MODEL], jnp.zeros((), BF16))
        g = jnp.sum(jnp.where(keep, xrow[:, D_MODEL:].astype(F32), 0.0), axis=-1, keepdims=True)
        hg = _dot(x, wg_ref[0].astype(BF16))
        hu = _dot(x, wu_ref[0].astype(BF16))
        act = (_silu(hg) * hu * g).astype(BF16)
        y_ref[...] = _dot(act, wd_ref[0].astype(BF16)).astype(BF16)


def _moe_expert(xg, tabs, wts):
    grid_spec = pltpu.PrefetchScalarGridSpec(
        num_scalar_prefetch=4, grid=(N_EXP_TILES,),
        in_specs=[pl.BlockSpec((EXP_TILE, DISPATCH_W), lambda j, te, tv, tin, tout: (tin[j], 0)),
                  pl.BlockSpec((1, D_MODEL, D_EXPERT), lambda j, te, tv, tin, tout: (te[j], 0, 0)),
                  pl.BlockSpec((1, D_MODEL, D_EXPERT), lambda j, te, tv, tin, tout: (te[j], 0, 0)),
                  pl.BlockSpec((1, D_EXPERT, D_MODEL), lambda j, te, tv, tin, tout: (te[j], 0, 0))],
        out_specs=pl.BlockSpec((EXP_TILE, D_MODEL), lambda j, te, tv, tin, tout: (tout[j], 0)))
    return pl.pallas_call(
        _moe_expert_kernel,
        out_shape=jax.ShapeDtypeStruct((DISPATCH_ROWS, D_MODEL), BF16),
        grid_spec=grid_spec,
        compiler_params=_cparams(("arbitrary",)),
        name="moe_expert",
    )(tabs["te"], tabs["tv"], tabs["tin"], tabs["tout"], xg, wts["w_gate"], wts["w_up"], wts["w_down"])


def _moe_combine_kernel(ce_ref, cb_ref, src_ref, ng_ref, yg_ref, gt_ref, h_ref, x1_ref, m_ref, gpost_ref,
                        wsg_ref, wsu_ref, wsd_ref, out_ref, stage, rank_s, acc_s, sem):
    b = pl.program_id(0)
    n_groups = ng_ref[b]

    def piece_copy(c):
        r0 = pl.multiple_of(c * PIECE, PIECE)
        s0 = pl.multiple_of(src_ref[b, c] * PIECE, PIECE)
        return pltpu.make_async_copy(yg_ref.at[pl.ds(s0, PIECE)], stage.at[pl.ds(r0, PIECE)], sem)

    def start_one(c, carry):
        piece_copy(c).start()
        return carry

    lax.fori_loop(0, n_groups * PIECES_PER_TILE, start_one, 0)
    rank_s[...] = _block_ranks(gt_ref[...])
    h = h_ref[...]
    sh = (_silu(_dot(h, wsg_ref[...])) * _dot(h, wsu_ref[...])).astype(BF16)
    acc_s[...] = _dot(sh, wsd_ref[...])

    def wait_one(c, carry):
        piece_copy(c).wait()
        return carry

    lax.fori_loop(0, n_groups * PIECES_PER_TILE, wait_one, 0)

    def group(g, carry):
        p = _piece_onehot(ce_ref, cb_ref, rank_s, b, g)
        r0 = pl.multiple_of(g * EXP_TILE, EXP_TILE)
        acc_s[...] += _dot_tn(p, stage[pl.ds(r0, EXP_TILE), :])
        return carry

    lax.fori_loop(0, n_groups, group, 0)
    m = m_ref[0]
    out_ref[...] = x1_ref[...] + m[5:6] * _rms(acc_s[...], gpost_ref[...])


def _moe_combine(yg, gates_t, h2, x1, mods, mod_index, tabs, wts):
    t = h2.shape[0]
    nb = t // MOE_BLOCK

    def full(a):
        return pl.BlockSpec(a.shape, lambda b, *_: (0,) * a.ndim)

    grid_spec = pltpu.PrefetchScalarGridSpec(
        num_scalar_prefetch=4, grid=(nb,),
        in_specs=[pl.BlockSpec(memory_space=pl.ANY),
                  pl.BlockSpec((N_EXPERTS, MOE_BLOCK), lambda b, *_: (0, b)),
                  pl.BlockSpec((MOE_BLOCK, D_MODEL), lambda b, *_: (b, 0)),
                  pl.BlockSpec((MOE_BLOCK, D_MODEL), lambda b, *_: (b, 0)),
                  pl.BlockSpec((1, 6, D_MODEL), lambda b, *_: (mod_index(b), 0, 0)),
                  full(wts["gpost2"]), full(wts["wsg"]), full(wts["wsu"]), full(wts["wsd"])],
        out_specs=pl.BlockSpec((MOE_BLOCK, D_MODEL), lambda b, *_: (b, 0)),
        scratch_shapes=[pltpu.VMEM((STAGE_PIECES * PIECE, D_MODEL), BF16),
                        pltpu.VMEM((N_EXPERTS, MOE_BLOCK), F32), pltpu.VMEM((MOE_BLOCK, D_MODEL), F32),
                        pltpu.SemaphoreType.DMA])
    return pl.pallas_call(
        _moe_combine_kernel,
        out_shape=jax.ShapeDtypeStruct((t, D_MODEL), F32),
        grid_spec=grid_spec,
        compiler_params=_cparams(("arbitrary",)),
        name="moe_combine",
    )(tabs["ce"], tabs["cb"], tabs["src"], tabs["ngroups"], yg, gates_t, h2, x1, mods,
      wts["gpost2"], wts["wsg"], wts["wsu"], wts["wsd"])


def _moe(h2, gates_t, cnt, x1, mods, mod_index, wts):
    tabs = _dispatch_tables(cnt[:, :, 0].astype(jnp.int32))
    xg = _moe_sort(h2, gates_t, tabs)
    yg = _moe_expert(xg, tabs, wts)
    return _moe_combine(yg, gates_t, h2, x1, mods, mod_index, tabs, wts)


def _rope_swap(w):
    nf = QK_ROPE // 4
    parts = [w[..., i * nf:(i + 1) * nf] for i in range(4)]
    return jnp.concatenate([parts[1], parts[0], parts[3], parts[2]], axis=-1)


def _head_block(nope, rope):
    lead = nope.shape[:-2] if nope is not None else rope.shape[:-2]
    nope = jnp.zeros(lead + (MLA_HEADS, QK_NOPE), F32) if nope is None else nope
    rope = jnp.zeros(lead + (MLA_HEADS, QK_ROPE), F32) if rope is None else rope
    pad = jnp.zeros(lead + (MLA_HEADS, HEAD_PAD - MLA_QK), F32)
    return jnp.concatenate([nope, rope, pad], axis=-1).reshape(lead + (MLA_HEADS * HEAD_PAD,))


def _rope_block(w):
    lead = w.shape[:-1]
    return jnp.concatenate([jnp.zeros(lead + (QK_NOPE,), F32), w,
                            jnp.zeros(lead + (HEAD_PAD - MLA_QK,), F32)], axis=-1)


def _rope_tables(n_tokens):
    rows = n_tokens // GRID_W
    row = np.repeat(np.arange(rows, dtype=np.float32), GRID_W)
    colv = np.tile(np.arange(GRID_W, dtype=np.float32), rows)
    nf = QK_ROPE // 4
    inv = jnp.asarray(ROPE_THETA, F32) ** (-jnp.arange(nf, dtype=F32) / nf)
    ang_r = jnp.asarray(row)[:, None] * inv
    ang_c = jnp.asarray(colv)[:, None] * inv
    cos32 = jnp.concatenate([jnp.cos(ang_r), jnp.cos(ang_r), jnp.cos(ang_c), jnp.cos(ang_c)], axis=-1)
    sin32 = jnp.concatenate([-jnp.sin(ang_r), jnp.sin(ang_r), -jnp.sin(ang_c), jnp.sin(ang_c)], axis=-1)
    ones = jnp.ones((n_tokens, QK_NOPE), F32)
    tail = jnp.zeros((n_tokens, HEAD_PAD - MLA_QK), F32)
    cos = jnp.concatenate([ones, cos32, tail], axis=-1)
    sin = jnp.concatenate([jnp.zeros((n_tokens, QK_NOPE), F32), sin32, tail], axis=-1)
    return cos, sin


def _expand_matrix():
    e = np.zeros((N_PAIRS, LANES, 4 * LANES), np.float32)
    for p in range(N_PAIRS):
        for blk in range(4):
            for hh in range(2):
                src = blk * GDN_HEADS + 2 * p + hh
                e[p, src, blk * LANES + hh * GDN_DK: blk * LANES + (hh + 1) * GDN_DK] = 1.0
    return jnp.asarray(e)


def _prepare_weights(w_in, q_norm_g, kv_norm_g, w_uq, w_ukv, w_oa, w_ob, w_o, g_post_mix, g_pre_ffn,
                     g_post_ffn, gdn_norm_g, w_router, e_bias, w_gate, w_up, w_down, ws_gate, ws_up, ws_down):
    offs = np.cumsum((Q_LORA, KV_LORA, QK_ROPE, 3 * GDN_W, GDN_W, 2 * GDN_HEADS, 2 * GDN_HEADS,
                      D_MODEL, D_MODEL))[:-1].tolist()
    cq, ckv, kr, qkv, z, a, b, ga, gb = jnp.split(w_in, offs, axis=-1)
    ab = jnp.concatenate([a, b, jnp.zeros((D_MODEL, LANES - 4 * GDN_HEADS), F32)], axis=-1)
    wcat = jnp.concatenate([cq, ckv, _rope_block(kr), _rope_block(_rope_swap(kr)), qkv, z, ab, ga, gb],
                           axis=-1).astype(BF16)
    uq = w_uq.reshape(Q_LORA, MLA_HEADS, MLA_QK)
    ukv = w_ukv.reshape(KV_LORA, MLA_HEADS, QK_NOPE + V_HEAD)
    return {
        "wcat": wcat,
        "qg": q_norm_g.reshape(1, Q_LORA), "kvg": kv_norm_g.reshape(1, KV_LORA),
        "wuq": _head_block(uq[..., :QK_NOPE], uq[..., QK_NOPE:]).astype(BF16),
        "wuqs": _head_block(None, _rope_swap(uq[..., QK_NOPE:])).astype(BF16),
        "wuk": _head_block(ukv[..., :QK_NOPE], None).astype(BF16),
        "wuv": ukv[..., QK_NOPE:].reshape(KV_LORA, MLA_HEADS * V_HEAD).astype(BF16),
        "woa": w_oa.astype(BF16), "wob": w_ob.astype(BF16), "wo": w_o.astype(BF16),
        "gpost": g_post_mix.reshape(1, D_MODEL), "gpre2": g_pre_ffn.reshape(1, D_MODEL),
        "gpost2": g_post_ffn.reshape(1, D_MODEL),
        "gdng": jnp.tile(gdn_norm_g.reshape(1, GDN_DV), (1, 2)),
        "wr_t": w_router.T, "eb": e_bias.reshape(N_EXPERTS, 1),
        "w_gate": w_gate, "w_up": w_up, "w_down": w_down,
        "wsg": ws_gate.astype(BF16), "wsu": ws_up.astype(BF16), "wsd": ws_down.astype(BF16),
    }


def _pad_lanes(v):
    v = v.reshape(1, -1)
    return jnp.concatenate([v, jnp.zeros((1, LANES - v.shape[1]), F32)], axis=-1)


def _layer_group(x, n_seq, seq_len, mods, mod_index, wts, gpre, conv_w, alog128, dtb128,
                 expand, rope_tabs, extra_kv, s0):
    pr = _proj(x, mods, mod_index, gpre, wts, rope_tabs)
    kvs = list(extra_kv) + [(pr["k"], pr["v"], seq_len)]
    omla = _attention(pr["q"], kvs, n_seq, seq_len, "attn_%d" % seq_len)
    prep = _gdn_prep(pr["qkv"], pr["ab"], conv_w, alog128, dtb128, expand, n_seq, seq_len)
    o_f, o_b, s_fin = _gdn_scan(prep, s0, n_seq, seq_len)
    t = n_seq * seq_len
    merged = _merge(x, mods, mod_index, omla, o_f.reshape(t, GDN_W), o_b.reshape(t, GDN_W), pr, wts)
    return merged, pr, s_fin


def _state_to_pairs(s):
    b = s.shape[0]
    s = s.reshape(b, 2, N_PAIRS, 2, GDN_DK, GDN_DV)
    return jnp.transpose(s, (0, 1, 2, 4, 3, 5)).reshape(b, 2, N_PAIRS, GDN_DK, 2 * GDN_DV)


def _pairs_to_state(s):
    b = s.shape[0]
    s = s.reshape(b, 2, N_PAIRS, GDN_DK, 2, GDN_DV)
    return jnp.transpose(s, (0, 1, 2, 4, 3, 5)).reshape(b, 2, GDN_HEADS, GDN_DK, GDN_DV)


def kernel(x_prompt, x_sample, cache_ckv, cache_krope, state_delta, c, c_ctx, w_ada, b_ada, g_pre_mix,
           g_post_mix, g_pre_ffn, g_post_ffn, w_in, q_norm_g, kv_norm_g, w_uq, w_ukv, conv_w, a_log,
           dt_bias, gdn_norm_g, w_oa, w_ob, w_o, w_router, e_bias, w_gate, w_up, w_down, ws_gate, ws_up,
           ws_down):
    batch, seq, _ = x_prompt.shape
    dec_batch, dec_seq, _ = x_sample.shape
    past = cache_ckv.shape[2]
    assert batch * seq + dec_batch * dec_seq == N_TOKENS, "dispatch buffers are sized for N_TOKENS"
    y_p = x_prompt.reshape(batch * seq, D_MODEL)
    y_s = x_sample.reshape(dec_batch * dec_seq, D_MODEL)
    expand = _expand_matrix()
    rope_tabs = _rope_tables(dec_seq)
    cond8 = jnp.concatenate([c_ctx[None], c, jnp.zeros((8 - 1 - dec_batch, D_MODEL), F32)], axis=0)
    ckv_out, krope_out, state_out = [], [], []
    for l in range(DEPTH):
        wts = _prepare_weights(w_in[l], q_norm_g[l], kv_norm_g[l], w_uq[l], w_ukv[l], w_oa[l], w_ob[l],
                               w_o[l], g_post_mix[l], g_pre_ffn[l], g_post_ffn[l], gdn_norm_g[l],
                               w_router[l], e_bias[l], w_gate[l], w_up[l], w_down[l], ws_gate[l],
                               ws_up[l], ws_down[l])
        gpre = g_pre_mix[l].reshape(1, D_MODEL)
        alog128 = _pad_lanes(a_log[l])
        dtb128 = _pad_lanes(dt_bias[l])
        mods = _mods(cond8, w_ada[l], b_ada[l].reshape(1, -1)).reshape(8, 6, D_MODEL)

        zero_state = jnp.zeros((batch, 2, N_PAIRS, GDN_DK, LANES), F32)
        merged_p, pr_p, s_fin = _layer_group(
            y_p, batch, seq, mods, lambda i: 0, wts, gpre, conv_w[l], alog128, dtb128,
            expand, None, [], zero_state)
        ckv_out.append(pr_p["ckv"].reshape(batch, seq, KV_LORA))
        krope_out.append(pr_p["kr"][:, QK_NOPE:MLA_QK].reshape(batch, seq, QK_ROPE))
        state_out.append(_pairs_to_state(s_fin))

        kr_ctx = _rope_block(cache_krope[:, l].reshape(dec_batch * past, QK_ROPE))
        k_ctx, v_ctx = _cache_kv(cache_ckv[:, l].reshape(dec_batch * past, KV_LORA), kr_ctx, wts)
        tiles_per_seq = dec_seq // ROW_TILE
        merged_s, _, _ = _layer_group(
            y_s, dec_batch, dec_seq, mods, lambda i: 1 + i // tiles_per_seq,
            wts, gpre, conv_w[l], alog128, dtb128, expand,
            rope_tabs, [(k_ctx, v_ctx, past)], _state_to_pairs(state_delta[:, l]))

        x1, h2 = (jnp.concatenate([a, b], axis=0) for a, b in zip(merged_p[:2], merged_s[:2]))
        gates_t = jnp.concatenate([merged_p[2], merged_s[2]], axis=1)
        cnt = jnp.concatenate([merged_p[3], merged_s[3]], axis=0)
        ctx_blocks = batch * seq // MOE_BLOCK
        blocks_per_seq = dec_seq // MOE_BLOCK
        y = _moe(h2, gates_t, cnt, x1, mods,
                 lambda b: jnp.where(b < ctx_blocks, 0, 1 + (b - ctx_blocks) // blocks_per_seq), wts)
        y_p, y_s = y[:batch * seq], y[batch * seq:]
    new_ckv = jnp.stack(ckv_out, axis=1)
    new_krope = jnp.stack(krope_out, axis=1)
    new_state = jnp.stack(state_out, axis=1)
    return (y_p.reshape(batch, seq, D_MODEL), y_s.reshape(dec_batch, dec_seq, D_MODEL),
            new_ckv, new_krope, new_state)
```

```python
import functools

import numpy as np
import jax
import jax.numpy as jnp
from jax import lax
from jax.experimental import pallas as pl
from jax.experimental.pallas import tpu as pltpu

F32 = jnp.float32
BF16 = jnp.bfloat16

D_MODEL = 1024
DEPTH = 1
GRID_W = 64
MLA_HEADS = 8
QK_NOPE = 64
QK_ROPE = 32
V_HEAD = 64
Q_LORA = 256
KV_LORA = 256
ROPE_THETA = 10000.0
GDN_HEADS = 8
GDN_DK = 64
GDN_DV = 64
CONV_K = 5
CHUNK = 64
N_EXPERTS = 64
TOP_K = 8
N_GROUPS = 8
TOPK_GROUPS = 4
D_EXPERT = 256
D_SHARED = 256
ROUTED_SCALE = 2.5
EPS = 1e-6

LANES = 128
MLA_QK = QK_NOPE + QK_ROPE
GDN_W = GDN_HEADS * GDN_DK
N_PAIRS = GDN_HEADS // 2
HEAD_PAD = LANES

_SEG = {}
_off = 0
for _name, _width in (("cq", Q_LORA), ("ckv", KV_LORA), ("kr", LANES), ("krs", LANES),
                      ("qkv", 3 * GDN_W), ("z", GDN_W), ("ab", LANES),
                      ("ga", D_MODEL), ("gb", D_MODEL)):
    _SEG[_name] = (_off, _off + _width)
    _off += _width
N_CAT = _off

ROW_TILE = 256
N_TOKENS = 8192
MOE_BLOCK = ROW_TILE
PIECE = 16
EXP_TILE = 256
PIECES_PER_TILE = EXP_TILE // PIECE
N_BLOCKS = N_TOKENS // MOE_BLOCK
STAGE_PIECES = -(-((MOE_BLOCK * TOP_K + N_EXPERTS * (PIECE - 1)) // PIECE) // PIECES_PER_TILE) * PIECES_PER_TILE
N_EXP_TILES = -(-((N_TOKENS * TOP_K + N_BLOCKS * N_EXPERTS * (PIECE - 1)) // PIECE
                  + N_EXPERTS * (PIECES_PER_TILE - 1)) // PIECES_PER_TILE)
DUMP_PIECE0 = N_EXP_TILES * PIECES_PER_TILE
DISPATCH_ROWS = (DUMP_PIECE0 + N_BLOCKS * PIECES_PER_TILE) * PIECE
DISPATCH_W = D_MODEL + LANES
PREP_GROUP = 8
PREP_ROWS = 1024
SCAN_SEQS = 2
VMEM_LIMIT = 56 * 1024 * 1024


def _dot(a, b):
    return jnp.dot(a, b, preferred_element_type=F32)


def _dot_nt(a, b):
    return lax.dot_general(a, b, (((1,), (1,)), ((), ())), preferred_element_type=F32)


def _dot_tn(a, b):
    return lax.dot_general(a, b, (((0,), (0,)), ((), ())), preferred_element_type=F32)


def _rms(x, g):
    return x * lax.rsqrt(jnp.mean(x * x, axis=-1, keepdims=True) + EPS) * g


def _silu(x):
    return x * jax.nn.sigmoid(x)


def _cparams(sem):
    return pltpu.CompilerParams(dimension_semantics=sem, vmem_limit_bytes=VMEM_LIMIT)


def _mods_kernel(c_ref, w_ref, b_ref, o_ref):
    s = _silu(c_ref[...]).astype(BF16)
    o_ref[...] = _dot(s, w_ref[...].astype(BF16)) + b_ref[...]


def _mods(cond8, w_ada, b_ada):
    n = w_ada.shape[1]
    bn = 512
    return pl.pallas_call(
        _mods_kernel,
        out_shape=jax.ShapeDtypeStruct((8, n), F32),
        grid=(n // bn,),
        in_specs=[pl.BlockSpec((8, D_MODEL), lambda j: (0, 0)),
                  pl.BlockSpec((D_MODEL, bn), lambda j: (0, j)),
                  pl.BlockSpec((1, bn), lambda j: (0, j))],
        out_specs=pl.BlockSpec((8, bn), lambda j: (0, j)),
        compiler_params=_cparams(("parallel",)),
        name="mods",
    )(cond8, w_ada, b_ada)


def _proj_kernel(rope, x_ref, m_ref, gpre_ref, wcat_ref, qg_ref, kvg_ref, wuq_ref, wuqs_ref,
                 wuk_ref, wuv_ref, cos_ref, sin_ref,
                 q_ref, k_ref, v_ref, ckv_ref, kr_ref, qkv_ref, z_ref, ab_ref, ga_ref, gb_ref):
    m = m_ref[0]
    h = (_rms(x_ref[...], gpre_ref[...]) * (1.0 + m[1:2]) + m[0:1]).astype(BF16)

    def seg(name):
        a, b = _SEG[name]
        return _dot(h, wcat_ref[:, a:b])

    qkv_ref[...] = seg("qkv")
    z_ref[...] = seg("z")
    ab_ref[...] = seg("ab")
    ga_ref[...] = seg("ga")
    gb_ref[...] = seg("gb")

    qn = _rms(seg("cq"), qg_ref[...]).astype(BF16)
    ckv = _rms(seg("ckv"), kvg_ref[...])
    ckv_ref[...] = ckv
    ckv_b = ckv.astype(BF16)
    kr = seg("kr")
    kr_ref[...] = kr
    qm = _dot(qn, wuq_ref[...])
    kk = _dot(ckv_b, wuk_ref[...])
    v_ref[...] = _dot(ckv_b, wuv_ref[...]).astype(BF16)
    scale = MLA_QK ** -0.5
    if rope:
        cos = cos_ref[...]
        sin = sin_ref[...]
        qs = _dot(qn, wuqs_ref[...])
        kr = kr * cos + seg("krs") * sin
    for hd in range(MLA_HEADS):
        sl = slice(hd * HEAD_PAD, (hd + 1) * HEAD_PAD)
        qh = qm[:, sl]
        if rope:
            qh = qh * cos + qs[:, sl] * sin
        q_ref[:, sl] = (qh * scale).astype(BF16)
        k_ref[:, sl] = (kk[:, sl] + kr).astype(BF16)


def _proj(x, mods, mod_index, gpre, wts, rope_tabs):
    t = x.shape[0]
    tm = ROW_TILE
    rope = rope_tabs is not None
    if rope:
        cos, sin = rope_tabs
        n_rope_blocks = cos.shape[0] // tm
        rope_spec = pl.BlockSpec((tm, LANES), lambda i: (i % n_rope_blocks, 0))
    else:
        cos = sin = jnp.zeros((8, LANES), F32)
        rope_spec = pl.BlockSpec((8, LANES), lambda i: (0, 0))

    def full(a):
        return pl.BlockSpec(a.shape, lambda i: (0,) * a.ndim)

    def rows(w):
        return pl.BlockSpec((tm, w), lambda i: (i, 0))

    out_widths = (("q", MLA_HEADS * HEAD_PAD, BF16), ("k", MLA_HEADS * HEAD_PAD, BF16),
                  ("v", MLA_HEADS * V_HEAD, BF16), ("ckv", KV_LORA, F32), ("kr", LANES, F32),
                  ("qkv", 3 * GDN_W, F32), ("z", GDN_W, F32), ("ab", LANES, F32),
                  ("ga", D_MODEL, F32), ("gb", D_MODEL, F32))
    outs = pl.pallas_call(
        functools.partial(_proj_kernel, rope),
        out_shape=[jax.ShapeDtypeStruct((t, w), dt) for _, w, dt in out_widths],
        grid=(t // tm,),
        in_specs=[rows(D_MODEL),
                  pl.BlockSpec((1, 6, D_MODEL), lambda i: (mod_index(i), 0, 0)),
                  full(gpre), full(wts["wcat"]), full(wts["qg"]), full(wts["kvg"]),
                  full(wts["wuq"]), full(wts["wuqs"]), full(wts["wuk"]), full(wts["wuv"]),
                  rope_spec, rope_spec],
        out_specs=[rows(w) for _, w, _ in out_widths],
        compiler_params=_cparams(("parallel",)),
        name="proj_rope" if rope else "proj",
    )(x, mods, gpre, wts["wcat"], wts["qg"], wts["kvg"], wts["wuq"], wts["wuqs"],
      wts["wuk"], wts["wuv"], cos, sin)
    return {name: o for (name, _, _), o in zip(out_widths, outs)}


def _cache_kv_kernel(ckv_ref, kr_ref, wuk_ref, wuv_ref, k_ref, v_ref):
    c = ckv_ref[...].astype(BF16)
    kk = _dot(c, wuk_ref[...])
    v_ref[...] = _dot(c, wuv_ref[...]).astype(BF16)
    kr = kr_ref[...]
    for hd in range(MLA_HEADS):
        sl = slice(hd * HEAD_PAD, (hd + 1) * HEAD_PAD)
        k_ref[:, sl] = (kk[:, sl] + kr).astype(BF16)


def _cache_kv(ckv, kr128, wts):
    t = ckv.shape[0]
    tm = 512
    return pl.pallas_call(
        _cache_kv_kernel,
        out_shape=[jax.ShapeDtypeStruct((t, MLA_HEADS * HEAD_PAD), BF16),
                   jax.ShapeDtypeStruct((t, MLA_HEADS * V_HEAD), BF16)],
        grid=(t // tm,),
        in_specs=[pl.BlockSpec((tm, KV_LORA), lambda i: (i, 0)),
                  pl.BlockSpec((tm, LANES), lambda i: (i, 0)),
                  pl.BlockSpec(wts["wuk"].shape, lambda i: (0, 0)),
                  pl.BlockSpec(wts["wuv"].shape, lambda i: (0, 0))],
        out_specs=[pl.BlockSpec((tm, MLA_HEADS * HEAD_PAD), lambda i: (i, 0)),
                   pl.BlockSpec((tm, MLA_HEADS * V_HEAD), lambda i: (i, 0))],
        compiler_params=_cparams(("parallel",)),
        name="cache_kv",
    )(ckv, kr128, wts["wuk"], wts["wuv"])


def _attn_kernel(n_kv, q_ref, *refs):
    k_refs = refs[:n_kv]
    v_refs = refs[n_kv:2 * n_kv]
    o_ref = refs[2 * n_kv]
    lane = lax.broadcasted_iota(jnp.int32, (1, LANES), 1)
    low = lane < V_HEAD
    for pr in range(MLA_HEADS // 2):
        halves = []
        for hd in (2 * pr, 2 * pr + 1):
            sl = slice(hd * HEAD_PAD, (hd + 1) * HEAD_PAD)
            qh = q_ref[:, sl]
            scores = [_dot_nt(qh, kr[:, sl]) for kr in k_refs]
            mx = functools.reduce(jnp.maximum, [jnp.max(s, axis=-1, keepdims=True) for s in scores])
            ps = [jnp.exp(s - mx) for s in scores]
            den = functools.reduce(jnp.add, [jnp.sum(p, axis=-1, keepdims=True) for p in ps])
            vsl = slice(pr * LANES, (pr + 1) * LANES)
            acc = functools.reduce(jnp.add, [_dot(p.astype(BF16), vr[:, vsl]) for p, vr in zip(ps, v_refs)])
            halves.append(acc / den)
        o_ref[:, pr * LANES:(pr + 1) * LANES] = jnp.where(low, halves[0], halves[1]).astype(BF16)


def _attention(q, kvs, n_seq, seq_len, name):
    tq = ROW_TILE
    nq = seq_len // tq
    n_kv = len(kvs)
    in_specs = [pl.BlockSpec((tq, MLA_HEADS * HEAD_PAD), lambda b, j: (b * nq + j, 0))]
    in_specs += [pl.BlockSpec((rows, MLA_HEADS * HEAD_PAD), lambda b, j: (b, 0)) for _, _, rows in kvs]
    in_specs += [pl.BlockSpec((rows, MLA_HEADS * V_HEAD), lambda b, j: (b, 0)) for _, _, rows in kvs]
    return pl.pallas_call(
        functools.partial(_attn_kernel, n_kv),
        out_shape=jax.ShapeDtypeStruct((n_seq * seq_len, MLA_HEADS * V_HEAD), BF16),
        grid=(n_seq, nq),
        in_specs=in_specs,
        out_specs=pl.BlockSpec((tq, MLA_HEADS * V_HEAD), lambda b, j: (b * nq + j, 0)),
        compiler_params=_cparams(("parallel", "parallel")),
        name=name,
    )(q, *[k for k, _, _ in kvs], *[v for _, v, _ in kvs])


def _pair_masks():
    lane = lax.broadcasted_iota(jnp.int32, (1, LANES), 1)
    return lane < GDN_DK


def _stack(x, low):
    zero = jnp.zeros_like(x)
    return jnp.concatenate([jnp.where(low, x, zero), jnp.where(low, zero, x)], axis=0)


def _split3(x):
    hi = x.astype(BF16)
    r = x - hi.astype(F32)
    mid = r.astype(BF16)
    lo = (r - mid.astype(F32)).astype(BF16)
    return hi, mid, lo


def _gdn_prep_kernel(seq_len, q_ref, k_ref, v_ref, cwq_ref, cwk_ref, cwv_ref, ab_ref, alog_ref, dtb_ref, e_ref,
                     uf_ref, ub_ref, wf_ref, wb_ref, af_ref, abk_ref, qdf_ref, qdb_ref, kdf_ref, kdb_ref,
                     glf_ref, glb_ref,
                     qn_s, kn_s, vn_s, gcb_s, gf_s):
    seq = q_ref.shape[0]
    n_chunks = seq // CHUNK
    low = _pair_masks()
    row = lax.broadcasted_iota(jnp.int32, (seq, 1), 0) % seq_len
    lane = lax.broadcasted_iota(jnp.int32, (1, LANES), 1)

    def conv(x_ref, cw_ref):
        x = x_ref[...]
        acc = jnp.zeros_like(x)
        for j in range(CONV_K):
            sh = CONV_K // 2 - j
            xs = x if sh == 0 else pltpu.roll(x, sh % seq, axis=0)
            src = row - sh
            valid = (src >= 0) & (src < seq_len)
            acc = acc + jnp.where(valid, xs, 0.0) * cw_ref[j:j + 1, :]
        return _silu(acc)

    def l2n(x):
        sq = x * x
        s0 = jnp.sum(jnp.where(low, sq, 0.0), axis=-1, keepdims=True)
        s1 = jnp.sum(jnp.where(low, 0.0, sq), axis=-1, keepdims=True)
        return x * lax.rsqrt(jnp.where(low, s0, s1) + EPS)

    qn_s[...] = l2n(conv(q_ref, cwq_ref)) * (GDN_DK ** -0.5)
    kn_s[...] = l2n(conv(k_ref, cwk_ref))
    vn_s[...] = conv(v_ref, cwv_ref)

    a = ab_ref[...]
    xg = a + dtb_ref[...]
    softplus = jnp.maximum(xg, 0.0) + jnp.log(1.0 + jnp.exp(-jnp.abs(xg)))
    act = jnp.where(lane < 2 * GDN_HEADS, -jnp.exp(alog_ref[...]) * softplus, jax.nn.sigmoid(a))

    ti = lax.broadcasted_iota(jnp.int32, (CHUNK, CHUNK), 0)
    tj = lax.broadcasted_iota(jnp.int32, (CHUNK, CHUNK), 1)
    tri_lo = (tj <= ti).astype(BF16)
    tri_up = (tj >= ti).astype(BF16)
    for c in range(n_chunks):
        ac = act[c * CHUNK:(c + 1) * CHUNK]
        pieces = _split3(ac)
        lo = functools.reduce(jnp.add, [_dot(tri_lo, pc) for pc in pieces])
        up = functools.reduce(jnp.add, [_dot(tri_up, pc) for pc in pieces])
        gcb_s[c * CHUNK:(c + 1) * CHUNK, :] = jnp.where(lane < GDN_HEADS, lo,
                                                        jnp.where(lane < 2 * GDN_HEADS, up, ac))
    expand = e_ref[0].astype(BF16)
    gf_s[...] = functools.reduce(jnp.add, [_dot(pc, expand) for pc in _split3(gcb_s[...])])

    ri = lax.broadcasted_iota(jnp.int32, (CHUNK, LANES), 0)
    cj = lax.broadcasted_iota(jnp.int32, (CHUNK, LANES), 1) % CHUNK
    eye = (ri == cj).astype(F32)
    ones_b = jnp.ones((CHUNK, LANES), BF16)
    lane0 = (lax.broadcasted_iota(jnp.int32, (1, LANES), 1) % CHUNK) == 0

    def pmm(x, y):
        return _dot(x.astype(BF16), _stack(y, low).astype(BF16))

    def pmm_split(x, y):
        xh, xm, _ = _split3(x)
        yh, ym, _ = _split3(_stack(y, low))
        return _dot(xh, yh) + (_dot(xh, ym) + _dot(xm, yh))

    out_refs = ((uf_ref, wf_ref, af_ref, qdf_ref, kdf_ref, glf_ref),
                (ub_ref, wb_ref, abk_ref, qdb_ref, kdb_ref, glb_ref))
    incl = (ri >= cj, ri <= cj)
    strict = (ri > cj, ri < cj)
    diag8 = (ri // 8) == (cj // 8)
    merge_masks = [((ri // (2 * s)) == (cj // (2 * s))) & ((ri // s) != (cj // s)) for s in (8, 16, 32)]

    def group(it, carry):
        cs = [it * PREP_GROUP + cc for cc in range(PREP_GROUP)]
        rows = [pl.ds(pl.multiple_of(c * CHUNK, CHUNK), CHUNK) for c in cs]
        qc = [qn_s[r, :] for r in rows]
        kc = [kn_s[r, :] for r in rows]
        vc = [vn_s[r, :] for r in rows]
        kst = [_stack(k, low).astype(BF16) for k in kc]
        kk = [_dot_nt(k.astype(BF16), ks) for k, ks in zip(kc, kst)]
        qk = [_dot_nt(q.astype(BF16), ks) for q, ks in zip(qc, kst)]
        chains = [(ci, d) for ci in range(PREP_GROUP) for d in range(2)]
        gc = [gf_s[rows[ci], d * LANES:(d + 1) * LANES] for ci, d in chains]
        beta = [gf_s[rows[ci], (2 + d) * LANES:(3 + d) * LANES] for ci, d in chains]
        pieces = [_split3(_stack(jnp.where(lane0, g, 0.0), low)) for g in gc]
        gr = [functools.reduce(jnp.add, [_dot_nt(ones_b, pc) for pc in ps]) for ps in pieces]
        dm = [jnp.exp(jnp.where(incl[d], g - r, -jnp.inf)) for (ci, d), g, r in zip(chains, gc, gr)]
        lm = [jnp.where(strict[d], b * kk[ci] * m, 0.0) for (ci, d), b, m in zip(chains, beta, dm)]
        aint = [(qk[ci] * m).astype(BF16) for (ci, d), m in zip(chains, dm)]
        x = [-jnp.where(diag8, l, 0.0) for l in lm]
        t = [eye + xx for xx in x]
        for _ in range(2):
            x = [pmm(xx, xx) for xx in x]
            t = [tt + pmm(tt, xx) for tt, xx in zip(t, x)]
        for off in merge_masks:
            tc = [pmm(tt, jnp.where(off, l, 0.0)) for tt, l in zip(t, lm)]
            t = [tt - pmm(a, tt) for tt, a in zip(t, tc)]
        resid = [eye - pmm_split(eye + l, tt) for l, tt in zip(lm, t)]
        t = [tt + pmm(tt, rr) for tt, rr in zip(t, resid)]
        egc = [jnp.exp(g) for g in gc]
        u = [pmm(tt, vc[ci] * b) for (ci, d), tt, b in zip(chains, t, beta)]
        w = [pmm(tt, kc[ci] * b * e).astype(BF16) for (ci, d), tt, b, e in zip(chains, t, beta, egc)]
        qd = [(qc[ci] * e).astype(BF16) for (ci, d), e in zip(chains, egc)]
        gtot = [g[CHUNK - 1:CHUNK, :] if d == 0 else g[0:1, :] for (ci, d), g in zip(chains, gc)]
        kd = [(kc[ci] * jnp.exp(gt - g)).astype(BF16) for (ci, d), gt, g in zip(chains, gtot, gc)]
        for n, (ci, d) in enumerate(chains):
            u_ref, w_ref, a_ref, qd_ref, kd_ref, gl_ref = out_refs[d]
            u_ref[0, rows[ci], :] = u[n]
            w_ref[0, rows[ci], :] = w[n]
            a_ref[0, rows[ci], :] = aint[n]
            qd_ref[0, rows[ci], :] = qd[n]
            kd_ref[0, rows[ci], :] = kd[n]
            gl_ref[0, pl.ds(cs[ci], 1), :, :] = jnp.broadcast_to(jnp.exp(gtot[n]), (1, 8, LANES))
        return carry

    lax.fori_loop(0, n_chunks // PREP_GROUP, group, 0)


def _gdn_prep(qkv, ab, conv_w, alog128, dtb128, expand, n_seq, seq_len):
    rb = max(seq_len, PREP_ROWS)
    nb = n_seq * seq_len // rb
    n_chunks = rb // CHUNK
    col = lambda off: pl.BlockSpec((rb, LANES), lambda s, p: (s, off + p))
    cw = lambda off: pl.BlockSpec((CONV_K, LANES), lambda s, p: (0, off + p))
    vec = pl.BlockSpec((1, LANES), lambda s, p: (0, 0))
    big = lambda: pl.BlockSpec((1, rb, LANES), lambda s, p: (s, 0, p))
    glspec = lambda: pl.BlockSpec((1, n_chunks, 8, LANES), lambda s, p: (s, 0, 0, p))
    shp = lambda dt: jax.ShapeDtypeStruct((nb, rb, GDN_W), dt)
    glshp = jax.ShapeDtypeStruct((nb, n_chunks, 8, GDN_W), F32)
    outs = pl.pallas_call(
        functools.partial(_gdn_prep_kernel, seq_len),
        out_shape=[shp(F32), shp(F32)] + [shp(BF16)] * 8 + [glshp, glshp],
        grid=(nb, N_PAIRS),
        in_specs=[col(0), col(N_PAIRS), col(2 * N_PAIRS), cw(0), cw(N_PAIRS), cw(2 * N_PAIRS),
                  pl.BlockSpec((rb, LANES), lambda s, p: (s, 0)), vec, vec,
                  pl.BlockSpec((1, LANES, 4 * LANES), lambda s, p: (p, 0, 0))],
        out_specs=[big() for _ in range(10)] + [glspec(), glspec()],
        scratch_shapes=[pltpu.VMEM((rb, LANES), F32)] * 4 + [pltpu.VMEM((rb, 4 * LANES), F32)],
        compiler_params=_cparams(("parallel", "parallel")),
        name="gdn_prep_%d" % seq_len,
    )(qkv, qkv, qkv, conv_w, conv_w, conv_w, ab, alog128, dtb128, expand)
    per_seq = [o.reshape(n_seq, seq_len, GDN_W) for o in outs[:10]]
    return per_seq + [o.reshape(n_seq, seq_len // CHUNK, 8, GDN_W) for o in outs[10:]]


def _gdn_scan_kernel(uf_ref, ub_ref, wf_ref, wb_ref, af_ref, abk_ref, qdf_ref, qdb_ref, kdf_ref, kdb_ref,
                     glf_ref, glb_ref, s0_ref, of_ref, ob_ref, sfin_ref, state):
    step = pl.program_id(1)
    n_steps = pl.num_programs(1)
    low = _pair_masks()
    chains = [(d, j, p) for d in range(2) for j in range(SCAN_SEQS) for p in range(N_PAIRS)]

    first = step == 0
    per_dir = ((uf_ref, wf_ref, af_ref, qdf_ref, kdf_ref, glf_ref, of_ref),
               (ub_ref, wb_ref, abk_ref, qdb_ref, kdb_ref, glb_ref, ob_ref))
    def rd(k, d, j, p):
        return per_dir[d][k][j, :, p * LANES:(p + 1) * LANES]

    s = [jnp.where(first, _stack(s0_ref[j, d, p], low), state[idx]) for idx, (d, j, p) in enumerate(chains)]
    sb = [x.astype(BF16) for x in s]
    ws = [_dot(rd(1, *c), b) for c, b in zip(chains, sb)]
    qs = [_dot(rd(3, *c), b) for c, b in zip(chains, sb)]
    vst = [_stack(rd(0, *c) - w, low).astype(BF16) for c, w in zip(chains, ws)]
    upd = [_dot_tn(_stack(rd(4, *c), low), v) for c, v in zip(chains, vst)]
    intra = [_dot(rd(2, *c), v) for c, v in zip(chains, vst)]
    for idx, (d, j, p) in enumerate(chains):
        sl = slice(p * LANES, (p + 1) * LANES)
        state[idx] = s[idx] * per_dir[d][5][j, 0, 0:1, sl] + upd[idx]
        per_dir[d][6][j, :, sl] = qs[idx] + intra[idx]

    @pl.when(step == n_steps - 1)
    def _():
        for idx, (d, j, p) in enumerate(chains):
            s = state[idx]
            sfin_ref[j, d, p] = s[:GDN_DK] + s[GDN_DK:]


def _gdn_scan(prep, s0, n_seq, seq_len):
    n_chunks = seq_len // CHUNK
    ns = SCAN_SEQS
    fwd = lambda: pl.BlockSpec((ns, CHUNK, GDN_W), lambda g, i: (g, i, 0))
    bwd = lambda: pl.BlockSpec((ns, CHUNK, GDN_W), lambda g, i: (g, n_chunks - 1 - i, 0))
    glf = pl.BlockSpec((ns, 1, 8, GDN_W), lambda g, i: (g, i, 0, 0))
    glb = pl.BlockSpec((ns, 1, 8, GDN_W), lambda g, i: (g, n_chunks - 1 - i, 0, 0))
    st = lambda: pl.BlockSpec((ns, 2, N_PAIRS, GDN_DK, LANES), lambda g, i: (g, 0, 0, 0, 0))
    oshape = jax.ShapeDtypeStruct((n_seq, seq_len, GDN_W), F32)
    return pl.pallas_call(
        _gdn_scan_kernel,
        out_shape=[oshape, oshape, jax.ShapeDtypeStruct((n_seq, 2, N_PAIRS, GDN_DK, LANES), F32)],
        grid=(n_seq // ns, n_chunks),
        in_specs=[fwd(), bwd()] * 5 + [glf, glb, st()],
        out_specs=[fwd(), bwd(), st()],
        scratch_shapes=[pltpu.VMEM((2 * ns * N_PAIRS, LANES, LANES), F32)],
        compiler_params=_cparams(("parallel", "arbitrary")),
        name="gdn_scan_%d" % seq_len,
    )(*prep, s0)


def _route(sel, s):
    per_group = N_EXPERTS // N_GROUPS
    ninf = -jnp.inf
    sub = lax.broadcasted_iota(jnp.int32, sel.shape, 1).astype(F32)
    gid = lax.broadcasted_iota(jnp.int32, (N_GROUPS, 1, sel.shape[2]), 0).astype(F32)
    m1 = jnp.max(sel, axis=1, keepdims=True)
    i1 = jnp.min(jnp.where(sel == m1, sub, float(per_group)), axis=1, keepdims=True)
    m2 = jnp.max(jnp.where(sub == i1, ninf, sel), axis=1, keepdims=True)
    work = m1 + m2
    gmask = jnp.zeros(work.shape, jnp.bool_)
    for _ in range(TOPK_GROUPS):
        m = jnp.max(work, axis=0, keepdims=True)
        idx = jnp.min(jnp.where(work == m, gid, float(N_GROUPS)), axis=0, keepdims=True)
        pick = gid == idx
        gmask = gmask | pick
        work = jnp.where(pick, ninf, work)
    work = jnp.where(gmask, sel, ninf)
    eid = gid * per_group + sub
    chosen = jnp.zeros(sel.shape, jnp.bool_)
    for _ in range(TOP_K):
        m = jnp.max(jnp.max(work, axis=1, keepdims=True), axis=0, keepdims=True)
        idx = jnp.min(jnp.min(jnp.where(work == m, eid, float(N_EXPERTS)), axis=1, keepdims=True),
                      axis=0, keepdims=True)
        pick = eid == idx
        chosen = chosen | pick
        work = jnp.where(pick, ninf, work)
    wk = jnp.where(chosen, s, 0.0)
    den = jnp.sum(jnp.sum(wk, axis=1, keepdims=True), axis=0, keepdims=True)
    return wk / den * ROUTED_SCALE


def _merge_kernel(x_ref, m_ref, omla_ref, of_ref, ob_ref, z_ref, ga_ref, gb_ref,
                  woa_ref, wob_ref, wo_ref, gpost_ref, gpre_ref, gdng_ref, wr_ref, eb_ref,
                  x1_ref, h2_ref, gates_ref, cnt_ref):
    m = m_ref[0]
    low = _pair_masks()
    o = of_ref[...] + ob_ref[...]
    z = z_ref[...]
    parts = []
    for p in range(N_PAIRS):
        sl = slice(p * LANES, (p + 1) * LANES)
        op = o[:, sl]
        sq = op * op
        s0 = jnp.sum(jnp.where(low, sq, 0.0), axis=-1, keepdims=True)
        s1 = jnp.sum(jnp.where(low, 0.0, sq), axis=-1, keepdims=True)
        ms = jnp.where(low, s0, s1) * (1.0 / GDN_DV)
        parts.append(op * lax.rsqrt(ms + EPS) * gdng_ref[...] * _silu(z[:, sl]))
    og = jnp.concatenate(parts, axis=1).astype(BF16)
    ya = _dot(omla_ref[...], woa_ref[...])
    yb = _dot(og, wob_ref[...])
    mix = (jax.nn.sigmoid(ga_ref[...]) * ya + jax.nn.sigmoid(gb_ref[...]) * yb).astype(BF16)
    y = _dot(mix, wo_ref[...])
    x1 = x_ref[...] + m[2:3] * _rms(y, gpost_ref[...])
    x1_ref[...] = x1
    h2 = _rms(x1, gpre_ref[...]) * (1.0 + m[4:5]) + m[3:4]
    h2_ref[...] = h2.astype(BF16)
    logits = lax.dot_general(wr_ref[...], h2, (((1,), (1,)), ((), ())),
                             preferred_element_type=F32, precision=lax.Precision.HIGHEST)
    s = jax.nn.sigmoid(logits)
    sel = s + eb_ref[...]
    tm = s.shape[1]
    shape3 = (N_GROUPS, N_EXPERTS // N_GROUPS, tm)
    gates_t = _route(sel.reshape(shape3), s.reshape(shape3)).reshape(N_EXPERTS, tm)
    gates_ref[...] = gates_t
    cnt_ref[0] = jnp.sum((gates_t > 0.0).astype(F32), axis=1, keepdims=True)


def _merge_kernel_into(*refs):
    n_in = 16
    _merge_kernel(*refs[:n_in], *refs[n_in + 4:])


def _merge(x, mods, mod_index, omla, o_f, o_b, pr, wts, tile0, total, into=None):
    t = x.shape[0]
    tm = ROW_TILE

    def full(a):
        return pl.BlockSpec(a.shape, lambda i: (0,) * a.ndim)

    def rows(w):
        return pl.BlockSpec((tm, w), lambda i: (i, 0))

    def out_rows(w):
        return pl.BlockSpec((tm, w), lambda i: (tile0 + i, 0))

    names = ("woa", "wob", "wo", "gpost", "gpre2", "gdng", "wr_t", "eb")
    args = [x, mods, omla, o_f, o_b, pr["z"], pr["ga"], pr["gb"]] + [wts[n] for n in names]
    in_specs = [rows(D_MODEL), pl.BlockSpec((1, 6, D_MODEL), lambda i: (mod_index(i), 0, 0)),
                rows(MLA_HEADS * V_HEAD), rows(GDN_W), rows(GDN_W), rows(GDN_W),
                rows(D_MODEL), rows(D_MODEL)] + [full(wts[n]) for n in names]
    aliases = {}
    if into is not None:
        aliases = {len(args) + k: k for k in range(4)}
        in_specs = in_specs + [pl.BlockSpec(memory_space=pl.ANY)] * 4
        args = args + list(into)
    return pl.pallas_call(
        _merge_kernel if into is None else _merge_kernel_into,
        out_shape=[jax.ShapeDtypeStruct((total, D_MODEL), F32), jax.ShapeDtypeStruct((total, D_MODEL), BF16),
                   jax.ShapeDtypeStruct((N_EXPERTS, total), F32),
                   jax.ShapeDtypeStruct((total // tm, N_EXPERTS, 1), F32)],
        grid=(t // tm,),
        in_specs=in_specs,
        out_specs=[out_rows(D_MODEL), out_rows(D_MODEL),
                   pl.BlockSpec((N_EXPERTS, tm), lambda i: (0, tile0 + i)),
                   pl.BlockSpec((1, N_EXPERTS, 1), lambda i: (tile0 + i, 0, 0))],
        input_output_aliases=aliases,
        compiler_params=_cparams(("parallel",)),
        name="merge",
    )(*args)


TABLE_W = 256
TILE_TABLE_W = 512


def _ceil_div(x, d):
    return jnp.floor((x + (d - 1)) * (1.0 / d))


def _moe_tables_kernel(cnt_ref, cnt_t_ref, ce_ref, cb_ref, dst_ref, src_ref, ng_ref, tile_ref):
    nb = cnt_ref.shape[0]
    ppt = float(PIECES_PER_TILE)
    ei = lax.broadcasted_iota(jnp.int32, (N_EXPERTS, N_EXPERTS), 0)
    ej = lax.broadcasted_iota(jnp.int32, (N_EXPERTS, N_EXPERTS), 1)
    tri = (ej <= ei).astype(BF16)

    def cumsum_experts(col):
        wide = jnp.broadcast_to(col, (N_EXPERTS, LANES))
        return functools.reduce(jnp.add, [_dot(tri, pc) for pc in _split3(wide)])[:, 0:1]

    eid = lax.broadcasted_iota(jnp.int32, (N_EXPERTS, 1), 0).astype(F32)
    pc_t = _ceil_div(cnt_t_ref[...], PIECE)
    tp = jnp.sum(pc_t, axis=1, keepdims=True)
    rp = _ceil_div(tp, PIECES_PER_TILE) * ppt
    gs_end = cumsum_experts(rp)
    gs = gs_end - rp
    blk = lax.broadcasted_iota(jnp.int32, (1, nb), 1)
    c = lax.broadcasted_iota(jnp.int32, (1, TABLE_W), 1).astype(F32)
    for b in range(nb):
        pc = _ceil_div(cnt_ref[b], PIECE)
        seg_end = cumsum_experts(pc)
        seg = seg_end - pc
        blk_off = jnp.sum(jnp.where(blk < b, pc_t, 0.0), axis=1, keepdims=True)
        nvalid = seg_end[N_EXPERTS - 1:N_EXPERTS, :]
        ce = jnp.minimum(jnp.sum((seg_end <= c).astype(F32), axis=0, keepdims=True), N_EXPERTS - 1.0)
        onehot = eid == ce
        seg_sel = jnp.sum(jnp.where(onehot, seg, 0.0), axis=0, keepdims=True)
        base_sel = jnp.sum(jnp.where(onehot, gs + blk_off - seg, 0.0), axis=0, keepdims=True)
        valid = c < nvalid
        dump = DUMP_PIECE0 + b * PIECES_PER_TILE + (c - ppt * jnp.floor(c * (1.0 / ppt)))
        dst = jnp.where(valid, base_sel + c, dump)
        row = slice(b, b + 1)
        ce_ref[row, :] = ce.astype(jnp.int32)
        cb_ref[row, :] = jnp.where(valid, (c - seg_sel) * PIECE, -float(1 << 20)).astype(jnp.int32)
        dst_ref[row, :] = dst.astype(jnp.int32)
        src_ref[row, :] = jnp.where(valid, dst, dst[:, 0:1]).astype(jnp.int32)
        ng_ref[row, :] = jnp.broadcast_to(_ceil_div(nvalid, PIECES_PER_TILE), (1, LANES)).astype(jnp.int32)
    j = lax.broadcasted_iota(jnp.int32, (1, TILE_TABLE_W), 1).astype(F32)
    start = j * ppt
    te = jnp.minimum(jnp.sum((gs_end <= start).astype(F32), axis=0, keepdims=True), N_EXPERTS - 1.0)
    onehot = eid == te
    tp_sel = jnp.sum(jnp.where(onehot, tp, 0.0), axis=0, keepdims=True)
    gs_sel = jnp.sum(jnp.where(onehot, gs, 0.0), axis=0, keepdims=True)
    n_used = gs_end[N_EXPERTS - 1:N_EXPERTS, :] * (1.0 / ppt)
    used = j < n_used
    tv = jnp.where(used, jnp.clip((tp_sel - (start - gs_sel)) * PIECE, 0.0, float(EXP_TILE)), 0.0)
    tin = jnp.where(used, j, n_used - 1.0)
    tout = jnp.where(used, j, float(N_EXP_TILES))
    tile_ref[...] = jnp.zeros(tile_ref.shape, jnp.int32)
    for r, v in enumerate((te, tv, tin, tout)):
        tile_ref[r:r + 1, :] = v.astype(jnp.int32)


def _dispatch_tables(cnt):
    nb = cnt.shape[0]
    tab = jax.ShapeDtypeStruct((nb, TABLE_W), jnp.int32)
    ce, cb, dst, src, ng, tile = pl.pallas_call(
        _moe_tables_kernel,
        out_shape=[tab, tab, tab, tab, jax.ShapeDtypeStruct((nb, LANES), jnp.int32),
                   jax.ShapeDtypeStruct((8, TILE_TABLE_W), jnp.int32)],
        name="moe_tables",
    )(cnt, cnt[:, :, 0].T)
    return {"ce": ce, "cb": cb, "dst": dst, "src": src, "ngroups": ng, "tile": tile}


def _piece_onehot(ce_ref, cb_ref, rank_s, b, g, extra=None):
    sub = lax.broadcasted_iota(jnp.int32, (PIECE, 1), 0).astype(F32)
    ps, ex = [], []
    for cc in range(PIECES_PER_TILE):
        c = g * PIECES_PER_TILE + cc
        e = ce_ref[b, c]
        base = cb_ref[b, c].astype(F32)
        hit = rank_s[pl.ds(e, 1), :] == base + sub
        ps.append(jnp.where(hit, 1.0, 0.0).astype(BF16))
        if extra is not None:
            ex.append(jnp.sum(jnp.where(hit, extra[pl.ds(e, 1), :], 0.0), axis=-1, keepdims=True))
    p = jnp.concatenate(ps, axis=0)
    return (p, jnp.concatenate(ex, axis=0)) if extra is not None else p


def _block_ranks(gt):
    n = gt.shape[1]
    ti = lax.broadcasted_iota(jnp.int32, (n, n), 0)
    tj = lax.broadcasted_iota(jnp.int32, (n, n), 1)
    before = (ti < tj).astype(BF16)
    member = gt > 0.0
    rank = _dot(member.astype(BF16), before)
    return jnp.where(member, rank, -1.0)


def _moe_sort_kernel(ce_ref, cb_ref, dst_ref, ng_ref, h_ref, gt_ref, xg_ref, stage, rank_s, gate_s, sem):
    b = pl.program_id(0)
    n_groups = ng_ref[b, 0]
    gt = gt_ref[...]
    rank_s[...] = _block_ranks(gt)
    gate_s[...] = gt
    lane = lax.broadcasted_iota(jnp.int32, (1, LANES), 1)

    def piece_copy(c):
        r0 = pl.multiple_of(c * PIECE, PIECE)
        d0 = pl.multiple_of(dst_ref[b, c] * PIECE, PIECE)
        return pltpu.make_async_copy(stage.at[pl.ds(r0, PIECE)], xg_ref.at[pl.ds(d0, PIECE)], sem)

    def group(g, carry):
        p, gcol = _piece_onehot(ce_ref, cb_ref, rank_s, b, g, gate_s)
        xs = _dot(p, h_ref[...]).astype(BF16)
        hi, mid, lo = (t.astype(F32) for t in _split3(gcol))
        gblk = jnp.where(lane == 0, hi, jnp.where(lane == 1, mid, jnp.where(lane == 2, lo, 0.0)))
        r0 = pl.multiple_of(g * EXP_TILE, EXP_TILE)
        stage[pl.ds(r0, EXP_TILE), :] = jnp.concatenate([xs, gblk.astype(BF16)], axis=1)
        for cc in range(PIECES_PER_TILE):
            piece_copy(g * PIECES_PER_TILE + cc).start()
        return carry

    lax.fori_loop(0, n_groups, group, 0)

    def wait_one(c, carry):
        piece_copy(c).wait()
        return carry

    lax.fori_loop(0, n_groups * PIECES_PER_TILE, wait_one, 0)


def _moe_sort(h2, gates_t, tabs):
    nb = h2.shape[0] // MOE_BLOCK
    grid_spec = pltpu.PrefetchScalarGridSpec(
        num_scalar_prefetch=4, grid=(nb,),
        in_specs=[pl.BlockSpec((MOE_BLOCK, D_MODEL), lambda b, *_: (b, 0)),
                  pl.BlockSpec((N_EXPERTS, MOE_BLOCK), lambda b, *_: (0, b))],
        out_specs=pl.BlockSpec(memory_space=pl.ANY),
        scratch_shapes=[pltpu.VMEM((STAGE_PIECES * PIECE, DISPATCH_W), BF16),
                        pltpu.VMEM((N_EXPERTS, MOE_BLOCK), F32), pltpu.VMEM((N_EXPERTS, MOE_BLOCK), F32),
                        pltpu.SemaphoreType.DMA])
    return pl.pallas_call(
        _moe_sort_kernel,
        out_shape=jax.ShapeDtypeStruct((DISPATCH_ROWS, DISPATCH_W), BF16),
        grid_spec=grid_spec,
        compiler_params=_cparams(("arbitrary",)),
        name="moe_sort",
    )(tabs["ce"], tabs["cb"], tabs["dst"], tabs["ngroups"], h2, gates_t)


def _moe_expert_kernel(tile_ref, x_ref, wg_ref, wu_ref, wd_ref, y_ref):
    valid = tile_ref[1, pl.program_id(0)]

    @pl.when(valid > 0)
    def _():
        keep = lax.broadcasted_iota(jnp.int32, (EXP_TILE, 1), 0) < valid
        xrow = x_ref[...]
        x = jnp.where(keep, xrow[:, :D_MODEL], jnp.zeros((), BF16))
        g = jnp.sum(jnp.where(keep, xrow[:, D_MODEL:].astype(F32), 0.0), axis=-1, keepdims=True)
        hg = _dot(x, wg_ref[0].astype(BF16))
        hu = _dot(x, wu_ref[0].astype(BF16))
        act = (_silu(hg) * hu * g).astype(BF16)
        y_ref[...] = _dot(act, wd_ref[0].astype(BF16)).astype(BF16)


def _moe_expert(xg, tabs, wts):
    grid_spec = pltpu.PrefetchScalarGridSpec(
        num_scalar_prefetch=1, grid=(N_EXP_TILES,),
        in_specs=[pl.BlockSpec((EXP_TILE, DISPATCH_W), lambda j, tt: (tt[2, j], 0)),
                  pl.BlockSpec((1, D_MODEL, D_EXPERT), lambda j, tt: (tt[0, j], 0, 0)),
                  pl.BlockSpec((1, D_MODEL, D_EXPERT), lambda j, tt: (tt[0, j], 0, 0)),
                  pl.BlockSpec((1, D_EXPERT, D_MODEL), lambda j, tt: (tt[0, j], 0, 0))],
        out_specs=pl.BlockSpec((EXP_TILE, D_MODEL), lambda j, tt: (tt[3, j], 0)))
    return pl.pallas_call(
        _moe_expert_kernel,
        out_shape=jax.ShapeDtypeStruct((DISPATCH_ROWS, D_MODEL), BF16),
        grid_spec=grid_spec,
        compiler_params=_cparams(("arbitrary",)),
        name="moe_expert",
    )(tabs["tile"], xg, wts["w_gate"], wts["w_up"], wts["w_down"])


def _moe_combine_kernel(ce_ref, cb_ref, src_ref, ng_ref, yg_ref, gt_ref, h_ref, x1_ref, m_ref, gpost_ref,
                        wsg_ref, wsu_ref, wsd_ref, out_ref, stage, rank_s, acc_s, sem):
    b = pl.program_id(0)
    n_groups = ng_ref[b, 0]

    def piece_copy(c):
        r0 = pl.multiple_of(c * PIECE, PIECE)
        s0 = pl.multiple_of(src_ref[b, c] * PIECE, PIECE)
        return pltpu.make_async_copy(yg_ref.at[pl.ds(s0, PIECE)], stage.at[pl.ds(r0, PIECE)], sem)

    def start_one(c, carry):
        piece_copy(c).start()
        return carry

    lax.fori_loop(0, n_groups * PIECES_PER_TILE, start_one, 0)
    rank_s[...] = _block_ranks(gt_ref[...])
    h = h_ref[...]
    sh = (_silu(_dot(h, wsg_ref[...])) * _dot(h, wsu_ref[...])).astype(BF16)
    acc_s[...] = _dot(sh, wsd_ref[...])

    def wait_one(c, carry):
        piece_copy(c).wait()
        return carry

    lax.fori_loop(0, n_groups * PIECES_PER_TILE, wait_one, 0)

    def group(g, carry):
        p = _piece_onehot(ce_ref, cb_ref, rank_s, b, g)
        r0 = pl.multiple_of(g * EXP_TILE, EXP_TILE)
        acc_s[...] += _dot_tn(p, stage[pl.ds(r0, EXP_TILE), :])
        return carry

    lax.fori_loop(0, n_groups, group, 0)
    m = m_ref[0]
    out_ref[...] = x1_ref[...] + m[5:6] * _rms(acc_s[...], gpost_ref[...])


def _moe_combine(yg, gates_t, h2, x1, mods, mod_index, tabs, wts):
    t = h2.shape[0]
    nb = t // MOE_BLOCK

    def full(a):
        return pl.BlockSpec(a.shape, lambda b, *_: (0,) * a.ndim)

    grid_spec = pltpu.PrefetchScalarGridSpec(
        num_scalar_prefetch=4, grid=(nb,),
        in_specs=[pl.BlockSpec(memory_space=pl.ANY),
                  pl.BlockSpec((N_EXPERTS, MOE_BLOCK), lambda b, *_: (0, b)),
                  pl.BlockSpec((MOE_BLOCK, D_MODEL), lambda b, *_: (b, 0)),
                  pl.BlockSpec((MOE_BLOCK, D_MODEL), lambda b, *_: (b, 0)),
                  pl.BlockSpec((1, 6, D_MODEL), lambda b, *_: (mod_index(b), 0, 0)),
                  full(wts["gpost2"]), full(wts["wsg"]), full(wts["wsu"]), full(wts["wsd"])],
        out_specs=pl.BlockSpec((MOE_BLOCK, D_MODEL), lambda b, *_: (b, 0)),
        scratch_shapes=[pltpu.VMEM((STAGE_PIECES * PIECE, D_MODEL), BF16),
                        pltpu.VMEM((N_EXPERTS, MOE_BLOCK), F32), pltpu.VMEM((MOE_BLOCK, D_MODEL), F32),
                        pltpu.SemaphoreType.DMA])
    return pl.pallas_call(
        _moe_combine_kernel,
        out_shape=jax.ShapeDtypeStruct((t, D_MODEL), F32),
        grid_spec=grid_spec,
        compiler_params=_cparams(("arbitrary",)),
        name="moe_combine",
    )(tabs["ce"], tabs["cb"], tabs["src"], tabs["ngroups"], yg, gates_t, h2, x1, mods,
      wts["gpost2"], wts["wsg"], wts["wsu"], wts["wsd"])


def _moe(h2, gates_t, cnt, x1, mods, mod_index, wts):
    tabs = _dispatch_tables(cnt)
    xg = _moe_sort(h2, gates_t, tabs)
    yg = _moe_expert(xg, tabs, wts)
    return _moe_combine(yg, gates_t, h2, x1, mods, mod_index, tabs, wts)


def _rope_swap(w):
    nf = QK_ROPE // 4
    parts = [w[..., i * nf:(i + 1) * nf] for i in range(4)]
    return jnp.concatenate([parts[1], parts[0], parts[3], parts[2]], axis=-1)


def _head_block(nope, rope):
    lead = nope.shape[:-2] if nope is not None else rope.shape[:-2]
    nope = jnp.zeros(lead + (MLA_HEADS, QK_NOPE), F32) if nope is None else nope
    rope = jnp.zeros(lead + (MLA_HEADS, QK_ROPE), F32) if rope is None else rope
    pad = jnp.zeros(lead + (MLA_HEADS, HEAD_PAD - MLA_QK), F32)
    return jnp.concatenate([nope, rope, pad], axis=-1).reshape(lead + (MLA_HEADS * HEAD_PAD,))


def _rope_block(w):
    lead = w.shape[:-1]
    return jnp.concatenate([jnp.zeros(lead + (QK_NOPE,), F32), w,
                            jnp.zeros(lead + (HEAD_PAD - MLA_QK,), F32)], axis=-1)


def _rope_tables(n_tokens):
    rows = n_tokens // GRID_W
    row = np.repeat(np.arange(rows, dtype=np.float32), GRID_W)
    colv = np.tile(np.arange(GRID_W, dtype=np.float32), rows)
    nf = QK_ROPE // 4
    inv = jnp.asarray(ROPE_THETA, F32) ** (-jnp.arange(nf, dtype=F32) / nf)
    ang_r = jnp.asarray(row)[:, None] * inv
    ang_c = jnp.asarray(colv)[:, None] * inv
    cos32 = jnp.concatenate([jnp.cos(ang_r), jnp.cos(ang_r), jnp.cos(ang_c), jnp.cos(ang_c)], axis=-1)
    sin32 = jnp.concatenate([-jnp.sin(ang_r), jnp.sin(ang_r), -jnp.sin(ang_c), jnp.sin(ang_c)], axis=-1)
    ones = jnp.ones((n_tokens, QK_NOPE), F32)
    tail = jnp.zeros((n_tokens, HEAD_PAD - MLA_QK), F32)
    cos = jnp.concatenate([ones, cos32, tail], axis=-1)
    sin = jnp.concatenate([jnp.zeros((n_tokens, QK_NOPE), F32), sin32, tail], axis=-1)
    return cos, sin


def _expand_matrix():
    e = np.zeros((N_PAIRS, LANES, 4 * LANES), np.float32)
    for p in range(N_PAIRS):
        for blk in range(4):
            for hh in range(2):
                src = blk * GDN_HEADS + 2 * p + hh
                e[p, src, blk * LANES + hh * GDN_DK: blk * LANES + (hh + 1) * GDN_DK] = 1.0
    return jnp.asarray(e)


def _prepare_weights(w_in, q_norm_g, kv_norm_g, w_uq, w_ukv, w_oa, w_ob, w_o, g_post_mix, g_pre_ffn,
                     g_post_ffn, gdn_norm_g, w_router, e_bias, w_gate, w_up, w_down, ws_gate, ws_up, ws_down):
    offs = np.cumsum((Q_LORA, KV_LORA, QK_ROPE, 3 * GDN_W, GDN_W, 2 * GDN_HEADS, 2 * GDN_HEADS,
                      D_MODEL, D_MODEL))[:-1].tolist()
    cq, ckv, kr, qkv, z, a, b, ga, gb = jnp.split(w_in, offs, axis=-1)
    ab = jnp.concatenate([a, b, jnp.zeros((D_MODEL, LANES - 4 * GDN_HEADS), F32)], axis=-1)
    wcat = jnp.concatenate([cq, ckv, _rope_block(kr), _rope_block(_rope_swap(kr)), qkv, z, ab, ga, gb],
                           axis=-1).astype(BF16)
    uq = w_uq.reshape(Q_LORA, MLA_HEADS, MLA_QK)
    ukv = w_ukv.reshape(KV_LORA, MLA_HEADS, QK_NOPE + V_HEAD)
    return {
        "wcat": wcat,
        "qg": q_norm_g.reshape(1, Q_LORA), "kvg": kv_norm_g.reshape(1, KV_LORA),
        "wuq": _head_block(uq[..., :QK_NOPE], uq[..., QK_NOPE:]).astype(BF16),
        "wuqs": _head_block(None, _rope_swap(uq[..., QK_NOPE:])).astype(BF16),
        "wuk": _head_block(ukv[..., :QK_NOPE], None).astype(BF16),
        "wuv": ukv[..., QK_NOPE:].reshape(KV_LORA, MLA_HEADS * V_HEAD).astype(BF16),
        "woa": w_oa.astype(BF16), "wob": w_ob.astype(BF16), "wo": w_o.astype(BF16),
        "gpost": g_post_mix.reshape(1, D_MODEL), "gpre2": g_pre_ffn.reshape(1, D_MODEL),
        "gpost2": g_post_ffn.reshape(1, D_MODEL),
        "gdng": jnp.tile(gdn_norm_g.reshape(1, GDN_DV), (1, 2)),
        "wr_t": w_router.T, "eb": e_bias.reshape(N_EXPERTS, 1),
        "w_gate": w_gate, "w_up": w_up, "w_down": w_down,
        "wsg": ws_gate.astype(BF16), "wsu": ws_up.astype(BF16), "wsd": ws_down.astype(BF16),
    }


def _pad_lanes(v):
    v = v.reshape(1, -1)
    return jnp.concatenate([v, jnp.zeros((1, LANES - v.shape[1]), F32)], axis=-1)


def _layer_group(x, n_seq, seq_len, mods, mod_index, wts, gpre, conv_w, alog128, dtb128,
                 expand, rope_tabs, extra_kv, s0, tile0, into):
    pr = _proj(x, mods, mod_index, gpre, wts, rope_tabs)
    kvs = list(extra_kv) + [(pr["k"], pr["v"], seq_len)]
    omla = _attention(pr["q"], kvs, n_seq, seq_len, "attn_%d" % seq_len)
    prep = _gdn_prep(pr["qkv"], pr["ab"], conv_w, alog128, dtb128, expand, n_seq, seq_len)
    o_f, o_b, s_fin = _gdn_scan(prep, s0, n_seq, seq_len)
    t = n_seq * seq_len
    merged = _merge(x, mods, mod_index, omla, o_f.reshape(t, GDN_W), o_b.reshape(t, GDN_W), pr, wts,
                    tile0, N_TOKENS, into)
    return merged, pr, s_fin


def _state_to_pairs(s):
    b = s.shape[0]
    s = s.reshape(b, 2, N_PAIRS, 2, GDN_DK, GDN_DV)
    return jnp.transpose(s, (0, 1, 2, 4, 3, 5)).reshape(b, 2, N_PAIRS, GDN_DK, 2 * GDN_DV)


def _pairs_to_state(s):
    b = s.shape[0]
    s = s.reshape(b, 2, N_PAIRS, GDN_DK, 2, GDN_DV)
    return jnp.transpose(s, (0, 1, 2, 4, 3, 5)).reshape(b, 2, GDN_HEADS, GDN_DK, GDN_DV)


def kernel(x_prompt, x_sample, cache_ckv, cache_krope, state_delta, c, c_ctx, w_ada, b_ada, g_pre_mix,
           g_post_mix, g_pre_ffn, g_post_ffn, w_in, q_norm_g, kv_norm_g, w_uq, w_ukv, conv_w, a_log,
           dt_bias, gdn_norm_g, w_oa, w_ob, w_o, w_router, e_bias, w_gate, w_up, w_down, ws_gate, ws_up,
           ws_down):
    batch, seq, _ = x_prompt.shape
    dec_batch, dec_seq, _ = x_sample.shape
    past = cache_ckv.shape[2]
    assert batch * seq + dec_batch * dec_seq == N_TOKENS, "dispatch buffers are sized for N_TOKENS"
    y_p = x_prompt.reshape(batch * seq, D_MODEL)
    y_s = x_sample.reshape(dec_batch * dec_seq, D_MODEL)
    expand = _expand_matrix()
    rope_tabs = _rope_tables(dec_seq)
    cond8 = jnp.concatenate([c_ctx[None], c, jnp.zeros((8 - 1 - dec_batch, D_MODEL), F32)], axis=0)
    ckv_out, krope_out, state_out = [], [], []
    for l in range(DEPTH):
        wts = _prepare_weights(w_in[l], q_norm_g[l], kv_norm_g[l], w_uq[l], w_ukv[l], w_oa[l], w_ob[l],
                               w_o[l], g_post_mix[l], g_pre_ffn[l], g_post_ffn[l], gdn_norm_g[l],
                               w_router[l], e_bias[l], w_gate[l], w_up[l], w_down[l], ws_gate[l],
                               ws_up[l], ws_down[l])
        gpre = g_pre_mix[l].reshape(1, D_MODEL)
        alog128 = _pad_lanes(a_log[l])
        dtb128 = _pad_lanes(dt_bias[l])
        mods = _mods(cond8, w_ada[l], b_ada[l].reshape(1, -1)).reshape(8, 6, D_MODEL)

        zero_state = jnp.zeros((batch, 2, N_PAIRS, GDN_DK, LANES), F32)
        merged_p, pr_p, s_fin = _layer_group(
            y_p, batch, seq, mods, lambda i: 0, wts, gpre, conv_w[l], alog128, dtb128,
            expand, None, [], zero_state, 0, None)
        ckv_out.append(pr_p["ckv"].reshape(batch, seq, KV_LORA))
        krope_out.append(pr_p["kr"][:, QK_NOPE:MLA_QK].reshape(batch, seq, QK_ROPE))
        state_out.append(_pairs_to_state(s_fin))

        kr_ctx = _rope_block(cache_krope[:, l].reshape(dec_batch * past, QK_ROPE))
        k_ctx, v_ctx = _cache_kv(cache_ckv[:, l].reshape(dec_batch * past, KV_LORA), kr_ctx, wts)
        tiles_per_seq = dec_seq // ROW_TILE
        merged_s, _, _ = _layer_group(
            y_s, dec_batch, dec_seq, mods, lambda i: 1 + i // tiles_per_seq,
            wts, gpre, conv_w[l], alog128, dtb128, expand,
            rope_tabs, [(k_ctx, v_ctx, past)], _state_to_pairs(state_delta[:, l]),
            batch * seq // ROW_TILE, merged_p)

        x1, h2, gates_t, cnt = merged_s
        ctx_blocks = batch * seq // MOE_BLOCK
        blocks_per_seq = dec_seq // MOE_BLOCK
        y = _moe(h2, gates_t, cnt, x1, mods,
                 lambda b: jnp.where(b < ctx_blocks, 0, 1 + (b - ctx_blocks) // blocks_per_seq), wts)
        y_p, y_s = y[:batch * seq], y[batch * seq:]
    new_ckv = jnp.stack(ckv_out, axis=1)
    new_krope = jnp.stack(krope_out, axis=1)
    new_state = jnp.stack(state_out, axis=1)
    return (y_p.reshape(batch, seq, D_MODEL), y_s.reshape(dec_batch, dec_seq, D_MODEL),
            new_ckv, new_krope, new_state)
```

```python
import functools

import numpy as np
import jax
import jax.numpy as jnp
from jax import lax
from jax.experimental import pallas as pl
from jax.experimental.pallas import tpu as pltpu

F32 = jnp.float32
BF16 = jnp.bfloat16

D_MODEL = 1024
DEPTH = 1
GRID_W = 64
MLA_HEADS = 8
QK_NOPE = 64
QK_ROPE = 32
V_HEAD = 64
Q_LORA = 256
KV_LORA = 256
ROPE_THETA = 10000.0
GDN_HEADS = 8
GDN_DK = 64
GDN_DV = 64
CONV_K = 5
CHUNK = 64
N_EXPERTS = 64
TOP_K = 8
N_GROUPS = 8
TOPK_GROUPS = 4
D_EXPERT = 256
D_SHARED = 256
ROUTED_SCALE = 2.5
EPS = 1e-6

LANES = 128
MLA_QK = QK_NOPE + QK_ROPE
GDN_W = GDN_HEADS * GDN_DK
N_PAIRS = GDN_HEADS // 2
HEAD_PAD = LANES

_SEG = {}
_off = 0
for _name, _width in (("cq", Q_LORA), ("ckv", KV_LORA), ("kr", LANES), ("krs", LANES),
                      ("qkv", 3 * GDN_W), ("z", GDN_W), ("ab", LANES),
                      ("ga", D_MODEL), ("gb", D_MODEL)):
    _SEG[_name] = (_off, _off + _width)
    _off += _width
N_CAT = _off

ROW_TILE = 256
N_TOKENS = 8192
MOE_BLOCK = ROW_TILE
PIECE = 16
GROUP_ROWS = 256
GROUP_PIECES = GROUP_ROWS // PIECE
EXP_TILE = 512
TILE_PIECES = EXP_TILE // PIECE
N_BLOCKS = N_TOKENS // MOE_BLOCK
STAGE_PIECES = -(-((MOE_BLOCK * TOP_K + N_EXPERTS * (PIECE - 1)) // PIECE) // GROUP_PIECES) * GROUP_PIECES
N_EXP_TILES = -(-((N_TOKENS * TOP_K + N_BLOCKS * N_EXPERTS * (PIECE - 1)) // PIECE
                  + N_EXPERTS * (TILE_PIECES - 1)) // TILE_PIECES)
DUMP_PIECE0 = N_EXP_TILES * TILE_PIECES
DISPATCH_ROWS = (DUMP_PIECE0 + max(N_BLOCKS * GROUP_PIECES, TILE_PIECES)) * PIECE
DISPATCH_W = D_MODEL + LANES
PREP_GROUP = 8
PREP_ROWS = 1024
SCAN_SEQS = 2
VMEM_LIMIT = 56 * 1024 * 1024


def _dot(a, b):
    return jnp.dot(a, b, preferred_element_type=F32)


def _dot_nt(a, b):
    return lax.dot_general(a, b, (((1,), (1,)), ((), ())), preferred_element_type=F32)


def _dot_tn(a, b):
    return lax.dot_general(a, b, (((0,), (0,)), ((), ())), preferred_element_type=F32)


def _rms(x, g):
    return x * lax.rsqrt(jnp.mean(x * x, axis=-1, keepdims=True) + EPS) * g


def _silu(x):
    return x * jax.nn.sigmoid(x)


def _cparams(sem):
    return pltpu.CompilerParams(dimension_semantics=sem, vmem_limit_bytes=VMEM_LIMIT)


def _mods_kernel(c_ref, w_ref, b_ref, o_ref):
    s = _silu(c_ref[...]).astype(BF16)
    o_ref[...] = _dot(s, w_ref[...].astype(BF16)) + b_ref[...]


def _mods(cond8, w_ada, b_ada):
    n = w_ada.shape[1]
    bn = 512
    return pl.pallas_call(
        _mods_kernel,
        out_shape=jax.ShapeDtypeStruct((8, n), F32),
        grid=(n // bn,),
        in_specs=[pl.BlockSpec((8, D_MODEL), lambda j: (0, 0)),
                  pl.BlockSpec((D_MODEL, bn), lambda j: (0, j)),
                  pl.BlockSpec((1, bn), lambda j: (0, j))],
        out_specs=pl.BlockSpec((8, bn), lambda j: (0, j)),
        compiler_params=_cparams(("parallel",)),
        name="mods",
    )(cond8, w_ada, b_ada)


def _proj_kernel(rope, x_ref, m_ref, gpre_ref, wcat_ref, qg_ref, kvg_ref, wuq_ref, wuqs_ref,
                 wuk_ref, wuv_ref, cos_ref, sin_ref,
                 q_ref, k_ref, v_ref, ckv_ref, kr_ref, qkv_ref, z_ref, ab_ref, ga_ref, gb_ref):
    m = m_ref[0]
    h = (_rms(x_ref[...], gpre_ref[...]) * (1.0 + m[1:2]) + m[0:1]).astype(BF16)

    def seg(name):
        a, b = _SEG[name]
        return _dot(h, wcat_ref[:, a:b])

    qkv_ref[...] = seg("qkv")
    z_ref[...] = seg("z")
    ab_ref[...] = seg("ab")
    ga_ref[...] = seg("ga")
    gb_ref[...] = seg("gb")

    qn = _rms(seg("cq"), qg_ref[...]).astype(BF16)
    ckv = _rms(seg("ckv"), kvg_ref[...])
    ckv_ref[...] = ckv
    ckv_b = ckv.astype(BF16)
    kr = seg("kr")
    kr_ref[...] = kr
    qm = _dot(qn, wuq_ref[...])
    kk = _dot(ckv_b, wuk_ref[...])
    v_ref[...] = _dot(ckv_b, wuv_ref[...]).astype(BF16)
    scale = MLA_QK ** -0.5
    if rope:
        cos = cos_ref[...]
        sin = sin_ref[...]
        qs = _dot(qn, wuqs_ref[...])
        kr = kr * cos + seg("krs") * sin
    for hd in range(MLA_HEADS):
        sl = slice(hd * HEAD_PAD, (hd + 1) * HEAD_PAD)
        qh = qm[:, sl]
        if rope:
            qh = qh * cos + qs[:, sl] * sin
        q_ref[:, sl] = (qh * scale).astype(BF16)
        k_ref[:, sl] = (kk[:, sl] + kr).astype(BF16)


def _proj(x, mods, mod_index, gpre, wts, rope_tabs):
    t = x.shape[0]
    tm = ROW_TILE
    rope = rope_tabs is not None
    if rope:
        cos, sin = rope_tabs
        n_rope_blocks = cos.shape[0] // tm
        rope_spec = pl.BlockSpec((tm, LANES), lambda i: (i % n_rope_blocks, 0))
    else:
        cos = sin = jnp.zeros((8, LANES), F32)
        rope_spec = pl.BlockSpec((8, LANES), lambda i: (0, 0))

    def full(a):
        return pl.BlockSpec(a.shape, lambda i: (0,) * a.ndim)

    def rows(w):
        return pl.BlockSpec((tm, w), lambda i: (i, 0))

    out_widths = (("q", MLA_HEADS * HEAD_PAD, BF16), ("k", MLA_HEADS * HEAD_PAD, BF16),
                  ("v", MLA_HEADS * V_HEAD, BF16), ("ckv", KV_LORA, F32), ("kr", LANES, F32),
                  ("qkv", 3 * GDN_W, F32), ("z", GDN_W, F32), ("ab", LANES, F32),
                  ("ga", D_MODEL, F32), ("gb", D_MODEL, F32))
    outs = pl.pallas_call(
        functools.partial(_proj_kernel, rope),
        out_shape=[jax.ShapeDtypeStruct((t, w), dt) for _, w, dt in out_widths],
        grid=(t // tm,),
        in_specs=[rows(D_MODEL),
                  pl.BlockSpec((1, 6, D_MODEL), lambda i: (mod_index(i), 0, 0)),
                  full(gpre), full(wts["wcat"]), full(wts["qg"]), full(wts["kvg"]),
                  full(wts["wuq"]), full(wts["wuqs"]), full(wts["wuk"]), full(wts["wuv"]),
                  rope_spec, rope_spec],
        out_specs=[rows(w) for _, w, _ in out_widths],
        compiler_params=_cparams(("parallel",)),
        name="proj_rope" if rope else "proj",
    )(x, mods, gpre, wts["wcat"], wts["qg"], wts["kvg"], wts["wuq"], wts["wuqs"],
      wts["wuk"], wts["wuv"], cos, sin)
    return {name: o for (name, _, _), o in zip(out_widths, outs)}


def _cache_kv_kernel(ckv_ref, kr_ref, wuk_ref, wuv_ref, k_ref, v_ref):
    c = ckv_ref[...].astype(BF16)
    kk = _dot(c, wuk_ref[...])
    v_ref[...] = _dot(c, wuv_ref[...]).astype(BF16)
    kr = kr_ref[...]
    for hd in range(MLA_HEADS):
        sl = slice(hd * HEAD_PAD, (hd + 1) * HEAD_PAD)
        k_ref[:, sl] = (kk[:, sl] + kr).astype(BF16)


def _cache_kv(ckv, kr128, wts):
    t = ckv.shape[0]
    tm = 512
    return pl.pallas_call(
        _cache_kv_kernel,
        out_shape=[jax.ShapeDtypeStruct((t, MLA_HEADS * HEAD_PAD), BF16),
                   jax.ShapeDtypeStruct((t, MLA_HEADS * V_HEAD), BF16)],
        grid=(t // tm,),
        in_specs=[pl.BlockSpec((tm, KV_LORA), lambda i: (i, 0)),
                  pl.BlockSpec((tm, LANES), lambda i: (i, 0)),
                  pl.BlockSpec(wts["wuk"].shape, lambda i: (0, 0)),
                  pl.BlockSpec(wts["wuv"].shape, lambda i: (0, 0))],
        out_specs=[pl.BlockSpec((tm, MLA_HEADS * HEAD_PAD), lambda i: (i, 0)),
                   pl.BlockSpec((tm, MLA_HEADS * V_HEAD), lambda i: (i, 0))],
        compiler_params=_cparams(("parallel",)),
        name="cache_kv",
    )(ckv, kr128, wts["wuk"], wts["wuv"])


def _attn_kernel(n_kv, q_ref, *refs):
    k_refs = refs[:n_kv]
    v_refs = refs[n_kv:2 * n_kv]
    o_ref = refs[2 * n_kv]
    lane = lax.broadcasted_iota(jnp.int32, (1, LANES), 1)
    low = lane < V_HEAD
    for pr in range(MLA_HEADS // 2):
        halves = []
        for hd in (2 * pr, 2 * pr + 1):
            sl = slice(hd * HEAD_PAD, (hd + 1) * HEAD_PAD)
            qh = q_ref[:, sl]
            scores = [_dot_nt(qh, kr[:, sl]) for kr in k_refs]
            mx = functools.reduce(jnp.maximum, [jnp.max(s, axis=-1, keepdims=True) for s in scores])
            ps = [jnp.exp(s - mx) for s in scores]
            den = functools.reduce(jnp.add, [jnp.sum(p, axis=-1, keepdims=True) for p in ps])
            vsl = slice(pr * LANES, (pr + 1) * LANES)
            acc = functools.reduce(jnp.add, [_dot(p.astype(BF16), vr[:, vsl]) for p, vr in zip(ps, v_refs)])
            halves.append(acc / den)
        o_ref[:, pr * LANES:(pr + 1) * LANES] = jnp.where(low, halves[0], halves[1]).astype(BF16)


def _attention(q, kvs, n_seq, seq_len, name):
    tq = ROW_TILE
    nq = seq_len // tq
    n_kv = len(kvs)
    in_specs = [pl.BlockSpec((tq, MLA_HEADS * HEAD_PAD), lambda b, j: (b * nq + j, 0))]
    in_specs += [pl.BlockSpec((rows, MLA_HEADS * HEAD_PAD), lambda b, j: (b, 0)) for _, _, rows in kvs]
    in_specs += [pl.BlockSpec((rows, MLA_HEADS * V_HEAD), lambda b, j: (b, 0)) for _, _, rows in kvs]
    return pl.pallas_call(
        functools.partial(_attn_kernel, n_kv),
        out_shape=jax.ShapeDtypeStruct((n_seq * seq_len, MLA_HEADS * V_HEAD), BF16),
        grid=(n_seq, nq),
        in_specs=in_specs,
        out_specs=pl.BlockSpec((tq, MLA_HEADS * V_HEAD), lambda b, j: (b * nq + j, 0)),
        compiler_params=_cparams(("parallel", "parallel")),
        name=name,
    )(q, *[k for k, _, _ in kvs], *[v for _, v, _ in kvs])


def _pair_masks():
    lane = lax.broadcasted_iota(jnp.int32, (1, LANES), 1)
    return lane < GDN_DK


def _stack(x, low):
    zero = jnp.zeros_like(x)
    return jnp.concatenate([jnp.where(low, x, zero), jnp.where(low, zero, x)], axis=0)


def _split3(x):
    hi = x.astype(BF16)
    r = x - hi.astype(F32)
    mid = r.astype(BF16)
    lo = (r - mid.astype(F32)).astype(BF16)
    return hi, mid, lo


def _gdn_prep_kernel(seq_len, q_ref, k_ref, v_ref, cwq_ref, cwk_ref, cwv_ref, ab_ref, alog_ref, dtb_ref, e_ref,
                     uf_ref, ub_ref, wf_ref, wb_ref, af_ref, abk_ref, qdf_ref, qdb_ref, kdf_ref, kdb_ref,
                     glf_ref, glb_ref,
                     qn_s, kn_s, vn_s, gcb_s, gf_s):
    seq = q_ref.shape[0]
    n_chunks = seq // CHUNK
    low = _pair_masks()
    row = lax.broadcasted_iota(jnp.int32, (seq, 1), 0) % seq_len
    lane = lax.broadcasted_iota(jnp.int32, (1, LANES), 1)

    def conv(x_ref, cw_ref):
        x = x_ref[...]
        acc = jnp.zeros_like(x)
        for j in range(CONV_K):
            sh = CONV_K // 2 - j
            xs = x if sh == 0 else pltpu.roll(x, sh % seq, axis=0)
            src = row - sh
            valid = (src >= 0) & (src < seq_len)
            acc = acc + jnp.where(valid, xs, 0.0) * cw_ref[j:j + 1, :]
        return _silu(acc)

    def l2n(x):
        sq = x * x
        s0 = jnp.sum(jnp.where(low, sq, 0.0), axis=-1, keepdims=True)
        s1 = jnp.sum(jnp.where(low, 0.0, sq), axis=-1, keepdims=True)
        return x * lax.rsqrt(jnp.where(low, s0, s1) + EPS)

    qn_s[...] = l2n(conv(q_ref, cwq_ref)) * (GDN_DK ** -0.5)
    kn_s[...] = l2n(conv(k_ref, cwk_ref))
    vn_s[...] = conv(v_ref, cwv_ref)

    a = ab_ref[...]
    xg = a + dtb_ref[...]
    softplus = jnp.maximum(xg, 0.0) + jnp.log(1.0 + jnp.exp(-jnp.abs(xg)))
    act = jnp.where(lane < 2 * GDN_HEADS, -jnp.exp(alog_ref[...]) * softplus, jax.nn.sigmoid(a))

    ti = lax.broadcasted_iota(jnp.int32, (CHUNK, CHUNK), 0)
    tj = lax.broadcasted_iota(jnp.int32, (CHUNK, CHUNK), 1)
    tri_lo = (tj <= ti).astype(BF16)
    tri_up = (tj >= ti).astype(BF16)
    for c in range(n_chunks):
        ac = act[c * CHUNK:(c + 1) * CHUNK]
        pieces = _split3(ac)
        lo = functools.reduce(jnp.add, [_dot(tri_lo, pc) for pc in pieces])
        up = functools.reduce(jnp.add, [_dot(tri_up, pc) for pc in pieces])
        gcb_s[c * CHUNK:(c + 1) * CHUNK, :] = jnp.where(lane < GDN_HEADS, lo,
                                                        jnp.where(lane < 2 * GDN_HEADS, up, ac))
    expand = e_ref[0].astype(BF16)
    gf_s[...] = functools.reduce(jnp.add, [_dot(pc, expand) for pc in _split3(gcb_s[...])])

    ri = lax.broadcasted_iota(jnp.int32, (CHUNK, LANES), 0)
    cj = lax.broadcasted_iota(jnp.int32, (CHUNK, LANES), 1) % CHUNK
    eye = (ri == cj).astype(F32)
    ones_b = jnp.ones((CHUNK, LANES), BF16)
    lane0 = (lax.broadcasted_iota(jnp.int32, (1, LANES), 1) % CHUNK) == 0

    def pmm(x, y):
        return _dot(x.astype(BF16), _stack(y, low).astype(BF16))

    def pmm_split(x, y):
        xh, xm, _ = _split3(x)
        yh, ym, _ = _split3(_stack(y, low))
        return _dot(xh, yh) + (_dot(xh, ym) + _dot(xm, yh))

    out_refs = ((uf_ref, wf_ref, af_ref, qdf_ref, kdf_ref, glf_ref),
                (ub_ref, wb_ref, abk_ref, qdb_ref, kdb_ref, glb_ref))
    incl = (ri >= cj, ri <= cj)
    strict = (ri > cj, ri < cj)
    diag8 = (ri // 8) == (cj // 8)
    merge_masks = [((ri // (2 * s)) == (cj // (2 * s))) & ((ri // s) != (cj // s)) for s in (8, 16, 32)]

    def group(it, carry):
        cs = [it * PREP_GROUP + cc for cc in range(PREP_GROUP)]
        rows = [pl.ds(pl.multiple_of(c * CHUNK, CHUNK), CHUNK) for c in cs]
        qc = [qn_s[r, :] for r in rows]
        kc = [kn_s[r, :] for r in rows]
        vc = [vn_s[r, :] for r in rows]
        kst = [_stack(k, low).astype(BF16) for k in kc]
        kk = [_dot_nt(k.astype(BF16), ks) for k, ks in zip(kc, kst)]
        qk = [_dot_nt(q.astype(BF16), ks) for q, ks in zip(qc, kst)]
        chains = [(ci, d) for ci in range(PREP_GROUP) for d in range(2)]
        gc = [gf_s[rows[ci], d * LANES:(d + 1) * LANES] for ci, d in chains]
        beta = [gf_s[rows[ci], (2 + d) * LANES:(3 + d) * LANES] for ci, d in chains]
        pieces = [_split3(_stack(jnp.where(lane0, g, 0.0), low)) for g in gc]
        gr = [functools.reduce(jnp.add, [_dot_nt(ones_b, pc) for pc in ps]) for ps in pieces]
        dm = [jnp.exp(jnp.where(incl[d], g - r, -jnp.inf)) for (ci, d), g, r in zip(chains, gc, gr)]
        lm = [jnp.where(strict[d], b * kk[ci] * m, 0.0) for (ci, d), b, m in zip(chains, beta, dm)]
        aint = [(qk[ci] * m).astype(BF16) for (ci, d), m in zip(chains, dm)]
        x = [-jnp.where(diag8, l, 0.0) for l in lm]
        t = [eye + xx for xx in x]
        for _ in range(2):
            x = [pmm(xx, xx) for xx in x]
            t = [tt + pmm(tt, xx) for tt, xx in zip(t, x)]
        for off in merge_masks:
            tc = [pmm(tt, jnp.where(off, l, 0.0)) for tt, l in zip(t, lm)]
            t = [tt - pmm(a, tt) for tt, a in zip(t, tc)]
        resid = [eye - pmm_split(eye + l, tt) for l, tt in zip(lm, t)]
        t = [tt + pmm(tt, rr) for tt, rr in zip(t, resid)]
        egc = [jnp.exp(g) for g in gc]
        u = [pmm(tt, vc[ci] * b) for (ci, d), tt, b in zip(chains, t, beta)]
        w = [pmm(tt, kc[ci] * b * e).astype(BF16) for (ci, d), tt, b, e in zip(chains, t, beta, egc)]
        qd = [(qc[ci] * e).astype(BF16) for (ci, d), e in zip(chains, egc)]
        gtot = [g[CHUNK - 1:CHUNK, :] if d == 0 else g[0:1, :] for (ci, d), g in zip(chains, gc)]
        kd = [(kc[ci] * jnp.exp(gt - g)).astype(BF16) for (ci, d), gt, g in zip(chains, gtot, gc)]
        for n, (ci, d) in enumerate(chains):
            u_ref, w_ref, a_ref, qd_ref, kd_ref, gl_ref = out_refs[d]
            u_ref[0, rows[ci], :] = u[n]
            w_ref[0, rows[ci], :] = w[n]
            a_ref[0, rows[ci], :] = aint[n]
            qd_ref[0, rows[ci], :] = qd[n]
            kd_ref[0, rows[ci], :] = kd[n]
            gl_ref[0, pl.ds(cs[ci], 1), :, :] = jnp.broadcast_to(jnp.exp(gtot[n]), (1, 8, LANES))
        return carry

    lax.fori_loop(0, n_chunks // PREP_GROUP, group, 0)


def _gdn_prep(qkv, ab, conv_w, alog128, dtb128, expand, n_seq, seq_len):
    rb = max(seq_len, PREP_ROWS)
    nb = n_seq * seq_len // rb
    n_chunks = rb // CHUNK
    col = lambda off: pl.BlockSpec((rb, LANES), lambda s, p: (s, off + p))
    cw = lambda off: pl.BlockSpec((CONV_K, LANES), lambda s, p: (0, off + p))
    vec = pl.BlockSpec((1, LANES), lambda s, p: (0, 0))
    big = lambda: pl.BlockSpec((1, rb, LANES), lambda s, p: (s, 0, p))
    glspec = lambda: pl.BlockSpec((1, n_chunks, 8, LANES), lambda s, p: (s, 0, 0, p))
    shp = lambda dt: jax.ShapeDtypeStruct((nb, rb, GDN_W), dt)
    glshp = jax.ShapeDtypeStruct((nb, n_chunks, 8, GDN_W), F32)
    outs = pl.pallas_call(
        functools.partial(_gdn_prep_kernel, seq_len),
        out_shape=[shp(F32), shp(F32)] + [shp(BF16)] * 8 + [glshp, glshp],
        grid=(nb, N_PAIRS),
        in_specs=[col(0), col(N_PAIRS), col(2 * N_PAIRS), cw(0), cw(N_PAIRS), cw(2 * N_PAIRS),
                  pl.BlockSpec((rb, LANES), lambda s, p: (s, 0)), vec, vec,
                  pl.BlockSpec((1, LANES, 4 * LANES), lambda s, p: (p, 0, 0))],
        out_specs=[big() for _ in range(10)] + [glspec(), glspec()],
        scratch_shapes=[pltpu.VMEM((rb, LANES), F32)] * 4 + [pltpu.VMEM((rb, 4 * LANES), F32)],
        compiler_params=_cparams(("parallel", "parallel")),
        name="gdn_prep_%d" % seq_len,
    )(qkv, qkv, qkv, conv_w, conv_w, conv_w, ab, alog128, dtb128, expand)
    per_seq = [o.reshape(n_seq, seq_len, GDN_W) for o in outs[:10]]
    return per_seq + [o.reshape(n_seq, seq_len // CHUNK, 8, GDN_W) for o in outs[10:]]


def _gdn_scan_kernel(uf_ref, ub_ref, wf_ref, wb_ref, af_ref, abk_ref, qdf_ref, qdb_ref, kdf_ref, kdb_ref,
                     glf_ref, glb_ref, s0_ref, of_ref, ob_ref, sfin_ref, state):
    step = pl.program_id(1)
    n_steps = pl.num_programs(1)
    low = _pair_masks()
    chains = [(d, j, p) for d in range(2) for j in range(SCAN_SEQS) for p in range(N_PAIRS)]

    first = step == 0
    per_dir = ((uf_ref, wf_ref, af_ref, qdf_ref, kdf_ref, glf_ref, of_ref),
               (ub_ref, wb_ref, abk_ref, qdb_ref, kdb_ref, glb_ref, ob_ref))
    def rd(k, d, j, p):
        return per_dir[d][k][j, :, p * LANES:(p + 1) * LANES]

    s = [jnp.where(first, _stack(s0_ref[j, d, p], low), state[idx]) for idx, (d, j, p) in enumerate(chains)]
    sb = [x.astype(BF16) for x in s]
    ws = [_dot(rd(1, *c), b) for c, b in zip(chains, sb)]
    qs = [_dot(rd(3, *c), b) for c, b in zip(chains, sb)]
    vst = [_stack(rd(0, *c) - w, low).astype(BF16) for c, w in zip(chains, ws)]
    upd = [_dot_tn(_stack(rd(4, *c), low), v) for c, v in zip(chains, vst)]
    intra = [_dot(rd(2, *c), v) for c, v in zip(chains, vst)]
    for idx, (d, j, p) in enumerate(chains):
        sl = slice(p * LANES, (p + 1) * LANES)
        state[idx] = s[idx] * per_dir[d][5][j, 0, 0:1, sl] + upd[idx]
        per_dir[d][6][j, :, sl] = qs[idx] + intra[idx]

    @pl.when(step == n_steps - 1)
    def _():
        for idx, (d, j, p) in enumerate(chains):
            s = state[idx]
            sfin_ref[j, d, p] = s[:GDN_DK] + s[GDN_DK:]


def _gdn_scan(prep, s0, n_seq, seq_len):
    n_chunks = seq_len // CHUNK
    ns = SCAN_SEQS
    fwd = lambda: pl.BlockSpec((ns, CHUNK, GDN_W), lambda g, i: (g, i, 0))
    bwd = lambda: pl.BlockSpec((ns, CHUNK, GDN_W), lambda g, i: (g, n_chunks - 1 - i, 0))
    glf = pl.BlockSpec((ns, 1, 8, GDN_W), lambda g, i: (g, i, 0, 0))
    glb = pl.BlockSpec((ns, 1, 8, GDN_W), lambda g, i: (g, n_chunks - 1 - i, 0, 0))
    st = lambda: pl.BlockSpec((ns, 2, N_PAIRS, GDN_DK, LANES), lambda g, i: (g, 0, 0, 0, 0))
    oshape = jax.ShapeDtypeStruct((n_seq, seq_len, GDN_W), F32)
    return pl.pallas_call(
        _gdn_scan_kernel,
        out_shape=[oshape, oshape, jax.ShapeDtypeStruct((n_seq, 2, N_PAIRS, GDN_DK, LANES), F32)],
        grid=(n_seq // ns, n_chunks),
        in_specs=[fwd(), bwd()] * 5 + [glf, glb, st()],
        out_specs=[fwd(), bwd(), st()],
        scratch_shapes=[pltpu.VMEM((2 * ns * N_PAIRS, LANES, LANES), F32)],
        compiler_params=_cparams(("parallel", "arbitrary")),
        name="gdn_scan_%d" % seq_len,
    )(*prep, s0)


def _route(sel, s):
    per_group = N_EXPERTS // N_GROUPS
    ninf = -jnp.inf
    sub = lax.broadcasted_iota(jnp.int32, sel.shape, 1).astype(F32)
    gid = lax.broadcasted_iota(jnp.int32, (N_GROUPS, 1, sel.shape[2]), 0).astype(F32)
    m1 = jnp.max(sel, axis=1, keepdims=True)
    i1 = jnp.min(jnp.where(sel == m1, sub, float(per_group)), axis=1, keepdims=True)
    m2 = jnp.max(jnp.where(sub == i1, ninf, sel), axis=1, keepdims=True)
    work = m1 + m2
    gmask = jnp.zeros(work.shape, jnp.bool_)
    for _ in range(TOPK_GROUPS):
        m = jnp.max(work, axis=0, keepdims=True)
        idx = jnp.min(jnp.where(work == m, gid, float(N_GROUPS)), axis=0, keepdims=True)
        pick = gid == idx
        gmask = gmask | pick
        work = jnp.where(pick, ninf, work)
    work = jnp.where(gmask, sel, ninf)
    eid = gid * per_group + sub
    chosen = jnp.zeros(sel.shape, jnp.bool_)
    for _ in range(TOP_K):
        m = jnp.max(jnp.max(work, axis=1, keepdims=True), axis=0, keepdims=True)
        idx = jnp.min(jnp.min(jnp.where(work == m, eid, float(N_EXPERTS)), axis=1, keepdims=True),
                      axis=0, keepdims=True)
        pick = eid == idx
        chosen = chosen | pick
        work = jnp.where(pick, ninf, work)
    wk = jnp.where(chosen, s, 0.0)
    den = jnp.sum(jnp.sum(wk, axis=1, keepdims=True), axis=0, keepdims=True)
    return wk / den * ROUTED_SCALE


def _merge_kernel(x_ref, m_ref, omla_ref, of_ref, ob_ref, z_ref, ga_ref, gb_ref,
                  woa_ref, wob_ref, wo_ref, gpost_ref, gpre_ref, gdng_ref, wr_ref, eb_ref,
                  x1_ref, h2_ref, gates_ref, cnt_ref):
    m = m_ref[0]
    low = _pair_masks()
    o = of_ref[...] + ob_ref[...]
    z = z_ref[...]
    parts = []
    for p in range(N_PAIRS):
        sl = slice(p * LANES, (p + 1) * LANES)
        op = o[:, sl]
        sq = op * op
        s0 = jnp.sum(jnp.where(low, sq, 0.0), axis=-1, keepdims=True)
        s1 = jnp.sum(jnp.where(low, 0.0, sq), axis=-1, keepdims=True)
        ms = jnp.where(low, s0, s1) * (1.0 / GDN_DV)
        parts.append(op * lax.rsqrt(ms + EPS) * gdng_ref[...] * _silu(z[:, sl]))
    og = jnp.concatenate(parts, axis=1).astype(BF16)
    ya = _dot(omla_ref[...], woa_ref[...])
    yb = _dot(og, wob_ref[...])
    mix = (jax.nn.sigmoid(ga_ref[...]) * ya + jax.nn.sigmoid(gb_ref[...]) * yb).astype(BF16)
    y = _dot(mix, wo_ref[...])
    x1 = x_ref[...] + m[2:3] * _rms(y, gpost_ref[...])
    x1_ref[...] = x1
    h2 = _rms(x1, gpre_ref[...]) * (1.0 + m[4:5]) + m[3:4]
    h2_ref[...] = h2.astype(BF16)
    logits = lax.dot_general(wr_ref[...], h2, (((1,), (1,)), ((), ())),
                             preferred_element_type=F32, precision=lax.Precision.HIGHEST)
    s = jax.nn.sigmoid(logits)
    sel = s + eb_ref[...]
    tm = s.shape[1]
    shape3 = (N_GROUPS, N_EXPERTS // N_GROUPS, tm)
    gates_t = _route(sel.reshape(shape3), s.reshape(shape3)).reshape(N_EXPERTS, tm)
    gates_ref[...] = gates_t
    cnt_ref[0] = jnp.sum((gates_t > 0.0).astype(F32), axis=1, keepdims=True)


def _merge_kernel_into(*refs):
    n_in = 16
    _merge_kernel(*refs[:n_in], *refs[n_in + 4:])


def _merge(x, mods, mod_index, omla, o_f, o_b, pr, wts, tile0, total, into=None):
    t = x.shape[0]
    tm = ROW_TILE

    def full(a):
        return pl.BlockSpec(a.shape, lambda i: (0,) * a.ndim)

    def rows(w):
        return pl.BlockSpec((tm, w), lambda i: (i, 0))

    def out_rows(w):
        return pl.BlockSpec((tm, w), lambda i: (tile0 + i, 0))

    names = ("woa", "wob", "wo", "gpost", "gpre2", "gdng", "wr_t", "eb")
    args = [x, mods, omla, o_f, o_b, pr["z"], pr["ga"], pr["gb"]] + [wts[n] for n in names]
    in_specs = [rows(D_MODEL), pl.BlockSpec((1, 6, D_MODEL), lambda i: (mod_index(i), 0, 0)),
                rows(MLA_HEADS * V_HEAD), rows(GDN_W), rows(GDN_W), rows(GDN_W),
                rows(D_MODEL), rows(D_MODEL)] + [full(wts[n]) for n in names]
    aliases = {}
    if into is not None:
        aliases = {len(args) + k: k for k in range(4)}
        in_specs = in_specs + [pl.BlockSpec(memory_space=pl.ANY)] * 4
        args = args + list(into)
    return pl.pallas_call(
        _merge_kernel if into is None else _merge_kernel_into,
        out_shape=[jax.ShapeDtypeStruct((total, D_MODEL), F32), jax.ShapeDtypeStruct((total, D_MODEL), BF16),
                   jax.ShapeDtypeStruct((N_EXPERTS, total), F32),
                   jax.ShapeDtypeStruct((total // tm, N_EXPERTS, 1), F32)],
        grid=(t // tm,),
        in_specs=in_specs,
        out_specs=[out_rows(D_MODEL), out_rows(D_MODEL),
                   pl.BlockSpec((N_EXPERTS, tm), lambda i: (0, tile0 + i)),
                   pl.BlockSpec((1, N_EXPERTS, 1), lambda i: (tile0 + i, 0, 0))],
        input_output_aliases=aliases,
        compiler_params=_cparams(("parallel",)),
        name="merge",
    )(*args)


TABLE_W = 256
TILE_TABLE_W = 512


def _ceil_div(x, d):
    return jnp.floor((x + (d - 1)) * (1.0 / d))


def _moe_tables_kernel(cnt_ref, cnt_t_ref, ce_ref, cb_ref, dst_ref, src_ref, ng_ref, tile_ref):
    nb = cnt_ref.shape[0]
    ppt = float(TILE_PIECES)
    ppg = float(GROUP_PIECES)
    ei = lax.broadcasted_iota(jnp.int32, (N_EXPERTS, N_EXPERTS), 0)
    ej = lax.broadcasted_iota(jnp.int32, (N_EXPERTS, N_EXPERTS), 1)
    tri = (ej <= ei).astype(BF16)

    def cumsum_experts(col):
        wide = jnp.broadcast_to(col, (N_EXPERTS, LANES))
        return functools.reduce(jnp.add, [_dot(tri, pc) for pc in _split3(wide)])[:, 0:1]

    eid = lax.broadcasted_iota(jnp.int32, (N_EXPERTS, 1), 0).astype(F32)
    pc_t = _ceil_div(cnt_t_ref[...], PIECE)
    tp = jnp.sum(pc_t, axis=1, keepdims=True)
    rp = _ceil_div(tp, TILE_PIECES) * ppt
    gs_end = cumsum_experts(rp)
    gs = gs_end - rp
    blk = lax.broadcasted_iota(jnp.int32, (1, nb), 1)
    c = lax.broadcasted_iota(jnp.int32, (1, TABLE_W), 1).astype(F32)
    for b in range(nb):
        pc = _ceil_div(cnt_ref[b], PIECE)
        seg_end = cumsum_experts(pc)
        seg = seg_end - pc
        blk_off = jnp.sum(jnp.where(blk < b, pc_t, 0.0), axis=1, keepdims=True)
        nvalid = seg_end[N_EXPERTS - 1:N_EXPERTS, :]
        ce = jnp.minimum(jnp.sum((seg_end <= c).astype(F32), axis=0, keepdims=True), N_EXPERTS - 1.0)
        onehot = eid == ce
        seg_sel = jnp.sum(jnp.where(onehot, seg, 0.0), axis=0, keepdims=True)
        base_sel = jnp.sum(jnp.where(onehot, gs + blk_off - seg, 0.0), axis=0, keepdims=True)
        valid = c < nvalid
        dump = DUMP_PIECE0 + b * GROUP_PIECES + (c - ppg * jnp.floor(c * (1.0 / ppg)))
        dst = jnp.where(valid, base_sel + c, dump)
        row = slice(b, b + 1)
        ce_ref[row, :] = ce.astype(jnp.int32)
        cb_ref[row, :] = jnp.where(valid, (c - seg_sel) * PIECE, -float(1 << 20)).astype(jnp.int32)
        dst_ref[row, :] = dst.astype(jnp.int32)
        src_ref[row, :] = jnp.where(valid, dst, dst[:, 0:1]).astype(jnp.int32)
        ng_ref[row, :] = jnp.broadcast_to(_ceil_div(nvalid, GROUP_PIECES), (1, LANES)).astype(jnp.int32)
    j = lax.broadcasted_iota(jnp.int32, (1, TILE_TABLE_W), 1).astype(F32)
    start = j * ppt
    te = jnp.minimum(jnp.sum((gs_end <= start).astype(F32), axis=0, keepdims=True), N_EXPERTS - 1.0)
    onehot = eid == te
    tp_sel = jnp.sum(jnp.where(onehot, tp, 0.0), axis=0, keepdims=True)
    gs_sel = jnp.sum(jnp.where(onehot, gs, 0.0), axis=0, keepdims=True)
    n_used = gs_end[N_EXPERTS - 1:N_EXPERTS, :] * (1.0 / ppt)
    used = j < n_used
    tv = jnp.where(used, jnp.clip((tp_sel - (start - gs_sel)) * PIECE, 0.0, float(EXP_TILE)), 0.0)
    tin = jnp.where(used, j, n_used - 1.0)
    tout = jnp.where(used, j, float(N_EXP_TILES))
    tile_ref[...] = jnp.zeros(tile_ref.shape, jnp.int32)
    for r, v in enumerate((te, tv, tin, tout)):
        tile_ref[r:r + 1, :] = v.astype(jnp.int32)


def _dispatch_tables(cnt):
    nb = cnt.shape[0]
    tab = jax.ShapeDtypeStruct((nb, TABLE_W), jnp.int32)
    ce, cb, dst, src, ng, tile = pl.pallas_call(
        _moe_tables_kernel,
        out_shape=[tab, tab, tab, tab, jax.ShapeDtypeStruct((nb, LANES), jnp.int32),
                   jax.ShapeDtypeStruct((8, TILE_TABLE_W), jnp.int32)],
        name="moe_tables",
    )(cnt, cnt[:, :, 0].T)
    return {"ce": ce, "cb": cb, "dst": dst, "src": src, "ngroups": ng, "tile": tile}


def _piece_onehot(ce_ref, cb_ref, rank_s, b, g, extra=None):
    sub = lax.broadcasted_iota(jnp.int32, (PIECE, 1), 0).astype(F32)
    ps, ex = [], []
    for cc in range(GROUP_PIECES):
        c = g * GROUP_PIECES + cc
        e = ce_ref[b, c]
        base = cb_ref[b, c].astype(F32)
        hit = rank_s[pl.ds(e, 1), :] == base + sub
        ps.append(jnp.where(hit, 1.0, 0.0).astype(BF16))
        if extra is not None:
            ex.append(jnp.sum(jnp.where(hit, extra[pl.ds(e, 1), :], 0.0), axis=-1, keepdims=True))
    p = jnp.concatenate(ps, axis=0)
    return (p, jnp.concatenate(ex, axis=0)) if extra is not None else p


def _block_ranks(gt):
    n = gt.shape[1]
    ti = lax.broadcasted_iota(jnp.int32, (n, n), 0)
    tj = lax.broadcasted_iota(jnp.int32, (n, n), 1)
    before = (ti < tj).astype(BF16)
    member = gt > 0.0
    rank = _dot(member.astype(BF16), before)
    return jnp.where(member, rank, -1.0)


def _moe_sort_kernel(ce_ref, cb_ref, dst_ref, ng_ref, h_ref, gt_ref, xg_ref, stage, rank_s, gate_s, sem):
    b = pl.program_id(0)
    slot = b % 2
    gt = gt_ref[...]
    rank_s[...] = _block_ranks(gt)
    gate_s[...] = gt
    lane = lax.broadcasted_iota(jnp.int32, (1, LANES), 1)

    def piece_copy(blk, sl, c):
        r0 = pl.multiple_of(c * PIECE, PIECE)
        d0 = pl.multiple_of(dst_ref[blk, c] * PIECE, PIECE)
        return pltpu.make_async_copy(stage.at[sl, pl.ds(r0, PIECE)], xg_ref.at[pl.ds(d0, PIECE)], sem.at[sl])

    def group(g, carry):
        p, gcol = _piece_onehot(ce_ref, cb_ref, rank_s, b, g, gate_s)
        xs = _dot(p, h_ref[...]).astype(BF16)
        hi, mid, lo = (t.astype(F32) for t in _split3(gcol))
        gblk = jnp.where(lane == 0, hi, jnp.where(lane == 1, mid, jnp.where(lane == 2, lo, 0.0)))
        r0 = pl.multiple_of(g * GROUP_ROWS, GROUP_ROWS)
        stage[slot, pl.ds(r0, GROUP_ROWS), :] = jnp.concatenate([xs, gblk.astype(BF16)], axis=1)
        for cc in range(GROUP_PIECES):
            piece_copy(b, slot, g * GROUP_PIECES + cc).start()
        return carry

    lax.fori_loop(0, ng_ref[b, 0], group, 0)

    def drain(blk, sl):
        def wait_one(c, carry):
            piece_copy(blk, sl, c).wait()
            return carry
        lax.fori_loop(0, ng_ref[blk, 0] * GROUP_PIECES, wait_one, 0)

    @pl.when(b > 0)
    def _():
        drain(b - 1, 1 - slot)

    @pl.when(b == pl.num_programs(0) - 1)
    def _():
        drain(b, slot)


def _moe_sort(h2, gates_t, tabs):
    nb = h2.shape[0] // MOE_BLOCK
    grid_spec = pltpu.PrefetchScalarGridSpec(
        num_scalar_prefetch=4, grid=(nb,),
        in_specs=[pl.BlockSpec((MOE_BLOCK, D_MODEL), lambda b, *_: (b, 0)),
                  pl.BlockSpec((N_EXPERTS, MOE_BLOCK), lambda b, *_: (0, b))],
        out_specs=pl.BlockSpec(memory_space=pl.ANY),
        scratch_shapes=[pltpu.VMEM((2, STAGE_PIECES * PIECE, DISPATCH_W), BF16),
                        pltpu.VMEM((N_EXPERTS, MOE_BLOCK), F32), pltpu.VMEM((N_EXPERTS, MOE_BLOCK), F32),
                        pltpu.SemaphoreType.DMA((2,))])
    return pl.pallas_call(
        _moe_sort_kernel,
        out_shape=jax.ShapeDtypeStruct((DISPATCH_ROWS, DISPATCH_W), BF16),
        grid_spec=grid_spec,
        compiler_params=_cparams(("arbitrary",)),
        name="moe_sort",
    )(tabs["ce"], tabs["cb"], tabs["dst"], tabs["ngroups"], h2, gates_t)


def _moe_expert_kernel(tile_ref, x_ref, wg_ref, wu_ref, wd_ref, y_ref):
    valid = tile_ref[1, pl.program_id(0)]

    @pl.when(valid > 0)
    def _():
        keep = lax.broadcasted_iota(jnp.int32, (EXP_TILE, 1), 0) < valid
        xrow = x_ref[...]
        x = jnp.where(keep, xrow[:, :D_MODEL], jnp.zeros((), BF16))
        g = jnp.sum(jnp.where(keep, xrow[:, D_MODEL:].astype(F32), 0.0), axis=-1, keepdims=True)
        hg = _dot(x, wg_ref[0].astype(BF16))
        hu = _dot(x, wu_ref[0].astype(BF16))
        act = (_silu(hg) * hu * g).astype(BF16)
        y_ref[...] = _dot(act, wd_ref[0].astype(BF16)).astype(BF16)


def _moe_expert(xg, tabs, wts):
    grid_spec = pltpu.PrefetchScalarGridSpec(
        num_scalar_prefetch=1, grid=(N_EXP_TILES,),
        in_specs=[pl.BlockSpec((EXP_TILE, DISPATCH_W), lambda j, tt: (tt[2, j], 0)),
                  pl.BlockSpec((1, D_MODEL, D_EXPERT), lambda j, tt: (tt[0, j], 0, 0)),
                  pl.BlockSpec((1, D_MODEL, D_EXPERT), lambda j, tt: (tt[0, j], 0, 0)),
                  pl.BlockSpec((1, D_EXPERT, D_MODEL), lambda j, tt: (tt[0, j], 0, 0))],
        out_specs=pl.BlockSpec((EXP_TILE, D_MODEL), lambda j, tt: (tt[3, j], 0)))
    return pl.pallas_call(
        _moe_expert_kernel,
        out_shape=jax.ShapeDtypeStruct((DISPATCH_ROWS, D_MODEL), BF16),
        grid_spec=grid_spec,
        compiler_params=_cparams(("arbitrary",)),
        name="moe_expert",
    )(tabs["tile"], xg, wts["w_gate"], wts["w_up"], wts["w_down"])


def _moe_combine_kernel(ce_ref, cb_ref, src_ref, ng_ref, yg_ref, gt_ref, h_ref, x1_ref, m_ref, gpost_ref,
                        wsg_ref, wsu_ref, wsd_ref, out_ref, stage, rank_s, acc_s, sem):
    b = pl.program_id(0)
    slot = b % 2

    def piece_copy(blk, sl, c):
        r0 = pl.multiple_of(c * PIECE, PIECE)
        s0 = pl.multiple_of(src_ref[blk, c] * PIECE, PIECE)
        return pltpu.make_async_copy(yg_ref.at[pl.ds(s0, PIECE)], stage.at[sl, pl.ds(r0, PIECE)], sem.at[sl])

    def fetch(blk, sl):
        def start_one(c, carry):
            piece_copy(blk, sl, c).start()
            return carry
        lax.fori_loop(0, ng_ref[blk, 0] * GROUP_PIECES, start_one, 0)

    @pl.when(b == 0)
    def _():
        fetch(0, 0)

    @pl.when(b + 1 < pl.num_programs(0))
    def _():
        fetch(b + 1, 1 - slot)

    rank_s[...] = _block_ranks(gt_ref[...])
    h = h_ref[...]
    sh = (_silu(_dot(h, wsg_ref[...])) * _dot(h, wsu_ref[...])).astype(BF16)
    acc_s[...] = _dot(sh, wsd_ref[...])

    def wait_one(c, carry):
        piece_copy(b, slot, c).wait()
        return carry

    lax.fori_loop(0, ng_ref[b, 0] * GROUP_PIECES, wait_one, 0)

    def group(g, carry):
        p = _piece_onehot(ce_ref, cb_ref, rank_s, b, g)
        r0 = pl.multiple_of(g * GROUP_ROWS, GROUP_ROWS)
        acc_s[...] += _dot_tn(p, stage[slot, pl.ds(r0, GROUP_ROWS), :])
        return carry

    lax.fori_loop(0, ng_ref[b, 0], group, 0)
    m = m_ref[0]
    out_ref[...] = x1_ref[...] + m[5:6] * _rms(acc_s[...], gpost_ref[...])


def _moe_combine(yg, gates_t, h2, x1, mods, mod_index, tabs, wts):
    t = h2.shape[0]
    nb = t // MOE_BLOCK

    def full(a):
        return pl.BlockSpec(a.shape, lambda b, *_: (0,) * a.ndim)

    grid_spec = pltpu.PrefetchScalarGridSpec(
        num_scalar_prefetch=4, grid=(nb,),
        in_specs=[pl.BlockSpec(memory_space=pl.ANY),
                  pl.BlockSpec((N_EXPERTS, MOE_BLOCK), lambda b, *_: (0, b)),
                  pl.BlockSpec((MOE_BLOCK, D_MODEL), lambda b, *_: (b, 0)),
                  pl.BlockSpec((MOE_BLOCK, D_MODEL), lambda b, *_: (b, 0)),
                  pl.BlockSpec((1, 6, D_MODEL), lambda b, *_: (mod_index(b), 0, 0)),
                  full(wts["gpost2"]), full(wts["wsg"]), full(wts["wsu"]), full(wts["wsd"])],
        out_specs=pl.BlockSpec((MOE_BLOCK, D_MODEL), lambda b, *_: (b, 0)),
        scratch_shapes=[pltpu.VMEM((2, STAGE_PIECES * PIECE, D_MODEL), BF16),
                        pltpu.VMEM((N_EXPERTS, MOE_BLOCK), F32), pltpu.VMEM((MOE_BLOCK, D_MODEL), F32),
                        pltpu.SemaphoreType.DMA((2,))])
    return pl.pallas_call(
        _moe_combine_kernel,
        out_shape=jax.ShapeDtypeStruct((t, D_MODEL), F32),
        grid_spec=grid_spec,
        compiler_params=_cparams(("arbitrary",)),
        name="moe_combine",
    )(tabs["ce"], tabs["cb"], tabs["src"], tabs["ngroups"], yg, gates_t, h2, x1, mods,
      wts["gpost2"], wts["wsg"], wts["wsu"], wts["wsd"])


def _moe(h2, gates_t, cnt, x1, mods, mod_index, wts):
    tabs = _dispatch_tables(cnt)
    xg = _moe_sort(h2, gates_t, tabs)
    yg = _moe_expert(xg, tabs, wts)
    return _moe_combine(yg, gates_t, h2, x1, mods, mod_index, tabs, wts)


def _rope_swap(w):
    nf = QK_ROPE // 4
    parts = [w[..., i * nf:(i + 1) * nf] for i in range(4)]
    return jnp.concatenate([parts[1], parts[0], parts[3], parts[2]], axis=-1)


def _head_block(nope, rope):
    lead = nope.shape[:-2] if nope is not None else rope.shape[:-2]
    nope = jnp.zeros(lead + (MLA_HEADS, QK_NOPE), F32) if nope is None else nope
    rope = jnp.zeros(lead + (MLA_HEADS, QK_ROPE), F32) if rope is None else rope
    pad = jnp.zeros(lead + (MLA_HEADS, HEAD_PAD - MLA_QK), F32)
    return jnp.concatenate([nope, rope, pad], axis=-1).reshape(lead + (MLA_HEADS * HEAD_PAD,))


def _rope_block(w):
    lead = w.shape[:-1]
    return jnp.concatenate([jnp.zeros(lead + (QK_NOPE,), F32), w,
                            jnp.zeros(lead + (HEAD_PAD - MLA_QK,), F32)], axis=-1)


def _rope_tables(n_tokens):
    rows = n_tokens // GRID_W
    row = np.repeat(np.arange(rows, dtype=np.float32), GRID_W)
    colv = np.tile(np.arange(GRID_W, dtype=np.float32), rows)
    nf = QK_ROPE // 4
    inv = jnp.asarray(ROPE_THETA, F32) ** (-jnp.arange(nf, dtype=F32) / nf)
    ang_r = jnp.asarray(row)[:, None] * inv
    ang_c = jnp.asarray(colv)[:, None] * inv
    cos32 = jnp.concatenate([jnp.cos(ang_r), jnp.cos(ang_r), jnp.cos(ang_c), jnp.cos(ang_c)], axis=-1)
    sin32 = jnp.concatenate([-jnp.sin(ang_r), jnp.sin(ang_r), -jnp.sin(ang_c), jnp.sin(ang_c)], axis=-1)
    ones = jnp.ones((n_tokens, QK_NOPE), F32)
    tail = jnp.zeros((n_tokens, HEAD_PAD - MLA_QK), F32)
    cos = jnp.concatenate([ones, cos32, tail], axis=-1)
    sin = jnp.concatenate([jnp.zeros((n_tokens, QK_NOPE), F32), sin32, tail], axis=-1)
    return cos, sin


def _expand_matrix():
    e = np.zeros((N_PAIRS, LANES, 4 * LANES), np.float32)
    for p in range(N_PAIRS):
        for blk in range(4):
            for hh in range(2):
                src = blk * GDN_HEADS + 2 * p + hh
                e[p, src, blk * LANES + hh * GDN_DK: blk * LANES + (hh + 1) * GDN_DK] = 1.0
    return jnp.asarray(e)


def _prepare_weights(w_in, q_norm_g, kv_norm_g, w_uq, w_ukv, w_oa, w_ob, w_o, g_post_mix, g_pre_ffn,
                     g_post_ffn, gdn_norm_g, w_router, e_bias, w_gate, w_up, w_down, ws_gate, ws_up, ws_down):
    offs = np.cumsum((Q_LORA, KV_LORA, QK_ROPE, 3 * GDN_W, GDN_W, 2 * GDN_HEADS, 2 * GDN_HEADS,
                      D_MODEL, D_MODEL))[:-1].tolist()
    cq, ckv, kr, qkv, z, a, b, ga, gb = jnp.split(w_in, offs, axis=-1)
    ab = jnp.concatenate([a, b, jnp.zeros((D_MODEL, LANES - 4 * GDN_HEADS), F32)], axis=-1)
    wcat = jnp.concatenate([cq, ckv, _rope_block(kr), _rope_block(_rope_swap(kr)), qkv, z, ab, ga, gb],
                           axis=-1).astype(BF16)
    uq = w_uq.reshape(Q_LORA, MLA_HEADS, MLA_QK)
    ukv = w_ukv.reshape(KV_LORA, MLA_HEADS, QK_NOPE + V_HEAD)
    return {
        "wcat": wcat,
        "qg": q_norm_g.reshape(1, Q_LORA), "kvg": kv_norm_g.reshape(1, KV_LORA),
        "wuq": _head_block(uq[..., :QK_NOPE], uq[..., QK_NOPE:]).astype(BF16),
        "wuqs": _head_block(None, _rope_swap(uq[..., QK_NOPE:])).astype(BF16),
        "wuk": _head_block(ukv[..., :QK_NOPE], None).astype(BF16),
        "wuv": ukv[..., QK_NOPE:].reshape(KV_LORA, MLA_HEADS * V_HEAD).astype(BF16),
        "woa": w_oa.astype(BF16), "wob": w_ob.astype(BF16), "wo": w_o.astype(BF16),
        "gpost": g_post_mix.reshape(1, D_MODEL), "gpre2": g_pre_ffn.reshape(1, D_MODEL),
        "gpost2": g_post_ffn.reshape(1, D_MODEL),
        "gdng": jnp.tile(gdn_norm_g.reshape(1, GDN_DV), (1, 2)),
        "wr_t": w_router.T, "eb": e_bias.reshape(N_EXPERTS, 1),
        "w_gate": w_gate, "w_up": w_up, "w_down": w_down,
        "wsg": ws_gate.astype(BF16), "wsu": ws_up.astype(BF16), "wsd": ws_down.astype(BF16),
    }


def _pad_lanes(v):
    v = v.reshape(1, -1)
    return jnp.concatenate([v, jnp.zeros((1, LANES - v.shape[1]), F32)], axis=-1)


def _layer_group(x, n_seq, seq_len, mods, mod_index, wts, gpre, conv_w, alog128, dtb128,
                 expand, rope_tabs, extra_kv, s0, tile0, into):
    pr = _proj(x, mods, mod_index, gpre, wts, rope_tabs)
    kvs = list(extra_kv) + [(pr["k"], pr["v"], seq_len)]
    omla = _attention(pr["q"], kvs, n_seq, seq_len, "attn_%d" % seq_len)
    prep = _gdn_prep(pr["qkv"], pr["ab"], conv_w, alog128, dtb128, expand, n_seq, seq_len)
    o_f, o_b, s_fin = _gdn_scan(prep, s0, n_seq, seq_len)
    t = n_seq * seq_len
    merged = _merge(x, mods, mod_index, omla, o_f.reshape(t, GDN_W), o_b.reshape(t, GDN_W), pr, wts,
                    tile0, N_TOKENS, into)
    return merged, pr, s_fin


def _state_to_pairs(s):
    b = s.shape[0]
    s = s.reshape(b, 2, N_PAIRS, 2, GDN_DK, GDN_DV)
    return jnp.transpose(s, (0, 1, 2, 4, 3, 5)).reshape(b, 2, N_PAIRS, GDN_DK, 2 * GDN_DV)


def _pairs_to_state(s):
    b = s.shape[0]
    s = s.reshape(b, 2, N_PAIRS, GDN_DK, 2, GDN_DV)
    return jnp.transpose(s, (0, 1, 2, 4, 3, 5)).reshape(b, 2, GDN_HEADS, GDN_DK, GDN_DV)


def kernel(x_prompt, x_sample, cache_ckv, cache_krope, state_delta, c, c_ctx, w_ada, b_ada, g_pre_mix,
           g_post_mix, g_pre_ffn, g_post_ffn, w_in, q_norm_g, kv_norm_g, w_uq, w_ukv, conv_w, a_log,
           dt_bias, gdn_norm_g, w_oa, w_ob, w_o, w_router, e_bias, w_gate, w_up, w_down, ws_gate, ws_up,
           ws_down):
    batch, seq, _ = x_prompt.shape
    dec_batch, dec_seq, _ = x_sample.shape
    past = cache_ckv.shape[2]
    assert batch * seq + dec_batch * dec_seq == N_TOKENS, "dispatch buffers are sized for N_TOKENS"
    y_p = x_prompt.reshape(batch * seq, D_MODEL)
    y_s = x_sample.reshape(dec_batch * dec_seq, D_MODEL)
    expand = _expand_matrix()
    rope_tabs = _rope_tables(dec_seq)
    cond8 = jnp.concatenate([c_ctx[None], c, jnp.zeros((8 - 1 - dec_batch, D_MODEL), F32)], axis=0)
    ckv_out, krope_out, state_out = [], [], []
    for l in range(DEPTH):
        wts = _prepare_weights(w_in[l], q_norm_g[l], kv_norm_g[l], w_uq[l], w_ukv[l], w_oa[l], w_ob[l],
                               w_o[l], g_post_mix[l], g_pre_ffn[l], g_post_ffn[l], gdn_norm_g[l],
                               w_router[l], e_bias[l], w_gate[l], w_up[l], w_down[l], ws_gate[l],
                               ws_up[l], ws_down[l])
        gpre = g_pre_mix[l].reshape(1, D_MODEL)
        alog128 = _pad_lanes(a_log[l])
        dtb128 = _pad_lanes(dt_bias[l])
        mods = _mods(cond8, w_ada[l], b_ada[l].reshape(1, -1)).reshape(8, 6, D_MODEL)

        zero_state = jnp.zeros((batch, 2, N_PAIRS, GDN_DK, LANES), F32)
        merged_p, pr_p, s_fin = _layer_group(
            y_p, batch, seq, mods, lambda i: 0, wts, gpre, conv_w[l], alog128, dtb128,
            expand, None, [], zero_state, 0, None)
        ckv_out.append(pr_p["ckv"].reshape(batch, seq, KV_LORA))
        krope_out.append(pr_p["kr"][:, QK_NOPE:MLA_QK].reshape(batch, seq, QK_ROPE))
        state_out.append(_pairs_to_state(s_fin))

        kr_ctx = _rope_block(cache_krope[:, l].reshape(dec_batch * past, QK_ROPE))
        k_ctx, v_ctx = _cache_kv(cache_ckv[:, l].reshape(dec_batch * past, KV_LORA), kr_ctx, wts)
        tiles_per_seq = dec_seq // ROW_TILE
        merged_s, _, _ = _layer_group(
            y_s, dec_batch, dec_seq, mods, lambda i: 1 + i // tiles_per_seq,
            wts, gpre, conv_w[l], alog128, dtb128, expand,
            rope_tabs, [(k_ctx, v_ctx, past)], _state_to_pairs(state_delta[:, l]),
            batch * seq // ROW_TILE, merged_p)

        x1, h2, gates_t, cnt = merged_s
        ctx_blocks = batch * seq // MOE_BLOCK
        blocks_per_seq = dec_seq // MOE_BLOCK
        y = _moe(h2, gates_t, cnt, x1, mods,
                 lambda b: jnp.where(b < ctx_blocks, 0, 1 + (b - ctx_blocks) // blocks_per_seq), wts)
        y_p, y_s = y[:batch * seq], y[batch * seq:]
    new_ckv = jnp.stack(ckv_out, axis=1)
    new_krope = jnp.stack(krope_out, axis=1)
    new_state = jnp.stack(state_out, axis=1)
    return (y_p.reshape(batch, seq, D_MODEL), y_s.reshape(dec_batch, dec_seq, D_MODEL),
            new_ckv, new_krope, new_state)
```

```python
import functools

import numpy as np
import jax
import jax.numpy as jnp
from jax import lax
from jax.experimental import pallas as pl
from jax.experimental.pallas import tpu as pltpu

F32 = jnp.float32
BF16 = jnp.bfloat16

D_MODEL = 1024
DEPTH = 1
GRID_W = 64
MLA_HEADS = 8
QK_NOPE = 64
QK_ROPE = 32
V_HEAD = 64
Q_LORA = 256
KV_LORA = 256
ROPE_THETA = 10000.0
GDN_HEADS = 8
GDN_DK = 64
GDN_DV = 64
CONV_K = 5
CHUNK = 64
N_EXPERTS = 64
TOP_K = 8
N_GROUPS = 8
TOPK_GROUPS = 4
D_EXPERT = 256
D_SHARED = 256
ROUTED_SCALE = 2.5
EPS = 1e-6

LANES = 128
LOG2_E = 1.4426950408889634
MLA_QK = QK_NOPE + QK_ROPE
GDN_W = GDN_HEADS * GDN_DK
N_PAIRS = GDN_HEADS // 2
HEAD_PAD = LANES

_SEG = {}
_off = 0
for _name, _width in (("cq", Q_LORA), ("ckv", KV_LORA), ("kr", LANES), ("krs", LANES),
                      ("qkv", 3 * GDN_W), ("z", GDN_W), ("ab", LANES),
                      ("ga", D_MODEL), ("gb", D_MODEL)):
    _SEG[_name] = (_off, _off + _width)
    _off += _width
N_CAT = _off

ROW_TILE = 256
N_TOKENS = 8192
MOE_BLOCK = ROW_TILE
PIECE = 16
GROUP_ROWS = 256
GROUP_PIECES = GROUP_ROWS // PIECE
LOOP_GROUPS = 2
LOOP_PIECES = LOOP_GROUPS * GROUP_PIECES
EXP_TILE = 512
TILE_PIECES = EXP_TILE // PIECE
N_BLOCKS = N_TOKENS // MOE_BLOCK
STAGE_PIECES = -(-((MOE_BLOCK * TOP_K + N_EXPERTS * (PIECE - 1)) // PIECE) // LOOP_PIECES) * LOOP_PIECES
N_EXP_TILES = -(-((N_TOKENS * TOP_K + N_BLOCKS * N_EXPERTS * (PIECE - 1)) // PIECE
                  + N_EXPERTS * (TILE_PIECES - 1)) // TILE_PIECES)
DUMP_PIECE0 = N_EXP_TILES * TILE_PIECES
DISPATCH_ROWS = (DUMP_PIECE0 + max(N_BLOCKS * LOOP_PIECES, TILE_PIECES)) * PIECE
DISPATCH_W = D_MODEL + LANES
PREP_GROUP = 8
PREP_ROWS = 1024
SCAN_SEQS = 2
VMEM_LIMIT = 56 * 1024 * 1024


def _dot(a, b):
    return jnp.dot(a, b, preferred_element_type=F32)


def _dot_nt(a, b):
    return lax.dot_general(a, b, (((1,), (1,)), ((), ())), preferred_element_type=F32)


def _dot_tn(a, b):
    return lax.dot_general(a, b, (((0,), (0,)), ((), ())), preferred_element_type=F32)


def _rms(x, g):
    return x * lax.rsqrt(jnp.mean(x * x, axis=-1, keepdims=True) + EPS) * g


def _silu(x):
    return x * jax.nn.sigmoid(x)


def _cparams(sem):
    return pltpu.CompilerParams(dimension_semantics=sem, vmem_limit_bytes=VMEM_LIMIT)


def _mods_kernel(c_ref, w_ref, b_ref, o_ref):
    s = _silu(c_ref[...]).astype(BF16)
    o_ref[...] = _dot(s, w_ref[...].astype(BF16)) + b_ref[...]


def _mods(cond8, w_ada, b_ada):
    n = w_ada.shape[1]
    bn = 512
    return pl.pallas_call(
        _mods_kernel,
        out_shape=jax.ShapeDtypeStruct((8, n), F32),
        grid=(n // bn,),
        in_specs=[pl.BlockSpec((8, D_MODEL), lambda j: (0, 0)),
                  pl.BlockSpec((D_MODEL, bn), lambda j: (0, j)),
                  pl.BlockSpec((1, bn), lambda j: (0, j))],
        out_specs=pl.BlockSpec((8, bn), lambda j: (0, j)),
        compiler_params=_cparams(("parallel",)),
        name="mods",
    )(cond8, w_ada, b_ada)


def _proj_kernel(rope, x_ref, m_ref, gpre_ref, wcat_ref, qg_ref, kvg_ref, wuq_ref, wuqs_ref,
                 wuk_ref, wuv_ref, cos_ref, sin_ref,
                 q_ref, k_ref, v_ref, ckv_ref, kr_ref, qkv_ref, z_ref, ab_ref, ga_ref, gb_ref):
    m = m_ref[0]
    h = (_rms(x_ref[...], gpre_ref[...]) * (1.0 + m[1:2]) + m[0:1]).astype(BF16)

    def seg(name):
        a, b = _SEG[name]
        return _dot(h, wcat_ref[:, a:b])

    qkv_ref[...] = seg("qkv")
    z_ref[...] = seg("z")
    ab_ref[...] = seg("ab")
    ga_ref[...] = seg("ga")
    gb_ref[...] = seg("gb")

    qn = _rms(seg("cq"), qg_ref[...]).astype(BF16)
    ckv = _rms(seg("ckv"), kvg_ref[...])
    ckv_ref[...] = ckv
    ckv_b = ckv.astype(BF16)
    kr = seg("kr")
    kr_ref[...] = kr
    qm = _dot(qn, wuq_ref[...])
    kk = _dot(ckv_b, wuk_ref[...])
    v_ref[...] = _dot(ckv_b, wuv_ref[...]).astype(BF16)
    scale = MLA_QK ** -0.5 * LOG2_E
    if rope:
        cos = cos_ref[...]
        sin = sin_ref[...]
        qs = _dot(qn, wuqs_ref[...])
        kr = kr * cos + seg("krs") * sin
    for hd in range(MLA_HEADS):
        sl = slice(hd * HEAD_PAD, (hd + 1) * HEAD_PAD)
        qh = qm[:, sl]
        if rope:
            qh = qh * cos + qs[:, sl] * sin
        q_ref[:, sl] = (qh * scale).astype(BF16)
        k_ref[:, sl] = (kk[:, sl] + kr).astype(BF16)


def _proj(x, mods, mod_index, gpre, wts, rope_tabs):
    t = x.shape[0]
    tm = ROW_TILE
    rope = rope_tabs is not None
    if rope:
        cos, sin = rope_tabs
        n_rope_blocks = cos.shape[0] // tm
        rope_spec = pl.BlockSpec((tm, LANES), lambda i: (i % n_rope_blocks, 0))
    else:
        cos = sin = jnp.zeros((8, LANES), F32)
        rope_spec = pl.BlockSpec((8, LANES), lambda i: (0, 0))

    def full(a):
        return pl.BlockSpec(a.shape, lambda i: (0,) * a.ndim)

    def rows(w):
        return pl.BlockSpec((tm, w), lambda i: (i, 0))

    out_widths = (("q", MLA_HEADS * HEAD_PAD, BF16), ("k", MLA_HEADS * HEAD_PAD, BF16),
                  ("v", MLA_HEADS * V_HEAD, BF16), ("ckv", KV_LORA, F32), ("kr", LANES, F32),
                  ("qkv", 3 * GDN_W, F32), ("z", GDN_W, F32), ("ab", LANES, F32),
                  ("ga", D_MODEL, F32), ("gb", D_MODEL, F32))
    outs = pl.pallas_call(
        functools.partial(_proj_kernel, rope),
        out_shape=[jax.ShapeDtypeStruct((t, w), dt) for _, w, dt in out_widths],
        grid=(t // tm,),
        in_specs=[rows(D_MODEL),
                  pl.BlockSpec((1, 6, D_MODEL), lambda i: (mod_index(i), 0, 0)),
                  full(gpre), full(wts["wcat"]), full(wts["qg"]), full(wts["kvg"]),
                  full(wts["wuq"]), full(wts["wuqs"]), full(wts["wuk"]), full(wts["wuv"]),
                  rope_spec, rope_spec],
        out_specs=[rows(w) for _, w, _ in out_widths],
        compiler_params=_cparams(("parallel",)),
        name="proj_rope" if rope else "proj",
    )(x, mods, gpre, wts["wcat"], wts["qg"], wts["kvg"], wts["wuq"], wts["wuqs"],
      wts["wuk"], wts["wuv"], cos, sin)
    return {name: o for (name, _, _), o in zip(out_widths, outs)}


def _cache_kv_kernel(ckv_ref, kr_ref, wuk_ref, wuv_ref, k_ref, v_ref):
    c = ckv_ref[...].astype(BF16)
    kk = _dot(c, wuk_ref[...])
    v_ref[...] = _dot(c, wuv_ref[...]).astype(BF16)
    kr = kr_ref[...]
    for hd in range(MLA_HEADS):
        sl = slice(hd * HEAD_PAD, (hd + 1) * HEAD_PAD)
        k_ref[:, sl] = (kk[:, sl] + kr).astype(BF16)


def _cache_kv(ckv, kr128, wts):
    t = ckv.shape[0]
    tm = 512
    return pl.pallas_call(
        _cache_kv_kernel,
        out_shape=[jax.ShapeDtypeStruct((t, MLA_HEADS * HEAD_PAD), BF16),
                   jax.ShapeDtypeStruct((t, MLA_HEADS * V_HEAD), BF16)],
        grid=(t // tm,),
        in_specs=[pl.BlockSpec((tm, KV_LORA), lambda i: (i, 0)),
                  pl.BlockSpec((tm, LANES), lambda i: (i, 0)),
                  pl.BlockSpec(wts["wuk"].shape, lambda i: (0, 0)),
                  pl.BlockSpec(wts["wuv"].shape, lambda i: (0, 0))],
        out_specs=[pl.BlockSpec((tm, MLA_HEADS * HEAD_PAD), lambda i: (i, 0)),
                   pl.BlockSpec((tm, MLA_HEADS * V_HEAD), lambda i: (i, 0))],
        compiler_params=_cparams(("parallel",)),
        name="cache_kv",
    )(ckv, kr128, wts["wuk"], wts["wuv"])


def _attn_kernel(n_kv, q_ref, *refs):
    k_refs = refs[:n_kv]
    v_refs = refs[n_kv:2 * n_kv]
    o_ref = refs[2 * n_kv]
    lane = lax.broadcasted_iota(jnp.int32, (1, LANES), 1)
    low = lane < V_HEAD
    for pr in range(MLA_HEADS // 2):
        halves = []
        for hd in (2 * pr, 2 * pr + 1):
            sl = slice(hd * HEAD_PAD, (hd + 1) * HEAD_PAD)
            qh = q_ref[:, sl]
            scores = [_dot_nt(qh, kr[:, sl]) for kr in k_refs]
            mx = functools.reduce(jnp.maximum, [jnp.max(s, axis=-1, keepdims=True) for s in scores])
            ps = [jnp.exp2(s - mx) for s in scores]
            den = functools.reduce(jnp.add, [jnp.sum(p, axis=-1, keepdims=True) for p in ps])
            vsl = slice(pr * LANES, (pr + 1) * LANES)
            acc = functools.reduce(jnp.add, [_dot(p.astype(BF16), vr[:, vsl]) for p, vr in zip(ps, v_refs)])
            halves.append(acc / den)
        o_ref[:, pr * LANES:(pr + 1) * LANES] = jnp.where(low, halves[0], halves[1]).astype(BF16)


def _attention(q, kvs, n_seq, seq_len, name):
    tq = ROW_TILE
    nq = seq_len // tq
    n_kv = len(kvs)
    in_specs = [pl.BlockSpec((tq, MLA_HEADS * HEAD_PAD), lambda b, j: (b * nq + j, 0))]
    in_specs += [pl.BlockSpec((rows, MLA_HEADS * HEAD_PAD), lambda b, j: (b, 0)) for _, _, rows in kvs]
    in_specs += [pl.BlockSpec((rows, MLA_HEADS * V_HEAD), lambda b, j: (b, 0)) for _, _, rows in kvs]
    return pl.pallas_call(
        functools.partial(_attn_kernel, n_kv),
        out_shape=jax.ShapeDtypeStruct((n_seq * seq_len, MLA_HEADS * V_HEAD), BF16),
        grid=(n_seq, nq),
        in_specs=in_specs,
        out_specs=pl.BlockSpec((tq, MLA_HEADS * V_HEAD), lambda b, j: (b * nq + j, 0)),
        compiler_params=_cparams(("parallel", "parallel")),
        name=name,
    )(q, *[k for k, _, _ in kvs], *[v for _, v, _ in kvs])


def _pair_masks():
    lane = lax.broadcasted_iota(jnp.int32, (1, LANES), 1)
    return lane < GDN_DK


def _stack(x, low):
    zero = jnp.zeros_like(x)
    return jnp.concatenate([jnp.where(low, x, zero), jnp.where(low, zero, x)], axis=0)


def _split3(x):
    hi = x.astype(BF16)
    r = x - hi.astype(F32)
    mid = r.astype(BF16)
    lo = (r - mid.astype(F32)).astype(BF16)
    return hi, mid, lo


def _gdn_prep_kernel(seq_len, q_ref, k_ref, v_ref, cwq_ref, cwk_ref, cwv_ref, ab_ref, alog_ref, dtb_ref, e_ref,
                     uf_ref, ub_ref, wf_ref, wb_ref, af_ref, abk_ref, qdf_ref, qdb_ref, kdf_ref, kdb_ref,
                     glf_ref, glb_ref,
                     qn_s, kn_s, vn_s, gcb_s, gf_s):
    seq = q_ref.shape[0]
    n_chunks = seq // CHUNK
    low = _pair_masks()
    row = lax.broadcasted_iota(jnp.int32, (seq, 1), 0) % seq_len
    lane = lax.broadcasted_iota(jnp.int32, (1, LANES), 1)

    def conv(x_ref, cw_ref):
        x = x_ref[...]
        acc = jnp.zeros_like(x)
        for j in range(CONV_K):
            sh = CONV_K // 2 - j
            xs = x if sh == 0 else pltpu.roll(x, sh % seq, axis=0)
            src = row - sh
            valid = (src >= 0) & (src < seq_len)
            acc = acc + jnp.where(valid, xs, 0.0) * cw_ref[j:j + 1, :]
        return _silu(acc)

    def l2n(x):
        sq = x * x
        s0 = jnp.sum(jnp.where(low, sq, 0.0), axis=-1, keepdims=True)
        s1 = jnp.sum(jnp.where(low, 0.0, sq), axis=-1, keepdims=True)
        return x * lax.rsqrt(jnp.where(low, s0, s1) + EPS)

    qn_s[...] = l2n(conv(q_ref, cwq_ref)) * (GDN_DK ** -0.5)
    kn_s[...] = l2n(conv(k_ref, cwk_ref))
    vn_s[...] = conv(v_ref, cwv_ref)

    a = ab_ref[...]
    xg = a + dtb_ref[...]
    softplus = jnp.maximum(xg, 0.0) + jnp.log(1.0 + jnp.exp(-jnp.abs(xg)))
    act = jnp.where(lane < 2 * GDN_HEADS, -jnp.exp(alog_ref[...]) * softplus, jax.nn.sigmoid(a))

    ti = lax.broadcasted_iota(jnp.int32, (CHUNK, CHUNK), 0)
    tj = lax.broadcasted_iota(jnp.int32, (CHUNK, CHUNK), 1)
    tri_lo = (tj <= ti).astype(BF16)
    tri_up = (tj >= ti).astype(BF16)
    for c in range(n_chunks):
        ac = act[c * CHUNK:(c + 1) * CHUNK]
        pieces = _split3(ac)
        lo = functools.reduce(jnp.add, [_dot(tri_lo, pc) for pc in pieces])
        up = functools.reduce(jnp.add, [_dot(tri_up, pc) for pc in pieces])
        gcb_s[c * CHUNK:(c + 1) * CHUNK, :] = jnp.where(lane < GDN_HEADS, lo,
                                                        jnp.where(lane < 2 * GDN_HEADS, up, ac))
    expand = e_ref[0].astype(BF16)
    gf_s[...] = functools.reduce(jnp.add, [_dot(pc, expand) for pc in _split3(gcb_s[...])])

    ri = lax.broadcasted_iota(jnp.int32, (CHUNK, LANES), 0)
    cj = lax.broadcasted_iota(jnp.int32, (CHUNK, LANES), 1) % CHUNK
    eye = (ri == cj).astype(F32)

    def pmm(x, y):
        return _dot(x.astype(BF16), _stack(y, low).astype(BF16))

    def row_form(g):
        gt = jnp.concatenate([g, jnp.zeros_like(g)], axis=0).T
        r0 = jnp.broadcast_to(gt[0:1, :], (CHUNK, LANES))
        r1 = jnp.broadcast_to(gt[GDN_DK:GDN_DK + 1, :], (CHUNK, LANES))
        return jnp.where(low, r0, pltpu.roll(r1, GDN_DK, axis=1))

    out_refs = ((uf_ref, wf_ref, af_ref, qdf_ref, kdf_ref, glf_ref),
                (ub_ref, wb_ref, abk_ref, qdb_ref, kdb_ref, glb_ref))
    incl = (ri >= cj, ri <= cj)
    strict = (ri > cj, ri < cj)
    diag8 = (ri // 8) == (cj // 8)
    merge_masks = [((ri // (2 * s)) == (cj // (2 * s))) & ((ri // s) != (cj // s)) for s in (8, 16, 32)]

    def group(it, carry):
        cs = [it * PREP_GROUP + cc for cc in range(PREP_GROUP)]
        rows = [pl.ds(pl.multiple_of(c * CHUNK, CHUNK), CHUNK) for c in cs]
        qc = [qn_s[r, :] for r in rows]
        kc = [kn_s[r, :] for r in rows]
        vc = [vn_s[r, :] for r in rows]
        kst = [_stack(k, low).astype(BF16) for k in kc]
        kk = [_dot_nt(k.astype(BF16), ks) for k, ks in zip(kc, kst)]
        qk = [_dot_nt(q.astype(BF16), ks) for q, ks in zip(qc, kst)]
        chains = [(ci, d) for ci in range(PREP_GROUP) for d in range(2)]
        gc = [gf_s[rows[ci], d * LANES:(d + 1) * LANES] for ci, d in chains]
        beta = [gf_s[rows[ci], (2 + d) * LANES:(3 + d) * LANES] for ci, d in chains]
        gr = [row_form(g) for g in gc]
        dm = [jnp.exp(jnp.where(incl[d], g - r, -jnp.inf)) for (ci, d), g, r in zip(chains, gc, gr)]
        lm = [jnp.where(strict[d], b * kk[ci] * m, 0.0) for (ci, d), b, m in zip(chains, beta, dm)]
        aint = [(qk[ci] * m).astype(BF16) for (ci, d), m in zip(chains, dm)]
        x = [-jnp.where(diag8, l, 0.0) for l in lm]
        t = [eye + xx for xx in x]
        for _ in range(2):
            x = [pmm(xx, xx) for xx in x]
            t = [tt + pmm(tt, xx) for tt, xx in zip(t, x)]
        for off in merge_masks:
            tc = [pmm(tt, jnp.where(off, l, 0.0)) for tt, l in zip(t, lm)]
            t = [tt - pmm(a, tt) for tt, a in zip(t, tc)]
        egc = [jnp.exp(g) for g in gc]
        u = [pmm(tt, vc[ci] * b) for (ci, d), tt, b in zip(chains, t, beta)]
        w = [pmm(tt, kc[ci] * b * e).astype(BF16) for (ci, d), tt, b, e in zip(chains, t, beta, egc)]
        qd = [(qc[ci] * e).astype(BF16) for (ci, d), e in zip(chains, egc)]
        gtot = [g[CHUNK - 1:CHUNK, :] if d == 0 else g[0:1, :] for (ci, d), g in zip(chains, gc)]
        kd = [(kc[ci] * jnp.exp(gt - g)).astype(BF16) for (ci, d), gt, g in zip(chains, gtot, gc)]
        for n, (ci, d) in enumerate(chains):
            u_ref, w_ref, a_ref, qd_ref, kd_ref, gl_ref = out_refs[d]
            u_ref[0, rows[ci], :] = u[n]
            w_ref[0, rows[ci], :] = w[n]
            a_ref[0, rows[ci], :] = aint[n]
            qd_ref[0, rows[ci], :] = qd[n]
            kd_ref[0, rows[ci], :] = kd[n]
            gl_ref[0, pl.ds(cs[ci], 1), :, :] = jnp.broadcast_to(jnp.exp(gtot[n]), (1, 8, LANES))
        return carry

    lax.fori_loop(0, n_chunks // PREP_GROUP, group, 0)


def _gdn_prep(qkv, ab, conv_w, alog128, dtb128, expand, n_seq, seq_len):
    rb = max(seq_len, PREP_ROWS)
    nb = n_seq * seq_len // rb
    n_chunks = rb // CHUNK
    col = lambda off: pl.BlockSpec((rb, LANES), lambda s, p: (s, off + p))
    cw = lambda off: pl.BlockSpec((CONV_K, LANES), lambda s, p: (0, off + p))
    vec = pl.BlockSpec((1, LANES), lambda s, p: (0, 0))
    big = lambda: pl.BlockSpec((1, rb, LANES), lambda s, p: (s, 0, p))
    glspec = lambda: pl.BlockSpec((1, n_chunks, 8, LANES), lambda s, p: (s, 0, 0, p))
    shp = lambda dt: jax.ShapeDtypeStruct((nb, rb, GDN_W), dt)
    glshp = jax.ShapeDtypeStruct((nb, n_chunks, 8, GDN_W), F32)
    outs = pl.pallas_call(
        functools.partial(_gdn_prep_kernel, seq_len),
        out_shape=[shp(F32), shp(F32)] + [shp(BF16)] * 8 + [glshp, glshp],
        grid=(nb, N_PAIRS),
        in_specs=[col(0), col(N_PAIRS), col(2 * N_PAIRS), cw(0), cw(N_PAIRS), cw(2 * N_PAIRS),
                  pl.BlockSpec((rb, LANES), lambda s, p: (s, 0)), vec, vec,
                  pl.BlockSpec((1, LANES, 4 * LANES), lambda s, p: (p, 0, 0))],
        out_specs=[big() for _ in range(10)] + [glspec(), glspec()],
        scratch_shapes=[pltpu.VMEM((rb, LANES), F32)] * 4 + [pltpu.VMEM((rb, 4 * LANES), F32)],
        compiler_params=_cparams(("parallel", "parallel")),
        name="gdn_prep_%d" % seq_len,
    )(qkv, qkv, qkv, conv_w, conv_w, conv_w, ab, alog128, dtb128, expand)
    per_seq = [o.reshape(n_seq, seq_len, GDN_W) for o in outs[:10]]
    return per_seq + [o.reshape(n_seq, seq_len // CHUNK, 8, GDN_W) for o in outs[10:]]


def _gdn_scan_kernel(uf_ref, ub_ref, wf_ref, wb_ref, af_ref, abk_ref, qdf_ref, qdb_ref, kdf_ref, kdb_ref,
                     glf_ref, glb_ref, s0_ref, of_ref, ob_ref, sfin_ref, state):
    step = pl.program_id(1)
    n_steps = pl.num_programs(1)
    low = _pair_masks()
    chains = [(d, j, p) for d in range(2) for j in range(SCAN_SEQS) for p in range(N_PAIRS)]

    first = step == 0
    per_dir = ((uf_ref, wf_ref, af_ref, qdf_ref, kdf_ref, glf_ref, of_ref),
               (ub_ref, wb_ref, abk_ref, qdb_ref, kdb_ref, glb_ref, ob_ref))
    def rd(k, d, j, p):
        return per_dir[d][k][j, :, p * LANES:(p + 1) * LANES]

    s = [jnp.where(first, _stack(s0_ref[j, d, p], low), state[idx]) for idx, (d, j, p) in enumerate(chains)]
    sb = [x.astype(BF16) for x in s]
    ws = [_dot(rd(1, *c), b) for c, b in zip(chains, sb)]
    qs = [_dot(rd(3, *c), b) for c, b in zip(chains, sb)]
    vst = [_stack(rd(0, *c) - w, low).astype(BF16) for c, w in zip(chains, ws)]
    upd = [_dot_tn(_stack(rd(4, *c), low), v) for c, v in zip(chains, vst)]
    intra = [_dot(rd(2, *c), v) for c, v in zip(chains, vst)]
    for idx, (d, j, p) in enumerate(chains):
        sl = slice(p * LANES, (p + 1) * LANES)
        state[idx] = s[idx] * per_dir[d][5][j, 0, 0:1, sl] + upd[idx]
        per_dir[d][6][j, :, sl] = qs[idx] + intra[idx]

    @pl.when(step == n_steps - 1)
    def _():
        for idx, (d, j, p) in enumerate(chains):
            s = state[idx]
            sfin_ref[j, d, p] = s[:GDN_DK] + s[GDN_DK:]


def _gdn_scan(prep, s0, n_seq, seq_len):
    n_chunks = seq_len // CHUNK
    ns = SCAN_SEQS
    fwd = lambda: pl.BlockSpec((ns, CHUNK, GDN_W), lambda g, i: (g, i, 0))
    bwd = lambda: pl.BlockSpec((ns, CHUNK, GDN_W), lambda g, i: (g, n_chunks - 1 - i, 0))
    glf = pl.BlockSpec((ns, 1, 8, GDN_W), lambda g, i: (g, i, 0, 0))
    glb = pl.BlockSpec((ns, 1, 8, GDN_W), lambda g, i: (g, n_chunks - 1 - i, 0, 0))
    st = lambda: pl.BlockSpec((ns, 2, N_PAIRS, GDN_DK, LANES), lambda g, i: (g, 0, 0, 0, 0))
    oshape = jax.ShapeDtypeStruct((n_seq, seq_len, GDN_W), F32)
    return pl.pallas_call(
        _gdn_scan_kernel,
        out_shape=[oshape, oshape, jax.ShapeDtypeStruct((n_seq, 2, N_PAIRS, GDN_DK, LANES), F32)],
        grid=(n_seq // ns, n_chunks),
        in_specs=[fwd(), bwd()] * 5 + [glf, glb, st()],
        out_specs=[fwd(), bwd(), st()],
        scratch_shapes=[pltpu.VMEM((2 * ns * N_PAIRS, LANES, LANES), F32)],
        compiler_params=_cparams(("parallel", "arbitrary")),
        name="gdn_scan_%d" % seq_len,
    )(*prep, s0)


def _route(sel, s):
    per_group = N_EXPERTS // N_GROUPS
    ninf = -jnp.inf
    sub = lax.broadcasted_iota(jnp.int32, sel.shape, 1).astype(F32)
    gid = lax.broadcasted_iota(jnp.int32, (N_GROUPS, 1, sel.shape[2]), 0).astype(F32)
    m1 = jnp.max(sel, axis=1, keepdims=True)
    i1 = jnp.min(jnp.where(sel == m1, sub, float(per_group)), axis=1, keepdims=True)
    m2 = jnp.max(jnp.where(sub == i1, ninf, sel), axis=1, keepdims=True)
    work = m1 + m2
    gmask = jnp.zeros(work.shape, jnp.bool_)
    for _ in range(TOPK_GROUPS):
        m = jnp.max(work, axis=0, keepdims=True)
        idx = jnp.min(jnp.where(work == m, gid, float(N_GROUPS)), axis=0, keepdims=True)
        pick = gid == idx
        gmask = gmask | pick
        work = jnp.where(pick, ninf, work)
    work = jnp.where(gmask, sel, ninf)
    eid = gid * per_group + sub
    chosen = jnp.zeros(sel.shape, jnp.bool_)
    for _ in range(TOP_K):
        m = jnp.max(jnp.max(work, axis=1, keepdims=True), axis=0, keepdims=True)
        idx = jnp.min(jnp.min(jnp.where(work == m, eid, float(N_EXPERTS)), axis=1, keepdims=True),
                      axis=0, keepdims=True)
        pick = eid == idx
        chosen = chosen | pick
        work = jnp.where(pick, ninf, work)
    wk = jnp.where(chosen, s, 0.0)
    den = jnp.sum(jnp.sum(wk, axis=1, keepdims=True), axis=0, keepdims=True)
    return wk / den * ROUTED_SCALE


def _merge_kernel(x_ref, m_ref, omla_ref, of_ref, ob_ref, z_ref, ga_ref, gb_ref,
                  woa_ref, wob_ref, wo_ref, gpost_ref, gpre_ref, gdng_ref, wr_ref, eb_ref,
                  x1_ref, h2_ref, gates_ref, cnt_ref):
    m = m_ref[0]
    low = _pair_masks()
    o = of_ref[...] + ob_ref[...]
    z = z_ref[...]
    parts = []
    for p in range(N_PAIRS):
        sl = slice(p * LANES, (p + 1) * LANES)
        op = o[:, sl]
        sq = op * op
        s0 = jnp.sum(jnp.where(low, sq, 0.0), axis=-1, keepdims=True)
        s1 = jnp.sum(jnp.where(low, 0.0, sq), axis=-1, keepdims=True)
        ms = jnp.where(low, s0, s1) * (1.0 / GDN_DV)
        parts.append(op * lax.rsqrt(ms + EPS) * gdng_ref[...] * _silu(z[:, sl]))
    og = jnp.concatenate(parts, axis=1).astype(BF16)
    ya = _dot(omla_ref[...], woa_ref[...])
    yb = _dot(og, wob_ref[...])
    mix = (jax.nn.sigmoid(ga_ref[...]) * ya + jax.nn.sigmoid(gb_ref[...]) * yb).astype(BF16)
    y = _dot(mix, wo_ref[...])
    x1 = x_ref[...] + m[2:3] * _rms(y, gpost_ref[...])
    x1_ref[...] = x1
    h2 = _rms(x1, gpre_ref[...]) * (1.0 + m[4:5]) + m[3:4]
    h2_ref[...] = h2.astype(BF16)
    wh, wl, _ = _split3(wr_ref[...])
    hh, hl, _ = _split3(h2)
    logits = _dot_nt(wh, hh) + (_dot_nt(wh, hl) + _dot_nt(wl, hh))
    s = jax.nn.sigmoid(logits)
    sel = s + eb_ref[...]
    tm = s.shape[1]
    shape3 = (N_GROUPS, N_EXPERTS // N_GROUPS, tm)
    gates_t = _route(sel.reshape(shape3), s.reshape(shape3)).reshape(N_EXPERTS, tm)
    gates_ref[...] = gates_t
    cnt_ref[0] = jnp.sum((gates_t > 0.0).astype(F32), axis=1, keepdims=True)


def _merge_kernel_into(*refs):
    n_in = 16
    _merge_kernel(*refs[:n_in], *refs[n_in + 4:])


def _merge(x, mods, mod_index, omla, o_f, o_b, pr, wts, tile0, total, into=None):
    t = x.shape[0]
    tm = ROW_TILE

    def full(a):
        return pl.BlockSpec(a.shape, lambda i: (0,) * a.ndim)

    def rows(w):
        return pl.BlockSpec((tm, w), lambda i: (i, 0))

    def out_rows(w):
        return pl.BlockSpec((tm, w), lambda i: (tile0 + i, 0))

    names = ("woa", "wob", "wo", "gpost", "gpre2", "gdng", "wr_t", "eb")
    args = [x, mods, omla, o_f, o_b, pr["z"], pr["ga"], pr["gb"]] + [wts[n] for n in names]
    in_specs = [rows(D_MODEL), pl.BlockSpec((1, 6, D_MODEL), lambda i: (mod_index(i), 0, 0)),
                rows(MLA_HEADS * V_HEAD), rows(GDN_W), rows(GDN_W), rows(GDN_W),
                rows(D_MODEL), rows(D_MODEL)] + [full(wts[n]) for n in names]
    aliases = {}
    if into is not None:
        aliases = {len(args) + k: k for k in range(4)}
        in_specs = in_specs + [pl.BlockSpec(memory_space=pl.ANY)] * 4
        args = args + list(into)
    return pl.pallas_call(
        _merge_kernel if into is None else _merge_kernel_into,
        out_shape=[jax.ShapeDtypeStruct((total, D_MODEL), F32), jax.ShapeDtypeStruct((total, D_MODEL), BF16),
                   jax.ShapeDtypeStruct((N_EXPERTS, total), F32),
                   jax.ShapeDtypeStruct((total // tm, N_EXPERTS, 1), F32)],
        grid=(t // tm,),
        in_specs=in_specs,
        out_specs=[out_rows(D_MODEL), out_rows(D_MODEL),
                   pl.BlockSpec((N_EXPERTS, tm), lambda i: (0, tile0 + i)),
                   pl.BlockSpec((1, N_EXPERTS, 1), lambda i: (tile0 + i, 0, 0))],
        input_output_aliases=aliases,
        compiler_params=_cparams(("parallel",)),
        name="merge",
    )(*args)


TABLE_W = 256
TILE_TABLE_W = 512


def _ceil_div(x, d):
    return jnp.floor((x + (d - 1)) * (1.0 / d))


def _moe_tables_kernel(cnt_ref, cnt_t_ref, ce_ref, cb_ref, dst_ref, src_ref, ng_ref, tile_ref):
    nb = cnt_ref.shape[0]
    ppt = float(TILE_PIECES)
    ppg = float(LOOP_PIECES)
    ei = lax.broadcasted_iota(jnp.int32, (N_EXPERTS, N_EXPERTS), 0)
    ej = lax.broadcasted_iota(jnp.int32, (N_EXPERTS, N_EXPERTS), 1)
    tri = (ej <= ei).astype(BF16)

    def cumsum_experts(col):
        wide = jnp.broadcast_to(col, (N_EXPERTS, LANES))
        return functools.reduce(jnp.add, [_dot(tri, pc) for pc in _split3(wide)])[:, 0:1]

    eid = lax.broadcasted_iota(jnp.int32, (N_EXPERTS, 1), 0).astype(F32)
    pc_t = _ceil_div(cnt_t_ref[...], PIECE)
    tp = jnp.sum(pc_t, axis=1, keepdims=True)
    rp = _ceil_div(tp, TILE_PIECES) * ppt
    gs_end = cumsum_experts(rp)
    gs = gs_end - rp
    blk = lax.broadcasted_iota(jnp.int32, (1, nb), 1)
    c = lax.broadcasted_iota(jnp.int32, (1, TABLE_W), 1).astype(F32)
    for b in range(nb):
        pc = _ceil_div(cnt_ref[b], PIECE)
        seg_end = cumsum_experts(pc)
        seg = seg_end - pc
        blk_off = jnp.sum(jnp.where(blk < b, pc_t, 0.0), axis=1, keepdims=True)
        nvalid = seg_end[N_EXPERTS - 1:N_EXPERTS, :]
        ce = jnp.minimum(jnp.sum((seg_end <= c).astype(F32), axis=0, keepdims=True), N_EXPERTS - 1.0)
        onehot = eid == ce
        seg_sel = jnp.sum(jnp.where(onehot, seg, 0.0), axis=0, keepdims=True)
        base_sel = jnp.sum(jnp.where(onehot, gs + blk_off - seg, 0.0), axis=0, keepdims=True)
        valid = c < nvalid
        dump = DUMP_PIECE0 + b * LOOP_PIECES + (c - ppg * jnp.floor(c * (1.0 / ppg)))
        dst = jnp.where(valid, base_sel + c, dump)
        row = slice(b, b + 1)
        ce_ref[row, :] = ce.astype(jnp.int32)
        cb_ref[row, :] = jnp.where(valid, (c - seg_sel) * PIECE, -float(1 << 20)).astype(jnp.int32)
        dst_ref[row, :] = dst.astype(jnp.int32)
        src_ref[row, :] = jnp.where(valid, dst, dst[:, 0:1]).astype(jnp.int32)
        ng_ref[row, :] = jnp.broadcast_to(_ceil_div(nvalid, LOOP_PIECES), (1, LANES)).astype(jnp.int32)
    j = lax.broadcasted_iota(jnp.int32, (1, TILE_TABLE_W), 1).astype(F32)
    start = j * ppt
    te = jnp.minimum(jnp.sum((gs_end <= start).astype(F32), axis=0, keepdims=True), N_EXPERTS - 1.0)
    onehot = eid == te
    tp_sel = jnp.sum(jnp.where(onehot, tp, 0.0), axis=0, keepdims=True)
    gs_sel = jnp.sum(jnp.where(onehot, gs, 0.0), axis=0, keepdims=True)
    n_used = gs_end[N_EXPERTS - 1:N_EXPERTS, :] * (1.0 / ppt)
    used = j < n_used
    tv = jnp.where(used, jnp.clip((tp_sel - (start - gs_sel)) * PIECE, 0.0, float(EXP_TILE)), 0.0)
    tin = jnp.where(used, j, n_used - 1.0)
    tout = jnp.where(used, j, float(N_EXP_TILES))
    tile_ref[...] = jnp.zeros(tile_ref.shape, jnp.int32)
    for r, v in enumerate((te, tv, tin, tout)):
        tile_ref[r:r + 1, :] = v.astype(jnp.int32)


def _dispatch_tables(cnt):
    nb = cnt.shape[0]
    tab = jax.ShapeDtypeStruct((nb, TABLE_W), jnp.int32)
    ce, cb, dst, src, ng, tile = pl.pallas_call(
        _moe_tables_kernel,
        out_shape=[tab, tab, tab, tab, jax.ShapeDtypeStruct((nb, LANES), jnp.int32),
                   jax.ShapeDtypeStruct((8, TILE_TABLE_W), jnp.int32)],
        name="moe_tables",
    )(cnt, cnt[:, :, 0].T)
    return {"ce": ce, "cb": cb, "dst": dst, "src": src, "ngroups": ng, "tile": tile}


def _piece_onehot(ce_ref, cb_ref, rank_s, b, g, extra=None):
    sub = lax.broadcasted_iota(jnp.int32, (PIECE, 1), 0).astype(F32)
    ps, ex = [], []
    for cc in range(GROUP_PIECES):
        c = g * GROUP_PIECES + cc
        e = ce_ref[b, c]
        base = cb_ref[b, c].astype(F32)
        hit = rank_s[pl.ds(e, 1), :] == base + sub
        ps.append(jnp.where(hit, 1.0, 0.0).astype(BF16))
        if extra is not None:
            ex.append(jnp.sum(jnp.where(hit, extra[pl.ds(e, 1), :], 0.0), axis=-1, keepdims=True))
    p = jnp.concatenate(ps, axis=0)
    return (p, jnp.concatenate(ex, axis=0)) if extra is not None else p


def _block_ranks(gt):
    n = gt.shape[1]
    ti = lax.broadcasted_iota(jnp.int32, (n, n), 0)
    tj = lax.broadcasted_iota(jnp.int32, (n, n), 1)
    before = (ti < tj).astype(BF16)
    member = gt > 0.0
    rank = _dot(member.astype(BF16), before)
    return jnp.where(member, rank, -1.0)


def _moe_sort_kernel(ce_ref, cb_ref, dst_ref, ng_ref, h_ref, gt_ref, xg_ref, stage, rank_s, gate_s, sem):
    b = pl.program_id(0)
    slot = b % 2
    gt = gt_ref[...]
    rank_s[...] = _block_ranks(gt)
    gate_s[...] = gt
    lane = lax.broadcasted_iota(jnp.int32, (1, LANES), 1)

    def piece_copy(blk, sl, c):
        r0 = pl.multiple_of(c * PIECE, PIECE)
        d0 = pl.multiple_of(dst_ref[blk, c] * PIECE, PIECE)
        return pltpu.make_async_copy(stage.at[sl, pl.ds(r0, PIECE)], xg_ref.at[pl.ds(d0, PIECE)], sem.at[sl])

    def groups(it, carry):
        gs = [it * LOOP_GROUPS + k for k in range(LOOP_GROUPS)]
        sel = [_piece_onehot(ce_ref, cb_ref, rank_s, b, g, gate_s) for g in gs]
        xs = [_dot(p, h_ref[...]).astype(BF16) for p, _ in sel]
        for g, x, (_, gcol) in zip(gs, xs, sel):
            hi, mid, lo = (t.astype(F32) for t in _split3(gcol))
            gblk = jnp.where(lane == 0, hi, jnp.where(lane == 1, mid, jnp.where(lane == 2, lo, 0.0)))
            r0 = pl.multiple_of(g * GROUP_ROWS, GROUP_ROWS)
            stage[slot, pl.ds(r0, GROUP_ROWS), :] = jnp.concatenate([x, gblk.astype(BF16)], axis=1)
        for cc in range(LOOP_PIECES):
            piece_copy(b, slot, it * LOOP_PIECES + cc).start()
        return carry

    lax.fori_loop(0, ng_ref[b, 0], groups, 0)

    def drain(blk, sl):
        def wait_one(c, carry):
            piece_copy(blk, sl, c).wait()
            return carry
        lax.fori_loop(0, ng_ref[blk, 0] * LOOP_PIECES, wait_one, 0)

    @pl.when(b > 0)
    def _():
        drain(b - 1, 1 - slot)

    @pl.when(b == pl.num_programs(0) - 1)
    def _():
        drain(b, slot)


def _moe_sort(h2, gates_t, tabs):
    nb = h2.shape[0] // MOE_BLOCK
    grid_spec = pltpu.PrefetchScalarGridSpec(
        num_scalar_prefetch=4, grid=(nb,),
        in_specs=[pl.BlockSpec((MOE_BLOCK, D_MODEL), lambda b, *_: (b, 0)),
                  pl.BlockSpec((N_EXPERTS, MOE_BLOCK), lambda b, *_: (0, b))],
        out_specs=pl.BlockSpec(memory_space=pl.ANY),
        scratch_shapes=[pltpu.VMEM((2, STAGE_PIECES * PIECE, DISPATCH_W), BF16),
                        pltpu.VMEM((N_EXPERTS, MOE_BLOCK), F32), pltpu.VMEM((N_EXPERTS, MOE_BLOCK), F32),
                        pltpu.SemaphoreType.DMA((2,))])
    return pl.pallas_call(
        _moe_sort_kernel,
        out_shape=jax.ShapeDtypeStruct((DISPATCH_ROWS, DISPATCH_W), BF16),
        grid_spec=grid_spec,
        compiler_params=_cparams(("arbitrary",)),
        name="moe_sort",
    )(tabs["ce"], tabs["cb"], tabs["dst"], tabs["ngroups"], h2, gates_t)


def _moe_expert_kernel(tile_ref, x_ref, wg_ref, wu_ref, wd_ref, y_ref):
    valid = tile_ref[1, pl.program_id(0)]

    @pl.when(valid > 0)
    def _():
        keep = lax.broadcasted_iota(jnp.int32, (EXP_TILE, 1), 0) < valid
        xrow = x_ref[...]
        x = jnp.where(keep, xrow[:, :D_MODEL], jnp.zeros((), BF16))
        g = jnp.sum(jnp.where(keep, xrow[:, D_MODEL:].astype(F32), 0.0), axis=-1, keepdims=True)
        hg = _dot(x, wg_ref[0].astype(BF16))
        hu = _dot(x, wu_ref[0].astype(BF16))
        act = (_silu(hg) * hu * g).astype(BF16)
        y_ref[...] = _dot(act, wd_ref[0].astype(BF16)).astype(BF16)


def _moe_expert(xg, tabs, wts):
    grid_spec = pltpu.PrefetchScalarGridSpec(
        num_scalar_prefetch=1, grid=(N_EXP_TILES,),
        in_specs=[pl.BlockSpec((EXP_TILE, DISPATCH_W), lambda j, tt: (tt[2, j], 0)),
                  pl.BlockSpec((1, D_MODEL, D_EXPERT), lambda j, tt: (tt[0, j], 0, 0)),
                  pl.BlockSpec((1, D_MODEL, D_EXPERT), lambda j, tt: (tt[0, j], 0, 0)),
                  pl.BlockSpec((1, D_EXPERT, D_MODEL), lambda j, tt: (tt[0, j], 0, 0))],
        out_specs=pl.BlockSpec((EXP_TILE, D_MODEL), lambda j, tt: (tt[3, j], 0)))
    return pl.pallas_call(
        _moe_expert_kernel,
        out_shape=jax.ShapeDtypeStruct((DISPATCH_ROWS, D_MODEL), BF16),
        grid_spec=grid_spec,
        compiler_params=_cparams(("arbitrary",)),
        name="moe_expert",
    )(tabs["tile"], xg, wts["w_gate"], wts["w_up"], wts["w_down"])


def _moe_combine_kernel(ce_ref, cb_ref, src_ref, ng_ref, yg_ref, gt_ref, h_ref, x1_ref, m_ref, gpost_ref,
                        wsg_ref, wsu_ref, wsd_ref, out_ref, stage, rank_s, acc_s, sem):
    b = pl.program_id(0)
    slot = b % 2

    def piece_copy(blk, sl, c):
        r0 = pl.multiple_of(c * PIECE, PIECE)
        s0 = pl.multiple_of(src_ref[blk, c] * PIECE, PIECE)
        return pltpu.make_async_copy(yg_ref.at[pl.ds(s0, PIECE)], stage.at[sl, pl.ds(r0, PIECE)], sem.at[sl])

    def fetch(blk, sl):
        def start_one(c, carry):
            piece_copy(blk, sl, c).start()
            return carry
        lax.fori_loop(0, ng_ref[blk, 0] * LOOP_PIECES, start_one, 0)

    @pl.when(b == 0)
    def _():
        fetch(0, 0)

    @pl.when(b + 1 < pl.num_programs(0))
    def _():
        fetch(b + 1, 1 - slot)

    rank_s[...] = _block_ranks(gt_ref[...])
    h = h_ref[...]
    sh = (_silu(_dot(h, wsg_ref[...])) * _dot(h, wsu_ref[...])).astype(BF16)
    acc_s[...] = _dot(sh, wsd_ref[...])

    def wait_one(c, carry):
        piece_copy(b, slot, c).wait()
        return carry

    lax.fori_loop(0, ng_ref[b, 0] * LOOP_PIECES, wait_one, 0)

    def groups(it, carry):
        gs = [it * LOOP_GROUPS + k for k in range(LOOP_GROUPS)]
        ps = [_piece_onehot(ce_ref, cb_ref, rank_s, b, g) for g in gs]
        ys = [stage[slot, pl.ds(pl.multiple_of(g * GROUP_ROWS, GROUP_ROWS), GROUP_ROWS), :] for g in gs]
        acc_s[...] += functools.reduce(jnp.add, [_dot_tn(p, y) for p, y in zip(ps, ys)])
        return carry

    lax.fori_loop(0, ng_ref[b, 0], groups, 0)
    m = m_ref[0]
    out_ref[...] = x1_ref[...] + m[5:6] * _rms(acc_s[...], gpost_ref[...])


def _moe_combine(yg, gates_t, h2, x1, mods, mod_index, tabs, wts):
    t = h2.shape[0]
    nb = t // MOE_BLOCK

    def full(a):
        return pl.BlockSpec(a.shape, lambda b, *_: (0,) * a.ndim)

    grid_spec = pltpu.PrefetchScalarGridSpec(
        num_scalar_prefetch=4, grid=(nb,),
        in_specs=[pl.BlockSpec(memory_space=pl.ANY),
                  pl.BlockSpec((N_EXPERTS, MOE_BLOCK), lambda b, *_: (0, b)),
                  pl.BlockSpec((MOE_BLOCK, D_MODEL), lambda b, *_: (b, 0)),
                  pl.BlockSpec((MOE_BLOCK, D_MODEL), lambda b, *_: (b, 0)),
                  pl.BlockSpec((1, 6, D_MODEL), lambda b, *_: (mod_index(b), 0, 0)),
                  full(wts["gpost2"]), full(wts["wsg"]), full(wts["wsu"]), full(wts["wsd"])],
        out_specs=pl.BlockSpec((MOE_BLOCK, D_MODEL), lambda b, *_: (b, 0)),
        scratch_shapes=[pltpu.VMEM((2, STAGE_PIECES * PIECE, D_MODEL), BF16),
                        pltpu.VMEM((N_EXPERTS, MOE_BLOCK), F32), pltpu.VMEM((MOE_BLOCK, D_MODEL), F32),
                        pltpu.SemaphoreType.DMA((2,))])
    return pl.pallas_call(
        _moe_combine_kernel,
        out_shape=jax.ShapeDtypeStruct((t, D_MODEL), F32),
        grid_spec=grid_spec,
        compiler_params=_cparams(("arbitrary",)),
        name="moe_combine",
    )(tabs["ce"], tabs["cb"], tabs["src"], tabs["ngroups"], yg, gates_t, h2, x1, mods,
      wts["gpost2"], wts["wsg"], wts["wsu"], wts["wsd"])


def _moe(h2, gates_t, cnt, x1, mods, mod_index, wts):
    tabs = _dispatch_tables(cnt)
    xg = _moe_sort(h2, gates_t, tabs)
    yg = _moe_expert(xg, tabs, wts)
    return _moe_combine(yg, gates_t, h2, x1, mods, mod_index, tabs, wts)


def _rope_swap(w):
    nf = QK_ROPE // 4
    parts = [w[..., i * nf:(i + 1) * nf] for i in range(4)]
    return jnp.concatenate([parts[1], parts[0], parts[3], parts[2]], axis=-1)


def _head_block(nope, rope):
    lead = nope.shape[:-2] if nope is not None else rope.shape[:-2]
    nope = jnp.zeros(lead + (MLA_HEADS, QK_NOPE), F32) if nope is None else nope
    rope = jnp.zeros(lead + (MLA_HEADS, QK_ROPE), F32) if rope is None else rope
    pad = jnp.zeros(lead + (MLA_HEADS, HEAD_PAD - MLA_QK), F32)
    return jnp.concatenate([nope, rope, pad], axis=-1).reshape(lead + (MLA_HEADS * HEAD_PAD,))


def _rope_block(w):
    lead = w.shape[:-1]
    return jnp.concatenate([jnp.zeros(lead + (QK_NOPE,), F32), w,
                            jnp.zeros(lead + (HEAD_PAD - MLA_QK,), F32)], axis=-1)


def _rope_tables(n_tokens):
    rows = n_tokens // GRID_W
    row = np.repeat(np.arange(rows, dtype=np.float32), GRID_W)
    colv = np.tile(np.arange(GRID_W, dtype=np.float32), rows)
    nf = QK_ROPE // 4
    inv = jnp.asarray(ROPE_THETA, F32) ** (-jnp.arange(nf, dtype=F32) / nf)
    ang_r = jnp.asarray(row)[:, None] * inv
    ang_c = jnp.asarray(colv)[:, None] * inv
    cos32 = jnp.concatenate([jnp.cos(ang_r), jnp.cos(ang_r), jnp.cos(ang_c), jnp.cos(ang_c)], axis=-1)
    sin32 = jnp.concatenate([-jnp.sin(ang_r), jnp.sin(ang_r), -jnp.sin(ang_c), jnp.sin(ang_c)], axis=-1)
    ones = jnp.ones((n_tokens, QK_NOPE), F32)
    tail = jnp.zeros((n_tokens, HEAD_PAD - MLA_QK), F32)
    cos = jnp.concatenate([ones, cos32, tail], axis=-1)
    sin = jnp.concatenate([jnp.zeros((n_tokens, QK_NOPE), F32), sin32, tail], axis=-1)
    return cos, sin


def _expand_matrix():
    e = np.zeros((N_PAIRS, LANES, 4 * LANES), np.float32)
    for p in range(N_PAIRS):
        for blk in range(4):
            for hh in range(2):
                src = blk * GDN_HEADS + 2 * p + hh
                e[p, src, blk * LANES + hh * GDN_DK: blk * LANES + (hh + 1) * GDN_DK] = 1.0
    return jnp.asarray(e)


def _prepare_weights(w_in, q_norm_g, kv_norm_g, w_uq, w_ukv, w_oa, w_ob, w_o, g_post_mix, g_pre_ffn,
                     g_post_ffn, gdn_norm_g, w_router, e_bias, w_gate, w_up, w_down, ws_gate, ws_up, ws_down):
    offs = np.cumsum((Q_LORA, KV_LORA, QK_ROPE, 3 * GDN_W, GDN_W, 2 * GDN_HEADS, 2 * GDN_HEADS,
                      D_MODEL, D_MODEL))[:-1].tolist()
    cq, ckv, kr, qkv, z, a, b, ga, gb = jnp.split(w_in, offs, axis=-1)
    ab = jnp.concatenate([a, b, jnp.zeros((D_MODEL, LANES - 4 * GDN_HEADS), F32)], axis=-1)
    wcat = jnp.concatenate([cq, ckv, _rope_block(kr), _rope_block(_rope_swap(kr)), qkv, z, ab, ga, gb],
                           axis=-1).astype(BF16)
    uq = w_uq.reshape(Q_LORA, MLA_HEADS, MLA_QK)
    ukv = w_ukv.reshape(KV_LORA, MLA_HEADS, QK_NOPE + V_HEAD)
    return {
        "wcat": wcat,
        "qg": q_norm_g.reshape(1, Q_LORA), "kvg": kv_norm_g.reshape(1, KV_LORA),
        "wuq": _head_block(uq[..., :QK_NOPE], uq[..., QK_NOPE:]).astype(BF16),
        "wuqs": _head_block(None, _rope_swap(uq[..., QK_NOPE:])).astype(BF16),
        "wuk": _head_block(ukv[..., :QK_NOPE], None).astype(BF16),
        "wuv": ukv[..., QK_NOPE:].reshape(KV_LORA, MLA_HEADS * V_HEAD).astype(BF16),
        "woa": w_oa.astype(BF16), "wob": w_ob.astype(BF16), "wo": w_o.astype(BF16),
        "gpost": g_post_mix.reshape(1, D_MODEL), "gpre2": g_pre_ffn.reshape(1, D_MODEL),
        "gpost2": g_post_ffn.reshape(1, D_MODEL),
        "gdng": jnp.tile(gdn_norm_g.reshape(1, GDN_DV), (1, 2)),
        "wr_t": w_router.T, "eb": e_bias.reshape(N_EXPERTS, 1),
        "w_gate": w_gate, "w_up": w_up, "w_down": w_down,
        "wsg": ws_gate.astype(BF16), "wsu": ws_up.astype(BF16), "wsd": ws_down.astype(BF16),
    }


def _pad_lanes(v):
    v = v.reshape(1, -1)
    return jnp.concatenate([v, jnp.zeros((1, LANES - v.shape[1]), F32)], axis=-1)


def _layer_group(x, n_seq, seq_len, mods, mod_index, wts, gpre, conv_w, alog128, dtb128,
                 expand, rope_tabs, extra_kv, s0, tile0, into):
    pr = _proj(x, mods, mod_index, gpre, wts, rope_tabs)
    kvs = list(extra_kv) + [(pr["k"], pr["v"], seq_len)]
    omla = _attention(pr["q"], kvs, n_seq, seq_len, "attn_%d" % seq_len)
    prep = _gdn_prep(pr["qkv"], pr["ab"], conv_w, alog128, dtb128, expand, n_seq, seq_len)
    o_f, o_b, s_fin = _gdn_scan(prep, s0, n_seq, seq_len)
    t = n_seq * seq_len
    merged = _merge(x, mods, mod_index, omla, o_f.reshape(t, GDN_W), o_b.reshape(t, GDN_W), pr, wts,
                    tile0, N_TOKENS, into)
    return merged, pr, s_fin


def _state_to_pairs(s):
    b = s.shape[0]
    s = s.reshape(b, 2, N_PAIRS, 2, GDN_DK, GDN_DV)
    return jnp.transpose(s, (0, 1, 2, 4, 3, 5)).reshape(b, 2, N_PAIRS, GDN_DK, 2 * GDN_DV)


def _pairs_to_state(s):
    b = s.shape[0]
    s = s.reshape(b, 2, N_PAIRS, GDN_DK, 2, GDN_DV)
    return jnp.transpose(s, (0, 1, 2, 4, 3, 5)).reshape(b, 2, GDN_HEADS, GDN_DK, GDN_DV)


def kernel(x_prompt, x_sample, cache_ckv, cache_krope, state_delta, c, c_ctx, w_ada, b_ada, g_pre_mix,
           g_post_mix, g_pre_ffn, g_post_ffn, w_in, q_norm_g, kv_norm_g, w_uq, w_ukv, conv_w, a_log,
           dt_bias, gdn_norm_g, w_oa, w_ob, w_o, w_router, e_bias, w_gate, w_up, w_down, ws_gate, ws_up,
           ws_down):
    batch, seq, _ = x_prompt.shape
    dec_batch, dec_seq, _ = x_sample.shape
    past = cache_ckv.shape[2]
    assert batch * seq + dec_batch * dec_seq == N_TOKENS, "dispatch buffers are sized for N_TOKENS"
    y_p = x_prompt.reshape(batch * seq, D_MODEL)
    y_s = x_sample.reshape(dec_batch * dec_seq, D_MODEL)
    expand = _expand_matrix()
    rope_tabs = _rope_tables(dec_seq)
    cond8 = jnp.concatenate([c_ctx[None], c, jnp.zeros((8 - 1 - dec_batch, D_MODEL), F32)], axis=0)
    ckv_out, krope_out, state_out = [], [], []
    for l in range(DEPTH):
        wts = _prepare_weights(w_in[l], q_norm_g[l], kv_norm_g[l], w_uq[l], w_ukv[l], w_oa[l], w_ob[l],
                               w_o[l], g_post_mix[l], g_pre_ffn[l], g_post_ffn[l], gdn_norm_g[l],
                               w_router[l], e_bias[l], w_gate[l], w_up[l], w_down[l], ws_gate[l],
                               ws_up[l], ws_down[l])
        gpre = g_pre_mix[l].reshape(1, D_MODEL)
        alog128 = _pad_lanes(a_log[l])
        dtb128 = _pad_lanes(dt_bias[l])
        mods = _mods(cond8, w_ada[l], b_ada[l].reshape(1, -1)).reshape(8, 6, D_MODEL)

        zero_state = jnp.zeros((batch, 2, N_PAIRS, GDN_DK, LANES), F32)
        merged_p, pr_p, s_fin = _layer_group(
            y_p, batch, seq, mods, lambda i: 0, wts, gpre, conv_w[l], alog128, dtb128,
            expand, None, [], zero_state, 0, None)
        ckv_out.append(pr_p["ckv"].reshape(batch, seq, KV_LORA))
        krope_out.append(pr_p["kr"][:, QK_NOPE:MLA_QK].reshape(batch, seq, QK_ROPE))
        state_out.append(_pairs_to_state(s_fin))

        kr_ctx = _rope_block(cache_krope[:, l].reshape(dec_batch * past, QK_ROPE))
        k_ctx, v_ctx = _cache_kv(cache_ckv[:, l].reshape(dec_batch * past, KV_LORA), kr_ctx, wts)
        tiles_per_seq = dec_seq // ROW_TILE
        merged_s, _, _ = _layer_group(
            y_s, dec_batch, dec_seq, mods, lambda i: 1 + i // tiles_per_seq,
            wts, gpre, conv_w[l], alog128, dtb128, expand,
            rope_tabs, [(k_ctx, v_ctx, past)], _state_to_pairs(state_delta[:, l]),
            batch * seq // ROW_TILE, merged_p)

        x1, h2, gates_t, cnt = merged_s
        ctx_blocks = batch * seq // MOE_BLOCK
        blocks_per_seq = dec_seq // MOE_BLOCK
        y = _moe(h2, gates_t, cnt, x1, mods,
                 lambda b: jnp.where(b < ctx_blocks, 0, 1 + (b - ctx_blocks) // blocks_per_seq), wts)
        y_p, y_s = y[:batch * seq], y[batch * seq:]
    new_ckv = jnp.stack(ckv_out, axis=1)
    new_krope = jnp.stack(krope_out, axis=1)
    new_state = jnp.stack(state_out, axis=1)
    return (y_p.reshape(batch, seq, D_MODEL), y_s.reshape(dec_batch, dec_seq, D_MODEL),
            new_ckv, new_krope, new_state)
```

```python
import functools

import numpy as np
import jax
import jax.numpy as jnp
from jax import lax
from jax.experimental import pallas as pl
from jax.experimental.pallas import tpu as pltpu

F32 = jnp.float32
BF16 = jnp.bfloat16

D_MODEL = 1024
DEPTH = 1
GRID_W = 64
MLA_HEADS = 8
QK_NOPE = 64
QK_ROPE = 32
V_HEAD = 64
Q_LORA = 256
KV_LORA = 256
ROPE_THETA = 10000.0
GDN_HEADS = 8
GDN_DK = 64
GDN_DV = 64
CONV_K = 5
CHUNK = 64
N_EXPERTS = 64
TOP_K = 8
N_GROUPS = 8
TOPK_GROUPS = 4
D_EXPERT = 256
D_SHARED = 256
ROUTED_SCALE = 2.5
EPS = 1e-6

LANES = 128
LOG2_E = 1.4426950408889634
MLA_QK = QK_NOPE + QK_ROPE
GDN_W = GDN_HEADS * GDN_DK
N_PAIRS = GDN_HEADS // 2
HEAD_PAD = LANES

_SEG = {}
_off = 0
for _name, _width in (("cq", Q_LORA), ("ckv", KV_LORA), ("kr", LANES), ("krs", LANES),
                      ("qkv", 3 * GDN_W), ("z", GDN_W), ("ab", LANES),
                      ("ga", D_MODEL), ("gb", D_MODEL)):
    _SEG[_name] = (_off, _off + _width)
    _off += _width
N_CAT = _off

ROW_TILE = 256
N_TOKENS = 8192
MOE_BLOCK = 256
MERGE_TILE = 512
PIECE = 16
GROUP_ROWS = 256
GROUP_PIECES = GROUP_ROWS // PIECE
LOOP_GROUPS = 2
LOOP_PIECES = LOOP_GROUPS * GROUP_PIECES
EXP_TILE = 512
TILE_PIECES = EXP_TILE // PIECE
N_BLOCKS = N_TOKENS // MOE_BLOCK
STAGE_PIECES = -(-((MOE_BLOCK * TOP_K + N_EXPERTS * (PIECE - 1)) // PIECE) // LOOP_PIECES) * LOOP_PIECES
N_EXP_TILES = -(-((N_TOKENS * TOP_K + N_BLOCKS * N_EXPERTS * (PIECE - 1)) // PIECE
                  + N_EXPERTS * (TILE_PIECES - 1)) // TILE_PIECES)
DUMP_PIECE0 = N_EXP_TILES * TILE_PIECES
DISPATCH_ROWS = (DUMP_PIECE0 + max(N_BLOCKS * LOOP_PIECES, TILE_PIECES)) * PIECE
DISPATCH_W = D_MODEL + LANES
PREP_GROUP = 8
PREP_ROWS = 1024
SCAN_SEQS = 2
VMEM_LIMIT = 56 * 1024 * 1024


def _dot(a, b):
    return jnp.dot(a, b, preferred_element_type=F32)


def _dot_nt(a, b):
    return lax.dot_general(a, b, (((1,), (1,)), ((), ())), preferred_element_type=F32)


def _dot_tn(a, b):
    return lax.dot_general(a, b, (((0,), (0,)), ((), ())), preferred_element_type=F32)


def _rms(x, g):
    return x * lax.rsqrt(jnp.mean(x * x, axis=-1, keepdims=True) + EPS) * g


def _silu(x):
    return x * jax.nn.sigmoid(x)


def _cparams(sem):
    return pltpu.CompilerParams(dimension_semantics=sem, vmem_limit_bytes=VMEM_LIMIT)


def _mods_kernel(c_ref, w_ref, b_ref, o_ref):
    s = _silu(c_ref[...]).astype(BF16)
    o_ref[...] = _dot(s, w_ref[...].astype(BF16)) + b_ref[...]


def _mods(cond8, w_ada, b_ada):
    n = w_ada.shape[1]
    bn = 512
    return pl.pallas_call(
        _mods_kernel,
        out_shape=jax.ShapeDtypeStruct((8, n), F32),
        grid=(n // bn,),
        in_specs=[pl.BlockSpec((8, D_MODEL), lambda j: (0, 0)),
                  pl.BlockSpec((D_MODEL, bn), lambda j: (0, j)),
                  pl.BlockSpec((1, bn), lambda j: (0, j))],
        out_specs=pl.BlockSpec((8, bn), lambda j: (0, j)),
        compiler_params=_cparams(("parallel",)),
        name="mods",
    )(cond8, w_ada, b_ada)


def _proj_kernel(rope, x_ref, m_ref, gpre_ref, wcat_ref, qg_ref, kvg_ref, wuq_ref, wuqs_ref,
                 wuk_ref, wuv_ref, cos_ref, sin_ref,
                 q_ref, k_ref, v_ref, ckv_ref, kr_ref, qkv_ref, z_ref, ab_ref, ga_ref, gb_ref):
    m = m_ref[0]
    h = (_rms(x_ref[...], gpre_ref[...]) * (1.0 + m[1:2]) + m[0:1]).astype(BF16)

    def seg(name):
        a, b = _SEG[name]
        return _dot(h, wcat_ref[:, a:b])

    qkv_ref[...] = seg("qkv")
    z_ref[...] = seg("z")
    ab_ref[...] = seg("ab")
    ga_ref[...] = seg("ga")
    gb_ref[...] = seg("gb")

    qn = _rms(seg("cq"), qg_ref[...]).astype(BF16)
    ckv = _rms(seg("ckv"), kvg_ref[...])
    ckv_ref[...] = ckv
    ckv_b = ckv.astype(BF16)
    kr = seg("kr")
    kr_ref[...] = kr
    qm = _dot(qn, wuq_ref[...])
    kk = _dot(ckv_b, wuk_ref[...])
    v_ref[...] = _dot(ckv_b, wuv_ref[...]).astype(BF16)
    scale = MLA_QK ** -0.5 * LOG2_E
    if rope:
        cos = cos_ref[...]
        sin = sin_ref[...]
        qs = _dot(qn, wuqs_ref[...])
        kr = kr * cos + seg("krs") * sin
    for hd in range(MLA_HEADS):
        sl = slice(hd * HEAD_PAD, (hd + 1) * HEAD_PAD)
        qh = qm[:, sl]
        if rope:
            qh = qh * cos + qs[:, sl] * sin
        q_ref[:, sl] = (qh * scale).astype(BF16)
        k_ref[:, sl] = (kk[:, sl] + kr).astype(BF16)


def _proj(x, mods, mod_index, gpre, wts, rope_tabs):
    t = x.shape[0]
    tm = ROW_TILE
    rope = rope_tabs is not None
    if rope:
        cos, sin = rope_tabs
        n_rope_blocks = cos.shape[0] // tm
        rope_spec = pl.BlockSpec((tm, LANES), lambda i: (i % n_rope_blocks, 0))
    else:
        cos = sin = jnp.zeros((8, LANES), F32)
        rope_spec = pl.BlockSpec((8, LANES), lambda i: (0, 0))

    def full(a):
        return pl.BlockSpec(a.shape, lambda i: (0,) * a.ndim)

    def rows(w):
        return pl.BlockSpec((tm, w), lambda i: (i, 0))

    out_widths = (("q", MLA_HEADS * HEAD_PAD, BF16), ("k", MLA_HEADS * HEAD_PAD, BF16),
                  ("v", MLA_HEADS * V_HEAD, BF16), ("ckv", KV_LORA, F32), ("kr", LANES, F32),
                  ("qkv", 3 * GDN_W, F32), ("z", GDN_W, F32), ("ab", LANES, F32),
                  ("ga", D_MODEL, F32), ("gb", D_MODEL, F32))
    outs = pl.pallas_call(
        functools.partial(_proj_kernel, rope),
        out_shape=[jax.ShapeDtypeStruct((t, w), dt) for _, w, dt in out_widths],
        grid=(t // tm,),
        in_specs=[rows(D_MODEL),
                  pl.BlockSpec((1, 6, D_MODEL), lambda i: (mod_index(i), 0, 0)),
                  full(gpre), full(wts["wcat"]), full(wts["qg"]), full(wts["kvg"]),
                  full(wts["wuq"]), full(wts["wuqs"]), full(wts["wuk"]), full(wts["wuv"]),
                  rope_spec, rope_spec],
        out_specs=[rows(w) for _, w, _ in out_widths],
        compiler_params=_cparams(("parallel",)),
        name="proj_rope" if rope else "proj",
    )(x, mods, gpre, wts["wcat"], wts["qg"], wts["kvg"], wts["wuq"], wts["wuqs"],
      wts["wuk"], wts["wuv"], cos, sin)
    return {name: o for (name, _, _), o in zip(out_widths, outs)}


def _cache_kv_kernel(ckv_ref, kr_ref, wuk_ref, wuv_ref, k_ref, v_ref):
    c = ckv_ref[...].astype(BF16)
    kk = _dot(c, wuk_ref[...])
    v_ref[...] = _dot(c, wuv_ref[...]).astype(BF16)
    kr = kr_ref[...]
    for hd in range(MLA_HEADS):
        sl = slice(hd * HEAD_PAD, (hd + 1) * HEAD_PAD)
        k_ref[:, sl] = (kk[:, sl] + kr).astype(BF16)


def _cache_kv(ckv, kr128, wts):
    t = ckv.shape[0]
    tm = 512
    return pl.pallas_call(
        _cache_kv_kernel,
        out_shape=[jax.ShapeDtypeStruct((t, MLA_HEADS * HEAD_PAD), BF16),
                   jax.ShapeDtypeStruct((t, MLA_HEADS * V_HEAD), BF16)],
        grid=(t // tm,),
        in_specs=[pl.BlockSpec((tm, KV_LORA), lambda i: (i, 0)),
                  pl.BlockSpec((tm, LANES), lambda i: (i, 0)),
                  pl.BlockSpec(wts["wuk"].shape, lambda i: (0, 0)),
                  pl.BlockSpec(wts["wuv"].shape, lambda i: (0, 0))],
        out_specs=[pl.BlockSpec((tm, MLA_HEADS * HEAD_PAD), lambda i: (i, 0)),
                   pl.BlockSpec((tm, MLA_HEADS * V_HEAD), lambda i: (i, 0))],
        compiler_params=_cparams(("parallel",)),
        name="cache_kv",
    )(ckv, kr128, wts["wuk"], wts["wuv"])


def _attn_kernel(n_kv, q_ref, *refs):
    k_refs = refs[:n_kv]
    v_refs = refs[n_kv:2 * n_kv]
    o_ref = refs[2 * n_kv]
    lane = lax.broadcasted_iota(jnp.int32, (1, LANES), 1)
    low = lane < V_HEAD
    for pr in range(MLA_HEADS // 2):
        halves = []
        for hd in (2 * pr, 2 * pr + 1):
            sl = slice(hd * HEAD_PAD, (hd + 1) * HEAD_PAD)
            qh = q_ref[:, sl]
            scores = [_dot_nt(qh, kr[:, sl]) for kr in k_refs]
            mx = functools.reduce(jnp.maximum, [jnp.max(s, axis=-1, keepdims=True) for s in scores])
            ps = [jnp.exp2(s - mx) for s in scores]
            den = functools.reduce(jnp.add, [jnp.sum(p, axis=-1, keepdims=True) for p in ps])
            vsl = slice(pr * LANES, (pr + 1) * LANES)
            acc = functools.reduce(jnp.add, [_dot(p.astype(BF16), vr[:, vsl]) for p, vr in zip(ps, v_refs)])
            halves.append(acc / den)
        o_ref[:, pr * LANES:(pr + 1) * LANES] = jnp.where(low, halves[0], halves[1]).astype(BF16)


def _attention(q, kvs, n_seq, seq_len, name):
    tq = ROW_TILE
    nq = seq_len // tq
    n_kv = len(kvs)
    in_specs = [pl.BlockSpec((tq, MLA_HEADS * HEAD_PAD), lambda b, j: (b * nq + j, 0))]
    in_specs += [pl.BlockSpec((rows, MLA_HEADS * HEAD_PAD), lambda b, j: (b, 0)) for _, _, rows in kvs]
    in_specs += [pl.BlockSpec((rows, MLA_HEADS * V_HEAD), lambda b, j: (b, 0)) for _, _, rows in kvs]
    return pl.pallas_call(
        functools.partial(_attn_kernel, n_kv),
        out_shape=jax.ShapeDtypeStruct((n_seq * seq_len, MLA_HEADS * V_HEAD), BF16),
        grid=(n_seq, nq),
        in_specs=in_specs,
        out_specs=pl.BlockSpec((tq, MLA_HEADS * V_HEAD), lambda b, j: (b * nq + j, 0)),
        compiler_params=_cparams(("parallel", "parallel")),
        name=name,
    )(q, *[k for k, _, _ in kvs], *[v for _, v, _ in kvs])


def _pair_masks():
    lane = lax.broadcasted_iota(jnp.int32, (1, LANES), 1)
    return lane < GDN_DK


def _stack(x, low):
    zero = jnp.zeros_like(x)
    return jnp.concatenate([jnp.where(low, x, zero), jnp.where(low, zero, x)], axis=0)


def _split3(x):
    hi = x.astype(BF16)
    r = x - hi.astype(F32)
    mid = r.astype(BF16)
    lo = (r - mid.astype(F32)).astype(BF16)
    return hi, mid, lo


def _gdn_prep_kernel(seq_len, q_ref, k_ref, v_ref, cwq_ref, cwk_ref, cwv_ref, ab_ref, alog_ref, dtb_ref, e_ref,
                     uf_ref, ub_ref, wf_ref, wb_ref, af_ref, abk_ref, qdf_ref, qdb_ref, kdf_ref, kdb_ref,
                     glf_ref, glb_ref,
                     qn_s, kn_s, vn_s, gcb_s, gf_s):
    seq = q_ref.shape[0]
    n_chunks = seq // CHUNK
    low = _pair_masks()
    row = lax.broadcasted_iota(jnp.int32, (seq, 1), 0) % seq_len
    lane = lax.broadcasted_iota(jnp.int32, (1, LANES), 1)

    def conv(x_ref, cw_ref):
        x = x_ref[...]
        acc = jnp.zeros_like(x)
        for j in range(CONV_K):
            sh = CONV_K // 2 - j
            xs = x if sh == 0 else pltpu.roll(x, sh % seq, axis=0)
            src = row - sh
            valid = (src >= 0) & (src < seq_len)
            acc = acc + jnp.where(valid, xs, 0.0) * cw_ref[j:j + 1, :]
        return _silu(acc)

    def l2n(x):
        sq = x * x
        s0 = jnp.sum(jnp.where(low, sq, 0.0), axis=-1, keepdims=True)
        s1 = jnp.sum(jnp.where(low, 0.0, sq), axis=-1, keepdims=True)
        return x * lax.rsqrt(jnp.where(low, s0, s1) + EPS)

    qn_s[...] = l2n(conv(q_ref, cwq_ref)) * (GDN_DK ** -0.5)
    kn_s[...] = l2n(conv(k_ref, cwk_ref))
    vn_s[...] = conv(v_ref, cwv_ref)

    a = ab_ref[...]
    xg = a + dtb_ref[...]
    softplus = jnp.maximum(xg, 0.0) + jnp.log(1.0 + jnp.exp(-jnp.abs(xg)))
    act = jnp.where(lane < 2 * GDN_HEADS, -jnp.exp(alog_ref[...]) * softplus, jax.nn.sigmoid(a))

    ti = lax.broadcasted_iota(jnp.int32, (CHUNK, CHUNK), 0)
    tj = lax.broadcasted_iota(jnp.int32, (CHUNK, CHUNK), 1)
    tri_lo = (tj <= ti).astype(BF16)
    tri_up = (tj >= ti).astype(BF16)
    for c in range(n_chunks):
        ac = act[c * CHUNK:(c + 1) * CHUNK]
        pieces = _split3(ac)
        lo = functools.reduce(jnp.add, [_dot(tri_lo, pc) for pc in pieces])
        up = functools.reduce(jnp.add, [_dot(tri_up, pc) for pc in pieces])
        gcb_s[c * CHUNK:(c + 1) * CHUNK, :] = jnp.where(lane < GDN_HEADS, lo,
                                                        jnp.where(lane < 2 * GDN_HEADS, up, ac))
    expand = e_ref[0].astype(BF16)
    gf_s[...] = functools.reduce(jnp.add, [_dot(pc, expand) for pc in _split3(gcb_s[...])])

    ri = lax.broadcasted_iota(jnp.int32, (CHUNK, LANES), 0)
    cj = lax.broadcasted_iota(jnp.int32, (CHUNK, LANES), 1) % CHUNK
    eye = (ri == cj).astype(F32)

    def pmm(x, y):
        return _dot(x.astype(BF16), _stack(y, low).astype(BF16))

    def row_form(g):
        gt = jnp.concatenate([g, jnp.zeros_like(g)], axis=0).T
        r0 = jnp.broadcast_to(gt[0:1, :], (CHUNK, LANES))
        r1 = jnp.broadcast_to(gt[GDN_DK:GDN_DK + 1, :], (CHUNK, LANES))
        return jnp.where(low, r0, pltpu.roll(r1, GDN_DK, axis=1))

    out_refs = ((uf_ref, wf_ref, af_ref, qdf_ref, kdf_ref, glf_ref),
                (ub_ref, wb_ref, abk_ref, qdb_ref, kdb_ref, glb_ref))
    incl = (ri >= cj, ri <= cj)
    strict = (ri > cj, ri < cj)
    diag8 = (ri // 8) == (cj // 8)
    merge_masks = [((ri // (2 * s)) == (cj // (2 * s))) & ((ri // s) != (cj // s)) for s in (8, 16, 32)]

    def group(it, carry):
        cs = [it * PREP_GROUP + cc for cc in range(PREP_GROUP)]
        rows = [pl.ds(pl.multiple_of(c * CHUNK, CHUNK), CHUNK) for c in cs]
        qc = [qn_s[r, :] for r in rows]
        kc = [kn_s[r, :] for r in rows]
        vc = [vn_s[r, :] for r in rows]
        kst = [_stack(k, low).astype(BF16) for k in kc]
        kk = [_dot_nt(k.astype(BF16), ks) for k, ks in zip(kc, kst)]
        qk = [_dot_nt(q.astype(BF16), ks) for q, ks in zip(qc, kst)]
        chains = [(ci, d) for ci in range(PREP_GROUP) for d in range(2)]
        gc = [gf_s[rows[ci], d * LANES:(d + 1) * LANES] for ci, d in chains]
        beta = [gf_s[rows[ci], (2 + d) * LANES:(3 + d) * LANES] for ci, d in chains]
        gr = [row_form(g) for g in gc]
        dm = [jnp.exp(jnp.where(incl[d], g - r, -jnp.inf)) for (ci, d), g, r in zip(chains, gc, gr)]
        lm = [jnp.where(strict[d], b * kk[ci] * m, 0.0) for (ci, d), b, m in zip(chains, beta, dm)]
        aint = [(qk[ci] * m).astype(BF16) for (ci, d), m in zip(chains, dm)]
        x = [-jnp.where(diag8, l, 0.0) for l in lm]
        t = [eye + xx for xx in x]
        for _ in range(2):
            x = [pmm(xx, xx) for xx in x]
            t = [tt + pmm(tt, xx) for tt, xx in zip(t, x)]
        for off in merge_masks:
            tc = [pmm(tt, jnp.where(off, l, 0.0)) for tt, l in zip(t, lm)]
            t = [tt - pmm(a, tt) for tt, a in zip(t, tc)]
        egc = [jnp.exp(g) for g in gc]
        u = [pmm(tt, vc[ci] * b) for (ci, d), tt, b in zip(chains, t, beta)]
        w = [pmm(tt, kc[ci] * b * e).astype(BF16) for (ci, d), tt, b, e in zip(chains, t, beta, egc)]
        qd = [(qc[ci] * e).astype(BF16) for (ci, d), e in zip(chains, egc)]
        gtot = [g[CHUNK - 1:CHUNK, :] if d == 0 else g[0:1, :] for (ci, d), g in zip(chains, gc)]
        kd = [(kc[ci] * jnp.exp(gt - g)).astype(BF16) for (ci, d), gt, g in zip(chains, gtot, gc)]
        for n, (ci, d) in enumerate(chains):
            u_ref, w_ref, a_ref, qd_ref, kd_ref, gl_ref = out_refs[d]
            u_ref[0, rows[ci], :] = u[n]
            w_ref[0, rows[ci], :] = w[n]
            a_ref[0, rows[ci], :] = aint[n]
            qd_ref[0, rows[ci], :] = qd[n]
            kd_ref[0, rows[ci], :] = kd[n]
            gl_ref[0, pl.ds(cs[ci], 1), :, :] = jnp.broadcast_to(jnp.exp(gtot[n]), (1, 8, LANES))
        return carry

    lax.fori_loop(0, n_chunks // PREP_GROUP, group, 0)


def _gdn_prep(qkv, ab, conv_w, alog128, dtb128, expand, n_seq, seq_len):
    rb = max(seq_len, PREP_ROWS)
    nb = n_seq * seq_len // rb
    n_chunks = rb // CHUNK
    col = lambda off: pl.BlockSpec((rb, LANES), lambda s, p: (s, off + p))
    cw = lambda off: pl.BlockSpec((CONV_K, LANES), lambda s, p: (0, off + p))
    vec = pl.BlockSpec((1, LANES), lambda s, p: (0, 0))
    big = lambda: pl.BlockSpec((1, rb, LANES), lambda s, p: (s, 0, p))
    glspec = lambda: pl.BlockSpec((1, n_chunks, 8, LANES), lambda s, p: (s, 0, 0, p))
    shp = lambda dt: jax.ShapeDtypeStruct((nb, rb, GDN_W), dt)
    glshp = jax.ShapeDtypeStruct((nb, n_chunks, 8, GDN_W), F32)
    outs = pl.pallas_call(
        functools.partial(_gdn_prep_kernel, seq_len),
        out_shape=[shp(F32), shp(F32)] + [shp(BF16)] * 8 + [glshp, glshp],
        grid=(nb, N_PAIRS),
        in_specs=[col(0), col(N_PAIRS), col(2 * N_PAIRS), cw(0), cw(N_PAIRS), cw(2 * N_PAIRS),
                  pl.BlockSpec((rb, LANES), lambda s, p: (s, 0)), vec, vec,
                  pl.BlockSpec((1, LANES, 4 * LANES), lambda s, p: (p, 0, 0))],
        out_specs=[big() for _ in range(10)] + [glspec(), glspec()],
        scratch_shapes=[pltpu.VMEM((rb, LANES), F32)] * 4 + [pltpu.VMEM((rb, 4 * LANES), F32)],
        compiler_params=_cparams(("parallel", "parallel")),
        name="gdn_prep_%d" % seq_len,
    )(qkv, qkv, qkv, conv_w, conv_w, conv_w, ab, alog128, dtb128, expand)
    per_seq = [o.reshape(n_seq, seq_len, GDN_W) for o in outs[:10]]
    return per_seq + [o.reshape(n_seq, seq_len // CHUNK, 8, GDN_W) for o in outs[10:]]


def _gdn_scan_kernel(uf_ref, ub_ref, wf_ref, wb_ref, af_ref, abk_ref, qdf_ref, qdb_ref, kdf_ref, kdb_ref,
                     glf_ref, glb_ref, s0_ref, of_ref, ob_ref, sfin_ref, state):
    step = pl.program_id(1)
    n_steps = pl.num_programs(1)
    low = _pair_masks()
    chains = [(d, j, p) for d in range(2) for j in range(SCAN_SEQS) for p in range(N_PAIRS)]

    first = step == 0
    per_dir = ((uf_ref, wf_ref, af_ref, qdf_ref, kdf_ref, glf_ref, of_ref),
               (ub_ref, wb_ref, abk_ref, qdb_ref, kdb_ref, glb_ref, ob_ref))
    def rd(k, d, j, p):
        return per_dir[d][k][j, :, p * LANES:(p + 1) * LANES]

    s = [jnp.where(first, _stack(s0_ref[j, d, p], low), state[idx]) for idx, (d, j, p) in enumerate(chains)]
    sb = [x.astype(BF16) for x in s]
    ws = [_dot(rd(1, *c), b) for c, b in zip(chains, sb)]
    qs = [_dot(rd(3, *c), b) for c, b in zip(chains, sb)]
    vst = [_stack(rd(0, *c) - w, low).astype(BF16) for c, w in zip(chains, ws)]
    upd = [_dot_tn(_stack(rd(4, *c), low), v) for c, v in zip(chains, vst)]
    intra = [_dot(rd(2, *c), v) for c, v in zip(chains, vst)]
    for idx, (d, j, p) in enumerate(chains):
        sl = slice(p * LANES, (p + 1) * LANES)
        state[idx] = s[idx] * per_dir[d][5][j, 0, 0:1, sl] + upd[idx]
        per_dir[d][6][j, :, sl] = qs[idx] + intra[idx]

    @pl.when(step == n_steps - 1)
    def _():
        for idx, (d, j, p) in enumerate(chains):
            s = state[idx]
            sfin_ref[j, d, p] = s[:GDN_DK] + s[GDN_DK:]


def _gdn_scan(prep, s0, n_seq, seq_len):
    n_chunks = seq_len // CHUNK
    ns = SCAN_SEQS
    fwd = lambda: pl.BlockSpec((ns, CHUNK, GDN_W), lambda g, i: (g, i, 0))
    bwd = lambda: pl.BlockSpec((ns, CHUNK, GDN_W), lambda g, i: (g, n_chunks - 1 - i, 0))
    glf = pl.BlockSpec((ns, 1, 8, GDN_W), lambda g, i: (g, i, 0, 0))
    glb = pl.BlockSpec((ns, 1, 8, GDN_W), lambda g, i: (g, n_chunks - 1 - i, 0, 0))
    st = lambda: pl.BlockSpec((ns, 2, N_PAIRS, GDN_DK, LANES), lambda g, i: (g, 0, 0, 0, 0))
    oshape = jax.ShapeDtypeStruct((n_seq, seq_len, GDN_W), F32)
    return pl.pallas_call(
        _gdn_scan_kernel,
        out_shape=[oshape, oshape, jax.ShapeDtypeStruct((n_seq, 2, N_PAIRS, GDN_DK, LANES), F32)],
        grid=(n_seq // ns, n_chunks),
        in_specs=[fwd(), bwd()] * 5 + [glf, glb, st()],
        out_specs=[fwd(), bwd(), st()],
        scratch_shapes=[pltpu.VMEM((2 * ns * N_PAIRS, LANES, LANES), F32)],
        compiler_params=_cparams(("parallel", "arbitrary")),
        name="gdn_scan_%d" % seq_len,
    )(*prep, s0)


def _route(sel, s):
    per_group = N_EXPERTS // N_GROUPS
    ninf = -jnp.inf
    sub = lax.broadcasted_iota(jnp.int32, sel.shape, 1).astype(F32)
    gid = lax.broadcasted_iota(jnp.int32, (N_GROUPS, 1, sel.shape[2]), 0).astype(F32)
    m1 = jnp.max(sel, axis=1, keepdims=True)
    i1 = jnp.min(jnp.where(sel == m1, sub, float(per_group)), axis=1, keepdims=True)
    m2 = jnp.max(jnp.where(sub == i1, ninf, sel), axis=1, keepdims=True)
    work = m1 + m2
    gmask = jnp.zeros(work.shape, jnp.bool_)
    for _ in range(TOPK_GROUPS):
        m = jnp.max(work, axis=0, keepdims=True)
        idx = jnp.min(jnp.where(work == m, gid, float(N_GROUPS)), axis=0, keepdims=True)
        pick = gid == idx
        gmask = gmask | pick
        work = jnp.where(pick, ninf, work)
    work = jnp.where(gmask, sel, ninf)
    eid = gid * per_group + sub
    chosen = jnp.zeros(sel.shape, jnp.bool_)
    for _ in range(TOP_K):
        m = jnp.max(jnp.max(work, axis=1, keepdims=True), axis=0, keepdims=True)
        idx = jnp.min(jnp.min(jnp.where(work == m, eid, float(N_EXPERTS)), axis=1, keepdims=True),
                      axis=0, keepdims=True)
        pick = eid == idx
        chosen = chosen | pick
        work = jnp.where(pick, ninf, work)
    wk = jnp.where(chosen, s, 0.0)
    den = jnp.sum(jnp.sum(wk, axis=1, keepdims=True), axis=0, keepdims=True)
    return wk / den * ROUTED_SCALE


def _merge_kernel(x_ref, m_ref, omla_ref, of_ref, ob_ref, z_ref, ga_ref, gb_ref,
                  woa_ref, wob_ref, wo_ref, gpost_ref, gpre_ref, gdng_ref, wr_ref, eb_ref,
                  x1_ref, h2_ref, gates_ref, cnt_ref):
    m = m_ref[0]
    low = _pair_masks()
    o = of_ref[...] + ob_ref[...]
    z = z_ref[...]
    parts = []
    for p in range(N_PAIRS):
        sl = slice(p * LANES, (p + 1) * LANES)
        op = o[:, sl]
        sq = op * op
        s0 = jnp.sum(jnp.where(low, sq, 0.0), axis=-1, keepdims=True)
        s1 = jnp.sum(jnp.where(low, 0.0, sq), axis=-1, keepdims=True)
        ms = jnp.where(low, s0, s1) * (1.0 / GDN_DV)
        parts.append(op * lax.rsqrt(ms + EPS) * gdng_ref[...] * _silu(z[:, sl]))
    og = jnp.concatenate(parts, axis=1).astype(BF16)
    ya = _dot(omla_ref[...], woa_ref[...])
    yb = _dot(og, wob_ref[...])
    mix = (jax.nn.sigmoid(ga_ref[...]) * ya + jax.nn.sigmoid(gb_ref[...]) * yb).astype(BF16)
    y = _dot(mix, wo_ref[...])
    x1 = x_ref[...] + m[2:3] * _rms(y, gpost_ref[...])
    x1_ref[...] = x1
    h2 = _rms(x1, gpre_ref[...]) * (1.0 + m[4:5]) + m[3:4]
    h2_ref[...] = h2.astype(BF16)
    wh, wl, _ = _split3(wr_ref[...])
    hh, hl, _ = _split3(h2)
    logits = _dot_nt(wh, hh) + (_dot_nt(wh, hl) + _dot_nt(wl, hh))
    s = jax.nn.sigmoid(logits)
    sel = s + eb_ref[...]
    tm = s.shape[1]
    shape3 = (N_GROUPS, N_EXPERTS // N_GROUPS, tm)
    gates_t = _route(sel.reshape(shape3), s.reshape(shape3)).reshape(N_EXPERTS, tm)
    gates_ref[...] = gates_t
    member = (gates_t > 0.0).astype(F32)
    for blk in range(tm // MOE_BLOCK):
        cnt_ref[blk] = jnp.sum(member[:, blk * MOE_BLOCK:(blk + 1) * MOE_BLOCK], axis=1, keepdims=True)


def _merge_kernel_into(*refs):
    n_in = 16
    _merge_kernel(*refs[:n_in], *refs[n_in + 4:])


def _merge(x, mods, mod_index, omla, o_f, o_b, pr, wts, tile0, total, into=None):
    t = x.shape[0]
    tm = MERGE_TILE
    bpt = tm // MOE_BLOCK

    def full(a):
        return pl.BlockSpec(a.shape, lambda i: (0,) * a.ndim)

    def rows(w):
        return pl.BlockSpec((tm, w), lambda i: (i, 0))

    def out_rows(w):
        return pl.BlockSpec((tm, w), lambda i: (tile0 + i, 0))

    names = ("woa", "wob", "wo", "gpost", "gpre2", "gdng", "wr_t", "eb")
    args = [x, mods, omla, o_f, o_b, pr["z"], pr["ga"], pr["gb"]] + [wts[n] for n in names]
    in_specs = [rows(D_MODEL), pl.BlockSpec((1, 6, D_MODEL), lambda i: (mod_index(i), 0, 0)),
                rows(MLA_HEADS * V_HEAD), rows(GDN_W), rows(GDN_W), rows(GDN_W),
                rows(D_MODEL), rows(D_MODEL)] + [full(wts[n]) for n in names]
    aliases = {}
    if into is not None:
        aliases = {len(args) + k: k for k in range(4)}
        in_specs = in_specs + [pl.BlockSpec(memory_space=pl.ANY)] * 4
        args = args + list(into)
    return pl.pallas_call(
        _merge_kernel if into is None else _merge_kernel_into,
        out_shape=[jax.ShapeDtypeStruct((total, D_MODEL), F32), jax.ShapeDtypeStruct((total, D_MODEL), BF16),
                   jax.ShapeDtypeStruct((N_EXPERTS, total), F32),
                   jax.ShapeDtypeStruct((total // MOE_BLOCK, N_EXPERTS, 1), F32)],
        grid=(t // tm,),
        in_specs=in_specs,
        out_specs=[out_rows(D_MODEL), out_rows(D_MODEL),
                   pl.BlockSpec((N_EXPERTS, tm), lambda i: (0, tile0 + i)),
                   pl.BlockSpec((bpt, N_EXPERTS, 1), lambda i: (tile0 + i, 0, 0))],
        input_output_aliases=aliases,
        compiler_params=_cparams(("parallel",)),
        name="merge",
    )(*args)


TABLE_W = 256
TILE_TABLE_W = 512


def _ceil_div(x, d):
    return jnp.floor((x + (d - 1)) * (1.0 / d))


def _moe_tables_kernel(cnt_ref, cnt_t_ref, ce_ref, cb_ref, dst_ref, src_ref, ng_ref, tile_ref):
    nb = cnt_ref.shape[0]
    ppt = float(TILE_PIECES)
    ppg = float(LOOP_PIECES)
    ei = lax.broadcasted_iota(jnp.int32, (N_EXPERTS, N_EXPERTS), 0)
    ej = lax.broadcasted_iota(jnp.int32, (N_EXPERTS, N_EXPERTS), 1)
    tri = (ej <= ei).astype(BF16)

    def cumsum_experts(col):
        wide = jnp.broadcast_to(col, (N_EXPERTS, LANES))
        return functools.reduce(jnp.add, [_dot(tri, pc) for pc in _split3(wide)])[:, 0:1]

    eid = lax.broadcasted_iota(jnp.int32, (N_EXPERTS, 1), 0).astype(F32)
    pc_t = _ceil_div(cnt_t_ref[...], PIECE)
    tp = jnp.sum(pc_t, axis=1, keepdims=True)
    rp = _ceil_div(tp, TILE_PIECES) * ppt
    gs_end = cumsum_experts(rp)
    gs = gs_end - rp
    blk = lax.broadcasted_iota(jnp.int32, (1, nb), 1)
    c = lax.broadcasted_iota(jnp.int32, (1, TABLE_W), 1).astype(F32)
    for b in range(nb):
        pc = _ceil_div(cnt_ref[b], PIECE)
        seg_end = cumsum_experts(pc)
        seg = seg_end - pc
        blk_off = jnp.sum(jnp.where(blk < b, pc_t, 0.0), axis=1, keepdims=True)
        nvalid = seg_end[N_EXPERTS - 1:N_EXPERTS, :]
        ce = jnp.minimum(jnp.sum((seg_end <= c).astype(F32), axis=0, keepdims=True), N_EXPERTS - 1.0)
        onehot = eid == ce
        seg_sel = jnp.sum(jnp.where(onehot, seg, 0.0), axis=0, keepdims=True)
        base_sel = jnp.sum(jnp.where(onehot, gs + blk_off - seg, 0.0), axis=0, keepdims=True)
        valid = c < nvalid
        dump = DUMP_PIECE0 + b * LOOP_PIECES + (c - ppg * jnp.floor(c * (1.0 / ppg)))
        dst = jnp.where(valid, base_sel + c, dump)
        row = slice(b, b + 1)
        ce_ref[row, :] = ce.astype(jnp.int32)
        cb_ref[row, :] = jnp.where(valid, (c - seg_sel) * PIECE, -float(1 << 20)).astype(jnp.int32)
        dst_ref[row, :] = dst.astype(jnp.int32)
        src_ref[row, :] = jnp.where(valid, dst, dst[:, 0:1]).astype(jnp.int32)
        ng_ref[row, :] = jnp.broadcast_to(_ceil_div(nvalid, LOOP_PIECES), (1, LANES)).astype(jnp.int32)
    j = lax.broadcasted_iota(jnp.int32, (1, TILE_TABLE_W), 1).astype(F32)
    start = j * ppt
    te = jnp.minimum(jnp.sum((gs_end <= start).astype(F32), axis=0, keepdims=True), N_EXPERTS - 1.0)
    onehot = eid == te
    tp_sel = jnp.sum(jnp.where(onehot, tp, 0.0), axis=0, keepdims=True)
    gs_sel = jnp.sum(jnp.where(onehot, gs, 0.0), axis=0, keepdims=True)
    n_used = gs_end[N_EXPERTS - 1:N_EXPERTS, :] * (1.0 / ppt)
    used = j < n_used
    tv = jnp.where(used, jnp.clip((tp_sel - (start - gs_sel)) * PIECE, 0.0, float(EXP_TILE)), 0.0)
    tin = jnp.where(used, j, n_used - 1.0)
    tout = jnp.where(used, j, float(N_EXP_TILES))
    tile_ref[...] = jnp.zeros(tile_ref.shape, jnp.int32)
    for r, v in enumerate((te, tv, tin, tout)):
        tile_ref[r:r + 1, :] = v.astype(jnp.int32)


def _dispatch_tables(cnt):
    nb = cnt.shape[0]
    tab = jax.ShapeDtypeStruct((nb, TABLE_W), jnp.int32)
    ce, cb, dst, src, ng, tile = pl.pallas_call(
        _moe_tables_kernel,
        out_shape=[tab, tab, tab, tab, jax.ShapeDtypeStruct((nb, LANES), jnp.int32),
                   jax.ShapeDtypeStruct((8, TILE_TABLE_W), jnp.int32)],
        name="moe_tables",
    )(cnt, cnt[:, :, 0].T)
    return {"ce": ce, "cb": cb, "dst": dst, "src": src, "ngroups": ng, "tile": tile}


def _piece_onehot(ce_ref, cb_ref, rank_s, b, g, extra=None):
    sub = lax.broadcasted_iota(jnp.int32, (PIECE, 1), 0).astype(F32)
    ps, ex = [], []
    for cc in range(GROUP_PIECES):
        c = g * GROUP_PIECES + cc
        e = ce_ref[b, c]
        base = cb_ref[b, c].astype(F32)
        hit = rank_s[pl.ds(e, 1), :] == base + sub
        ps.append(jnp.where(hit, 1.0, 0.0).astype(BF16))
        if extra is not None:
            ex.append(jnp.sum(jnp.where(hit, extra[pl.ds(e, 1), :], 0.0), axis=-1, keepdims=True))
    p = jnp.concatenate(ps, axis=0)
    return (p, jnp.concatenate(ex, axis=0)) if extra is not None else p


def _block_ranks(gt):
    n = gt.shape[1]
    ti = lax.broadcasted_iota(jnp.int32, (n, n), 0)
    tj = lax.broadcasted_iota(jnp.int32, (n, n), 1)
    before = (ti < tj).astype(BF16)
    member = gt > 0.0
    rank = _dot(member.astype(BF16), before)
    return jnp.where(member, rank, -1.0)


def _moe_sort_kernel(ce_ref, cb_ref, dst_ref, ng_ref, h_ref, gt_ref, xg_ref, stage, rank_s, gate_s, sem):
    b = pl.program_id(0)
    slot = b % 2
    gt = gt_ref[...]
    rank_s[...] = _block_ranks(gt)
    gate_s[...] = gt
    lane = lax.broadcasted_iota(jnp.int32, (1, LANES), 1)

    def piece_copy(blk, sl, c):
        r0 = pl.multiple_of(c * PIECE, PIECE)
        d0 = pl.multiple_of(dst_ref[blk, c] * PIECE, PIECE)
        return pltpu.make_async_copy(stage.at[sl, pl.ds(r0, PIECE)], xg_ref.at[pl.ds(d0, PIECE)], sem.at[sl])

    def groups(it, carry):
        gs = [it * LOOP_GROUPS + k for k in range(LOOP_GROUPS)]
        sel = [_piece_onehot(ce_ref, cb_ref, rank_s, b, g, gate_s) for g in gs]
        xs = [_dot(p, h_ref[...]).astype(BF16) for p, _ in sel]
        for g, x, (_, gcol) in zip(gs, xs, sel):
            hi, mid, lo = (t.astype(F32) for t in _split3(gcol))
            gblk = jnp.where(lane == 0, hi, jnp.where(lane == 1, mid, jnp.where(lane == 2, lo, 0.0)))
            r0 = pl.multiple_of(g * GROUP_ROWS, GROUP_ROWS)
            stage[slot, pl.ds(r0, GROUP_ROWS), :] = jnp.concatenate([x, gblk.astype(BF16)], axis=1)
        for cc in range(LOOP_PIECES):
            piece_copy(b, slot, it * LOOP_PIECES + cc).start()
        return carry

    lax.fori_loop(0, ng_ref[b, 0], groups, 0)

    def drain(blk, sl):
        def wait_some(it, carry):
            for cc in range(LOOP_PIECES):
                piece_copy(blk, sl, it * LOOP_PIECES + cc).wait()
            return carry
        lax.fori_loop(0, ng_ref[blk, 0], wait_some, 0)

    @pl.when(b > 0)
    def _():
        drain(b - 1, 1 - slot)

    @pl.when(b == pl.num_programs(0) - 1)
    def _():
        drain(b, slot)


def _moe_sort(h2, gates_t, tabs):
    nb = h2.shape[0] // MOE_BLOCK
    grid_spec = pltpu.PrefetchScalarGridSpec(
        num_scalar_prefetch=4, grid=(nb,),
        in_specs=[pl.BlockSpec((MOE_BLOCK, D_MODEL), lambda b, *_: (b, 0)),
                  pl.BlockSpec((N_EXPERTS, MOE_BLOCK), lambda b, *_: (0, b))],
        out_specs=pl.BlockSpec(memory_space=pl.ANY),
        scratch_shapes=[pltpu.VMEM((2, STAGE_PIECES * PIECE, DISPATCH_W), BF16),
                        pltpu.VMEM((N_EXPERTS, MOE_BLOCK), F32), pltpu.VMEM((N_EXPERTS, MOE_BLOCK), F32),
                        pltpu.SemaphoreType.DMA((2,))])
    return pl.pallas_call(
        _moe_sort_kernel,
        out_shape=jax.ShapeDtypeStruct((DISPATCH_ROWS, DISPATCH_W), BF16),
        grid_spec=grid_spec,
        compiler_params=_cparams(("arbitrary",)),
        name="moe_sort",
    )(tabs["ce"], tabs["cb"], tabs["dst"], tabs["ngroups"], h2, gates_t)


def _moe_expert_kernel(tile_ref, x_ref, wg_ref, wu_ref, wd_ref, y_ref):
    valid = tile_ref[1, pl.program_id(0)]

    @pl.when(valid > 0)
    def _():
        keep = lax.broadcasted_iota(jnp.int32, (EXP_TILE, 1), 0) < valid
        xrow = x_ref[...]
        x = jnp.where(keep, xrow[:, :D_MODEL], jnp.zeros((), BF16))
        g = jnp.sum(jnp.where(keep, xrow[:, D_MODEL:].astype(F32), 0.0), axis=-1, keepdims=True)
        hg = _dot(x, wg_ref[0].astype(BF16))
        hu = _dot(x, wu_ref[0].astype(BF16))
        act = (_silu(hg) * hu * g).astype(BF16)
        y_ref[...] = _dot(act, wd_ref[0].astype(BF16)).astype(BF16)


def _moe_expert(xg, tabs, wts):
    grid_spec = pltpu.PrefetchScalarGridSpec(
        num_scalar_prefetch=1, grid=(N_EXP_TILES,),
        in_specs=[pl.BlockSpec((EXP_TILE, DISPATCH_W), lambda j, tt: (tt[2, j], 0)),
                  pl.BlockSpec((1, D_MODEL, D_EXPERT), lambda j, tt: (tt[0, j], 0, 0)),
                  pl.BlockSpec((1, D_MODEL, D_EXPERT), lambda j, tt: (tt[0, j], 0, 0)),
                  pl.BlockSpec((1, D_EXPERT, D_MODEL), lambda j, tt: (tt[0, j], 0, 0))],
        out_specs=pl.BlockSpec((EXP_TILE, D_MODEL), lambda j, tt: (tt[3, j], 0)))
    return pl.pallas_call(
        _moe_expert_kernel,
        out_shape=jax.ShapeDtypeStruct((DISPATCH_ROWS, D_MODEL), BF16),
        grid_spec=grid_spec,
        compiler_params=_cparams(("arbitrary",)),
        name="moe_expert",
    )(tabs["tile"], xg, wts["w_gate"], wts["w_up"], wts["w_down"])


def _moe_combine_kernel(ce_ref, cb_ref, src_ref, ng_ref, yg_ref, gt_ref, h_ref, x1_ref, m_ref, gpost_ref,
                        wsg_ref, wsu_ref, wsd_ref, out_ref, stage, rank_s, acc_s, sem):
    b = pl.program_id(0)
    slot = b % 2

    def piece_copy(blk, sl, c):
        r0 = pl.multiple_of(c * PIECE, PIECE)
        s0 = pl.multiple_of(src_ref[blk, c] * PIECE, PIECE)
        return pltpu.make_async_copy(yg_ref.at[pl.ds(s0, PIECE)], stage.at[sl, pl.ds(r0, PIECE)], sem.at[sl])

    def fetch(blk, sl):
        def start_some(it, carry):
            for cc in range(LOOP_PIECES):
                piece_copy(blk, sl, it * LOOP_PIECES + cc).start()
            return carry
        lax.fori_loop(0, ng_ref[blk, 0], start_some, 0)

    @pl.when(b == 0)
    def _():
        fetch(0, 0)

    @pl.when(b + 1 < pl.num_programs(0))
    def _():
        fetch(b + 1, 1 - slot)

    rank_s[...] = _block_ranks(gt_ref[...])
    h = h_ref[...]
    sh = (_silu(_dot(h, wsg_ref[...])) * _dot(h, wsu_ref[...])).astype(BF16)
    acc_s[...] = _dot(sh, wsd_ref[...])

    def wait_some(it, carry):
        for cc in range(LOOP_PIECES):
            piece_copy(b, slot, it * LOOP_PIECES + cc).wait()
        return carry

    lax.fori_loop(0, ng_ref[b, 0], wait_some, 0)

    def groups(it, carry):
        gs = [it * LOOP_GROUPS + k for k in range(LOOP_GROUPS)]
        ps = [_piece_onehot(ce_ref, cb_ref, rank_s, b, g) for g in gs]
        ys = [stage[slot, pl.ds(pl.multiple_of(g * GROUP_ROWS, GROUP_ROWS), GROUP_ROWS), :] for g in gs]
        acc_s[...] += functools.reduce(jnp.add, [_dot_tn(p, y) for p, y in zip(ps, ys)])
        return carry

    lax.fori_loop(0, ng_ref[b, 0], groups, 0)
    m = m_ref[0]
    out_ref[...] = x1_ref[...] + m[5:6] * _rms(acc_s[...], gpost_ref[...])


def _moe_combine(yg, gates_t, h2, x1, mods, mod_index, tabs, wts):
    t = h2.shape[0]
    nb = t // MOE_BLOCK

    def full(a):
        return pl.BlockSpec(a.shape, lambda b, *_: (0,) * a.ndim)

    grid_spec = pltpu.PrefetchScalarGridSpec(
        num_scalar_prefetch=4, grid=(nb,),
        in_specs=[pl.BlockSpec(memory_space=pl.ANY),
                  pl.BlockSpec((N_EXPERTS, MOE_BLOCK), lambda b, *_: (0, b)),
                  pl.BlockSpec((MOE_BLOCK, D_MODEL), lambda b, *_: (b, 0)),
                  pl.BlockSpec((MOE_BLOCK, D_MODEL), lambda b, *_: (b, 0)),
                  pl.BlockSpec((1, 6, D_MODEL), lambda b, *_: (mod_index(b), 0, 0)),
                  full(wts["gpost2"]), full(wts["wsg"]), full(wts["wsu"]), full(wts["wsd"])],
        out_specs=pl.BlockSpec((MOE_BLOCK, D_MODEL), lambda b, *_: (b, 0)),
        scratch_shapes=[pltpu.VMEM((2, STAGE_PIECES * PIECE, D_MODEL), BF16),
                        pltpu.VMEM((N_EXPERTS, MOE_BLOCK), F32), pltpu.VMEM((MOE_BLOCK, D_MODEL), F32),
                        pltpu.SemaphoreType.DMA((2,))])
    return pl.pallas_call(
        _moe_combine_kernel,
        out_shape=jax.ShapeDtypeStruct((t, D_MODEL), F32),
        grid_spec=grid_spec,
        compiler_params=_cparams(("arbitrary",)),
        name="moe_combine",
    )(tabs["ce"], tabs["cb"], tabs["src"], tabs["ngroups"], yg, gates_t, h2, x1, mods,
      wts["gpost2"], wts["wsg"], wts["wsu"], wts["wsd"])


def _moe(h2, gates_t, cnt, x1, mods, mod_index, wts):
    tabs = _dispatch_tables(cnt)
    xg = _moe_sort(h2, gates_t, tabs)
    yg = _moe_expert(xg, tabs, wts)
    return _moe_combine(yg, gates_t, h2, x1, mods, mod_index, tabs, wts)


def _rope_swap(w):
    nf = QK_ROPE // 4
    parts = [w[..., i * nf:(i + 1) * nf] for i in range(4)]
    return jnp.concatenate([parts[1], parts[0], parts[3], parts[2]], axis=-1)


def _head_block(nope, rope):
    lead = nope.shape[:-2] if nope is not None else rope.shape[:-2]
    nope = jnp.zeros(lead + (MLA_HEADS, QK_NOPE), F32) if nope is None else nope
    rope = jnp.zeros(lead + (MLA_HEADS, QK_ROPE), F32) if rope is None else rope
    pad = jnp.zeros(lead + (MLA_HEADS, HEAD_PAD - MLA_QK), F32)
    return jnp.concatenate([nope, rope, pad], axis=-1).reshape(lead + (MLA_HEADS * HEAD_PAD,))


def _rope_block(w):
    lead = w.shape[:-1]
    return jnp.concatenate([jnp.zeros(lead + (QK_NOPE,), F32), w,
                            jnp.zeros(lead + (HEAD_PAD - MLA_QK,), F32)], axis=-1)


def _rope_tables(n_tokens):
    rows = n_tokens // GRID_W
    row = np.repeat(np.arange(rows, dtype=np.float64), GRID_W)
    colv = np.tile(np.arange(GRID_W, dtype=np.float64), rows)
    nf = QK_ROPE // 4
    inv = ROPE_THETA ** (-np.arange(nf, dtype=np.float64) / nf)
    ang_r = row[:, None] * inv
    ang_c = colv[:, None] * inv
    cos32 = np.concatenate([np.cos(ang_r), np.cos(ang_r), np.cos(ang_c), np.cos(ang_c)], axis=-1)
    sin32 = np.concatenate([-np.sin(ang_r), np.sin(ang_r), -np.sin(ang_c), np.sin(ang_c)], axis=-1)
    ones = np.ones((n_tokens, QK_NOPE))
    tail = np.zeros((n_tokens, HEAD_PAD - MLA_QK))
    cos = np.concatenate([ones, cos32, tail], axis=-1)
    sin = np.concatenate([np.zeros((n_tokens, QK_NOPE)), sin32, tail], axis=-1)
    return jnp.asarray(cos, F32), jnp.asarray(sin, F32)


def _expand_matrix():
    e = np.zeros((N_PAIRS, LANES, 4 * LANES), np.float32)
    for p in range(N_PAIRS):
        for blk in range(4):
            for hh in range(2):
                src = blk * GDN_HEADS + 2 * p + hh
                e[p, src, blk * LANES + hh * GDN_DK: blk * LANES + (hh + 1) * GDN_DK] = 1.0
    return jnp.asarray(e)


def _prepare_weights(w_in, q_norm_g, kv_norm_g, w_uq, w_ukv, w_oa, w_ob, w_o, g_post_mix, g_pre_ffn,
                     g_post_ffn, gdn_norm_g, w_router, e_bias, w_gate, w_up, w_down, ws_gate, ws_up, ws_down):
    offs = np.cumsum((Q_LORA, KV_LORA, QK_ROPE, 3 * GDN_W, GDN_W, 2 * GDN_HEADS, 2 * GDN_HEADS,
                      D_MODEL, D_MODEL))[:-1].tolist()
    cq, ckv, kr, qkv, z, a, b, ga, gb = jnp.split(w_in, offs, axis=-1)
    ab = jnp.concatenate([a, b, jnp.zeros((D_MODEL, LANES - 4 * GDN_HEADS), F32)], axis=-1)
    wcat = jnp.concatenate([cq, ckv, _rope_block(kr), _rope_block(_rope_swap(kr)), qkv, z, ab, ga, gb],
                           axis=-1).astype(BF16)
    uq = w_uq.reshape(Q_LORA, MLA_HEADS, MLA_QK)
    ukv = w_ukv.reshape(KV_LORA, MLA_HEADS, QK_NOPE + V_HEAD)
    return {
        "wcat": wcat,
        "qg": q_norm_g.reshape(1, Q_LORA), "kvg": kv_norm_g.reshape(1, KV_LORA),
        "wuq": _head_block(uq[..., :QK_NOPE], uq[..., QK_NOPE:]).astype(BF16),
        "wuqs": _head_block(None, _rope_swap(uq[..., QK_NOPE:])).astype(BF16),
        "wuk": _head_block(ukv[..., :QK_NOPE], None).astype(BF16),
        "wuv": ukv[..., QK_NOPE:].reshape(KV_LORA, MLA_HEADS * V_HEAD).astype(BF16),
        "woa": w_oa.astype(BF16), "wob": w_ob.astype(BF16), "wo": w_o.astype(BF16),
        "gpost": g_post_mix.reshape(1, D_MODEL), "gpre2": g_pre_ffn.reshape(1, D_MODEL),
        "gpost2": g_post_ffn.reshape(1, D_MODEL),
        "gdng": jnp.tile(gdn_norm_g.reshape(1, GDN_DV), (1, 2)),
        "wr_t": w_router.T, "eb": e_bias.reshape(N_EXPERTS, 1),
        "w_gate": w_gate, "w_up": w_up, "w_down": w_down,
        "wsg": ws_gate.astype(BF16), "wsu": ws_up.astype(BF16), "wsd": ws_down.astype(BF16),
    }


def _pad_lanes(v):
    v = v.reshape(1, -1)
    return jnp.concatenate([v, jnp.zeros((1, LANES - v.shape[1]), F32)], axis=-1)


def _layer_group(x, n_seq, seq_len, mods, mod_index, wts, gpre, conv_w, alog128, dtb128,
                 expand, rope_tabs, extra_kv, s0, tile0, into):
    pr = _proj(x, mods, mod_index, gpre, wts, rope_tabs)
    kvs = list(extra_kv) + [(pr["k"], pr["v"], seq_len)]
    omla = _attention(pr["q"], kvs, n_seq, seq_len, "attn_%d" % seq_len)
    prep = _gdn_prep(pr["qkv"], pr["ab"], conv_w, alog128, dtb128, expand, n_seq, seq_len)
    o_f, o_b, s_fin = _gdn_scan(prep, s0, n_seq, seq_len)
    t = n_seq * seq_len
    merged = _merge(x, mods, lambda i: mod_index(i * (MERGE_TILE // ROW_TILE)), omla,
                    o_f.reshape(t, GDN_W), o_b.reshape(t, GDN_W), pr, wts, tile0, N_TOKENS, into)
    return merged, pr, s_fin


def _state_to_pairs(s):
    b = s.shape[0]
    s = s.reshape(b, 2, N_PAIRS, 2, GDN_DK, GDN_DV)
    return jnp.transpose(s, (0, 1, 2, 4, 3, 5)).reshape(b, 2, N_PAIRS, GDN_DK, 2 * GDN_DV)


def _pairs_to_state(s):
    b = s.shape[0]
    s = s.reshape(b, 2, N_PAIRS, GDN_DK, 2, GDN_DV)
    return jnp.transpose(s, (0, 1, 2, 4, 3, 5)).reshape(b, 2, GDN_HEADS, GDN_DK, GDN_DV)


def kernel(x_prompt, x_sample, cache_ckv, cache_krope, state_delta, c, c_ctx, w_ada, b_ada, g_pre_mix,
           g_post_mix, g_pre_ffn, g_post_ffn, w_in, q_norm_g, kv_norm_g, w_uq, w_ukv, conv_w, a_log,
           dt_bias, gdn_norm_g, w_oa, w_ob, w_o, w_router, e_bias, w_gate, w_up, w_down, ws_gate, ws_up,
           ws_down):
    batch, seq, _ = x_prompt.shape
    dec_batch, dec_seq, _ = x_sample.shape
    past = cache_ckv.shape[2]
    assert batch * seq + dec_batch * dec_seq == N_TOKENS, "dispatch buffers are sized for N_TOKENS"
    y_p = x_prompt.reshape(batch * seq, D_MODEL)
    y_s = x_sample.reshape(dec_batch * dec_seq, D_MODEL)
    expand = _expand_matrix()
    rope_tabs = _rope_tables(dec_seq)
    cond8 = jnp.concatenate([c_ctx[None], c, jnp.zeros((8 - 1 - dec_batch, D_MODEL), F32)], axis=0)
    ckv_out, krope_out, state_out = [], [], []
    for l in range(DEPTH):
        wts = _prepare_weights(w_in[l], q_norm_g[l], kv_norm_g[l], w_uq[l], w_ukv[l], w_oa[l], w_ob[l],
                               w_o[l], g_post_mix[l], g_pre_ffn[l], g_post_ffn[l], gdn_norm_g[l],
                               w_router[l], e_bias[l], w_gate[l], w_up[l], w_down[l], ws_gate[l],
                               ws_up[l], ws_down[l])
        gpre = g_pre_mix[l].reshape(1, D_MODEL)
        alog128 = _pad_lanes(a_log[l])
        dtb128 = _pad_lanes(dt_bias[l])
        mods = _mods(cond8, w_ada[l], b_ada[l].reshape(1, -1)).reshape(8, 6, D_MODEL)

        zero_state = jnp.zeros((batch, 2, N_PAIRS, GDN_DK, LANES), F32)
        merged_p, pr_p, s_fin = _layer_group(
            y_p, batch, seq, mods, lambda i: 0, wts, gpre, conv_w[l], alog128, dtb128,
            expand, None, [], zero_state, 0, None)
        ckv_out.append(pr_p["ckv"].reshape(batch, seq, KV_LORA))
        krope_out.append(pr_p["kr"][:, QK_NOPE:MLA_QK].reshape(batch, seq, QK_ROPE))
        state_out.append(_pairs_to_state(s_fin))

        kr_ctx = _rope_block(cache_krope[:, l].reshape(dec_batch * past, QK_ROPE))
        k_ctx, v_ctx = _cache_kv(cache_ckv[:, l].reshape(dec_batch * past, KV_LORA), kr_ctx, wts)
        tiles_per_seq = dec_seq // ROW_TILE
        merged_s, _, _ = _layer_group(
            y_s, dec_batch, dec_seq, mods, lambda i: 1 + i // tiles_per_seq,
            wts, gpre, conv_w[l], alog128, dtb128, expand,
            rope_tabs, [(k_ctx, v_ctx, past)], _state_to_pairs(state_delta[:, l]),
            batch * seq // MERGE_TILE, merged_p)

        x1, h2, gates_t, cnt = merged_s
        ctx_blocks = batch * seq // MOE_BLOCK
        blocks_per_seq = dec_seq // MOE_BLOCK
        y = _moe(h2, gates_t, cnt, x1, mods,
                 lambda b: jnp.where(b < ctx_blocks, 0, 1 + (b - ctx_blocks) // blocks_per_seq), wts)
        y_p, y_s = y[:batch * seq], y[batch * seq:]
    new_ckv = jnp.stack(ckv_out, axis=1)
    new_krope = jnp.stack(krope_out, axis=1)
    new_state = jnp.stack(state_out, axis=1)
    return (y_p.reshape(batch, seq, D_MODEL), y_s.reshape(dec_batch, dec_seq, D_MODEL),
            new_ckv, new_krope, new_state)
```

```python
import functools

import numpy as np
import jax
import jax.numpy as jnp
from jax import lax
from jax.experimental import pallas as pl
from jax.experimental.pallas import tpu as pltpu

F32 = jnp.float32
BF16 = jnp.bfloat16

D_MODEL = 1024
DEPTH = 1
GRID_W = 64
MLA_HEADS = 8
QK_NOPE = 64
QK_ROPE = 32
V_HEAD = 64
Q_LORA = 256
KV_LORA = 256
ROPE_THETA = 10000.0
GDN_HEADS = 8
GDN_DK = 64
GDN_DV = 64
CONV_K = 5
CHUNK = 64
N_EXPERTS = 64
TOP_K = 8
N_GROUPS = 8
TOPK_GROUPS = 4
D_EXPERT = 256
D_SHARED = 256
ROUTED_SCALE = 2.5
EPS = 1e-6

LANES = 128
LOG2_E = 1.4426950408889634
MLA_QK = QK_NOPE + QK_ROPE
GDN_W = GDN_HEADS * GDN_DK
N_PAIRS = GDN_HEADS // 2
HEAD_PAD = LANES

_SEG = {}
_off = 0
for _name, _width in (("cq", Q_LORA), ("ckv", KV_LORA), ("kr", LANES), ("krs", LANES),
                      ("qkv", 3 * GDN_W), ("z", GDN_W), ("ab", LANES),
                      ("ga", D_MODEL), ("gb", D_MODEL)):
    _SEG[_name] = (_off, _off + _width)
    _off += _width
N_CAT = _off

ROW_TILE = 256
ATTN_TILE = 512
N_TOKENS = 8192
MOE_BLOCK = 256
MERGE_TILE = 512
PIECE = 16
GROUP_ROWS = 256
GROUP_PIECES = GROUP_ROWS // PIECE
LOOP_GROUPS = 2
LOOP_PIECES = LOOP_GROUPS * GROUP_PIECES
EXP_TILE = 512
TILE_PIECES = EXP_TILE // PIECE
N_BLOCKS = N_TOKENS // MOE_BLOCK
STAGE_PIECES = -(-((MOE_BLOCK * TOP_K + N_EXPERTS * (PIECE - 1)) // PIECE) // LOOP_PIECES) * LOOP_PIECES
N_EXP_TILES = -(-((N_TOKENS * TOP_K + N_BLOCKS * N_EXPERTS * (PIECE - 1)) // PIECE
                  + N_EXPERTS * (TILE_PIECES - 1)) // TILE_PIECES)
DUMP_PIECE0 = N_EXP_TILES * TILE_PIECES
DISPATCH_ROWS = (DUMP_PIECE0 + max(N_BLOCKS * LOOP_PIECES, TILE_PIECES)) * PIECE
DISPATCH_W = D_MODEL + LANES
PREP_GROUP = 8
PREP_ROWS = 1024
SCAN_SEQS = 2
SCAN_CHUNKS = 2
VMEM_LIMIT = 56 * 1024 * 1024


def _dot(a, b):
    return jnp.dot(a, b, preferred_element_type=F32)


def _dot_nt(a, b):
    return lax.dot_general(a, b, (((1,), (1,)), ((), ())), preferred_element_type=F32)


def _dot_tn(a, b):
    return lax.dot_general(a, b, (((0,), (0,)), ((), ())), preferred_element_type=F32)


def _rms(x, g):
    return x * lax.rsqrt(jnp.mean(x * x, axis=-1, keepdims=True) + EPS) * g


def _silu(x):
    return x * jax.nn.sigmoid(x)


def _cparams(sem):
    return pltpu.CompilerParams(dimension_semantics=sem, vmem_limit_bytes=VMEM_LIMIT)


def _mods_kernel(c_ref, w_ref, b_ref, o_ref):
    s = _silu(c_ref[...]).astype(BF16)
    o_ref[...] = _dot(s, w_ref[...].astype(BF16)) + b_ref[...]


def _mods(cond8, w_ada, b_ada):
    n = w_ada.shape[1]
    bn = 512
    return pl.pallas_call(
        _mods_kernel,
        out_shape=jax.ShapeDtypeStruct((8, n), F32),
        grid=(n // bn,),
        in_specs=[pl.BlockSpec((8, D_MODEL), lambda j: (0, 0)),
                  pl.BlockSpec((D_MODEL, bn), lambda j: (0, j)),
                  pl.BlockSpec((1, bn), lambda j: (0, j))],
        out_specs=pl.BlockSpec((8, bn), lambda j: (0, j)),
        compiler_params=_cparams(("parallel",)),
        name="mods",
    )(cond8, w_ada, b_ada)


def _proj_kernel(rope, x_ref, m_ref, gpre_ref, wcat_ref, qg_ref, kvg_ref, wuq_ref, wuqs_ref,
                 wuk_ref, wuv_ref, cos_ref, sin_ref,
                 q_ref, k_ref, v_ref, ckv_ref, kr_ref, qkv_ref, z_ref, ab_ref, ga_ref, gb_ref):
    m = m_ref[0]
    h = (_rms(x_ref[...], gpre_ref[...]) * (1.0 + m[1:2]) + m[0:1]).astype(BF16)

    def seg(name):
        a, b = _SEG[name]
        return _dot(h, wcat_ref[:, a:b])

    qkv_ref[...] = seg("qkv")
    z_ref[...] = seg("z")
    ab_ref[...] = seg("ab")
    ga_ref[...] = seg("ga")
    gb_ref[...] = seg("gb")

    qn = _rms(seg("cq"), qg_ref[...]).astype(BF16)
    ckv = _rms(seg("ckv"), kvg_ref[...])
    ckv_ref[...] = ckv
    ckv_b = ckv.astype(BF16)
    kr = seg("kr")
    kr_ref[...] = kr
    qm = _dot(qn, wuq_ref[...])
    kk = _dot(ckv_b, wuk_ref[...])
    v_ref[...] = _dot(ckv_b, wuv_ref[...]).astype(BF16)
    scale = MLA_QK ** -0.5 * LOG2_E
    if rope:
        cos = cos_ref[...]
        sin = sin_ref[...]
        qs = _dot(qn, wuqs_ref[...])
        kr = kr * cos + seg("krs") * sin
    for hd in range(MLA_HEADS):
        sl = slice(hd * HEAD_PAD, (hd + 1) * HEAD_PAD)
        qh = qm[:, sl]
        if rope:
            qh = qh * cos + qs[:, sl] * sin
        q_ref[:, sl] = (qh * scale).astype(BF16)
        k_ref[:, sl] = (kk[:, sl] + kr).astype(BF16)


def _proj(x, mods, mod_index, gpre, wts, rope_tabs):
    t = x.shape[0]
    tm = ROW_TILE
    rope = rope_tabs is not None
    if rope:
        cos, sin = rope_tabs
        n_rope_blocks = cos.shape[0] // tm
        rope_spec = pl.BlockSpec((tm, LANES), lambda i: (i % n_rope_blocks, 0))
    else:
        cos = sin = jnp.zeros((8, LANES), F32)
        rope_spec = pl.BlockSpec((8, LANES), lambda i: (0, 0))

    def full(a):
        return pl.BlockSpec(a.shape, lambda i: (0,) * a.ndim)

    def rows(w):
        return pl.BlockSpec((tm, w), lambda i: (i, 0))

    out_widths = (("q", MLA_HEADS * HEAD_PAD, BF16), ("k", MLA_HEADS * HEAD_PAD, BF16),
                  ("v", MLA_HEADS * V_HEAD, BF16), ("ckv", KV_LORA, F32), ("kr", LANES, F32),
                  ("qkv", 3 * GDN_W, F32), ("z", GDN_W, F32), ("ab", LANES, F32),
                  ("ga", D_MODEL, F32), ("gb", D_MODEL, F32))
    outs = pl.pallas_call(
        functools.partial(_proj_kernel, rope),
        out_shape=[jax.ShapeDtypeStruct((t, w), dt) for _, w, dt in out_widths],
        grid=(t // tm,),
        in_specs=[rows(D_MODEL),
                  pl.BlockSpec((1, 6, D_MODEL), lambda i: (mod_index(i), 0, 0)),
                  full(gpre), full(wts["wcat"]), full(wts["qg"]), full(wts["kvg"]),
                  full(wts["wuq"]), full(wts["wuqs"]), full(wts["wuk"]), full(wts["wuv"]),
                  rope_spec, rope_spec],
        out_specs=[rows(w) for _, w, _ in out_widths],
        compiler_params=_cparams(("parallel",)),
        name="proj_rope" if rope else "proj",
    )(x, mods, gpre, wts["wcat"], wts["qg"], wts["kvg"], wts["wuq"], wts["wuqs"],
      wts["wuk"], wts["wuv"], cos, sin)
    return {name: o for (name, _, _), o in zip(out_widths, outs)}


def _cache_kv_kernel(ckv_ref, kr_ref, wuk_ref, wuv_ref, k_ref, v_ref):
    c = ckv_ref[...].astype(BF16)
    kk = _dot(c, wuk_ref[...])
    v_ref[...] = _dot(c, wuv_ref[...]).astype(BF16)
    kr = kr_ref[...]
    for hd in range(MLA_HEADS):
        sl = slice(hd * HEAD_PAD, (hd + 1) * HEAD_PAD)
        k_ref[:, sl] = (kk[:, sl] + kr).astype(BF16)


def _cache_kv(ckv, kr128, wts):
    t = ckv.shape[0]
    tm = 512
    return pl.pallas_call(
        _cache_kv_kernel,
        out_shape=[jax.ShapeDtypeStruct((t, MLA_HEADS * HEAD_PAD), BF16),
                   jax.ShapeDtypeStruct((t, MLA_HEADS * V_HEAD), BF16)],
        grid=(t // tm,),
        in_specs=[pl.BlockSpec((tm, KV_LORA), lambda i: (i, 0)),
                  pl.BlockSpec((tm, LANES), lambda i: (i, 0)),
                  pl.BlockSpec(wts["wuk"].shape, lambda i: (0, 0)),
                  pl.BlockSpec(wts["wuv"].shape, lambda i: (0, 0))],
        out_specs=[pl.BlockSpec((tm, MLA_HEADS * HEAD_PAD), lambda i: (i, 0)),
                   pl.BlockSpec((tm, MLA_HEADS * V_HEAD), lambda i: (i, 0))],
        compiler_params=_cparams(("parallel",)),
        name="cache_kv",
    )(ckv, kr128, wts["wuk"], wts["wuv"])


def _attn_kernel(n_kv, q_ref, *refs):
    k_refs = refs[:n_kv]
    v_refs = refs[n_kv:2 * n_kv]
    o_ref = refs[2 * n_kv]
    lane = lax.broadcasted_iota(jnp.int32, (1, LANES), 1)
    low = lane < V_HEAD
    for pr in range(MLA_HEADS // 2):
        halves = []
        for hd in (2 * pr, 2 * pr + 1):
            sl = slice(hd * HEAD_PAD, (hd + 1) * HEAD_PAD)
            qh = q_ref[:, sl]
            scores = [_dot_nt(qh, kr[:, sl]) for kr in k_refs]
            mx = functools.reduce(jnp.maximum, [jnp.max(s, axis=-1, keepdims=True) for s in scores])
            ps = [jnp.exp2(s - mx) for s in scores]
            den = functools.reduce(jnp.add, [jnp.sum(p, axis=-1, keepdims=True) for p in ps])
            vsl = slice(pr * LANES, (pr + 1) * LANES)
            acc = functools.reduce(jnp.add, [_dot(p.astype(BF16), vr[:, vsl]) for p, vr in zip(ps, v_refs)])
            halves.append(acc / den)
        o_ref[:, pr * LANES:(pr + 1) * LANES] = jnp.where(low, halves[0], halves[1]).astype(BF16)


def _attention(q, kvs, n_seq, seq_len, name):
    tq = min(seq_len, ATTN_TILE)
    nq = seq_len // tq
    n_kv = len(kvs)
    in_specs = [pl.BlockSpec((tq, MLA_HEADS * HEAD_PAD), lambda b, j: (b * nq + j, 0))]
    in_specs += [pl.BlockSpec((rows, MLA_HEADS * HEAD_PAD), lambda b, j: (b, 0)) for _, _, rows in kvs]
    in_specs += [pl.BlockSpec((rows, MLA_HEADS * V_HEAD), lambda b, j: (b, 0)) for _, _, rows in kvs]
    return pl.pallas_call(
        functools.partial(_attn_kernel, n_kv),
        out_shape=jax.ShapeDtypeStruct((n_seq * seq_len, MLA_HEADS * V_HEAD), BF16),
        grid=(n_seq, nq),
        in_specs=in_specs,
        out_specs=pl.BlockSpec((tq, MLA_HEADS * V_HEAD), lambda b, j: (b * nq + j, 0)),
        compiler_params=_cparams(("parallel", "parallel")),
        name=name,
    )(q, *[k for k, _, _ in kvs], *[v for _, v, _ in kvs])


def _pair_masks():
    lane = lax.broadcasted_iota(jnp.int32, (1, LANES), 1)
    return lane < GDN_DK


def _stack(x, low):
    zero = jnp.zeros_like(x)
    return jnp.concatenate([jnp.where(low, x, zero), jnp.where(low, zero, x)], axis=0)


def _split3(x):
    hi = x.astype(BF16)
    r = x - hi.astype(F32)
    mid = r.astype(BF16)
    lo = (r - mid.astype(F32)).astype(BF16)
    return hi, mid, lo


def _gdn_prep_kernel(seq_len, q_ref, k_ref, v_ref, cwq_ref, cwk_ref, cwv_ref, ab_ref, alog_ref, dtb_ref, e_ref,
                     uf_ref, ub_ref, wf_ref, wb_ref, af_ref, abk_ref, qdf_ref, qdb_ref, kdf_ref, kdb_ref,
                     glf_ref, glb_ref,
                     qn_s, kn_s, vn_s, gcb_s, gf_s):
    seq = q_ref.shape[0]
    n_chunks = seq // CHUNK
    low = _pair_masks()
    row = lax.broadcasted_iota(jnp.int32, (seq, 1), 0) % seq_len
    lane = lax.broadcasted_iota(jnp.int32, (1, LANES), 1)

    def conv(x_ref, cw_ref):
        x = x_ref[...]
        acc = jnp.zeros_like(x)
        for j in range(CONV_K):
            sh = CONV_K // 2 - j
            xs = x if sh == 0 else pltpu.roll(x, sh % seq, axis=0)
            src = row - sh
            valid = (src >= 0) & (src < seq_len)
            acc = acc + jnp.where(valid, xs, 0.0) * cw_ref[j:j + 1, :]
        return _silu(acc)

    def l2n(x):
        sq = x * x
        s0 = jnp.sum(jnp.where(low, sq, 0.0), axis=-1, keepdims=True)
        s1 = jnp.sum(jnp.where(low, 0.0, sq), axis=-1, keepdims=True)
        return x * lax.rsqrt(jnp.where(low, s0, s1) + EPS)

    qn_s[...] = l2n(conv(q_ref, cwq_ref)) * (GDN_DK ** -0.5)
    kn_s[...] = l2n(conv(k_ref, cwk_ref))
    vn_s[...] = conv(v_ref, cwv_ref)

    a = ab_ref[...]
    xg = a + dtb_ref[...]
    softplus = jnp.maximum(xg, 0.0) + jnp.log(1.0 + jnp.exp(-jnp.abs(xg)))
    act = jnp.where(lane < 2 * GDN_HEADS, -jnp.exp(alog_ref[...]) * softplus, jax.nn.sigmoid(a))

    ti = lax.broadcasted_iota(jnp.int32, (CHUNK, CHUNK), 0)
    tj = lax.broadcasted_iota(jnp.int32, (CHUNK, CHUNK), 1)
    tri_lo = (tj <= ti).astype(BF16)
    tri_up = (tj >= ti).astype(BF16)
    for c in range(n_chunks):
        ac = act[c * CHUNK:(c + 1) * CHUNK]
        pieces = _split3(ac)
        lo = functools.reduce(jnp.add, [_dot(tri_lo, pc) for pc in pieces])
        up = functools.reduce(jnp.add, [_dot(tri_up, pc) for pc in pieces])
        gcb_s[c * CHUNK:(c + 1) * CHUNK, :] = jnp.where(lane < GDN_HEADS, lo,
                                                        jnp.where(lane < 2 * GDN_HEADS, up, ac))
    expand = e_ref[0].astype(BF16)
    gf_s[...] = functools.reduce(jnp.add, [_dot(pc, expand) for pc in _split3(gcb_s[...])])

    ri = lax.broadcasted_iota(jnp.int32, (CHUNK, LANES), 0)
    cj = lax.broadcasted_iota(jnp.int32, (CHUNK, LANES), 1) % CHUNK
    eye = (ri == cj).astype(F32)

    def pmm(x, y):
        return _dot(x.astype(BF16), _stack(y, low).astype(BF16))

    def row_form(g):
        gt = jnp.concatenate([g, jnp.zeros_like(g)], axis=0).T
        r0 = jnp.broadcast_to(gt[0:1, :], (CHUNK, LANES))
        r1 = jnp.broadcast_to(gt[GDN_DK:GDN_DK + 1, :], (CHUNK, LANES))
        return jnp.where(low, r0, pltpu.roll(r1, GDN_DK, axis=1))

    out_refs = ((uf_ref, wf_ref, af_ref, qdf_ref, kdf_ref, glf_ref),
                (ub_ref, wb_ref, abk_ref, qdb_ref, kdb_ref, glb_ref))
    incl = (ri >= cj, ri <= cj)
    strict = (ri > cj, ri < cj)
    diag8 = (ri // 8) == (cj // 8)
    merge_masks = [((ri // (2 * s)) == (cj // (2 * s))) & ((ri // s) != (cj // s)) for s in (8, 16, 32)]

    def group(it, carry):
        cs = [it * PREP_GROUP + cc for cc in range(PREP_GROUP)]
        rows = [pl.ds(pl.multiple_of(c * CHUNK, CHUNK), CHUNK) for c in cs]
        qc = [qn_s[r, :] for r in rows]
        kc = [kn_s[r, :] for r in rows]
        vc = [vn_s[r, :] for r in rows]
        kst = [_stack(k, low).astype(BF16) for k in kc]
        kk = [_dot_nt(k.astype(BF16), ks) for k, ks in zip(kc, kst)]
        qk = [_dot_nt(q.astype(BF16), ks) for q, ks in zip(qc, kst)]
        chains = [(ci, d) for ci in range(PREP_GROUP) for d in range(2)]
        gc = [gf_s[rows[ci], d * LANES:(d + 1) * LANES] for ci, d in chains]
        beta = [gf_s[rows[ci], (2 + d) * LANES:(3 + d) * LANES] for ci, d in chains]
        gr = [row_form(g) for g in gc]
        dm = [jnp.exp(jnp.where(incl[d], g - r, -jnp.inf)) for (ci, d), g, r in zip(chains, gc, gr)]
        lm = [jnp.where(strict[d], b * kk[ci] * m, 0.0) for (ci, d), b, m in zip(chains, beta, dm)]
        aint = [(qk[ci] * m).astype(BF16) for (ci, d), m in zip(chains, dm)]
        x = [-jnp.where(diag8, l, 0.0) for l in lm]
        t = [eye + xx for xx in x]
        for _ in range(2):
            x = [pmm(xx, xx) for xx in x]
            t = [tt + pmm(tt, xx) for tt, xx in zip(t, x)]
        for off in merge_masks:
            tc = [pmm(tt, jnp.where(off, l, 0.0)) for tt, l in zip(t, lm)]
            t = [tt - pmm(a, tt) for tt, a in zip(t, tc)]
        egc = [jnp.exp(g) for g in gc]
        u = [pmm(tt, vc[ci] * b) for (ci, d), tt, b in zip(chains, t, beta)]
        w = [pmm(tt, kc[ci] * b * e).astype(BF16) for (ci, d), tt, b, e in zip(chains, t, beta, egc)]
        qd = [(qc[ci] * e).astype(BF16) for (ci, d), e in zip(chains, egc)]
        gtot = [g[CHUNK - 1:CHUNK, :] if d == 0 else g[0:1, :] for (ci, d), g in zip(chains, gc)]
        kd = [(kc[ci] * jnp.exp(gt - g)).astype(BF16) for (ci, d), gt, g in zip(chains, gtot, gc)]
        for n, (ci, d) in enumerate(chains):
            u_ref, w_ref, a_ref, qd_ref, kd_ref, gl_ref = out_refs[d]
            u_ref[0, rows[ci], :] = u[n]
            w_ref[0, rows[ci], :] = w[n]
            a_ref[0, rows[ci], :] = aint[n]
            qd_ref[0, rows[ci], :] = qd[n]
            kd_ref[0, rows[ci], :] = kd[n]
            gl_ref[0, pl.ds(cs[ci], 1), :, :] = jnp.broadcast_to(jnp.exp(gtot[n]), (1, 8, LANES))
        return carry

    lax.fori_loop(0, n_chunks // PREP_GROUP, group, 0)


def _gdn_prep(qkv, ab, conv_w, alog128, dtb128, expand, n_seq, seq_len):
    rb = max(seq_len, PREP_ROWS)
    nb = n_seq * seq_len // rb
    n_chunks = rb // CHUNK
    col = lambda off: pl.BlockSpec((rb, LANES), lambda s, p: (s, off + p))
    cw = lambda off: pl.BlockSpec((CONV_K, LANES), lambda s, p: (0, off + p))
    vec = pl.BlockSpec((1, LANES), lambda s, p: (0, 0))
    big = lambda: pl.BlockSpec((1, rb, LANES), lambda s, p: (s, 0, p))
    glspec = lambda: pl.BlockSpec((1, n_chunks, 8, LANES), lambda s, p: (s, 0, 0, p))
    shp = lambda dt: jax.ShapeDtypeStruct((nb, rb, GDN_W), dt)
    glshp = jax.ShapeDtypeStruct((nb, n_chunks, 8, GDN_W), F32)
    outs = pl.pallas_call(
        functools.partial(_gdn_prep_kernel, seq_len),
        out_shape=[shp(F32), shp(F32)] + [shp(BF16)] * 8 + [glshp, glshp],
        grid=(nb, N_PAIRS),
        in_specs=[col(0), col(N_PAIRS), col(2 * N_PAIRS), cw(0), cw(N_PAIRS), cw(2 * N_PAIRS),
                  pl.BlockSpec((rb, LANES), lambda s, p: (s, 0)), vec, vec,
                  pl.BlockSpec((1, LANES, 4 * LANES), lambda s, p: (p, 0, 0))],
        out_specs=[big() for _ in range(10)] + [glspec(), glspec()],
        scratch_shapes=[pltpu.VMEM((rb, LANES), F32)] * 4 + [pltpu.VMEM((rb, 4 * LANES), F32)],
        compiler_params=_cparams(("parallel", "parallel")),
        name="gdn_prep_%d" % seq_len,
    )(qkv, qkv, qkv, conv_w, conv_w, conv_w, ab, alog128, dtb128, expand)
    per_seq = [o.reshape(n_seq, seq_len, GDN_W) for o in outs[:10]]
    return per_seq + [o.reshape(n_seq, seq_len // CHUNK, 8, GDN_W) for o in outs[10:]]


def _gdn_scan_kernel(uf_ref, ub_ref, wf_ref, wb_ref, af_ref, abk_ref, qdf_ref, qdb_ref, kdf_ref, kdb_ref,
                     glf_ref, glb_ref, s0_ref, of_ref, ob_ref, sfin_ref, state):
    step = pl.program_id(1)
    n_steps = pl.num_programs(1)
    low = _pair_masks()
    chains = [(d, j, p) for d in range(2) for j in range(SCAN_SEQS) for p in range(N_PAIRS)]

    first = step == 0
    per_dir = ((uf_ref, wf_ref, af_ref, qdf_ref, kdf_ref, glf_ref, of_ref),
               (ub_ref, wb_ref, abk_ref, qdb_ref, kdb_ref, glb_ref, ob_ref))

    s = [jnp.where(first, _stack(s0_ref[j, d, p], low), state[idx]) for idx, (d, j, p) in enumerate(chains)]
    for sub in range(SCAN_CHUNKS):
        at = (sub, SCAN_CHUNKS - 1 - sub)

        def rd(k, d, j, p):
            return per_dir[d][k][j, at[d] * CHUNK:(at[d] + 1) * CHUNK, p * LANES:(p + 1) * LANES]

        sb = [x.astype(BF16) for x in s]
        ws = [_dot(rd(1, *c), b) for c, b in zip(chains, sb)]
        qs = [_dot(rd(3, *c), b) for c, b in zip(chains, sb)]
        vst = [_stack(rd(0, *c) - w, low).astype(BF16) for c, w in zip(chains, ws)]
        upd = [_dot_tn(_stack(rd(4, *c), low), v) for c, v in zip(chains, vst)]
        intra = [_dot(rd(2, *c), v) for c, v in zip(chains, vst)]
        nxt = []
        for idx, (d, j, p) in enumerate(chains):
            sl = slice(p * LANES, (p + 1) * LANES)
            nxt.append(s[idx] * per_dir[d][5][j, at[d], 0:1, sl] + upd[idx])
            per_dir[d][6][j, at[d] * CHUNK:(at[d] + 1) * CHUNK, sl] = qs[idx] + intra[idx]
        s = nxt
    for idx in range(len(chains)):
        state[idx] = s[idx]

    @pl.when(step == n_steps - 1)
    def _():
        for idx, (d, j, p) in enumerate(chains):
            fin = state[idx]
            sfin_ref[j, d, 2 * p] = fin[:GDN_DK, :GDN_DV]
            sfin_ref[j, d, 2 * p + 1] = pltpu.roll(fin[GDN_DK:], GDN_DV, axis=1)[:, :GDN_DV]


def _gdn_scan(prep, s0, n_seq, seq_len):
    n_steps = seq_len // (CHUNK * SCAN_CHUNKS)
    ns = SCAN_SEQS
    rows = CHUNK * SCAN_CHUNKS
    fwd = lambda: pl.BlockSpec((ns, rows, GDN_W), lambda g, i: (g, i, 0))
    bwd = lambda: pl.BlockSpec((ns, rows, GDN_W), lambda g, i: (g, n_steps - 1 - i, 0))
    glf = pl.BlockSpec((ns, SCAN_CHUNKS, 8, GDN_W), lambda g, i: (g, i, 0, 0))
    glb = pl.BlockSpec((ns, SCAN_CHUNKS, 8, GDN_W), lambda g, i: (g, n_steps - 1 - i, 0, 0))
    st = pl.BlockSpec((ns, 2, N_PAIRS, GDN_DK, LANES), lambda g, i: (g, 0, 0, 0, 0))
    st_out = pl.BlockSpec((ns, 2, GDN_HEADS, GDN_DK, GDN_DV), lambda g, i: (g, 0, 0, 0, 0))
    oshape = jax.ShapeDtypeStruct((n_seq, seq_len, GDN_W), F32)
    return pl.pallas_call(
        _gdn_scan_kernel,
        out_shape=[oshape, oshape, jax.ShapeDtypeStruct((n_seq, 2, GDN_HEADS, GDN_DK, GDN_DV), F32)],
        grid=(n_seq // ns, n_steps),
        in_specs=[fwd(), bwd()] * 5 + [glf, glb, st],
        out_specs=[fwd(), bwd(), st_out],
        scratch_shapes=[pltpu.VMEM((2 * ns * N_PAIRS, LANES, LANES), F32)],
        compiler_params=_cparams(("parallel", "arbitrary")),
        name="gdn_scan_%d" % seq_len,
    )(*prep, s0)


def _route(sel, s):
    per_group = N_EXPERTS // N_GROUPS
    ninf = -jnp.inf
    sub = lax.broadcasted_iota(jnp.int32, sel.shape, 1).astype(F32)
    gid = lax.broadcasted_iota(jnp.int32, (N_GROUPS, 1, sel.shape[2]), 0).astype(F32)
    m1 = jnp.max(sel, axis=1, keepdims=True)
    i1 = jnp.min(jnp.where(sel == m1, sub, float(per_group)), axis=1, keepdims=True)
    m2 = jnp.max(jnp.where(sub == i1, ninf, sel), axis=1, keepdims=True)
    work = m1 + m2
    gmask = jnp.zeros(work.shape, jnp.bool_)
    for _ in range(TOPK_GROUPS):
        m = jnp.max(work, axis=0, keepdims=True)
        idx = jnp.min(jnp.where(work == m, gid, float(N_GROUPS)), axis=0, keepdims=True)
        pick = gid == idx
        gmask = gmask | pick
        work = jnp.where(pick, ninf, work)
    work = jnp.where(gmask, sel, ninf)
    eid = gid * per_group + sub
    chosen = jnp.zeros(sel.shape, jnp.bool_)
    for _ in range(TOP_K):
        m = jnp.max(jnp.max(work, axis=1, keepdims=True), axis=0, keepdims=True)
        idx = jnp.min(jnp.min(jnp.where(work == m, eid, float(N_EXPERTS)), axis=1, keepdims=True),
                      axis=0, keepdims=True)
        pick = eid == idx
        chosen = chosen | pick
        work = jnp.where(pick, ninf, work)
    wk = jnp.where(chosen, s, 0.0)
    den = jnp.sum(jnp.sum(wk, axis=1, keepdims=True), axis=0, keepdims=True)
    return wk / den * ROUTED_SCALE


def _merge_kernel(x_ref, m_ref, omla_ref, of_ref, ob_ref, z_ref, ga_ref, gb_ref,
                  woa_ref, wob_ref, wo_ref, gpost_ref, gpre_ref, gdng_ref, wr_ref, eb_ref,
                  x1_ref, h2_ref, gates_ref, cnt_ref):
    m = m_ref[0]
    low = _pair_masks()
    o = of_ref[...] + ob_ref[...]
    z = z_ref[...]
    parts = []
    for p in range(N_PAIRS):
        sl = slice(p * LANES, (p + 1) * LANES)
        op = o[:, sl]
        sq = op * op
        s0 = jnp.sum(jnp.where(low, sq, 0.0), axis=-1, keepdims=True)
        s1 = jnp.sum(jnp.where(low, 0.0, sq), axis=-1, keepdims=True)
        ms = jnp.where(low, s0, s1) * (1.0 / GDN_DV)
        parts.append(op * lax.rsqrt(ms + EPS) * gdng_ref[...] * _silu(z[:, sl]))
    og = jnp.concatenate(parts, axis=1).astype(BF16)
    ya = _dot(omla_ref[...], woa_ref[...])
    yb = _dot(og, wob_ref[...])
    mix = (jax.nn.sigmoid(ga_ref[...]) * ya + jax.nn.sigmoid(gb_ref[...]) * yb).astype(BF16)
    y = _dot(mix, wo_ref[...])
    x1 = x_ref[...] + m[2:3] * _rms(y, gpost_ref[...])
    x1_ref[...] = x1
    h2 = _rms(x1, gpre_ref[...]) * (1.0 + m[4:5]) + m[3:4]
    h2_ref[...] = h2.astype(BF16)
    wh, wl, _ = _split3(wr_ref[...])
    hh, hl, _ = _split3(h2)
    logits = _dot_nt(wh, hh) + (_dot_nt(wh, hl) + _dot_nt(wl, hh))
    s = jax.nn.sigmoid(logits)
    sel = s + eb_ref[...]
    tm = s.shape[1]
    shape3 = (N_GROUPS, N_EXPERTS // N_GROUPS, tm)
    gates_t = _route(sel.reshape(shape3), s.reshape(shape3)).reshape(N_EXPERTS, tm)
    gates_ref[...] = gates_t
    member = (gates_t > 0.0).astype(F32)
    for blk in range(tm // MOE_BLOCK):
        cnt_ref[blk] = jnp.sum(member[:, blk * MOE_BLOCK:(blk + 1) * MOE_BLOCK], axis=1, keepdims=True)


def _merge_kernel_into(*refs):
    n_in = 16
    _merge_kernel(*refs[:n_in], *refs[n_in + 4:])


def _merge(x, mods, mod_index, omla, o_f, o_b, pr, wts, tile0, total, into=None):
    t = x.shape[0]
    tm = MERGE_TILE
    bpt = tm // MOE_BLOCK

    def full(a):
        return pl.BlockSpec(a.shape, lambda i: (0,) * a.ndim)

    def rows(w):
        return pl.BlockSpec((tm, w), lambda i: (i, 0))

    def out_rows(w):
        return pl.BlockSpec((tm, w), lambda i: (tile0 + i, 0))

    names = ("woa", "wob", "wo", "gpost", "gpre2", "gdng", "wr_t", "eb")
    args = [x, mods, omla, o_f, o_b, pr["z"], pr["ga"], pr["gb"]] + [wts[n] for n in names]
    in_specs = [rows(D_MODEL), pl.BlockSpec((1, 6, D_MODEL), lambda i: (mod_index(i), 0, 0)),
                rows(MLA_HEADS * V_HEAD), rows(GDN_W), rows(GDN_W), rows(GDN_W),
                rows(D_MODEL), rows(D_MODEL)] + [full(wts[n]) for n in names]
    aliases = {}
    if into is not None:
        aliases = {len(args) + k: k for k in range(4)}
        in_specs = in_specs + [pl.BlockSpec(memory_space=pl.ANY)] * 4
        args = args + list(into)
    return pl.pallas_call(
        _merge_kernel if into is None else _merge_kernel_into,
        out_shape=[jax.ShapeDtypeStruct((total, D_MODEL), F32), jax.ShapeDtypeStruct((total, D_MODEL), BF16),
                   jax.ShapeDtypeStruct((N_EXPERTS, total), F32),
                   jax.ShapeDtypeStruct((total // MOE_BLOCK, N_EXPERTS, 1), F32)],
        grid=(t // tm,),
        in_specs=in_specs,
        out_specs=[out_rows(D_MODEL), out_rows(D_MODEL),
                   pl.BlockSpec((N_EXPERTS, tm), lambda i: (0, tile0 + i)),
                   pl.BlockSpec((bpt, N_EXPERTS, 1), lambda i: (tile0 + i, 0, 0))],
        input_output_aliases=aliases,
        compiler_params=_cparams(("parallel",)),
        name="merge",
    )(*args)


TABLE_W = 256
TILE_TABLE_W = 512


def _ceil_div(x, d):
    return jnp.floor((x + (d - 1)) * (1.0 / d))


def _moe_tables_kernel(cnt_ref, cnt_t_ref, ce_ref, cb_ref, dst_ref, src_ref, ng_ref, tile_ref):
    nb = cnt_ref.shape[0]
    ppt = float(TILE_PIECES)
    ppg = float(LOOP_PIECES)
    ei = lax.broadcasted_iota(jnp.int32, (N_EXPERTS, N_EXPERTS), 0)
    ej = lax.broadcasted_iota(jnp.int32, (N_EXPERTS, N_EXPERTS), 1)
    tri = (ej <= ei).astype(BF16)

    def cumsum_experts(col):
        wide = jnp.broadcast_to(col, (N_EXPERTS, LANES))
        return functools.reduce(jnp.add, [_dot(tri, pc) for pc in _split3(wide)])[:, 0:1]

    eid = lax.broadcasted_iota(jnp.int32, (N_EXPERTS, 1), 0).astype(F32)
    pc_t = _ceil_div(cnt_t_ref[...], PIECE)
    tp = jnp.sum(pc_t, axis=1, keepdims=True)
    rp = _ceil_div(tp, TILE_PIECES) * ppt
    gs_end = cumsum_experts(rp)
    gs = gs_end - rp
    blk = lax.broadcasted_iota(jnp.int32, (1, nb), 1)
    c = lax.broadcasted_iota(jnp.int32, (1, TABLE_W), 1).astype(F32)
    for b in range(nb):
        pc = _ceil_div(cnt_ref[b], PIECE)
        seg_end = cumsum_experts(pc)
        seg = seg_end - pc
        blk_off = jnp.sum(jnp.where(blk < b, pc_t, 0.0), axis=1, keepdims=True)
        nvalid = seg_end[N_EXPERTS - 1:N_EXPERTS, :]
        ce = jnp.minimum(jnp.sum((seg_end <= c).astype(F32), axis=0, keepdims=True), N_EXPERTS - 1.0)
        onehot = eid == ce
        seg_sel = jnp.sum(jnp.where(onehot, seg, 0.0), axis=0, keepdims=True)
        base_sel = jnp.sum(jnp.where(onehot, gs + blk_off - seg, 0.0), axis=0, keepdims=True)
        valid = c < nvalid
        dump = DUMP_PIECE0 + b * LOOP_PIECES + (c - ppg * jnp.floor(c * (1.0 / ppg)))
        dst = jnp.where(valid, base_sel + c, dump)
        row = slice(b, b + 1)
        ce_ref[row, :] = ce.astype(jnp.int32)
        cb_ref[row, :] = jnp.where(valid, (c - seg_sel) * PIECE, -float(1 << 20)).astype(jnp.int32)
        dst_ref[row, :] = dst.astype(jnp.int32)
        src_ref[row, :] = jnp.where(valid, dst, dst[:, 0:1]).astype(jnp.int32)
        ng_ref[row, :] = jnp.broadcast_to(_ceil_div(nvalid, LOOP_PIECES), (1, LANES)).astype(jnp.int32)
    j = lax.broadcasted_iota(jnp.int32, (1, TILE_TABLE_W), 1).astype(F32)
    start = j * ppt
    te = jnp.minimum(jnp.sum((gs_end <= start).astype(F32), axis=0, keepdims=True), N_EXPERTS - 1.0)
    onehot = eid == te
    tp_sel = jnp.sum(jnp.where(onehot, tp, 0.0), axis=0, keepdims=True)
    gs_sel = jnp.sum(jnp.where(onehot, gs, 0.0), axis=0, keepdims=True)
    n_used = gs_end[N_EXPERTS - 1:N_EXPERTS, :] * (1.0 / ppt)
    used = j < n_used
    tv = jnp.where(used, jnp.clip((tp_sel - (start - gs_sel)) * PIECE, 0.0, float(EXP_TILE)), 0.0)
    tin = jnp.where(used, j, n_used - 1.0)
    tout = jnp.where(used, j, float(N_EXP_TILES))
    tile_ref[...] = jnp.zeros(tile_ref.shape, jnp.int32)
    for r, v in enumerate((te, tv, tin, tout)):
        tile_ref[r:r + 1, :] = v.astype(jnp.int32)


def _dispatch_tables(cnt):
    nb = cnt.shape[0]
    tab = jax.ShapeDtypeStruct((nb, TABLE_W), jnp.int32)
    ce, cb, dst, src, ng, tile = pl.pallas_call(
        _moe_tables_kernel,
        out_shape=[tab, tab, tab, tab, jax.ShapeDtypeStruct((nb, LANES), jnp.int32),
                   jax.ShapeDtypeStruct((8, TILE_TABLE_W), jnp.int32)],
        name="moe_tables",
    )(cnt, cnt[:, :, 0].T)
    return {"ce": ce, "cb": cb, "dst": dst, "src": src, "ngroups": ng, "tile": tile}


def _piece_onehot(ce_ref, cb_ref, rank_s, b, g, extra=None):
    sub = lax.broadcasted_iota(jnp.int32, (PIECE, 1), 0).astype(F32)
    ps, ex = [], []
    for cc in range(GROUP_PIECES):
        c = g * GROUP_PIECES + cc
        e = ce_ref[b, c]
        base = cb_ref[b, c].astype(F32)
        hit = rank_s[pl.ds(e, 1), :] == base + sub
        ps.append(jnp.where(hit, 1.0, 0.0).astype(BF16))
        if extra is not None:
            ex.append(jnp.sum(jnp.where(hit, extra[pl.ds(e, 1), :], 0.0), axis=-1, keepdims=True))
    p = jnp.concatenate(ps, axis=0)
    return (p, jnp.concatenate(ex, axis=0)) if extra is not None else p


def _block_ranks(gt):
    n = gt.shape[1]
    ti = lax.broadcasted_iota(jnp.int32, (n, n), 0)
    tj = lax.broadcasted_iota(jnp.int32, (n, n), 1)
    before = (ti < tj).astype(BF16)
    member = gt > 0.0
    rank = _dot(member.astype(BF16), before)
    return jnp.where(member, rank, -1.0)


def _moe_sort_kernel(ce_ref, cb_ref, dst_ref, ng_ref, h_ref, gt_ref, xg_ref, stage, rank_s, gate_s, sem):
    b = pl.program_id(0)
    slot = b % 2
    gt = gt_ref[...]
    rank_s[...] = _block_ranks(gt)
    gate_s[...] = gt
    lane = lax.broadcasted_iota(jnp.int32, (1, LANES), 1)

    def piece_copy(blk, sl, c):
        r0 = pl.multiple_of(c * PIECE, PIECE)
        d0 = pl.multiple_of(dst_ref[blk, c] * PIECE, PIECE)
        return pltpu.make_async_copy(stage.at[sl, pl.ds(r0, PIECE)], xg_ref.at[pl.ds(d0, PIECE)], sem.at[sl])

    def groups(it, carry):
        gs = [it * LOOP_GROUPS + k for k in range(LOOP_GROUPS)]
        sel = [_piece_onehot(ce_ref, cb_ref, rank_s, b, g, gate_s) for g in gs]
        xs = [_dot(p, h_ref[...]).astype(BF16) for p, _ in sel]
        for g, x, (_, gcol) in zip(gs, xs, sel):
            hi, mid, lo = (t.astype(F32) for t in _split3(gcol))
            gblk = jnp.where(lane == 0, hi, jnp.where(lane == 1, mid, jnp.where(lane == 2, lo, 0.0)))
            r0 = pl.multiple_of(g * GROUP_ROWS, GROUP_ROWS)
            stage[slot, pl.ds(r0, GROUP_ROWS), :] = jnp.concatenate([x, gblk.astype(BF16)], axis=1)
        for cc in range(LOOP_PIECES):
            piece_copy(b, slot, it * LOOP_PIECES + cc).start()
        return carry

    lax.fori_loop(0, ng_ref[b, 0], groups, 0)

    def drain(blk, sl):
        def wait_some(it, carry):
            for cc in range(LOOP_PIECES):
                piece_copy(blk, sl, it * LOOP_PIECES + cc).wait()
            return carry
        lax.fori_loop(0, ng_ref[blk, 0], wait_some, 0)

    @pl.when(b > 0)
    def _():
        drain(b - 1, 1 - slot)

    @pl.when(b == pl.num_programs(0) - 1)
    def _():
        drain(b, slot)


def _moe_sort(h2, gates_t, tabs):
    nb = h2.shape[0] // MOE_BLOCK
    grid_spec = pltpu.PrefetchScalarGridSpec(
        num_scalar_prefetch=4, grid=(nb,),
        in_specs=[pl.BlockSpec((MOE_BLOCK, D_MODEL), lambda b, *_: (b, 0)),
                  pl.BlockSpec((N_EXPERTS, MOE_BLOCK), lambda b, *_: (0, b))],
        out_specs=pl.BlockSpec(memory_space=pl.ANY),
        scratch_shapes=[pltpu.VMEM((2, STAGE_PIECES * PIECE, DISPATCH_W), BF16),
                        pltpu.VMEM((N_EXPERTS, MOE_BLOCK), F32), pltpu.VMEM((N_EXPERTS, MOE_BLOCK), F32),
                        pltpu.SemaphoreType.DMA((2,))])
    return pl.pallas_call(
        _moe_sort_kernel,
        out_shape=jax.ShapeDtypeStruct((DISPATCH_ROWS, DISPATCH_W), BF16),
        grid_spec=grid_spec,
        compiler_params=_cparams(("arbitrary",)),
        name="moe_sort",
    )(tabs["ce"], tabs["cb"], tabs["dst"], tabs["ngroups"], h2, gates_t)


def _moe_expert_kernel(tile_ref, x_ref, wg_ref, wu_ref, wd_ref, y_ref):
    valid = tile_ref[1, pl.program_id(0)]

    @pl.when(valid > 0)
    def _():
        keep = lax.broadcasted_iota(jnp.int32, (EXP_TILE, 1), 0) < valid
        xrow = x_ref[...]
        x = jnp.where(keep, xrow[:, :D_MODEL], jnp.zeros((), BF16))
        g = jnp.sum(jnp.where(keep, xrow[:, D_MODEL:].astype(F32), 0.0), axis=-1, keepdims=True)
        hg = _dot(x, wg_ref[0].astype(BF16))
        hu = _dot(x, wu_ref[0].astype(BF16))
        act = (_silu(hg) * hu * g).astype(BF16)
        y_ref[...] = _dot(act, wd_ref[0].astype(BF16)).astype(BF16)


def _moe_expert(xg, tabs, wts):
    grid_spec = pltpu.PrefetchScalarGridSpec(
        num_scalar_prefetch=1, grid=(N_EXP_TILES,),
        in_specs=[pl.BlockSpec((EXP_TILE, DISPATCH_W), lambda j, tt: (tt[2, j], 0)),
                  pl.BlockSpec((1, D_MODEL, D_EXPERT), lambda j, tt: (tt[0, j], 0, 0)),
                  pl.BlockSpec((1, D_MODEL, D_EXPERT), lambda j, tt: (tt[0, j], 0, 0)),
                  pl.BlockSpec((1, D_EXPERT, D_MODEL), lambda j, tt: (tt[0, j], 0, 0))],
        out_specs=pl.BlockSpec((EXP_TILE, D_MODEL), lambda j, tt: (tt[3, j], 0)))
    return pl.pallas_call(
        _moe_expert_kernel,
        out_shape=jax.ShapeDtypeStruct((DISPATCH_ROWS, D_MODEL), BF16),
        grid_spec=grid_spec,
        compiler_params=_cparams(("arbitrary",)),
        name="moe_expert",
    )(tabs["tile"], xg, wts["w_gate"], wts["w_up"], wts["w_down"])


def _moe_combine_kernel(ctx_blocks, ce_ref, cb_ref, src_ref, ng_ref, yg_ref, gt_ref, h_ref, x1_ref, m_ref,
                        gpost_ref, wsg_ref, wsu_ref, wsd_ref, outp_ref, outs_ref, stage, rank_s, acc_s, sem):
    b = pl.program_id(0)
    slot = b % 2

    def piece_copy(blk, sl, c):
        r0 = pl.multiple_of(c * PIECE, PIECE)
        s0 = pl.multiple_of(src_ref[blk, c] * PIECE, PIECE)
        return pltpu.make_async_copy(yg_ref.at[pl.ds(s0, PIECE)], stage.at[sl, pl.ds(r0, PIECE)], sem.at[sl])

    def fetch(blk, sl):
        def start_some(it, carry):
            for cc in range(LOOP_PIECES):
                piece_copy(blk, sl, it * LOOP_PIECES + cc).start()
            return carry
        lax.fori_loop(0, ng_ref[blk, 0], start_some, 0)

    @pl.when(b == 0)
    def _():
        fetch(0, 0)

    @pl.when(b + 1 < pl.num_programs(0))
    def _():
        fetch(b + 1, 1 - slot)

    rank_s[...] = _block_ranks(gt_ref[...])
    h = h_ref[...]
    sh = (_silu(_dot(h, wsg_ref[...])) * _dot(h, wsu_ref[...])).astype(BF16)
    acc_s[...] = _dot(sh, wsd_ref[...])

    def wait_some(it, carry):
        for cc in range(LOOP_PIECES):
            piece_copy(b, slot, it * LOOP_PIECES + cc).wait()
        return carry

    lax.fori_loop(0, ng_ref[b, 0], wait_some, 0)

    def groups(it, carry):
        gs = [it * LOOP_GROUPS + k for k in range(LOOP_GROUPS)]
        ps = [_piece_onehot(ce_ref, cb_ref, rank_s, b, g) for g in gs]
        ys = [stage[slot, pl.ds(pl.multiple_of(g * GROUP_ROWS, GROUP_ROWS), GROUP_ROWS), :] for g in gs]
        acc_s[...] += functools.reduce(jnp.add, [_dot_tn(p, y) for p, y in zip(ps, ys)])
        return carry

    lax.fori_loop(0, ng_ref[b, 0], groups, 0)
    m = m_ref[0]
    y = x1_ref[...] + m[5:6] * _rms(acc_s[...], gpost_ref[...])

    @pl.when(b < ctx_blocks)
    def _():
        outp_ref[...] = y

    @pl.when(b >= ctx_blocks)
    def _():
        outs_ref[...] = y


def _moe_combine(yg, gates_t, h2, x1, mods, mod_index, tabs, wts, ctx_tokens):
    t = h2.shape[0]
    nb = t // MOE_BLOCK
    ctx_blocks = ctx_tokens // MOE_BLOCK

    def full(a):
        return pl.BlockSpec(a.shape, lambda b, *_: (0,) * a.ndim)

    grid_spec = pltpu.PrefetchScalarGridSpec(
        num_scalar_prefetch=4, grid=(nb,),
        in_specs=[pl.BlockSpec(memory_space=pl.ANY),
                  pl.BlockSpec((N_EXPERTS, MOE_BLOCK), lambda b, *_: (0, b)),
                  pl.BlockSpec((MOE_BLOCK, D_MODEL), lambda b, *_: (b, 0)),
                  pl.BlockSpec((MOE_BLOCK, D_MODEL), lambda b, *_: (b, 0)),
                  pl.BlockSpec((1, 6, D_MODEL), lambda b, *_: (mod_index(b), 0, 0)),
                  full(wts["gpost2"]), full(wts["wsg"]), full(wts["wsu"]), full(wts["wsd"])],
        out_specs=[pl.BlockSpec((MOE_BLOCK, D_MODEL), lambda b, *_: (jnp.minimum(b, ctx_blocks - 1), 0)),
                   pl.BlockSpec((MOE_BLOCK, D_MODEL), lambda b, *_: (jnp.maximum(b - ctx_blocks, 0), 0))],
        scratch_shapes=[pltpu.VMEM((2, STAGE_PIECES * PIECE, D_MODEL), BF16),
                        pltpu.VMEM((N_EXPERTS, MOE_BLOCK), F32), pltpu.VMEM((MOE_BLOCK, D_MODEL), F32),
                        pltpu.SemaphoreType.DMA((2,))])
    return pl.pallas_call(
        functools.partial(_moe_combine_kernel, ctx_blocks),
        out_shape=[jax.ShapeDtypeStruct((ctx_tokens, D_MODEL), F32),
                   jax.ShapeDtypeStruct((t - ctx_tokens, D_MODEL), F32)],
        grid_spec=grid_spec,
        compiler_params=_cparams(("arbitrary",)),
        name="moe_combine",
    )(tabs["ce"], tabs["cb"], tabs["src"], tabs["ngroups"], yg, gates_t, h2, x1, mods,
      wts["gpost2"], wts["wsg"], wts["wsu"], wts["wsd"])


def _moe(h2, gates_t, cnt, x1, mods, mod_index, wts, ctx_tokens):
    tabs = _dispatch_tables(cnt)
    xg = _moe_sort(h2, gates_t, tabs)
    yg = _moe_expert(xg, tabs, wts)
    return _moe_combine(yg, gates_t, h2, x1, mods, mod_index, tabs, wts, ctx_tokens)


def _rope_swap(w):
    nf = QK_ROPE // 4
    parts = [w[..., i * nf:(i + 1) * nf] for i in range(4)]
    return jnp.concatenate([parts[1], parts[0], parts[3], parts[2]], axis=-1)


def _head_block(nope, rope):
    lead = nope.shape[:-2] if nope is not None else rope.shape[:-2]
    nope = jnp.zeros(lead + (MLA_HEADS, QK_NOPE), F32) if nope is None else nope
    rope = jnp.zeros(lead + (MLA_HEADS, QK_ROPE), F32) if rope is None else rope
    pad = jnp.zeros(lead + (MLA_HEADS, HEAD_PAD - MLA_QK), F32)
    return jnp.concatenate([nope, rope, pad], axis=-1).reshape(lead + (MLA_HEADS * HEAD_PAD,))


def _rope_block(w):
    lead = w.shape[:-1]
    return jnp.concatenate([jnp.zeros(lead + (QK_NOPE,), F32), w,
                            jnp.zeros(lead + (HEAD_PAD - MLA_QK,), F32)], axis=-1)


def _rope_tables(n_tokens):
    rows = n_tokens // GRID_W
    row = np.repeat(np.arange(rows, dtype=np.float64), GRID_W)
    colv = np.tile(np.arange(GRID_W, dtype=np.float64), rows)
    nf = QK_ROPE // 4
    inv = ROPE_THETA ** (-np.arange(nf, dtype=np.float64) / nf)
    ang_r = row[:, None] * inv
    ang_c = colv[:, None] * inv
    cos32 = np.concatenate([np.cos(ang_r), np.cos(ang_r), np.cos(ang_c), np.cos(ang_c)], axis=-1)
    sin32 = np.concatenate([-np.sin(ang_r), np.sin(ang_r), -np.sin(ang_c), np.sin(ang_c)], axis=-1)
    ones = np.ones((n_tokens, QK_NOPE))
    tail = np.zeros((n_tokens, HEAD_PAD - MLA_QK))
    cos = np.concatenate([ones, cos32, tail], axis=-1)
    sin = np.concatenate([np.zeros((n_tokens, QK_NOPE)), sin32, tail], axis=-1)
    return jnp.asarray(cos, F32), jnp.asarray(sin, F32)


def _expand_matrix():
    e = np.zeros((N_PAIRS, LANES, 4 * LANES), np.float32)
    for p in range(N_PAIRS):
        for blk in range(4):
            for hh in range(2):
                src = blk * GDN_HEADS + 2 * p + hh
                e[p, src, blk * LANES + hh * GDN_DK: blk * LANES + (hh + 1) * GDN_DK] = 1.0
    return jnp.asarray(e)


def _prepare_weights(w_in, q_norm_g, kv_norm_g, w_uq, w_ukv, w_oa, w_ob, w_o, g_post_mix, g_pre_ffn,
                     g_post_ffn, gdn_norm_g, w_router, e_bias, w_gate, w_up, w_down, ws_gate, ws_up, ws_down):
    offs = np.cumsum((Q_LORA, KV_LORA, QK_ROPE, 3 * GDN_W, GDN_W, 2 * GDN_HEADS, 2 * GDN_HEADS,
                      D_MODEL, D_MODEL))[:-1].tolist()
    cq, ckv, kr, qkv, z, a, b, ga, gb = jnp.split(w_in, offs, axis=-1)
    ab = jnp.concatenate([a, b, jnp.zeros((D_MODEL, LANES - 4 * GDN_HEADS), F32)], axis=-1)
    wcat = jnp.concatenate([cq, ckv, _rope_block(kr), _rope_block(_rope_swap(kr)), qkv, z, ab, ga, gb],
                           axis=-1).astype(BF16)
    uq = w_uq.reshape(Q_LORA, MLA_HEADS, MLA_QK)
    ukv = w_ukv.reshape(KV_LORA, MLA_HEADS, QK_NOPE + V_HEAD)
    return {
        "wcat": wcat,
        "qg": q_norm_g.reshape(1, Q_LORA), "kvg": kv_norm_g.reshape(1, KV_LORA),
        "wuq": _head_block(uq[..., :QK_NOPE], uq[..., QK_NOPE:]).astype(BF16),
        "wuqs": _head_block(None, _rope_swap(uq[..., QK_NOPE:])).astype(BF16),
        "wuk": _head_block(ukv[..., :QK_NOPE], None).astype(BF16),
        "wuv": ukv[..., QK_NOPE:].reshape(KV_LORA, MLA_HEADS * V_HEAD).astype(BF16),
        "woa": w_oa.astype(BF16), "wob": w_ob.astype(BF16), "wo": w_o.astype(BF16),
        "gpost": g_post_mix.reshape(1, D_MODEL), "gpre2": g_pre_ffn.reshape(1, D_MODEL),
        "gpost2": g_post_ffn.reshape(1, D_MODEL),
        "gdng": jnp.tile(gdn_norm_g.reshape(1, GDN_DV), (1, 2)),
        "wr_t": w_router.T, "eb": e_bias.reshape(N_EXPERTS, 1),
        "w_gate": w_gate, "w_up": w_up, "w_down": w_down,
        "wsg": ws_gate.astype(BF16), "wsu": ws_up.astype(BF16), "wsd": ws_down.astype(BF16),
    }


def _pad_lanes(v):
    v = v.reshape(1, -1)
    return jnp.concatenate([v, jnp.zeros((1, LANES - v.shape[1]), F32)], axis=-1)


def _layer_group(x, n_seq, seq_len, mods, mod_index, wts, gpre, conv_w, alog128, dtb128,
                 expand, rope_tabs, extra_kv, s0, tile0, into):
    pr = _proj(x, mods, mod_index, gpre, wts, rope_tabs)
    kvs = list(extra_kv) + [(pr["k"], pr["v"], seq_len)]
    omla = _attention(pr["q"], kvs, n_seq, seq_len, "attn_%d" % seq_len)
    prep = _gdn_prep(pr["qkv"], pr["ab"], conv_w, alog128, dtb128, expand, n_seq, seq_len)
    o_f, o_b, s_fin = _gdn_scan(prep, s0, n_seq, seq_len)
    t = n_seq * seq_len
    merged = _merge(x, mods, lambda i: mod_index(i * (MERGE_TILE // ROW_TILE)), omla,
                    o_f.reshape(t, GDN_W), o_b.reshape(t, GDN_W), pr, wts, tile0, N_TOKENS, into)
    return merged, pr, s_fin


def _state_to_pairs(s):
    b = s.shape[0]
    s = s.reshape(b, 2, N_PAIRS, 2, GDN_DK, GDN_DV)
    return jnp.transpose(s, (0, 1, 2, 4, 3, 5)).reshape(b, 2, N_PAIRS, GDN_DK, 2 * GDN_DV)


def kernel(x_prompt, x_sample, cache_ckv, cache_krope, state_delta, c, c_ctx, w_ada, b_ada, g_pre_mix,
           g_post_mix, g_pre_ffn, g_post_ffn, w_in, q_norm_g, kv_norm_g, w_uq, w_ukv, conv_w, a_log,
           dt_bias, gdn_norm_g, w_oa, w_ob, w_o, w_router, e_bias, w_gate, w_up, w_down, ws_gate, ws_up,
           ws_down):
    batch, seq, _ = x_prompt.shape
    dec_batch, dec_seq, _ = x_sample.shape
    past = cache_ckv.shape[2]
    assert batch * seq + dec_batch * dec_seq == N_TOKENS, "dispatch buffers are sized for N_TOKENS"
    y_p = x_prompt.reshape(batch * seq, D_MODEL)
    y_s = x_sample.reshape(dec_batch * dec_seq, D_MODEL)
    expand = _expand_matrix()
    rope_tabs = _rope_tables(dec_seq)
    cond8 = jnp.concatenate([c_ctx[None], c, jnp.zeros((8 - 1 - dec_batch, D_MODEL), F32)], axis=0)
    ckv_out, krope_out, state_out = [], [], []
    for l in range(DEPTH):
        wts = _prepare_weights(w_in[l], q_norm_g[l], kv_norm_g[l], w_uq[l], w_ukv[l], w_oa[l], w_ob[l],
                               w_o[l], g_post_mix[l], g_pre_ffn[l], g_post_ffn[l], gdn_norm_g[l],
                               w_router[l], e_bias[l], w_gate[l], w_up[l], w_down[l], ws_gate[l],
                               ws_up[l], ws_down[l])
        gpre = g_pre_mix[l].reshape(1, D_MODEL)
        alog128 = _pad_lanes(a_log[l])
        dtb128 = _pad_lanes(dt_bias[l])
        mods = _mods(cond8, w_ada[l], b_ada[l].reshape(1, -1)).reshape(8, 6, D_MODEL)

        zero_state = jnp.zeros((batch, 2, N_PAIRS, GDN_DK, LANES), F32)
        merged_p, pr_p, s_fin = _layer_group(
            y_p, batch, seq, mods, lambda i: 0, wts, gpre, conv_w[l], alog128, dtb128,
            expand, None, [], zero_state, 0, None)
        ckv_out.append(pr_p["ckv"].reshape(batch, seq, KV_LORA))
        krope_out.append(pr_p["kr"][:, QK_NOPE:MLA_QK].reshape(batch, seq, QK_ROPE))
        state_out.append(s_fin)

        kr_ctx = _rope_block(cache_krope[:, l].reshape(dec_batch * past, QK_ROPE))
        k_ctx, v_ctx = _cache_kv(cache_ckv[:, l].reshape(dec_batch * past, KV_LORA), kr_ctx, wts)
        tiles_per_seq = dec_seq // ROW_TILE
        merged_s, _, _ = _layer_group(
            y_s, dec_batch, dec_seq, mods, lambda i: 1 + i // tiles_per_seq,
            wts, gpre, conv_w[l], alog128, dtb128, expand,
            rope_tabs, [(k_ctx, v_ctx, past)], _state_to_pairs(state_delta[:, l]),
            batch * seq // MERGE_TILE, merged_p)

        x1, h2, gates_t, cnt = merged_s
        ctx_blocks = batch * seq // MOE_BLOCK
        blocks_per_seq = dec_seq // MOE_BLOCK
        y_p, y_s = _moe(h2, gates_t, cnt, x1, mods,
                        lambda b: jnp.where(b < ctx_blocks, 0, 1 + (b - ctx_blocks) // blocks_per_seq),
                        wts, batch * seq)
    new_ckv = jnp.stack(ckv_out, axis=1)
    new_krope = jnp.stack(krope_out, axis=1)
    new_state = jnp.stack(state_out, axis=1)
    return (y_p.reshape(batch, seq, D_MODEL), y_s.reshape(dec_batch, dec_seq, D_MODEL),
            new_ckv, new_krope, new_state)
```

```python
import functools

import numpy as np
import jax
import jax.numpy as jnp
from jax import lax
from jax.experimental import pallas as pl
from jax.experimental.pallas import tpu as pltpu

F32 = jnp.float32
BF16 = jnp.bfloat16

D_MODEL = 1024
DEPTH = 1
GRID_W = 64
MLA_HEADS = 8
QK_NOPE = 64
QK_ROPE = 32
V_HEAD = 64
Q_LORA = 256
KV_LORA = 256
ROPE_THETA = 10000.0
GDN_HEADS = 8
GDN_DK = 64
GDN_DV = 64
CONV_K = 5
CHUNK = 64
N_EXPERTS = 64
TOP_K = 8
N_GROUPS = 8
TOPK_GROUPS = 4
D_EXPERT = 256
D_SHARED = 256
ROUTED_SCALE = 2.5
EPS = 1e-6

LANES = 128
LOG2_E = 1.4426950408889634
MLA_QK = QK_NOPE + QK_ROPE
GDN_W = GDN_HEADS * GDN_DK
N_PAIRS = GDN_HEADS // 2
HEAD_PAD = LANES

_SEG = {"cq": ("w_lo", 0, Q_LORA), "ckv": ("w_lo", Q_LORA, Q_LORA + KV_LORA),
        "kr": ("w_small", 0, LANES), "krs": ("w_small", LANES, 2 * LANES), "ab": ("w_small", 2 * LANES, 3 * LANES),
        "qkv": ("w_mid", 0, 3 * GDN_W), "z": ("w_mid", 3 * GDN_W, 4 * GDN_W),
        "ga": ("w_hi", 0, D_MODEL), "gb": ("w_hi", D_MODEL, 2 * D_MODEL)}
_PROJ_W = ("w_lo", "w_small", "w_mid", "w_hi")

ROW_TILE = 512
ATTN_TILE = 512
N_TOKENS = 8192
MOE_BLOCK = 256
MERGE_TILE = 512
PIECE = 16
GROUP_ROWS = 256
GROUP_PIECES = GROUP_ROWS // PIECE
LOOP_GROUPS = 2
LOOP_PIECES = LOOP_GROUPS * GROUP_PIECES
EXP_TILE = 1024
TILE_PIECES = EXP_TILE // PIECE
N_BLOCKS = N_TOKENS // MOE_BLOCK
STAGE_PIECES = -(-((MOE_BLOCK * TOP_K + N_EXPERTS * (PIECE - 1)) // PIECE) // LOOP_PIECES) * LOOP_PIECES
N_EXP_TILES = -(-((N_TOKENS * TOP_K + N_BLOCKS * N_EXPERTS * (PIECE - 1)) // PIECE
                  + N_EXPERTS * (TILE_PIECES - 1)) // TILE_PIECES)
DUMP_PIECE0 = N_EXP_TILES * TILE_PIECES
DISPATCH_ROWS = (DUMP_PIECE0 + max(N_BLOCKS * LOOP_PIECES, TILE_PIECES)) * PIECE
DISPATCH_W = D_MODEL + LANES
PREP_GROUP = 8
PREP_ROWS = 1024
SCAN_SEQS = 4
SCAN_CHUNKS = 4
VMEM_LIMIT = 56 * 1024 * 1024


def _dot(a, b):
    return jnp.dot(a, b, preferred_element_type=F32)


def _dot_nt(a, b):
    return lax.dot_general(a, b, (((1,), (1,)), ((), ())), preferred_element_type=F32)


def _dot_tn(a, b):
    return lax.dot_general(a, b, (((0,), (0,)), ((), ())), preferred_element_type=F32)


def _rms(x, g):
    return x * lax.rsqrt(jnp.mean(x * x, axis=-1, keepdims=True) + EPS) * g


def _silu(x):
    return x * jax.nn.sigmoid(x)


def _cparams(sem):
    return pltpu.CompilerParams(dimension_semantics=sem, vmem_limit_bytes=VMEM_LIMIT)


def _mods_kernel(c_ref, w_ref, b_ref, o_ref):
    s = _silu(c_ref[...]).astype(BF16)
    o_ref[...] = _dot(s, w_ref[...].astype(BF16)) + b_ref[...]


def _mods(cond8, w_ada, b_ada):
    n = w_ada.shape[1]
    bn = 1024
    return pl.pallas_call(
        _mods_kernel,
        out_shape=jax.ShapeDtypeStruct((8, n), F32),
        grid=(n // bn,),
        in_specs=[pl.BlockSpec((8, D_MODEL), lambda j: (0, 0)),
                  pl.BlockSpec((D_MODEL, bn), lambda j: (0, j)),
                  pl.BlockSpec((1, bn), lambda j: (0, j))],
        out_specs=pl.BlockSpec((8, bn), lambda j: (0, j)),
        compiler_params=_cparams(("parallel",)),
        name="mods",
    )(cond8, w_ada, b_ada)


def _proj_kernel(rope, x_ref, m_ref, gpre_ref, wlo_ref, wsmall_ref, wmid_ref, whi_ref, qg_ref, kvg_ref,
                 wuq_ref, wuqs_ref, wuk_ref, wuv_ref, cos_ref, sin_ref,
                 q_ref, k_ref, v_ref, ckv_ref, kr_ref, qkv_ref, z_ref, ab_ref, ga_ref, gb_ref):
    m = m_ref[0]
    h = (_rms(x_ref[...], gpre_ref[...]) * (1.0 + m[1:2]) + m[0:1]).astype(BF16)
    w_refs = dict(zip(_PROJ_W, (wlo_ref, wsmall_ref, wmid_ref, whi_ref)))

    def seg(name):
        op, a, b = _SEG[name]
        return _dot(h, w_refs[op][:, a:b])

    qkv_ref[...] = seg("qkv")
    z_ref[...] = seg("z")
    ab_ref[...] = seg("ab")
    ga_ref[...] = seg("ga")
    gb_ref[...] = seg("gb")

    qn = _rms(seg("cq"), qg_ref[...]).astype(BF16)
    ckv = _rms(seg("ckv"), kvg_ref[...])
    ckv_ref[...] = ckv
    ckv_b = ckv.astype(BF16)
    kr = seg("kr")
    kr_ref[...] = kr
    qm = _dot(qn, wuq_ref[...])
    kk = _dot(ckv_b, wuk_ref[...])
    v_ref[...] = _dot(ckv_b, wuv_ref[...]).astype(BF16)
    scale = MLA_QK ** -0.5 * LOG2_E
    if rope:
        cos = cos_ref[...]
        sin = sin_ref[...]
        qs = _dot(qn, wuqs_ref[...])
        kr = kr * cos + seg("krs") * sin
    for hd in range(MLA_HEADS):
        sl = slice(hd * HEAD_PAD, (hd + 1) * HEAD_PAD)
        qh = qm[:, sl]
        if rope:
            qh = qh * cos + qs[:, sl] * sin
        q_ref[:, sl] = (qh * scale).astype(BF16)
        k_ref[:, sl] = (kk[:, sl] + kr).astype(BF16)


def _proj(x, mods, mod_index, gpre, wts, rope_tabs):
    t = x.shape[0]
    tm = ROW_TILE
    rope = rope_tabs is not None
    if rope:
        cos, sin = rope_tabs
        n_rope_blocks = cos.shape[0] // tm
        rope_spec = pl.BlockSpec((tm, LANES), lambda i: (i % n_rope_blocks, 0))
    else:
        cos = sin = jnp.zeros((8, LANES), F32)
        rope_spec = pl.BlockSpec((8, LANES), lambda i: (0, 0))

    def full(a):
        return pl.BlockSpec(a.shape, lambda i: (0,) * a.ndim)

    def rows(w):
        return pl.BlockSpec((tm, w), lambda i: (i, 0))

    out_widths = (("q", MLA_HEADS * HEAD_PAD, BF16), ("k", MLA_HEADS * HEAD_PAD, BF16),
                  ("v", MLA_HEADS * V_HEAD, BF16), ("ckv", KV_LORA, F32), ("kr", LANES, F32),
                  ("qkv", 3 * GDN_W, F32), ("z", GDN_W, F32), ("ab", LANES, F32),
                  ("ga", D_MODEL, F32), ("gb", D_MODEL, F32))
    outs = pl.pallas_call(
        functools.partial(_proj_kernel, rope),
        out_shape=[jax.ShapeDtypeStruct((t, w), dt) for _, w, dt in out_widths],
        grid=(t // tm,),
        in_specs=[rows(D_MODEL),
                  pl.BlockSpec((1, 6, D_MODEL), lambda i: (mod_index(i), 0, 0)),
                  full(gpre)] + [full(wts[n]) for n in _PROJ_W] + [full(wts["qg"]), full(wts["kvg"]),
                  full(wts["wuq"]), full(wts["wuqs"]), full(wts["wuk"]), full(wts["wuv"]),
                  rope_spec, rope_spec],
        out_specs=[rows(w) for _, w, _ in out_widths],
        compiler_params=_cparams(("parallel",)),
        name="proj_rope" if rope else "proj",
    )(x, mods, gpre, *[wts[n] for n in _PROJ_W], wts["qg"], wts["kvg"], wts["wuq"], wts["wuqs"],
      wts["wuk"], wts["wuv"], cos, sin)
    return {name: o for (name, _, _), o in zip(out_widths, outs)}


def _cache_kv_kernel(ckv_ref, kr_ref, wuk_ref, wuv_ref, k_ref, v_ref):
    c = ckv_ref[...].astype(BF16)
    kk = _dot(c, wuk_ref[...])
    v_ref[...] = _dot(c, wuv_ref[...]).astype(BF16)
    kr = kr_ref[...]
    for hd in range(MLA_HEADS):
        sl = slice(hd * HEAD_PAD, (hd + 1) * HEAD_PAD)
        k_ref[:, sl] = (kk[:, sl] + kr).astype(BF16)


def _cache_kv(ckv, kr128, wts):
    t = ckv.shape[0]
    tm = 512
    return pl.pallas_call(
        _cache_kv_kernel,
        out_shape=[jax.ShapeDtypeStruct((t, MLA_HEADS * HEAD_PAD), BF16),
                   jax.ShapeDtypeStruct((t, MLA_HEADS * V_HEAD), BF16)],
        grid=(t // tm,),
        in_specs=[pl.BlockSpec((tm, KV_LORA), lambda i: (i, 0)),
                  pl.BlockSpec((tm, LANES), lambda i: (i, 0)),
                  pl.BlockSpec(wts["wuk"].shape, lambda i: (0, 0)),
                  pl.BlockSpec(wts["wuv"].shape, lambda i: (0, 0))],
        out_specs=[pl.BlockSpec((tm, MLA_HEADS * HEAD_PAD), lambda i: (i, 0)),
                   pl.BlockSpec((tm, MLA_HEADS * V_HEAD), lambda i: (i, 0))],
        compiler_params=_cparams(("parallel",)),
        name="cache_kv",
    )(ckv, kr128, wts["wuk"], wts["wuv"])


def _attn_kernel(n_kv, q_ref, *refs):
    k_refs = refs[:n_kv]
    v_refs = refs[n_kv:2 * n_kv]
    o_ref = refs[2 * n_kv]
    lane = lax.broadcasted_iota(jnp.int32, (1, LANES), 1)
    low = lane < V_HEAD
    for pr in range(MLA_HEADS // 2):
        halves = []
        for hd in (2 * pr, 2 * pr + 1):
            sl = slice(hd * HEAD_PAD, (hd + 1) * HEAD_PAD)
            qh = q_ref[:, sl]
            scores = [_dot_nt(qh, kr[:, sl]) for kr in k_refs]
            mx = functools.reduce(jnp.maximum, [jnp.max(s, axis=-1, keepdims=True) for s in scores])
            ps = [jnp.exp2(s - mx) for s in scores]
            den = functools.reduce(jnp.add, [jnp.sum(p, axis=-1, keepdims=True) for p in ps])
            vsl = slice(pr * LANES, (pr + 1) * LANES)
            acc = functools.reduce(jnp.add, [_dot(p.astype(BF16), vr[:, vsl]) for p, vr in zip(ps, v_refs)])
            halves.append(acc / den)
        o_ref[:, pr * LANES:(pr + 1) * LANES] = jnp.where(low, halves[0], halves[1]).astype(BF16)


def _attention(q, kvs, n_seq, seq_len, name):
    tq = min(seq_len, ATTN_TILE)
    nq = seq_len // tq
    n_kv = len(kvs)
    in_specs = [pl.BlockSpec((tq, MLA_HEADS * HEAD_PAD), lambda b, j: (b * nq + j, 0))]
    in_specs += [pl.BlockSpec((rows, MLA_HEADS * HEAD_PAD), lambda b, j: (b, 0)) for _, _, rows in kvs]
    in_specs += [pl.BlockSpec((rows, MLA_HEADS * V_HEAD), lambda b, j: (b, 0)) for _, _, rows in kvs]
    return pl.pallas_call(
        functools.partial(_attn_kernel, n_kv),
        out_shape=jax.ShapeDtypeStruct((n_seq * seq_len, MLA_HEADS * V_HEAD), BF16),
        grid=(n_seq, nq),
        in_specs=in_specs,
        out_specs=pl.BlockSpec((tq, MLA_HEADS * V_HEAD), lambda b, j: (b * nq + j, 0)),
        compiler_params=_cparams(("parallel", "parallel")),
        name=name,
    )(q, *[k for k, _, _ in kvs], *[v for _, v, _ in kvs])


def _pair_masks():
    lane = lax.broadcasted_iota(jnp.int32, (1, LANES), 1)
    return lane < GDN_DK


def _stack(x, low):
    zero = jnp.zeros_like(x)
    return jnp.concatenate([jnp.where(low, x, zero), jnp.where(low, zero, x)], axis=0)


def _split3(x):
    hi = x.astype(BF16)
    r = x - hi.astype(F32)
    mid = r.astype(BF16)
    lo = (r - mid.astype(F32)).astype(BF16)
    return hi, mid, lo


def _gdn_prep_kernel(seq_len, q_ref, k_ref, v_ref, cwq_ref, cwk_ref, cwv_ref, ab_ref, alog_ref, dtb_ref, e_ref,
                     uf_ref, ub_ref, wf_ref, wb_ref, af_ref, abk_ref, qdf_ref, qdb_ref, kdf_ref, kdb_ref,
                     glf_ref, glb_ref,
                     qn_s, kn_s, vn_s, gcb_s, gf_s):
    seq = q_ref.shape[0]
    n_chunks = seq // CHUNK
    low = _pair_masks()
    row = lax.broadcasted_iota(jnp.int32, (seq, 1), 0) % seq_len
    lane = lax.broadcasted_iota(jnp.int32, (1, LANES), 1)

    def conv(x_ref, cw_ref):
        x = x_ref[...]
        acc = jnp.zeros_like(x)
        for j in range(CONV_K):
            sh = CONV_K // 2 - j
            xs = x if sh == 0 else pltpu.roll(x, sh % seq, axis=0)
            src = row - sh
            valid = (src >= 0) & (src < seq_len)
            acc = acc + jnp.where(valid, xs, 0.0) * cw_ref[j:j + 1, :]
        return _silu(acc)

    def l2n(x):
        sq = x * x
        s0 = jnp.sum(jnp.where(low, sq, 0.0), axis=-1, keepdims=True)
        s1 = jnp.sum(jnp.where(low, 0.0, sq), axis=-1, keepdims=True)
        return x * lax.rsqrt(jnp.where(low, s0, s1) + EPS)

    qn_s[...] = l2n(conv(q_ref, cwq_ref)) * (GDN_DK ** -0.5)
    kn_s[...] = l2n(conv(k_ref, cwk_ref))
    vn_s[...] = conv(v_ref, cwv_ref)

    a = ab_ref[...]
    xg = a + dtb_ref[...]
    softplus = jnp.maximum(xg, 0.0) + jnp.log(1.0 + jnp.exp(-jnp.abs(xg)))
    act = jnp.where(lane < 2 * GDN_HEADS, -jnp.exp(alog_ref[...]) * softplus, jax.nn.sigmoid(a))

    ti = lax.broadcasted_iota(jnp.int32, (CHUNK, CHUNK), 0)
    tj = lax.broadcasted_iota(jnp.int32, (CHUNK, CHUNK), 1)
    tri_lo = (tj <= ti).astype(BF16)
    tri_up = (tj >= ti).astype(BF16)
    for c in range(n_chunks):
        ac = act[c * CHUNK:(c + 1) * CHUNK]
        pieces = _split3(ac)
        lo = functools.reduce(jnp.add, [_dot(tri_lo, pc) for pc in pieces])
        up = functools.reduce(jnp.add, [_dot(tri_up, pc) for pc in pieces])
        gcb_s[c * CHUNK:(c + 1) * CHUNK, :] = jnp.where(lane < GDN_HEADS, lo,
                                                        jnp.where(lane < 2 * GDN_HEADS, up, ac))
    expand = e_ref[0].astype(BF16)
    gf_s[...] = functools.reduce(jnp.add, [_dot(pc, expand) for pc in _split3(gcb_s[...])])

    ri = lax.broadcasted_iota(jnp.int32, (CHUNK, LANES), 0)
    cj = lax.broadcasted_iota(jnp.int32, (CHUNK, LANES), 1) % CHUNK
    eye = (ri == cj).astype(F32)

    def pmm(x, y):
        return _dot(x.astype(BF16), _stack(y, low).astype(BF16))

    def row_form(g):
        gt = jnp.concatenate([g, jnp.zeros_like(g)], axis=0).T
        r0 = jnp.broadcast_to(gt[0:1, :], (CHUNK, LANES))
        r1 = jnp.broadcast_to(gt[GDN_DK:GDN_DK + 1, :], (CHUNK, LANES))
        return jnp.where(low, r0, pltpu.roll(r1, GDN_DK, axis=1))

    out_refs = ((uf_ref, wf_ref, af_ref, qdf_ref, kdf_ref, glf_ref),
                (ub_ref, wb_ref, abk_ref, qdb_ref, kdb_ref, glb_ref))
    incl = (ri >= cj, ri <= cj)
    strict = (ri > cj, ri < cj)
    diag8 = (ri // 8) == (cj // 8)
    merge_masks = [((ri // (2 * s)) == (cj // (2 * s))) & ((ri // s) != (cj // s)) for s in (8, 16, 32)]

    def group(it, carry):
        cs = [it * PREP_GROUP + cc for cc in range(PREP_GROUP)]
        rows = [pl.ds(pl.multiple_of(c * CHUNK, CHUNK), CHUNK) for c in cs]
        qc = [qn_s[r, :] for r in rows]
        kc = [kn_s[r, :] for r in rows]
        vc = [vn_s[r, :] for r in rows]
        kst = [_stack(k, low).astype(BF16) for k in kc]
        kk = [_dot_nt(k.astype(BF16), ks) for k, ks in zip(kc, kst)]
        qk = [_dot_nt(q.astype(BF16), ks) for q, ks in zip(qc, kst)]
        chains = [(ci, d) for ci in range(PREP_GROUP) for d in range(2)]
        gc = [gf_s[rows[ci], d * LANES:(d + 1) * LANES] for ci, d in chains]
        beta = [gf_s[rows[ci], (2 + d) * LANES:(3 + d) * LANES] for ci, d in chains]
        gr = [row_form(g) for g in gc]
        dm = [jnp.exp(jnp.where(incl[d], g - r, -jnp.inf)) for (ci, d), g, r in zip(chains, gc, gr)]
        lm = [jnp.where(strict[d], b * kk[ci] * m, 0.0) for (ci, d), b, m in zip(chains, beta, dm)]
        aint = [(qk[ci] * m).astype(BF16) for (ci, d), m in zip(chains, dm)]
        x = [-jnp.where(diag8, l, 0.0) for l in lm]
        t = [eye + xx for xx in x]
        for _ in range(2):
            x = [pmm(xx, xx) for xx in x]
            t = [tt + pmm(tt, xx) for tt, xx in zip(t, x)]
        for off in merge_masks:
            tc = [pmm(tt, jnp.where(off, l, 0.0)) for tt, l in zip(t, lm)]
            t = [tt - pmm(a, tt) for tt, a in zip(t, tc)]
        egc = [jnp.exp(g) for g in gc]
        u = [pmm(tt, vc[ci] * b) for (ci, d), tt, b in zip(chains, t, beta)]
        w = [pmm(tt, kc[ci] * b * e).astype(BF16) for (ci, d), tt, b, e in zip(chains, t, beta, egc)]
        qd = [(qc[ci] * e).astype(BF16) for (ci, d), e in zip(chains, egc)]
        gtot = [g[CHUNK - 1:CHUNK, :] if d == 0 else g[0:1, :] for (ci, d), g in zip(chains, gc)]
        kd = [(kc[ci] * jnp.exp(gt - g)).astype(BF16) for (ci, d), gt, g in zip(chains, gtot, gc)]
        for n, (ci, d) in enumerate(chains):
            u_ref, w_ref, a_ref, qd_ref, kd_ref, gl_ref = out_refs[d]
            u_ref[0, rows[ci], :] = u[n]
            w_ref[0, rows[ci], :] = w[n]
            a_ref[0, rows[ci], :] = aint[n]
            qd_ref[0, rows[ci], :] = qd[n]
            kd_ref[0, rows[ci], :] = kd[n]
            gl_ref[0, pl.ds(cs[ci], 1), :, :] = jnp.broadcast_to(jnp.exp(gtot[n]), (1, 8, LANES))
        return carry

    lax.fori_loop(0, n_chunks // PREP_GROUP, group, 0)


def _gdn_prep(qkv, ab, conv_w, alog128, dtb128, expand, n_seq, seq_len):
    rb = max(seq_len, PREP_ROWS)
    nb = n_seq * seq_len // rb
    n_chunks = rb // CHUNK
    col = lambda off: pl.BlockSpec((rb, LANES), lambda s, p: (s, off + p))
    cw = lambda off: pl.BlockSpec((CONV_K, LANES), lambda s, p: (0, off + p))
    vec = pl.BlockSpec((1, LANES), lambda s, p: (0, 0))
    big = lambda: pl.BlockSpec((1, rb, LANES), lambda s, p: (s, 0, p))
    glspec = lambda: pl.BlockSpec((1, n_chunks, 8, LANES), lambda s, p: (s, 0, 0, p))
    shp = lambda dt: jax.ShapeDtypeStruct((nb, rb, GDN_W), dt)
    glshp = jax.ShapeDtypeStruct((nb, n_chunks, 8, GDN_W), F32)
    outs = pl.pallas_call(
        functools.partial(_gdn_prep_kernel, seq_len),
        out_shape=[shp(F32), shp(F32)] + [shp(BF16)] * 8 + [glshp, glshp],
        grid=(nb, N_PAIRS),
        in_specs=[col(0), col(N_PAIRS), col(2 * N_PAIRS), cw(0), cw(N_PAIRS), cw(2 * N_PAIRS),
                  pl.BlockSpec((rb, LANES), lambda s, p: (s, 0)), vec, vec,
                  pl.BlockSpec((1, LANES, 4 * LANES), lambda s, p: (p, 0, 0))],
        out_specs=[big() for _ in range(10)] + [glspec(), glspec()],
        scratch_shapes=[pltpu.VMEM((rb, LANES), F32)] * 4 + [pltpu.VMEM((rb, 4 * LANES), F32)],
        compiler_params=_cparams(("parallel", "parallel")),
        name="gdn_prep_%d" % seq_len,
    )(qkv, qkv, qkv, conv_w, conv_w, conv_w, ab, alog128, dtb128, expand)
    per_seq = [o.reshape(n_seq, seq_len, GDN_W) for o in outs[:10]]
    return per_seq + [o.reshape(n_seq, seq_len // CHUNK, 8, GDN_W) for o in outs[10:]]


def _gdn_scan_kernel(uf_ref, ub_ref, wf_ref, wb_ref, af_ref, abk_ref, qdf_ref, qdb_ref, kdf_ref, kdb_ref,
                     glf_ref, glb_ref, s0_ref, of_ref, ob_ref, sfin_ref, state):
    step = pl.program_id(1)
    n_steps = pl.num_programs(1)
    low = _pair_masks()
    chains = [(d, j, p) for d in range(2) for j in range(uf_ref.shape[0]) for p in range(N_PAIRS)]

    first = step == 0
    per_dir = ((uf_ref, wf_ref, af_ref, qdf_ref, kdf_ref, glf_ref, of_ref),
               (ub_ref, wb_ref, abk_ref, qdb_ref, kdb_ref, glb_ref, ob_ref))

    s = [jnp.where(first, _stack(s0_ref[j, d, p], low), state[idx]) for idx, (d, j, p) in enumerate(chains)]
    for sub in range(SCAN_CHUNKS):
        at = (sub, SCAN_CHUNKS - 1 - sub)

        def rd(k, d, j, p):
            return per_dir[d][k][j, at[d] * CHUNK:(at[d] + 1) * CHUNK, p * LANES:(p + 1) * LANES]

        sb = [x.astype(BF16) for x in s]
        ws = [_dot(rd(1, *c), b) for c, b in zip(chains, sb)]
        qs = [_dot(rd(3, *c), b) for c, b in zip(chains, sb)]
        vst = [_stack(rd(0, *c) - w, low).astype(BF16) for c, w in zip(chains, ws)]
        upd = [_dot_tn(_stack(rd(4, *c), low), v) for c, v in zip(chains, vst)]
        intra = [_dot(rd(2, *c), v) for c, v in zip(chains, vst)]
        nxt = []
        for idx, (d, j, p) in enumerate(chains):
            sl = slice(p * LANES, (p + 1) * LANES)
            nxt.append(s[idx] * per_dir[d][5][j, at[d], 0:1, sl] + upd[idx])
            per_dir[d][6][j, at[d] * CHUNK:(at[d] + 1) * CHUNK, sl] = qs[idx] + intra[idx]
        s = nxt
    for idx in range(len(chains)):
        state[idx] = s[idx]

    @pl.when(step == n_steps - 1)
    def _():
        for idx, (d, j, p) in enumerate(chains):
            fin = state[idx]
            sfin_ref[j, d, 2 * p] = fin[:GDN_DK, :GDN_DV]
            sfin_ref[j, d, 2 * p + 1] = pltpu.roll(fin[GDN_DK:], GDN_DV, axis=1)[:, :GDN_DV]


def _gdn_scan(prep, s0, n_seq, seq_len):
    n_steps = seq_len // (CHUNK * SCAN_CHUNKS)
    ns = min(SCAN_SEQS, n_seq)
    rows = CHUNK * SCAN_CHUNKS
    fwd = lambda: pl.BlockSpec((ns, rows, GDN_W), lambda g, i: (g, i, 0))
    bwd = lambda: pl.BlockSpec((ns, rows, GDN_W), lambda g, i: (g, n_steps - 1 - i, 0))
    glf = pl.BlockSpec((ns, SCAN_CHUNKS, 8, GDN_W), lambda g, i: (g, i, 0, 0))
    glb = pl.BlockSpec((ns, SCAN_CHUNKS, 8, GDN_W), lambda g, i: (g, n_steps - 1 - i, 0, 0))
    st = pl.BlockSpec((ns, 2, N_PAIRS, GDN_DK, LANES), lambda g, i: (g, 0, 0, 0, 0))
    st_out = pl.BlockSpec((ns, 2, GDN_HEADS, GDN_DK, GDN_DV), lambda g, i: (g, 0, 0, 0, 0))
    oshape = jax.ShapeDtypeStruct((n_seq, seq_len, GDN_W), F32)
    return pl.pallas_call(
        _gdn_scan_kernel,
        out_shape=[oshape, oshape, jax.ShapeDtypeStruct((n_seq, 2, GDN_HEADS, GDN_DK, GDN_DV), F32)],
        grid=(n_seq // ns, n_steps),
        in_specs=[fwd(), bwd()] * 5 + [glf, glb, st],
        out_specs=[fwd(), bwd(), st_out],
        scratch_shapes=[pltpu.VMEM((2 * ns * N_PAIRS, LANES, LANES), F32)],
        compiler_params=_cparams(("parallel", "arbitrary")),
        name="gdn_scan_%d" % seq_len,
    )(*prep, s0)


def _route(sel, s):
    per_group = N_EXPERTS // N_GROUPS
    ninf = -jnp.inf
    sub = lax.broadcasted_iota(jnp.int32, sel.shape, 1).astype(F32)
    gid = lax.broadcasted_iota(jnp.int32, (N_GROUPS, 1, sel.shape[2]), 0).astype(F32)
    m1 = jnp.max(sel, axis=1, keepdims=True)
    i1 = jnp.min(jnp.where(sel == m1, sub, float(per_group)), axis=1, keepdims=True)
    m2 = jnp.max(jnp.where(sub == i1, ninf, sel), axis=1, keepdims=True)
    work = m1 + m2
    gmask = jnp.zeros(work.shape, jnp.bool_)
    for _ in range(TOPK_GROUPS):
        m = jnp.max(work, axis=0, keepdims=True)
        idx = jnp.min(jnp.where(work == m, gid, float(N_GROUPS)), axis=0, keepdims=True)
        pick = gid == idx
        gmask = gmask | pick
        work = jnp.where(pick, ninf, work)
    work = jnp.where(gmask, sel, ninf)
    eid = gid * per_group + sub
    chosen = jnp.zeros(sel.shape, jnp.bool_)
    for _ in range(TOP_K):
        m = jnp.max(jnp.max(work, axis=1, keepdims=True), axis=0, keepdims=True)
        idx = jnp.min(jnp.min(jnp.where(work == m, eid, float(N_EXPERTS)), axis=1, keepdims=True),
                      axis=0, keepdims=True)
        pick = eid == idx
        chosen = chosen | pick
        work = jnp.where(pick, ninf, work)
    wk = jnp.where(chosen, s, 0.0)
    den = jnp.sum(jnp.sum(wk, axis=1, keepdims=True), axis=0, keepdims=True)
    return wk / den * ROUTED_SCALE


def _merge_kernel(x_ref, m_ref, omla_ref, of_ref, ob_ref, z_ref, ga_ref, gb_ref,
                  woa_ref, wob_ref, wo_ref, gpost_ref, gpre_ref, gdng_ref, wr_ref, eb_ref,
                  x1_ref, h2_ref, gates_ref, cnt_ref):
    m = m_ref[0]
    low = _pair_masks()
    n_blk = x_ref.shape[0] // MOE_BLOCK

    def mix_and_norm(blk):
        rows = slice(blk * MOE_BLOCK, (blk + 1) * MOE_BLOCK)
        parts = []
        for p in range(N_PAIRS):
            sl = slice(p * LANES, (p + 1) * LANES)
            op = of_ref[rows, sl] + ob_ref[rows, sl]
            sq = op * op
            s0 = jnp.sum(jnp.where(low, sq, 0.0), axis=-1, keepdims=True)
            s1 = jnp.sum(jnp.where(low, 0.0, sq), axis=-1, keepdims=True)
            ms = jnp.where(low, s0, s1) * (1.0 / GDN_DV)
            parts.append(op * lax.rsqrt(ms + EPS) * gdng_ref[...] * _silu(z_ref[rows, sl]))
        og = jnp.concatenate(parts, axis=1).astype(BF16)
        ya = _dot(omla_ref[rows, :], woa_ref[...])
        yb = _dot(og, wob_ref[...])
        mix = (jax.nn.sigmoid(ga_ref[rows, :]) * ya + jax.nn.sigmoid(gb_ref[rows, :]) * yb).astype(BF16)
        y = _dot(mix, wo_ref[...])
        x1 = x_ref[rows, :] + m[2:3] * _rms(y, gpost_ref[...])
        x1_ref[rows, :] = x1
        h2 = _rms(x1, gpre_ref[...]) * (1.0 + m[4:5]) + m[3:4]
        h2_ref[rows, :] = h2.astype(BF16)
        return h2

    def route(blk, h2):
        wh, wl, _ = _split3(wr_ref[...])
        hh, hl, _ = _split3(h2)
        logits = _dot_nt(wh, hh) + (_dot_nt(wh, hl) + _dot_nt(wl, hh))
        s = jax.nn.sigmoid(logits)
        sel = s + eb_ref[...]
        shape3 = (N_GROUPS, N_EXPERTS // N_GROUPS, MOE_BLOCK)
        gates_t = _route(sel.reshape(shape3), s.reshape(shape3)).reshape(N_EXPERTS, MOE_BLOCK)
        gates_ref[:, blk * MOE_BLOCK:(blk + 1) * MOE_BLOCK] = gates_t
        cnt_ref[blk] = jnp.sum((gates_t > 0.0).astype(F32), axis=1, keepdims=True)

    h2s = [mix_and_norm(blk) for blk in range(n_blk)]
    for blk in range(n_blk):
        route(blk, h2s[blk])


def _merge_kernel_into(*refs):
    n_in = 16
    _merge_kernel(*refs[:n_in], *refs[n_in + 4:])


def _merge(x, mods, mod_index, omla, o_f, o_b, pr, wts, tile0, total, into=None):
    t = x.shape[0]
    tm = MERGE_TILE
    bpt = tm // MOE_BLOCK

    def full(a):
        return pl.BlockSpec(a.shape, lambda i: (0,) * a.ndim)

    def rows(w):
        return pl.BlockSpec((tm, w), lambda i: (i, 0))

    def out_rows(w):
        return pl.BlockSpec((tm, w), lambda i: (tile0 + i, 0))

    names = ("woa", "wob", "wo", "gpost", "gpre2", "gdng", "wr_t", "eb")
    args = [x, mods, omla, o_f, o_b, pr["z"], pr["ga"], pr["gb"]] + [wts[n] for n in names]
    in_specs = [rows(D_MODEL), pl.BlockSpec((1, 6, D_MODEL), lambda i: (mod_index(i), 0, 0)),
                rows(MLA_HEADS * V_HEAD), rows(GDN_W), rows(GDN_W), rows(GDN_W),
                rows(D_MODEL), rows(D_MODEL)] + [full(wts[n]) for n in names]
    aliases = {}
    if into is not None:
        aliases = {len(args) + k: k for k in range(4)}
        in_specs = in_specs + [pl.BlockSpec(memory_space=pl.ANY)] * 4
        args = args + list(into)
    return pl.pallas_call(
        _merge_kernel if into is None else _merge_kernel_into,
        out_shape=[jax.ShapeDtypeStruct((total, D_MODEL), F32), jax.ShapeDtypeStruct((total, D_MODEL), BF16),
                   jax.ShapeDtypeStruct((N_EXPERTS, total), F32),
                   jax.ShapeDtypeStruct((total // MOE_BLOCK, N_EXPERTS, 1), F32)],
        grid=(t // tm,),
        in_specs=in_specs,
        out_specs=[out_rows(D_MODEL), out_rows(D_MODEL),
                   pl.BlockSpec((N_EXPERTS, tm), lambda i: (0, tile0 + i)),
                   pl.BlockSpec((bpt, N_EXPERTS, 1), lambda i: (tile0 + i, 0, 0))],
        input_output_aliases=aliases,
        compiler_params=_cparams(("parallel",)),
        name="merge",
    )(*args)


TABLE_W = 256
TILE_TABLE_W = 512


def _ceil_div(x, d):
    return jnp.floor((x + (d - 1)) * (1.0 / d))


def _moe_tables_kernel(cnt_ref, cnt_t_ref, ce_ref, cb_ref, dst_ref, src_ref, ng_ref, tile_ref):
    nb = cnt_ref.shape[0]
    ppt = float(TILE_PIECES)
    ppg = float(LOOP_PIECES)
    ei = lax.broadcasted_iota(jnp.int32, (N_EXPERTS, N_EXPERTS), 0)
    ej = lax.broadcasted_iota(jnp.int32, (N_EXPERTS, N_EXPERTS), 1)
    tri = (ej <= ei).astype(BF16)

    def cumsum_experts(col):
        wide = jnp.broadcast_to(col, (N_EXPERTS, LANES))
        return functools.reduce(jnp.add, [_dot(tri, pc) for pc in _split3(wide)])[:, 0:1]

    eid = lax.broadcasted_iota(jnp.int32, (N_EXPERTS, 1), 0).astype(F32)
    pc_t = _ceil_div(cnt_t_ref[...], PIECE)
    tp = jnp.sum(pc_t, axis=1, keepdims=True)
    rp = _ceil_div(tp, TILE_PIECES) * ppt
    gs_end = cumsum_experts(rp)
    gs = gs_end - rp
    blk = lax.broadcasted_iota(jnp.int32, (1, nb), 1)
    c = lax.broadcasted_iota(jnp.int32, (1, TABLE_W), 1).astype(F32)
    for b in range(nb):
        pc = _ceil_div(cnt_ref[b], PIECE)
        seg_end = cumsum_experts(pc)
        seg = seg_end - pc
        blk_off = jnp.sum(jnp.where(blk < b, pc_t, 0.0), axis=1, keepdims=True)
        nvalid = seg_end[N_EXPERTS - 1:N_EXPERTS, :]
        ce = jnp.minimum(jnp.sum((seg_end <= c).astype(F32), axis=0, keepdims=True), N_EXPERTS - 1.0)
        onehot = eid == ce
        seg_sel = jnp.sum(jnp.where(onehot, seg, 0.0), axis=0, keepdims=True)
        base_sel = jnp.sum(jnp.where(onehot, gs + blk_off - seg, 0.0), axis=0, keepdims=True)
        valid = c < nvalid
        dump = DUMP_PIECE0 + b * LOOP_PIECES + (c - ppg * jnp.floor(c * (1.0 / ppg)))
        dst = jnp.where(valid, base_sel + c, dump)
        row = slice(b, b + 1)
        ce_ref[row, :] = ce.astype(jnp.int32)
        cb_ref[row, :] = jnp.where(valid, (c - seg_sel) * PIECE, -float(1 << 20)).astype(jnp.int32)
        dst_ref[row, :] = dst.astype(jnp.int32)
        src_ref[row, :] = jnp.where(valid, dst, dst[:, 0:1]).astype(jnp.int32)
        ng_ref[row, :] = jnp.broadcast_to(_ceil_div(nvalid, LOOP_PIECES), (1, LANES)).astype(jnp.int32)
    j = lax.broadcasted_iota(jnp.int32, (1, TILE_TABLE_W), 1).astype(F32)
    start = j * ppt
    te = jnp.minimum(jnp.sum((gs_end <= start).astype(F32), axis=0, keepdims=True), N_EXPERTS - 1.0)
    onehot = eid == te
    tp_sel = jnp.sum(jnp.where(onehot, tp, 0.0), axis=0, keepdims=True)
    gs_sel = jnp.sum(jnp.where(onehot, gs, 0.0), axis=0, keepdims=True)
    n_used = gs_end[N_EXPERTS - 1:N_EXPERTS, :] * (1.0 / ppt)
    used = j < n_used
    tv = jnp.where(used, jnp.clip((tp_sel - (start - gs_sel)) * PIECE, 0.0, float(EXP_TILE)), 0.0)
    tin = jnp.where(used, j, n_used - 1.0)
    tout = jnp.where(used, j, float(N_EXP_TILES))
    tile_ref[...] = jnp.zeros(tile_ref.shape, jnp.int32)
    for r, v in enumerate((te, tv, tin, tout)):
        tile_ref[r:r + 1, :] = v.astype(jnp.int32)


def _dispatch_tables(cnt):
    nb = cnt.shape[0]
    tab = jax.ShapeDtypeStruct((nb, TABLE_W), jnp.int32)
    ce, cb, dst, src, ng, tile = pl.pallas_call(
        _moe_tables_kernel,
        out_shape=[tab, tab, tab, tab, jax.ShapeDtypeStruct((nb, LANES), jnp.int32),
                   jax.ShapeDtypeStruct((8, TILE_TABLE_W), jnp.int32)],
        name="moe_tables",
    )(cnt, cnt[:, :, 0].T)
    return {"ce": ce, "cb": cb, "dst": dst, "src": src, "ngroups": ng, "tile": tile}


def _piece_onehot(ce_ref, cb_ref, rank_s, b, g, extra=None):
    sub = lax.broadcasted_iota(jnp.int32, (PIECE, 1), 0).astype(F32)
    ps, ex = [], []
    for cc in range(GROUP_PIECES):
        c = g * GROUP_PIECES + cc
        e = ce_ref[b, c]
        base = cb_ref[b, c].astype(F32)
        hit = rank_s[pl.ds(e, 1), :] == base + sub
        ps.append(jnp.where(hit, 1.0, 0.0).astype(BF16))
        if extra is not None:
            ex.append(jnp.sum(jnp.where(hit, extra[pl.ds(e, 1), :], 0.0), axis=-1, keepdims=True))
    p = jnp.concatenate(ps, axis=0)
    return (p, jnp.concatenate(ex, axis=0)) if extra is not None else p


def _block_ranks(gt):
    n = gt.shape[1]
    ti = lax.broadcasted_iota(jnp.int32, (n, n), 0)
    tj = lax.broadcasted_iota(jnp.int32, (n, n), 1)
    before = (ti < tj).astype(BF16)
    member = gt > 0.0
    rank = _dot(member.astype(BF16), before)
    return jnp.where(member, rank, -1.0)


def _moe_sort_kernel(ce_ref, cb_ref, dst_ref, ng_ref, h_ref, gt_ref, xg_ref, stage, rank_s, gate_s, sem):
    b = pl.program_id(0)
    slot = b % 2
    gt = gt_ref[...]
    rank_s[...] = _block_ranks(gt)
    gate_s[...] = gt
    lane = lax.broadcasted_iota(jnp.int32, (1, LANES), 1)

    def piece_copy(blk, sl, c):
        r0 = pl.multiple_of(c * PIECE, PIECE)
        d0 = pl.multiple_of(dst_ref[blk, c] * PIECE, PIECE)
        return pltpu.make_async_copy(stage.at[sl, pl.ds(r0, PIECE)], xg_ref.at[pl.ds(d0, PIECE)], sem.at[sl])

    def groups(it, carry):
        gs = [it * LOOP_GROUPS + k for k in range(LOOP_GROUPS)]
        sel = [_piece_onehot(ce_ref, cb_ref, rank_s, b, g, gate_s) for g in gs]
        xs = [_dot(p, h_ref[...]).astype(BF16) for p, _ in sel]
        for g, x, (_, gcol) in zip(gs, xs, sel):
            hi, mid, lo = (t.astype(F32) for t in _split3(gcol))
            gblk = jnp.where(lane == 0, hi, jnp.where(lane == 1, mid, jnp.where(lane == 2, lo, 0.0)))
            r0 = pl.multiple_of(g * GROUP_ROWS, GROUP_ROWS)
            stage[slot, pl.ds(r0, GROUP_ROWS), :] = jnp.concatenate([x, gblk.astype(BF16)], axis=1)
        for cc in range(LOOP_PIECES):
            piece_copy(b, slot, it * LOOP_PIECES + cc).start()
        return carry

    lax.fori_loop(0, ng_ref[b, 0], groups, 0)

    def drain(blk, sl):
        def wait_some(it, carry):
            for cc in range(LOOP_PIECES):
                piece_copy(blk, sl, it * LOOP_PIECES + cc).wait()
            return carry
        lax.fori_loop(0, ng_ref[blk, 0], wait_some, 0)

    @pl.when(b > 0)
    def _():
        drain(b - 1, 1 - slot)

    @pl.when(b == pl.num_programs(0) - 1)
    def _():
        drain(b, slot)


def _moe_sort(h2, gates_t, tabs):
    nb = h2.shape[0] // MOE_BLOCK
    grid_spec = pltpu.PrefetchScalarGridSpec(
        num_scalar_prefetch=4, grid=(nb,),
        in_specs=[pl.BlockSpec((MOE_BLOCK, D_MODEL), lambda b, *_: (b, 0)),
                  pl.BlockSpec((N_EXPERTS, MOE_BLOCK), lambda b, *_: (0, b))],
        out_specs=pl.BlockSpec(memory_space=pl.ANY),
        scratch_shapes=[pltpu.VMEM((2, STAGE_PIECES * PIECE, DISPATCH_W), BF16),
                        pltpu.VMEM((N_EXPERTS, MOE_BLOCK), F32), pltpu.VMEM((N_EXPERTS, MOE_BLOCK), F32),
                        pltpu.SemaphoreType.DMA((2,))])
    return pl.pallas_call(
        _moe_sort_kernel,
        out_shape=jax.ShapeDtypeStruct((DISPATCH_ROWS, DISPATCH_W), BF16),
        grid_spec=grid_spec,
        compiler_params=_cparams(("arbitrary",)),
        name="moe_sort",
    )(tabs["ce"], tabs["cb"], tabs["dst"], tabs["ngroups"], h2, gates_t)


def _moe_expert_kernel(tile_ref, x_ref, wg_ref, wu_ref, wd_ref, y_ref):
    valid = tile_ref[1, pl.program_id(0)]

    @pl.when(valid > 0)
    def _():
        keep = lax.broadcasted_iota(jnp.int32, (EXP_TILE, 1), 0) < valid
        xrow = x_ref[...]
        x = jnp.where(keep, xrow[:, :D_MODEL], jnp.zeros((), BF16))
        g = jnp.sum(jnp.where(keep, xrow[:, D_MODEL:].astype(F32), 0.0), axis=-1, keepdims=True)
        hg = _dot(x, wg_ref[0].astype(BF16))
        hu = _dot(x, wu_ref[0].astype(BF16))
        act = (_silu(hg) * hu * g).astype(BF16)
        y_ref[...] = _dot(act, wd_ref[0].astype(BF16)).astype(BF16)


def _moe_expert(xg, tabs, wts):
    grid_spec = pltpu.PrefetchScalarGridSpec(
        num_scalar_prefetch=1, grid=(N_EXP_TILES,),
        in_specs=[pl.BlockSpec((EXP_TILE, DISPATCH_W), lambda j, tt: (tt[2, j], 0)),
                  pl.BlockSpec((1, D_MODEL, D_EXPERT), lambda j, tt: (tt[0, j], 0, 0)),
                  pl.BlockSpec((1, D_MODEL, D_EXPERT), lambda j, tt: (tt[0, j], 0, 0)),
                  pl.BlockSpec((1, D_EXPERT, D_MODEL), lambda j, tt: (tt[0, j], 0, 0))],
        out_specs=pl.BlockSpec((EXP_TILE, D_MODEL), lambda j, tt: (tt[3, j], 0)))
    return pl.pallas_call(
        _moe_expert_kernel,
        out_shape=jax.ShapeDtypeStruct((DISPATCH_ROWS, D_MODEL), BF16),
        grid_spec=grid_spec,
        compiler_params=_cparams(("arbitrary",)),
        name="moe_expert",
    )(tabs["tile"], xg, wts["w_gate"], wts["w_up"], wts["w_down"])


def _moe_combine_kernel(ctx_blocks, ce_ref, cb_ref, src_ref, ng_ref, yg_ref, gt_ref, h_ref, x1_ref, m_ref,
                        gpost_ref, wsg_ref, wsu_ref, wsd_ref, outp_ref, outs_ref, stage, rank_s, acc_s, sem):
    b = pl.program_id(0)
    slot = b % 2

    def piece_copy(blk, sl, c):
        r0 = pl.multiple_of(c * PIECE, PIECE)
        s0 = pl.multiple_of(src_ref[blk, c] * PIECE, PIECE)
        return pltpu.make_async_copy(yg_ref.at[pl.ds(s0, PIECE)], stage.at[sl, pl.ds(r0, PIECE)], sem.at[sl])

    def start_pieces(blk, sl, it):
        for cc in range(LOOP_PIECES):
            piece_copy(blk, sl, it * LOOP_PIECES + cc).start()

    def fetch(blk, sl):
        def start_some(it, carry):
            start_pieces(blk, sl, it)
            return carry
        lax.fori_loop(0, ng_ref[blk, 0], start_some, 0)

    ng_cur = ng_ref[b, 0]

    @pl.when(b == 0)
    def _():
        fetch(0, 0)

    @pl.when(b + 1 < pl.num_programs(0))
    def _():
        fetch(b + 1, 1 - slot)

    rank_s[...] = _block_ranks(gt_ref[...])
    h = h_ref[...]
    sh = (_silu(_dot(h, wsg_ref[...])) * _dot(h, wsu_ref[...])).astype(BF16)
    acc_s[...] = _dot(sh, wsd_ref[...])

    def wait_some(it, carry):
        for cc in range(LOOP_PIECES):
            piece_copy(b, slot, it * LOOP_PIECES + cc).wait()
        return carry

    lax.fori_loop(0, ng_cur, wait_some, 0)

    def groups(it, carry):
        gs = [it * LOOP_GROUPS + k for k in range(LOOP_GROUPS)]
        ps = [_piece_onehot(ce_ref, cb_ref, rank_s, b, g) for g in gs]
        ys = [stage[slot, pl.ds(pl.multiple_of(g * GROUP_ROWS, GROUP_ROWS), GROUP_ROWS), :] for g in gs]
        acc_s[...] += functools.reduce(jnp.add, [_dot_tn(p, y) for p, y in zip(ps, ys)])
        return carry

    lax.fori_loop(0, ng_cur, groups, 0)
    m = m_ref[0]
    y = x1_ref[...] + m[5:6] * _rms(acc_s[...], gpost_ref[...])

    @pl.when(b < ctx_blocks)
    def _():
        outp_ref[...] = y

    @pl.when(b >= ctx_blocks)
    def _():
        outs_ref[...] = y


def _moe_combine(yg, gates_t, h2, x1, mods, mod_index, tabs, wts, ctx_tokens):
    t = h2.shape[0]
    nb = t // MOE_BLOCK
    ctx_blocks = ctx_tokens // MOE_BLOCK

    def full(a):
        return pl.BlockSpec(a.shape, lambda b, *_: (0,) * a.ndim)

    grid_spec = pltpu.PrefetchScalarGridSpec(
        num_scalar_prefetch=4, grid=(nb,),
        in_specs=[pl.BlockSpec(memory_space=pl.ANY),
                  pl.BlockSpec((N_EXPERTS, MOE_BLOCK), lambda b, *_: (0, b)),
                  pl.BlockSpec((MOE_BLOCK, D_MODEL), lambda b, *_: (b, 0)),
                  pl.BlockSpec((MOE_BLOCK, D_MODEL), lambda b, *_: (b, 0)),
                  pl.BlockSpec((1, 6, D_MODEL), lambda b, *_: (mod_index(b), 0, 0)),
                  full(wts["gpost2"]), full(wts["wsg"]), full(wts["wsu"]), full(wts["wsd"])],
        out_specs=[pl.BlockSpec((MOE_BLOCK, D_MODEL), lambda b, *_: (jnp.minimum(b, ctx_blocks - 1), 0)),
                   pl.BlockSpec((MOE_BLOCK, D_MODEL), lambda b, *_: (jnp.maximum(b - ctx_blocks, 0), 0))],
        scratch_shapes=[pltpu.VMEM((2, STAGE_PIECES * PIECE, D_MODEL), BF16),
                        pltpu.VMEM((N_EXPERTS, MOE_BLOCK), F32), pltpu.VMEM((MOE_BLOCK, D_MODEL), F32),
                        pltpu.SemaphoreType.DMA((2,))])
    return pl.pallas_call(
        functools.partial(_moe_combine_kernel, ctx_blocks),
        out_shape=[jax.ShapeDtypeStruct((ctx_tokens, D_MODEL), F32),
                   jax.ShapeDtypeStruct((t - ctx_tokens, D_MODEL), F32)],
        grid_spec=grid_spec,
        compiler_params=_cparams(("arbitrary",)),
        name="moe_combine",
    )(tabs["ce"], tabs["cb"], tabs["src"], tabs["ngroups"], yg, gates_t, h2, x1, mods,
      wts["gpost2"], wts["wsg"], wts["wsu"], wts["wsd"])


def _moe(h2, gates_t, cnt, x1, mods, mod_index, wts, ctx_tokens):
    tabs = _dispatch_tables(cnt)
    xg = _moe_sort(h2, gates_t, tabs)
    yg = _moe_expert(xg, tabs, wts)
    return _moe_combine(yg, gates_t, h2, x1, mods, mod_index, tabs, wts, ctx_tokens)


def _rope_swap(w):
    nf = QK_ROPE // 4
    parts = [w[..., i * nf:(i + 1) * nf] for i in range(4)]
    return jnp.concatenate([parts[1], parts[0], parts[3], parts[2]], axis=-1)


def _head_block(nope, rope):
    lead = nope.shape[:-2] if nope is not None else rope.shape[:-2]
    nope = jnp.zeros(lead + (MLA_HEADS, QK_NOPE), F32) if nope is None else nope
    rope = jnp.zeros(lead + (MLA_HEADS, QK_ROPE), F32) if rope is None else rope
    pad = jnp.zeros(lead + (MLA_HEADS, HEAD_PAD - MLA_QK), F32)
    return jnp.concatenate([nope, rope, pad], axis=-1).reshape(lead + (MLA_HEADS * HEAD_PAD,))


def _rope_block(w):
    lead = w.shape[:-1]
    return jnp.concatenate([jnp.zeros(lead + (QK_NOPE,), F32), w,
                            jnp.zeros(lead + (HEAD_PAD - MLA_QK,), F32)], axis=-1)


def _rope_tables(n_tokens):
    rows = n_tokens // GRID_W
    row = np.repeat(np.arange(rows, dtype=np.float64), GRID_W)
    colv = np.tile(np.arange(GRID_W, dtype=np.float64), rows)
    nf = QK_ROPE // 4
    inv = ROPE_THETA ** (-np.arange(nf, dtype=np.float64) / nf)
    ang_r = row[:, None] * inv
    ang_c = colv[:, None] * inv
    cos32 = np.concatenate([np.cos(ang_r), np.cos(ang_r), np.cos(ang_c), np.cos(ang_c)], axis=-1)
    sin32 = np.concatenate([-np.sin(ang_r), np.sin(ang_r), -np.sin(ang_c), np.sin(ang_c)], axis=-1)
    ones = np.ones((n_tokens, QK_NOPE))
    tail = np.zeros((n_tokens, HEAD_PAD - MLA_QK))
    cos = np.concatenate([ones, cos32, tail], axis=-1)
    sin = np.concatenate([np.zeros((n_tokens, QK_NOPE)), sin32, tail], axis=-1)
    return jnp.asarray(cos, F32), jnp.asarray(sin, F32)


def _expand_matrix():
    e = np.zeros((N_PAIRS, LANES, 4 * LANES), np.float32)
    for p in range(N_PAIRS):
        for blk in range(4):
            for hh in range(2):
                src = blk * GDN_HEADS + 2 * p + hh
                e[p, src, blk * LANES + hh * GDN_DK: blk * LANES + (hh + 1) * GDN_DK] = 1.0
    return jnp.asarray(e)


def _prepare_weights(w_in, q_norm_g, kv_norm_g, w_uq, w_ukv, w_oa, w_ob, w_o, g_post_mix, g_pre_ffn,
                     g_post_ffn, gdn_norm_g, w_router, e_bias, w_gate, w_up, w_down, ws_gate, ws_up, ws_down):
    offs = np.cumsum((Q_LORA, KV_LORA, QK_ROPE, 3 * GDN_W, GDN_W, 2 * GDN_HEADS, 2 * GDN_HEADS,
                      D_MODEL, D_MODEL))[:-1].tolist()
    cq, ckv, kr, qkv, z, a, b, ga, gb = jnp.split(w_in, offs, axis=-1)
    ab = jnp.concatenate([a, b, jnp.zeros((D_MODEL, LANES - 4 * GDN_HEADS), F32)], axis=-1)
    small = jnp.concatenate([_rope_block(kr), _rope_block(_rope_swap(kr)), ab], axis=-1)
    lo0, mid0, hi0 = 0, offs[2], offs[6]
    uq = w_uq.reshape(Q_LORA, MLA_HEADS, MLA_QK)
    ukv = w_ukv.reshape(KV_LORA, MLA_HEADS, QK_NOPE + V_HEAD)
    return {
        "w_lo": w_in[:, lo0:lo0 + Q_LORA + KV_LORA].astype(BF16),
        "w_small": small.astype(BF16),
        "w_mid": w_in[:, mid0:mid0 + 4 * GDN_W].astype(BF16),
        "w_hi": w_in[:, hi0:hi0 + 2 * D_MODEL].astype(BF16),
        "qg": q_norm_g.reshape(1, Q_LORA), "kvg": kv_norm_g.reshape(1, KV_LORA),
        "wuq": _head_block(uq[..., :QK_NOPE], uq[..., QK_NOPE:]).astype(BF16),
        "wuqs": _head_block(None, _rope_swap(uq[..., QK_NOPE:])).astype(BF16),
        "wuk": _head_block(ukv[..., :QK_NOPE], None).astype(BF16),
        "wuv": ukv[..., QK_NOPE:].reshape(KV_LORA, MLA_HEADS * V_HEAD).astype(BF16),
        "woa": w_oa.astype(BF16), "wob": w_ob.astype(BF16), "wo": w_o.astype(BF16),
        "gpost": g_post_mix.reshape(1, D_MODEL), "gpre2": g_pre_ffn.reshape(1, D_MODEL),
        "gpost2": g_post_ffn.reshape(1, D_MODEL),
        "gdng": jnp.tile(gdn_norm_g.reshape(1, GDN_DV), (1, 2)),
        "wr_t": w_router.T, "eb": e_bias.reshape(N_EXPERTS, 1),
        "w_gate": w_gate, "w_up": w_up, "w_down": w_down,
        "wsg": ws_gate.astype(BF16), "wsu": ws_up.astype(BF16), "wsd": ws_down.astype(BF16),
    }


def _pad_lanes(v):
    v = v.reshape(1, -1)
    return jnp.concatenate([v, jnp.zeros((1, LANES - v.shape[1]), F32)], axis=-1)


def _layer_group(x, n_seq, seq_len, mods, mod_index, wts, gpre, conv_w, alog128, dtb128,
                 expand, rope_tabs, extra_kv, s0, tile0, into):
    pr = _proj(x, mods, mod_index, gpre, wts, rope_tabs)
    kvs = list(extra_kv) + [(pr["k"], pr["v"], seq_len)]
    omla = _attention(pr["q"], kvs, n_seq, seq_len, "attn_%d" % seq_len)
    prep = _gdn_prep(pr["qkv"], pr["ab"], conv_w, alog128, dtb128, expand, n_seq, seq_len)
    o_f, o_b, s_fin = _gdn_scan(prep, s0, n_seq, seq_len)
    t = n_seq * seq_len
    merged = _merge(x, mods, lambda i: mod_index(i * (MERGE_TILE // ROW_TILE)), omla,
                    o_f.reshape(t, GDN_W), o_b.reshape(t, GDN_W), pr, wts, tile0, N_TOKENS, into)
    return merged, pr, s_fin


def _state_to_pairs(s):
    b = s.shape[0]
    s = s.reshape(b, 2, N_PAIRS, 2, GDN_DK, GDN_DV)
    return jnp.transpose(s, (0, 1, 2, 4, 3, 5)).reshape(b, 2, N_PAIRS, GDN_DK, 2 * GDN_DV)


def kernel(x_prompt, x_sample, cache_ckv, cache_krope, state_delta, c, c_ctx, w_ada, b_ada, g_pre_mix,
           g_post_mix, g_pre_ffn, g_post_ffn, w_in, q_norm_g, kv_norm_g, w_uq, w_ukv, conv_w, a_log,
           dt_bias, gdn_norm_g, w_oa, w_ob, w_o, w_router, e_bias, w_gate, w_up, w_down, ws_gate, ws_up,
           ws_down):
    batch, seq, _ = x_prompt.shape
    dec_batch, dec_seq, _ = x_sample.shape
    past = cache_ckv.shape[2]
    assert batch * seq + dec_batch * dec_seq == N_TOKENS, "dispatch buffers are sized for N_TOKENS"
    y_p = x_prompt.reshape(batch * seq, D_MODEL)
    y_s = x_sample.reshape(dec_batch * dec_seq, D_MODEL)
    expand = _expand_matrix()
    rope_tabs = _rope_tables(dec_seq)
    cond8 = jnp.concatenate([c_ctx[None], c, jnp.zeros((8 - 1 - dec_batch, D_MODEL), F32)], axis=0)
    ckv_out, krope_out, state_out = [], [], []
    for l in range(DEPTH):
        wts = _prepare_weights(w_in[l], q_norm_g[l], kv_norm_g[l], w_uq[l], w_ukv[l], w_oa[l], w_ob[l],
                               w_o[l], g_post_mix[l], g_pre_ffn[l], g_post_ffn[l], gdn_norm_g[l],
                               w_router[l], e_bias[l], w_gate[l], w_up[l], w_down[l], ws_gate[l],
                               ws_up[l], ws_down[l])
        gpre = g_pre_mix[l].reshape(1, D_MODEL)
        alog128 = _pad_lanes(a_log[l])
        dtb128 = _pad_lanes(dt_bias[l])
        mods = _mods(cond8, w_ada[l], b_ada[l].reshape(1, -1)).reshape(8, 6, D_MODEL)

        zero_state = jnp.zeros((batch, 2, N_PAIRS, GDN_DK, LANES), F32)
        merged_p, pr_p, s_fin = _layer_group(
            y_p, batch, seq, mods, lambda i: 0, wts, gpre, conv_w[l], alog128, dtb128,
            expand, None, [], zero_state, 0, None)
        ckv_out.append(pr_p["ckv"].reshape(batch, seq, KV_LORA))
        krope_out.append(pr_p["kr"][:, QK_NOPE:MLA_QK].reshape(batch, seq, QK_ROPE))
        state_out.append(s_fin)

        kr_ctx = _rope_block(cache_krope[:, l].reshape(dec_batch * past, QK_ROPE))
        k_ctx, v_ctx = _cache_kv(cache_ckv[:, l].reshape(dec_batch * past, KV_LORA), kr_ctx, wts)
        tiles_per_seq = dec_seq // ROW_TILE
        merged_s, _, _ = _layer_group(
            y_s, dec_batch, dec_seq, mods, lambda i: 1 + i // tiles_per_seq,
            wts, gpre, conv_w[l], alog128, dtb128, expand,
            rope_tabs, [(k_ctx, v_ctx, past)], _state_to_pairs(state_delta[:, l]),
            batch * seq // MERGE_TILE, merged_p)

        x1, h2, gates_t, cnt = merged_s
        ctx_blocks = batch * seq // MOE_BLOCK
        blocks_per_seq = dec_seq // MOE_BLOCK
        y_p, y_s = _moe(h2, gates_t, cnt, x1, mods,
                        lambda b: jnp.where(b < ctx_blocks, 0, 1 + (b - ctx_blocks) // blocks_per_seq),
                        wts, batch * seq)
    new_ckv = jnp.stack(ckv_out, axis=1)
    new_krope = jnp.stack(krope_out, axis=1)
    new_state = jnp.stack(state_out, axis=1)
    return (y_p.reshape(batch, seq, D_MODEL), y_s.reshape(dec_batch, dec_seq, D_MODEL),
            new_ckv, new_krope, new_state)
```

```python
import functools

import numpy as np
import jax
import jax.numpy as jnp
from jax import lax
from jax.experimental import pallas as pl
from jax.experimental.pallas import tpu as pltpu

F32 = jnp.float32
BF16 = jnp.bfloat16

D_MODEL = 1024
DEPTH = 1
GRID_W = 64
MLA_HEADS = 8
QK_NOPE = 64
QK_ROPE = 32
V_HEAD = 64
Q_LORA = 256
KV_LORA = 256
ROPE_THETA = 10000.0
GDN_HEADS = 8
GDN_DK = 64
GDN_DV = 64
CONV_K = 5
CHUNK = 64
N_EXPERTS = 64
TOP_K = 8
N_GROUPS = 8
TOPK_GROUPS = 4
D_EXPERT = 256
D_SHARED = 256
ROUTED_SCALE = 2.5
EPS = 1e-6

LANES = 128
LOG2_E = 1.4426950408889634
MLA_QK = QK_NOPE + QK_ROPE
GDN_W = GDN_HEADS * GDN_DK
N_PAIRS = GDN_HEADS // 2
HEAD_PAD = LANES

_SEG = {"cq": ("w_lo", 0, Q_LORA), "ckv": ("w_lo", Q_LORA, Q_LORA + KV_LORA),
        "kr": ("w_small", 0, LANES), "krs": ("w_small", LANES, 2 * LANES), "ab": ("w_small", 2 * LANES, 3 * LANES),
        "qkv": ("w_mid", 0, 3 * GDN_W), "z": ("w_mid", 3 * GDN_W, 4 * GDN_W),
        "ga": ("w_hi", 0, D_MODEL), "gb": ("w_hi", D_MODEL, 2 * D_MODEL)}
_PROJ_W = ("w_lo", "w_small", "w_mid", "w_hi")

ROW_TILE = 512
ATTN_TILE = 512
N_TOKENS = 8192
MOE_BLOCK = 256
MERGE_TILE = 512
PIECE = 16
GROUP_ROWS = 256
GROUP_PIECES = GROUP_ROWS // PIECE
LOOP_GROUPS = 2
LOOP_PIECES = LOOP_GROUPS * GROUP_PIECES
EXP_TILE = 1024
EXP_SUB = 512
TILE_PIECES = EXP_TILE // PIECE
N_BLOCKS = N_TOKENS // MOE_BLOCK
STAGE_PIECES = -(-((MOE_BLOCK * TOP_K + N_EXPERTS * (PIECE - 1)) // PIECE) // LOOP_PIECES) * LOOP_PIECES
N_EXP_TILES = -(-((N_TOKENS * TOP_K + N_BLOCKS * N_EXPERTS * (PIECE - 1)) // PIECE
                  + N_EXPERTS * (TILE_PIECES - 1)) // TILE_PIECES)
DUMP_PIECE0 = N_EXP_TILES * TILE_PIECES
DISPATCH_ROWS = (DUMP_PIECE0 + max(N_BLOCKS * LOOP_PIECES, TILE_PIECES)) * PIECE
DISPATCH_W = D_MODEL + LANES
PREP_GROUP = 8
PREP_ROWS = 1024
SCAN_SEQS = 4
SCAN_CHUNKS = 4
VMEM_LIMIT = 56 * 1024 * 1024


def _dot(a, b):
    return jnp.dot(a, b, preferred_element_type=F32)


def _dot_nt(a, b):
    return lax.dot_general(a, b, (((1,), (1,)), ((), ())), preferred_element_type=F32)


def _dot_tn(a, b):
    return lax.dot_general(a, b, (((0,), (0,)), ((), ())), preferred_element_type=F32)


def _rms(x, g):
    return x * lax.rsqrt(jnp.mean(x * x, axis=-1, keepdims=True) + EPS) * g


def _silu(x):
    return x * jax.nn.sigmoid(x)


def _cparams(sem):
    return pltpu.CompilerParams(dimension_semantics=sem, vmem_limit_bytes=VMEM_LIMIT)


def _mods_kernel(c_ref, w_ref, b_ref, o_ref):
    s = _silu(c_ref[...]).astype(BF16)
    o_ref[...] = _dot(s, w_ref[...].astype(BF16)) + b_ref[...]


def _mods(cond8, w_ada, b_ada):
    n = w_ada.shape[1]
    bn = 1024
    return pl.pallas_call(
        _mods_kernel,
        out_shape=jax.ShapeDtypeStruct((8, n), F32),
        grid=(n // bn,),
        in_specs=[pl.BlockSpec((8, D_MODEL), lambda j: (0, 0)),
                  pl.BlockSpec((D_MODEL, bn), lambda j: (0, j)),
                  pl.BlockSpec((1, bn), lambda j: (0, j))],
        out_specs=pl.BlockSpec((8, bn), lambda j: (0, j)),
        compiler_params=_cparams(("parallel",)),
        name="mods",
    )(cond8, w_ada, b_ada)


def _proj_kernel(rope, x_ref, m_ref, gpre_ref, wlo_ref, wsmall_ref, wmid_ref, whi_ref, qg_ref, kvg_ref,
                 wuq_ref, wuqs_ref, wuk_ref, wuv_ref, cos_ref, sin_ref,
                 q_ref, k_ref, v_ref, ckv_ref, kr_ref, qkv_ref, z_ref, ab_ref, ga_ref, gb_ref):
    m = m_ref[0]
    h = (_rms(x_ref[...], gpre_ref[...]) * (1.0 + m[1:2]) + m[0:1]).astype(BF16)
    w_refs = dict(zip(_PROJ_W, (wlo_ref, wsmall_ref, wmid_ref, whi_ref)))

    def seg(name):
        op, a, b = _SEG[name]
        return _dot(h, w_refs[op][:, a:b])

    qkv_ref[...] = seg("qkv")
    z_ref[...] = seg("z")
    ab_ref[...] = seg("ab")
    ga_ref[...] = seg("ga")
    gb_ref[...] = seg("gb")

    qn = _rms(seg("cq"), qg_ref[...]).astype(BF16)
    ckv = _rms(seg("ckv"), kvg_ref[...])
    ckv_ref[...] = ckv
    ckv_b = ckv.astype(BF16)
    kr = seg("kr")
    kr_ref[...] = kr
    qm = _dot(qn, wuq_ref[...])
    kk = _dot(ckv_b, wuk_ref[...])
    v_ref[...] = _dot(ckv_b, wuv_ref[...]).astype(BF16)
    scale = MLA_QK ** -0.5 * LOG2_E
    if rope:
        cos = cos_ref[...]
        sin = sin_ref[...]
        qs = _dot(qn, wuqs_ref[...])
        kr = kr * cos + seg("krs") * sin
    for hd in range(MLA_HEADS):
        sl = slice(hd * HEAD_PAD, (hd + 1) * HEAD_PAD)
        qh = qm[:, sl]
        if rope:
            qh = qh * cos + qs[:, sl] * sin
        q_ref[:, sl] = (qh * scale).astype(BF16)
        k_ref[:, sl] = (kk[:, sl] + kr).astype(BF16)


def _proj(x, mods, mod_index, gpre, wts, rope_tabs):
    t = x.shape[0]
    tm = ROW_TILE
    rope = rope_tabs is not None
    if rope:
        cos, sin = rope_tabs
        n_rope_blocks = cos.shape[0] // tm
        rope_spec = pl.BlockSpec((tm, LANES), lambda i: (i % n_rope_blocks, 0))
    else:
        cos = sin = jnp.zeros((8, LANES), F32)
        rope_spec = pl.BlockSpec((8, LANES), lambda i: (0, 0))

    def full(a):
        return pl.BlockSpec(a.shape, lambda i: (0,) * a.ndim)

    def rows(w):
        return pl.BlockSpec((tm, w), lambda i: (i, 0))

    out_widths = (("q", MLA_HEADS * HEAD_PAD, BF16), ("k", MLA_HEADS * HEAD_PAD, BF16),
                  ("v", MLA_HEADS * V_HEAD, BF16), ("ckv", KV_LORA, F32), ("kr", LANES, F32),
                  ("qkv", 3 * GDN_W, F32), ("z", GDN_W, F32), ("ab", LANES, F32),
                  ("ga", D_MODEL, F32), ("gb", D_MODEL, F32))
    outs = pl.pallas_call(
        functools.partial(_proj_kernel, rope),
        out_shape=[jax.ShapeDtypeStruct((t, w), dt) for _, w, dt in out_widths],
        grid=(t // tm,),
        in_specs=[rows(D_MODEL),
                  pl.BlockSpec((1, 6, D_MODEL), lambda i: (mod_index(i), 0, 0)),
                  full(gpre)] + [full(wts[n]) for n in _PROJ_W] + [full(wts["qg"]), full(wts["kvg"]),
                  full(wts["wuq"]), full(wts["wuqs"]), full(wts["wuk"]), full(wts["wuv"]),
                  rope_spec, rope_spec],
        out_specs=[rows(w) for _, w, _ in out_widths],
        compiler_params=_cparams(("parallel",)),
        name="proj_rope" if rope else "proj",
    )(x, mods, gpre, *[wts[n] for n in _PROJ_W], wts["qg"], wts["kvg"], wts["wuq"], wts["wuqs"],
      wts["wuk"], wts["wuv"], cos, sin)
    return {name: o for (name, _, _), o in zip(out_widths, outs)}


def _cache_kv_kernel(ckv_ref, kr_ref, wuk_ref, wuv_ref, k_ref, v_ref):
    c = ckv_ref[...].astype(BF16)
    kk = _dot(c, wuk_ref[...])
    v_ref[...] = _dot(c, wuv_ref[...]).astype(BF16)
    kr = kr_ref[...]
    for hd in range(MLA_HEADS):
        sl = slice(hd * HEAD_PAD, (hd + 1) * HEAD_PAD)
        k_ref[:, sl] = (kk[:, sl] + kr).astype(BF16)


def _cache_kv(ckv, kr128, wts):
    t = ckv.shape[0]
    tm = 512
    return pl.pallas_call(
        _cache_kv_kernel,
        out_shape=[jax.ShapeDtypeStruct((t, MLA_HEADS * HEAD_PAD), BF16),
                   jax.ShapeDtypeStruct((t, MLA_HEADS * V_HEAD), BF16)],
        grid=(t // tm,),
        in_specs=[pl.BlockSpec((tm, KV_LORA), lambda i: (i, 0)),
                  pl.BlockSpec((tm, LANES), lambda i: (i, 0)),
                  pl.BlockSpec(wts["wuk"].shape, lambda i: (0, 0)),
                  pl.BlockSpec(wts["wuv"].shape, lambda i: (0, 0))],
        out_specs=[pl.BlockSpec((tm, MLA_HEADS * HEAD_PAD), lambda i: (i, 0)),
                   pl.BlockSpec((tm, MLA_HEADS * V_HEAD), lambda i: (i, 0))],
        compiler_params=_cparams(("parallel",)),
        name="cache_kv",
    )(ckv, kr128, wts["wuk"], wts["wuv"])


def _attn_kernel(n_kv, q_ref, *refs):
    k_refs = refs[:n_kv]
    v_refs = refs[n_kv:2 * n_kv]
    o_ref = refs[2 * n_kv]
    lane = lax.broadcasted_iota(jnp.int32, (1, LANES), 1)
    low = lane < V_HEAD
    for pr in range(MLA_HEADS // 2):
        halves = []
        for hd in (2 * pr, 2 * pr + 1):
            sl = slice(hd * HEAD_PAD, (hd + 1) * HEAD_PAD)
            qh = q_ref[:, sl]
            scores = [_dot_nt(qh, kr[:, sl]) for kr in k_refs]
            mx = functools.reduce(jnp.maximum, [jnp.max(s, axis=-1, keepdims=True) for s in scores])
            ps = [jnp.exp2(s - mx) for s in scores]
            den = functools.reduce(jnp.add, [jnp.sum(p, axis=-1, keepdims=True) for p in ps])
            vsl = slice(pr * LANES, (pr + 1) * LANES)
            acc = functools.reduce(jnp.add, [_dot(p.astype(BF16), vr[:, vsl]) for p, vr in zip(ps, v_refs)])
            halves.append(acc / den)
        o_ref[:, pr * LANES:(pr + 1) * LANES] = jnp.where(low, halves[0], halves[1]).astype(BF16)


def _attention(q, kvs, n_seq, seq_len, name):
    tq = min(seq_len, ATTN_TILE)
    nq = seq_len // tq
    n_kv = len(kvs)
    in_specs = [pl.BlockSpec((tq, MLA_HEADS * HEAD_PAD), lambda b, j: (b * nq + j, 0))]
    in_specs += [pl.BlockSpec((rows, MLA_HEADS * HEAD_PAD), lambda b, j: (b, 0)) for _, _, rows in kvs]
    in_specs += [pl.BlockSpec((rows, MLA_HEADS * V_HEAD), lambda b, j: (b, 0)) for _, _, rows in kvs]
    return pl.pallas_call(
        functools.partial(_attn_kernel, n_kv),
        out_shape=jax.ShapeDtypeStruct((n_seq * seq_len, MLA_HEADS * V_HEAD), BF16),
        grid=(n_seq, nq),
        in_specs=in_specs,
        out_specs=pl.BlockSpec((tq, MLA_HEADS * V_HEAD), lambda b, j: (b * nq + j, 0)),
        compiler_params=_cparams(("parallel", "parallel")),
        name=name,
    )(q, *[k for k, _, _ in kvs], *[v for _, v, _ in kvs])


def _pair_masks():
    lane = lax.broadcasted_iota(jnp.int32, (1, LANES), 1)
    return lane < GDN_DK


def _stack(x, low):
    zero = jnp.zeros_like(x)
    return jnp.concatenate([jnp.where(low, x, zero), jnp.where(low, zero, x)], axis=0)


def _split3(x):
    hi = x.astype(BF16)
    r = x - hi.astype(F32)
    mid = r.astype(BF16)
    lo = (r - mid.astype(F32)).astype(BF16)
    return hi, mid, lo


def _gdn_prep_kernel(seq_len, q_ref, k_ref, v_ref, cwq_ref, cwk_ref, cwv_ref, ab_ref, alog_ref, dtb_ref, e_ref,
                     uf_ref, ub_ref, wf_ref, wb_ref, af_ref, abk_ref, qdf_ref, qdb_ref, kdf_ref, kdb_ref,
                     glf_ref, glb_ref,
                     qn_s, kn_s, vn_s, gcb_s, gf_s):
    seq = q_ref.shape[0]
    n_chunks = seq // CHUNK
    low = _pair_masks()
    row = lax.broadcasted_iota(jnp.int32, (seq, 1), 0) % seq_len
    lane = lax.broadcasted_iota(jnp.int32, (1, LANES), 1)

    def conv(x_ref, cw_ref):
        x = x_ref[...]
        acc = jnp.zeros_like(x)
        for j in range(CONV_K):
            sh = CONV_K // 2 - j
            xs = x if sh == 0 else pltpu.roll(x, sh % seq, axis=0)
            src = row - sh
            valid = (src >= 0) & (src < seq_len)
            acc = acc + jnp.where(valid, xs, 0.0) * cw_ref[j:j + 1, :]
        return _silu(acc)

    def l2n(x):
        sq = x * x
        s0 = jnp.sum(jnp.where(low, sq, 0.0), axis=-1, keepdims=True)
        s1 = jnp.sum(jnp.where(low, 0.0, sq), axis=-1, keepdims=True)
        return x * lax.rsqrt(jnp.where(low, s0, s1) + EPS)

    qn_s[...] = l2n(conv(q_ref, cwq_ref)) * (GDN_DK ** -0.5)
    kn_s[...] = l2n(conv(k_ref, cwk_ref))
    vn_s[...] = conv(v_ref, cwv_ref)

    a = ab_ref[...]
    xg = a + dtb_ref[...]
    softplus = jnp.maximum(xg, 0.0) + jnp.log(1.0 + jnp.exp(-jnp.abs(xg)))
    act = jnp.where(lane < 2 * GDN_HEADS, -jnp.exp(alog_ref[...]) * softplus, jax.nn.sigmoid(a))

    ti = lax.broadcasted_iota(jnp.int32, (CHUNK, CHUNK), 0)
    tj = lax.broadcasted_iota(jnp.int32, (CHUNK, CHUNK), 1)
    tri_lo = (tj <= ti).astype(BF16)
    tri_up = (tj >= ti).astype(BF16)
    for c in range(n_chunks):
        ac = act[c * CHUNK:(c + 1) * CHUNK]
        pieces = _split3(ac)
        lo = functools.reduce(jnp.add, [_dot(tri_lo, pc) for pc in pieces])
        up = functools.reduce(jnp.add, [_dot(tri_up, pc) for pc in pieces])
        gcb_s[c * CHUNK:(c + 1) * CHUNK, :] = jnp.where(lane < GDN_HEADS, lo,
                                                        jnp.where(lane < 2 * GDN_HEADS, up, ac))
    expand = e_ref[0].astype(BF16)
    gf_s[...] = functools.reduce(jnp.add, [_dot(pc, expand) for pc in _split3(gcb_s[...])])

    ri = lax.broadcasted_iota(jnp.int32, (CHUNK, LANES), 0)
    cj = lax.broadcasted_iota(jnp.int32, (CHUNK, LANES), 1) % CHUNK
    eye = (ri == cj).astype(F32)

    def pmm(x, y):
        return _dot(x.astype(BF16), _stack(y, low).astype(BF16))

    def row_form(g):
        gt = jnp.concatenate([g, jnp.zeros_like(g)], axis=0).T
        r0 = jnp.broadcast_to(gt[0:1, :], (CHUNK, LANES))
        r1 = jnp.broadcast_to(gt[GDN_DK:GDN_DK + 1, :], (CHUNK, LANES))
        return jnp.where(low, r0, pltpu.roll(r1, GDN_DK, axis=1))

    out_refs = ((uf_ref, wf_ref, af_ref, qdf_ref, kdf_ref, glf_ref),
                (ub_ref, wb_ref, abk_ref, qdb_ref, kdb_ref, glb_ref))
    incl = (ri >= cj, ri <= cj)
    strict = (ri > cj, ri < cj)
    diag8 = (ri // 8) == (cj // 8)
    merge_masks = [((ri // (2 * s)) == (cj // (2 * s))) & ((ri // s) != (cj // s)) for s in (8, 16, 32)]

    def group(it, carry):
        cs = [it * PREP_GROUP + cc for cc in range(PREP_GROUP)]
        rows = [pl.ds(pl.multiple_of(c * CHUNK, CHUNK), CHUNK) for c in cs]
        qc = [qn_s[r, :] for r in rows]
        kc = [kn_s[r, :] for r in rows]
        vc = [vn_s[r, :] for r in rows]
        kst = [_stack(k, low).astype(BF16) for k in kc]
        kk = [_dot_nt(k.astype(BF16), ks) for k, ks in zip(kc, kst)]
        qk = [_dot_nt(q.astype(BF16), ks) for q, ks in zip(qc, kst)]
        chains = [(ci, d) for ci in range(PREP_GROUP) for d in range(2)]
        gc = [gf_s[rows[ci], d * LANES:(d + 1) * LANES] for ci, d in chains]
        beta = [gf_s[rows[ci], (2 + d) * LANES:(3 + d) * LANES] for ci, d in chains]
        gr = [row_form(g) for g in gc]
        dm = [jnp.exp(jnp.where(incl[d], g - r, -jnp.inf)) for (ci, d), g, r in zip(chains, gc, gr)]
        lm = [jnp.where(strict[d], b * kk[ci] * m, 0.0) for (ci, d), b, m in zip(chains, beta, dm)]
        aint = [(qk[ci] * m).astype(BF16) for (ci, d), m in zip(chains, dm)]
        x = [-jnp.where(diag8, l, 0.0) for l in lm]
        t = [eye + xx for xx in x]
        for _ in range(2):
            x = [pmm(xx, xx) for xx in x]
            t = [tt + pmm(tt, xx) for tt, xx in zip(t, x)]
        for off in merge_masks:
            tc = [pmm(tt, jnp.where(off, l, 0.0)) for tt, l in zip(t, lm)]
            t = [tt - pmm(a, tt) for tt, a in zip(t, tc)]
        egc = [jnp.exp(g) for g in gc]
        u = [pmm(tt, vc[ci] * b) for (ci, d), tt, b in zip(chains, t, beta)]
        w = [pmm(tt, kc[ci] * b * e).astype(BF16) for (ci, d), tt, b, e in zip(chains, t, beta, egc)]
        qd = [(qc[ci] * e).astype(BF16) for (ci, d), e in zip(chains, egc)]
        gtot = [g[CHUNK - 1:CHUNK, :] if d == 0 else g[0:1, :] for (ci, d), g in zip(chains, gc)]
        kd = [(kc[ci] * jnp.exp(gt - g)).astype(BF16) for (ci, d), gt, g in zip(chains, gtot, gc)]
        for n, (ci, d) in enumerate(chains):
            u_ref, w_ref, a_ref, qd_ref, kd_ref, gl_ref = out_refs[d]
            u_ref[0, rows[ci], :] = u[n]
            w_ref[0, rows[ci], :] = w[n]
            a_ref[0, rows[ci], :] = aint[n]
            qd_ref[0, rows[ci], :] = qd[n]
            kd_ref[0, rows[ci], :] = kd[n]
            gl_ref[0, pl.ds(cs[ci], 1), :, :] = jnp.broadcast_to(jnp.exp(gtot[n]), (1, 8, LANES))
        return carry

    lax.fori_loop(0, n_chunks // PREP_GROUP, group, 0)


def _gdn_prep(qkv, ab, conv_w, alog128, dtb128, expand, n_seq, seq_len):
    rb = max(seq_len, PREP_ROWS)
    nb = n_seq * seq_len // rb
    n_chunks = rb // CHUNK
    col = lambda off: pl.BlockSpec((rb, LANES), lambda s, p: (s, off + p))
    cw = lambda off: pl.BlockSpec((CONV_K, LANES), lambda s, p: (0, off + p))
    vec = pl.BlockSpec((1, LANES), lambda s, p: (0, 0))
    big = lambda: pl.BlockSpec((1, rb, LANES), lambda s, p: (s, 0, p))
    glspec = lambda: pl.BlockSpec((1, n_chunks, 8, LANES), lambda s, p: (s, 0, 0, p))
    shp = lambda dt: jax.ShapeDtypeStruct((nb, rb, GDN_W), dt)
    glshp = jax.ShapeDtypeStruct((nb, n_chunks, 8, GDN_W), F32)
    outs = pl.pallas_call(
        functools.partial(_gdn_prep_kernel, seq_len),
        out_shape=[shp(F32), shp(F32)] + [shp(BF16)] * 8 + [glshp, glshp],
        grid=(nb, N_PAIRS),
        in_specs=[col(0), col(N_PAIRS), col(2 * N_PAIRS), cw(0), cw(N_PAIRS), cw(2 * N_PAIRS),
                  pl.BlockSpec((rb, LANES), lambda s, p: (s, 0)), vec, vec,
                  pl.BlockSpec((1, LANES, 4 * LANES), lambda s, p: (p, 0, 0))],
        out_specs=[big() for _ in range(10)] + [glspec(), glspec()],
        scratch_shapes=[pltpu.VMEM((rb, LANES), F32)] * 4 + [pltpu.VMEM((rb, 4 * LANES), F32)],
        compiler_params=_cparams(("parallel", "parallel")),
        name="gdn_prep_%d" % seq_len,
    )(qkv, qkv, qkv, conv_w, conv_w, conv_w, ab, alog128, dtb128, expand)
    per_seq = [o.reshape(n_seq, seq_len, GDN_W) for o in outs[:10]]
    return per_seq + [o.reshape(n_seq, seq_len // CHUNK, 8, GDN_W) for o in outs[10:]]


def _gdn_scan_kernel(uf_ref, ub_ref, wf_ref, wb_ref, af_ref, abk_ref, qdf_ref, qdb_ref, kdf_ref, kdb_ref,
                     glf_ref, glb_ref, s0_ref, of_ref, ob_ref, sfin_ref, state):
    step = pl.program_id(1)
    n_steps = pl.num_programs(1)
    low = _pair_masks()
    chains = [(d, j, p) for d in range(2) for j in range(uf_ref.shape[0]) for p in range(N_PAIRS)]

    first = step == 0
    per_dir = ((uf_ref, wf_ref, af_ref, qdf_ref, kdf_ref, glf_ref, of_ref),
               (ub_ref, wb_ref, abk_ref, qdb_ref, kdb_ref, glb_ref, ob_ref))

    s = [jnp.where(first, _stack(s0_ref[j, d, p], low), state[idx]) for idx, (d, j, p) in enumerate(chains)]
    for sub in range(SCAN_CHUNKS):
        at = (sub, SCAN_CHUNKS - 1 - sub)

        def rd(k, d, j, p):
            return per_dir[d][k][j, at[d] * CHUNK:(at[d] + 1) * CHUNK, p * LANES:(p + 1) * LANES]

        sb = [x.astype(BF16) for x in s]
        ws = [_dot(rd(1, *c), b) for c, b in zip(chains, sb)]
        qs = [_dot(rd(3, *c), b) for c, b in zip(chains, sb)]
        vst = [_stack(rd(0, *c) - w, low).astype(BF16) for c, w in zip(chains, ws)]
        upd = [_dot_tn(_stack(rd(4, *c), low), v) for c, v in zip(chains, vst)]
        intra = [_dot(rd(2, *c), v) for c, v in zip(chains, vst)]
        nxt = []
        for idx, (d, j, p) in enumerate(chains):
            sl = slice(p * LANES, (p + 1) * LANES)
            nxt.append(s[idx] * per_dir[d][5][j, at[d], 0:1, sl] + upd[idx])
            per_dir[d][6][j, at[d] * CHUNK:(at[d] + 1) * CHUNK, sl] = qs[idx] + intra[idx]
        s = nxt
    for idx in range(len(chains)):
        state[idx] = s[idx]

    @pl.when(step == n_steps - 1)
    def _():
        for idx, (d, j, p) in enumerate(chains):
            fin = state[idx]
            sfin_ref[j, d, 2 * p] = fin[:GDN_DK, :GDN_DV]
            sfin_ref[j, d, 2 * p + 1] = pltpu.roll(fin[GDN_DK:], GDN_DV, axis=1)[:, :GDN_DV]


def _gdn_scan(prep, s0, n_seq, seq_len):
    n_steps = seq_len // (CHUNK * SCAN_CHUNKS)
    ns = min(SCAN_SEQS, n_seq)
    rows = CHUNK * SCAN_CHUNKS
    fwd = lambda: pl.BlockSpec((ns, rows, GDN_W), lambda g, i: (g, i, 0))
    bwd = lambda: pl.BlockSpec((ns, rows, GDN_W), lambda g, i: (g, n_steps - 1 - i, 0))
    glf = pl.BlockSpec((ns, SCAN_CHUNKS, 8, GDN_W), lambda g, i: (g, i, 0, 0))
    glb = pl.BlockSpec((ns, SCAN_CHUNKS, 8, GDN_W), lambda g, i: (g, n_steps - 1 - i, 0, 0))
    st = pl.BlockSpec((ns, 2, N_PAIRS, GDN_DK, LANES), lambda g, i: (g, 0, 0, 0, 0))
    st_out = pl.BlockSpec((ns, 2, GDN_HEADS, GDN_DK, GDN_DV), lambda g, i: (g, 0, 0, 0, 0))
    oshape = jax.ShapeDtypeStruct((n_seq, seq_len, GDN_W), F32)
    return pl.pallas_call(
        _gdn_scan_kernel,
        out_shape=[oshape, oshape, jax.ShapeDtypeStruct((n_seq, 2, GDN_HEADS, GDN_DK, GDN_DV), F32)],
        grid=(n_seq // ns, n_steps),
        in_specs=[fwd(), bwd()] * 5 + [glf, glb, st],
        out_specs=[fwd(), bwd(), st_out],
        scratch_shapes=[pltpu.VMEM((2 * ns * N_PAIRS, LANES, LANES), F32)],
        compiler_params=_cparams(("parallel", "arbitrary")),
        name="gdn_scan_%d" % seq_len,
    )(*prep, s0)


def _route(sel, s):
    per_group = N_EXPERTS // N_GROUPS
    ninf = -jnp.inf
    sub = lax.broadcasted_iota(jnp.int32, sel.shape, 1).astype(F32)
    gid = lax.broadcasted_iota(jnp.int32, (N_GROUPS, 1, sel.shape[2]), 0).astype(F32)
    m1 = jnp.max(sel, axis=1, keepdims=True)
    i1 = jnp.min(jnp.where(sel == m1, sub, float(per_group)), axis=1, keepdims=True)
    m2 = jnp.max(jnp.where(sub == i1, ninf, sel), axis=1, keepdims=True)
    work = m1 + m2
    gmask = jnp.zeros(work.shape, jnp.bool_)
    for _ in range(TOPK_GROUPS):
        m = jnp.max(work, axis=0, keepdims=True)
        idx = jnp.min(jnp.where(work == m, gid, float(N_GROUPS)), axis=0, keepdims=True)
        pick = gid == idx
        gmask = gmask | pick
        work = jnp.where(pick, ninf, work)
    work = jnp.where(gmask, sel, ninf)
    eid = gid * per_group + sub
    chosen = jnp.zeros(sel.shape, jnp.bool_)
    for _ in range(TOP_K):
        m = jnp.max(jnp.max(work, axis=1, keepdims=True), axis=0, keepdims=True)
        idx = jnp.min(jnp.min(jnp.where(work == m, eid, float(N_EXPERTS)), axis=1, keepdims=True),
                      axis=0, keepdims=True)
        pick = eid == idx
        chosen = chosen | pick
        work = jnp.where(pick, ninf, work)
    wk = jnp.where(chosen, s, 0.0)
    den = jnp.sum(jnp.sum(wk, axis=1, keepdims=True), axis=0, keepdims=True)
    return wk / den * ROUTED_SCALE


def _merge_kernel(x_ref, m_ref, omla_ref, of_ref, ob_ref, z_ref, ga_ref, gb_ref,
                  woa_ref, wob_ref, wo_ref, gpost_ref, gpre_ref, gdng_ref, wr_ref, eb_ref,
                  x1_ref, h2_ref, gates_ref, cnt_ref):
    m = m_ref[0]
    low = _pair_masks()
    n_blk = x_ref.shape[0] // MOE_BLOCK

    def mix_and_norm(blk):
        rows = slice(blk * MOE_BLOCK, (blk + 1) * MOE_BLOCK)
        parts = []
        for p in range(N_PAIRS):
            sl = slice(p * LANES, (p + 1) * LANES)
            op = of_ref[rows, sl] + ob_ref[rows, sl]
            sq = op * op
            s0 = jnp.sum(jnp.where(low, sq, 0.0), axis=-1, keepdims=True)
            s1 = jnp.sum(jnp.where(low, 0.0, sq), axis=-1, keepdims=True)
            ms = jnp.where(low, s0, s1) * (1.0 / GDN_DV)
            parts.append(op * lax.rsqrt(ms + EPS) * gdng_ref[...] * _silu(z_ref[rows, sl]))
        og = jnp.concatenate(parts, axis=1).astype(BF16)
        ya = _dot(omla_ref[rows, :], woa_ref[...])
        yb = _dot(og, wob_ref[...])
        mix = (jax.nn.sigmoid(ga_ref[rows, :]) * ya + jax.nn.sigmoid(gb_ref[rows, :]) * yb).astype(BF16)
        y = _dot(mix, wo_ref[...])
        x1 = x_ref[rows, :] + m[2:3] * _rms(y, gpost_ref[...])
        x1_ref[rows, :] = x1
        h2 = _rms(x1, gpre_ref[...]) * (1.0 + m[4:5]) + m[3:4]
        h2_ref[rows, :] = h2.astype(BF16)
        return h2

    def route(blk, h2):
        wh, wl, _ = _split3(wr_ref[...])
        hh, hl, _ = _split3(h2)
        logits = _dot_nt(wh, hh) + (_dot_nt(wh, hl) + _dot_nt(wl, hh))
        s = jax.nn.sigmoid(logits)
        sel = s + eb_ref[...]
        shape3 = (N_GROUPS, N_EXPERTS // N_GROUPS, MOE_BLOCK)
        gates_t = _route(sel.reshape(shape3), s.reshape(shape3)).reshape(N_EXPERTS, MOE_BLOCK)
        gates_ref[:, blk * MOE_BLOCK:(blk + 1) * MOE_BLOCK] = gates_t
        cnt_ref[blk] = jnp.sum((gates_t > 0.0).astype(F32), axis=1, keepdims=True)

    h2s = [mix_and_norm(blk) for blk in range(n_blk)]
    for blk in range(n_blk):
        route(blk, h2s[blk])


def _merge_kernel_into(*refs):
    n_in = 16
    _merge_kernel(*refs[:n_in], *refs[n_in + 4:])


def _merge(x, mods, mod_index, omla, o_f, o_b, pr, wts, tile0, total, into=None):
    t = x.shape[0]
    tm = MERGE_TILE
    bpt = tm // MOE_BLOCK

    def full(a):
        return pl.BlockSpec(a.shape, lambda i: (0,) * a.ndim)

    def rows(w):
        return pl.BlockSpec((tm, w), lambda i: (i, 0))

    def out_rows(w):
        return pl.BlockSpec((tm, w), lambda i: (tile0 + i, 0))

    names = ("woa", "wob", "wo", "gpost", "gpre2", "gdng", "wr_t", "eb")
    args = [x, mods, omla, o_f, o_b, pr["z"], pr["ga"], pr["gb"]] + [wts[n] for n in names]
    in_specs = [rows(D_MODEL), pl.BlockSpec((1, 6, D_MODEL), lambda i: (mod_index(i), 0, 0)),
                rows(MLA_HEADS * V_HEAD), rows(GDN_W), rows(GDN_W), rows(GDN_W),
                rows(D_MODEL), rows(D_MODEL)] + [full(wts[n]) for n in names]
    aliases = {}
    if into is not None:
        aliases = {len(args) + k: k for k in range(4)}
        in_specs = in_specs + [pl.BlockSpec(memory_space=pl.ANY)] * 4
        args = args + list(into)
    return pl.pallas_call(
        _merge_kernel if into is None else _merge_kernel_into,
        out_shape=[jax.ShapeDtypeStruct((total, D_MODEL), F32), jax.ShapeDtypeStruct((total, D_MODEL), BF16),
                   jax.ShapeDtypeStruct((N_EXPERTS, total), F32),
                   jax.ShapeDtypeStruct((total // MOE_BLOCK, N_EXPERTS, 1), F32)],
        grid=(t // tm,),
        in_specs=in_specs,
        out_specs=[out_rows(D_MODEL), out_rows(D_MODEL),
                   pl.BlockSpec((N_EXPERTS, tm), lambda i: (0, tile0 + i)),
                   pl.BlockSpec((bpt, N_EXPERTS, 1), lambda i: (tile0 + i, 0, 0))],
        input_output_aliases=aliases,
        compiler_params=_cparams(("parallel",)),
        name="merge",
    )(*args)


TABLE_W = 256
TILE_TABLE_W = 512


def _ceil_div(x, d):
    return jnp.floor((x + (d - 1)) * (1.0 / d))


def _moe_tables_kernel(cnt_ref, cnt_t_ref, ce_ref, cb_ref, dst_ref, src_ref, ng_ref, tile_ref):
    nb = cnt_ref.shape[0]
    ppt = float(TILE_PIECES)
    ppg = float(LOOP_PIECES)
    ei = lax.broadcasted_iota(jnp.int32, (N_EXPERTS, N_EXPERTS), 0)
    ej = lax.broadcasted_iota(jnp.int32, (N_EXPERTS, N_EXPERTS), 1)
    tri = (ej <= ei).astype(BF16)

    def cumsum_experts(col):
        wide = jnp.broadcast_to(col, (N_EXPERTS, LANES))
        return functools.reduce(jnp.add, [_dot(tri, pc) for pc in _split3(wide)])[:, 0:1]

    eid = lax.broadcasted_iota(jnp.int32, (N_EXPERTS, 1), 0).astype(F32)
    pc_t = _ceil_div(cnt_t_ref[...], PIECE)
    tp = jnp.sum(pc_t, axis=1, keepdims=True)
    rp = _ceil_div(tp, TILE_PIECES) * ppt
    gs_end = cumsum_experts(rp)
    gs = gs_end - rp
    blk = lax.broadcasted_iota(jnp.int32, (1, nb), 1)
    c = lax.broadcasted_iota(jnp.int32, (1, TABLE_W), 1).astype(F32)
    for b in range(nb):
        pc = _ceil_div(cnt_ref[b], PIECE)
        seg_end = cumsum_experts(pc)
        seg = seg_end - pc
        blk_off = jnp.sum(jnp.where(blk < b, pc_t, 0.0), axis=1, keepdims=True)
        nvalid = seg_end[N_EXPERTS - 1:N_EXPERTS, :]
        ce = jnp.minimum(jnp.sum((seg_end <= c).astype(F32), axis=0, keepdims=True), N_EXPERTS - 1.0)
        onehot = eid == ce
        seg_sel = jnp.sum(jnp.where(onehot, seg, 0.0), axis=0, keepdims=True)
        base_sel = jnp.sum(jnp.where(onehot, gs + blk_off - seg, 0.0), axis=0, keepdims=True)
        valid = c < nvalid
        dump = DUMP_PIECE0 + b * LOOP_PIECES + (c - ppg * jnp.floor(c * (1.0 / ppg)))
        dst = jnp.where(valid, base_sel + c, dump)
        row = slice(b, b + 1)
        ce_ref[row, :] = ce.astype(jnp.int32)
        cb_ref[row, :] = jnp.where(valid, (c - seg_sel) * PIECE, -float(1 << 20)).astype(jnp.int32)
        dst_ref[row, :] = dst.astype(jnp.int32)
        src_ref[row, :] = jnp.where(valid, dst, dst[:, 0:1]).astype(jnp.int32)
        ng_ref[row, :] = jnp.broadcast_to(_ceil_div(nvalid, LOOP_PIECES), (1, LANES)).astype(jnp.int32)
    j = lax.broadcasted_iota(jnp.int32, (1, TILE_TABLE_W), 1).astype(F32)
    start = j * ppt
    te = jnp.minimum(jnp.sum((gs_end <= start).astype(F32), axis=0, keepdims=True), N_EXPERTS - 1.0)
    onehot = eid == te
    tp_sel = jnp.sum(jnp.where(onehot, tp, 0.0), axis=0, keepdims=True)
    gs_sel = jnp.sum(jnp.where(onehot, gs, 0.0), axis=0, keepdims=True)
    n_used = gs_end[N_EXPERTS - 1:N_EXPERTS, :] * (1.0 / ppt)
    used = j < n_used
    tv = jnp.where(used, jnp.clip((tp_sel - (start - gs_sel)) * PIECE, 0.0, float(EXP_TILE)), 0.0)
    tin = jnp.where(used, j, n_used - 1.0)
    tout = jnp.where(used, j, float(N_EXP_TILES))
    tile_ref[...] = jnp.zeros(tile_ref.shape, jnp.int32)
    for r, v in enumerate((te, tv, tin, tout)):
        tile_ref[r:r + 1, :] = v.astype(jnp.int32)


def _dispatch_tables(cnt):
    nb = cnt.shape[0]
    tab = jax.ShapeDtypeStruct((nb, TABLE_W), jnp.int32)
    ce, cb, dst, src, ng, tile = pl.pallas_call(
        _moe_tables_kernel,
        out_shape=[tab, tab, tab, tab, jax.ShapeDtypeStruct((nb, LANES), jnp.int32),
                   jax.ShapeDtypeStruct((8, TILE_TABLE_W), jnp.int32)],
        name="moe_tables",
    )(cnt, cnt[:, :, 0].T)
    return {"ce": ce, "cb": cb, "dst": dst, "src": src, "ngroups": ng, "tile": tile}


def _piece_onehot(ce_ref, cb_ref, rank_s, b, g, extra=None):
    sub = lax.broadcasted_iota(jnp.int32, (PIECE, 1), 0).astype(F32)
    ps, ex = [], []
    for cc in range(GROUP_PIECES):
        c = g * GROUP_PIECES + cc
        e = ce_ref[b, c]
        base = cb_ref[b, c].astype(F32)
        hit = rank_s[pl.ds(e, 1), :] == base + sub
        ps.append(jnp.where(hit, 1.0, 0.0).astype(BF16))
        if extra is not None:
            ex.append(jnp.sum(jnp.where(hit, extra[pl.ds(e, 1), :], 0.0), axis=-1, keepdims=True))
    p = jnp.concatenate(ps, axis=0)
    return (p, jnp.concatenate(ex, axis=0)) if extra is not None else p


def _block_ranks(gt):
    n = gt.shape[1]
    ti = lax.broadcasted_iota(jnp.int32, (n, n), 0)
    tj = lax.broadcasted_iota(jnp.int32, (n, n), 1)
    before = (ti < tj).astype(BF16)
    member = gt > 0.0
    rank = _dot(member.astype(BF16), before)
    return jnp.where(member, rank, -1.0)


def _moe_sort_kernel(ce_ref, cb_ref, dst_ref, ng_ref, h_ref, gt_ref, xg_ref, stage, rank_s, gate_s, sem):
    b = pl.program_id(0)
    slot = b % 2
    gt = gt_ref[...]
    rank_s[...] = _block_ranks(gt)
    gate_s[...] = gt
    lane = lax.broadcasted_iota(jnp.int32, (1, LANES), 1)

    def piece_copy(blk, sl, c):
        r0 = pl.multiple_of(c * PIECE, PIECE)
        d0 = pl.multiple_of(dst_ref[blk, c] * PIECE, PIECE)
        return pltpu.make_async_copy(stage.at[sl, pl.ds(r0, PIECE)], xg_ref.at[pl.ds(d0, PIECE)], sem.at[sl])

    def groups(it, carry):
        gs = [it * LOOP_GROUPS + k for k in range(LOOP_GROUPS)]
        sel = [_piece_onehot(ce_ref, cb_ref, rank_s, b, g, gate_s) for g in gs]
        xs = [_dot(p, h_ref[...]).astype(BF16) for p, _ in sel]
        for g, x, (_, gcol) in zip(gs, xs, sel):
            hi, mid, lo = (t.astype(F32) for t in _split3(gcol))
            gblk = jnp.where(lane == 0, hi, jnp.where(lane == 1, mid, jnp.where(lane == 2, lo, 0.0)))
            r0 = pl.multiple_of(g * GROUP_ROWS, GROUP_ROWS)
            stage[slot, pl.ds(r0, GROUP_ROWS), :] = jnp.concatenate([x, gblk.astype(BF16)], axis=1)
        for cc in range(LOOP_PIECES):
            piece_copy(b, slot, it * LOOP_PIECES + cc).start()
        return carry

    lax.fori_loop(0, ng_ref[b, 0], groups, 0)

    def drain(blk, sl):
        def wait_some(it, carry):
            for cc in range(LOOP_PIECES):
                piece_copy(blk, sl, it * LOOP_PIECES + cc).wait()
            return carry
        lax.fori_loop(0, ng_ref[blk, 0], wait_some, 0)

    @pl.when(b > 0)
    def _():
        drain(b - 1, 1 - slot)

    @pl.when(b == pl.num_programs(0) - 1)
    def _():
        drain(b, slot)


def _moe_sort(h2, gates_t, tabs):
    nb = h2.shape[0] // MOE_BLOCK
    grid_spec = pltpu.PrefetchScalarGridSpec(
        num_scalar_prefetch=4, grid=(nb,),
        in_specs=[pl.BlockSpec((MOE_BLOCK, D_MODEL), lambda b, *_: (b, 0)),
                  pl.BlockSpec((N_EXPERTS, MOE_BLOCK), lambda b, *_: (0, b))],
        out_specs=pl.BlockSpec(memory_space=pl.ANY),
        scratch_shapes=[pltpu.VMEM((2, STAGE_PIECES * PIECE, DISPATCH_W), BF16),
                        pltpu.VMEM((N_EXPERTS, MOE_BLOCK), F32), pltpu.VMEM((N_EXPERTS, MOE_BLOCK), F32),
                        pltpu.SemaphoreType.DMA((2,))])
    return pl.pallas_call(
        _moe_sort_kernel,
        out_shape=jax.ShapeDtypeStruct((DISPATCH_ROWS, DISPATCH_W), BF16),
        grid_spec=grid_spec,
        compiler_params=_cparams(("arbitrary",)),
        name="moe_sort",
    )(tabs["ce"], tabs["cb"], tabs["dst"], tabs["ngroups"], h2, gates_t)


def _moe_expert_kernel(tile_ref, x_ref, wg_ref, wu_ref, wd_ref, y_ref):
    valid = tile_ref[1, pl.program_id(0)]
    wg = wg_ref[0].astype(BF16)
    wu = wu_ref[0].astype(BF16)
    wd = wd_ref[0].astype(BF16)
    for r0 in range(0, EXP_TILE, EXP_SUB):
        @pl.when(valid > r0)
        def _():
            rows = slice(r0, r0 + EXP_SUB)
            keep = lax.broadcasted_iota(jnp.int32, (EXP_SUB, 1), 0) < valid - r0
            xrow = x_ref[rows, :]
            x = jnp.where(keep, xrow[:, :D_MODEL], jnp.zeros((), BF16))
            g = jnp.sum(jnp.where(keep, xrow[:, D_MODEL:].astype(F32), 0.0), axis=-1, keepdims=True)
            hg = _dot(x, wg)
            hu = _dot(x, wu)
            act = (_silu(hg) * hu * g).astype(BF16)
            y_ref[rows, :] = _dot(act, wd).astype(BF16)


def _moe_expert(xg, tabs, wts):
    grid_spec = pltpu.PrefetchScalarGridSpec(
        num_scalar_prefetch=1, grid=(N_EXP_TILES,),
        in_specs=[pl.BlockSpec((EXP_TILE, DISPATCH_W), lambda j, tt: (tt[2, j], 0)),
                  pl.BlockSpec((1, D_MODEL, D_EXPERT), lambda j, tt: (tt[0, j], 0, 0)),
                  pl.BlockSpec((1, D_MODEL, D_EXPERT), lambda j, tt: (tt[0, j], 0, 0)),
                  pl.BlockSpec((1, D_EXPERT, D_MODEL), lambda j, tt: (tt[0, j], 0, 0))],
        out_specs=pl.BlockSpec((EXP_TILE, D_MODEL), lambda j, tt: (tt[3, j], 0)))
    return pl.pallas_call(
        _moe_expert_kernel,
        out_shape=jax.ShapeDtypeStruct((DISPATCH_ROWS, D_MODEL), BF16),
        grid_spec=grid_spec,
        compiler_params=_cparams(("arbitrary",)),
        name="moe_expert",
    )(tabs["tile"], xg, wts["w_gate"], wts["w_up"], wts["w_down"])


def _moe_combine_kernel(ctx_blocks, ce_ref, cb_ref, src_ref, ng_ref, yg_ref, gt_ref, h_ref, x1_ref, m_ref,
                        gpost_ref, wsg_ref, wsu_ref, wsd_ref, outp_ref, outs_ref, stage, rank_s, acc_s, sem):
    b = pl.program_id(0)
    slot = b % 2

    def piece_copy(blk, sl, c):
        r0 = pl.multiple_of(c * PIECE, PIECE)
        s0 = pl.multiple_of(src_ref[blk, c] * PIECE, PIECE)
        return pltpu.make_async_copy(yg_ref.at[pl.ds(s0, PIECE)], stage.at[sl, pl.ds(r0, PIECE)], sem.at[sl])

    def start_pieces(blk, sl, it):
        for cc in range(LOOP_PIECES):
            piece_copy(blk, sl, it * LOOP_PIECES + cc).start()

    def fetch(blk, sl):
        def start_some(it, carry):
            start_pieces(blk, sl, it)
            return carry
        lax.fori_loop(0, ng_ref[blk, 0], start_some, 0)

    ng_cur = ng_ref[b, 0]

    @pl.when(b == 0)
    def _():
        fetch(0, 0)

    @pl.when(b + 1 < pl.num_programs(0))
    def _():
        fetch(b + 1, 1 - slot)

    rank_s[...] = _block_ranks(gt_ref[...])
    h = h_ref[...]
    sh = (_silu(_dot(h, wsg_ref[...])) * _dot(h, wsu_ref[...])).astype(BF16)
    acc_s[...] = _dot(sh, wsd_ref[...])

    def wait_some(it, carry):
        for cc in range(LOOP_PIECES):
            piece_copy(b, slot, it * LOOP_PIECES + cc).wait()
        return carry

    lax.fori_loop(0, ng_cur, wait_some, 0)

    def groups(it, carry):
        gs = [it * LOOP_GROUPS + k for k in range(LOOP_GROUPS)]
        ps = [_piece_onehot(ce_ref, cb_ref, rank_s, b, g) for g in gs]
        ys = [stage[slot, pl.ds(pl.multiple_of(g * GROUP_ROWS, GROUP_ROWS), GROUP_ROWS), :] for g in gs]
        acc_s[...] += functools.reduce(jnp.add, [_dot_tn(p, y) for p, y in zip(ps, ys)])
        return carry

    lax.fori_loop(0, ng_cur, groups, 0)
    m = m_ref[0]
    y = x1_ref[...] + m[5:6] * _rms(acc_s[...], gpost_ref[...])

    @pl.when(b < ctx_blocks)
    def _():
        outp_ref[...] = y

    @pl.when(b >= ctx_blocks)
    def _():
        outs_ref[...] = y


def _moe_combine(yg, gates_t, h2, x1, mods, mod_index, tabs, wts, ctx_tokens):
    t = h2.shape[0]
    nb = t // MOE_BLOCK
    ctx_blocks = ctx_tokens // MOE_BLOCK

    def full(a):
        return pl.BlockSpec(a.shape, lambda b, *_: (0,) * a.ndim)

    grid_spec = pltpu.PrefetchScalarGridSpec(
        num_scalar_prefetch=4, grid=(nb,),
        in_specs=[pl.BlockSpec(memory_space=pl.ANY),
                  pl.BlockSpec((N_EXPERTS, MOE_BLOCK), lambda b, *_: (0, b)),
                  pl.BlockSpec((MOE_BLOCK, D_MODEL), lambda b, *_: (b, 0)),
                  pl.BlockSpec((MOE_BLOCK, D_MODEL), lambda b, *_: (b, 0)),
                  pl.BlockSpec((1, 6, D_MODEL), lambda b, *_: (mod_index(b), 0, 0)),
                  full(wts["gpost2"]), full(wts["wsg"]), full(wts["wsu"]), full(wts["wsd"])],
        out_specs=[pl.BlockSpec((MOE_BLOCK, D_MODEL), lambda b, *_: (jnp.minimum(b, ctx_blocks - 1), 0)),
                   pl.BlockSpec((MOE_BLOCK, D_MODEL), lambda b, *_: (jnp.maximum(b - ctx_blocks, 0), 0))],
        scratch_shapes=[pltpu.VMEM((2, STAGE_PIECES * PIECE, D_MODEL), BF16),
                        pltpu.VMEM((N_EXPERTS, MOE_BLOCK), F32), pltpu.VMEM((MOE_BLOCK, D_MODEL), F32),
                        pltpu.SemaphoreType.DMA((2,))])
    return pl.pallas_call(
        functools.partial(_moe_combine_kernel, ctx_blocks),
        out_shape=[jax.ShapeDtypeStruct((ctx_tokens, D_MODEL), F32),
                   jax.ShapeDtypeStruct((t - ctx_tokens, D_MODEL), F32)],
        grid_spec=grid_spec,
        compiler_params=_cparams(("arbitrary",)),
        name="moe_combine",
    )(tabs["ce"], tabs["cb"], tabs["src"], tabs["ngroups"], yg, gates_t, h2, x1, mods,
      wts["gpost2"], wts["wsg"], wts["wsu"], wts["wsd"])


def _moe(h2, gates_t, cnt, x1, mods, mod_index, wts, ctx_tokens):
    tabs = _dispatch_tables(cnt)
    xg = _moe_sort(h2, gates_t, tabs)
    yg = _moe_expert(xg, tabs, wts)
    return _moe_combine(yg, gates_t, h2, x1, mods, mod_index, tabs, wts, ctx_tokens)


def _rope_swap(w):
    nf = QK_ROPE // 4
    parts = [w[..., i * nf:(i + 1) * nf] for i in range(4)]
    return jnp.concatenate([parts[1], parts[0], parts[3], parts[2]], axis=-1)


def _head_block(nope, rope):
    lead = nope.shape[:-2] if nope is not None else rope.shape[:-2]
    nope = jnp.zeros(lead + (MLA_HEADS, QK_NOPE), F32) if nope is None else nope
    rope = jnp.zeros(lead + (MLA_HEADS, QK_ROPE), F32) if rope is None else rope
    pad = jnp.zeros(lead + (MLA_HEADS, HEAD_PAD - MLA_QK), F32)
    return jnp.concatenate([nope, rope, pad], axis=-1).reshape(lead + (MLA_HEADS * HEAD_PAD,))


def _rope_block(w):
    lead = w.shape[:-1]
    return jnp.concatenate([jnp.zeros(lead + (QK_NOPE,), F32), w,
                            jnp.zeros(lead + (HEAD_PAD - MLA_QK,), F32)], axis=-1)


def _rope_tables(n_tokens):
    rows = n_tokens // GRID_W
    row = np.repeat(np.arange(rows, dtype=np.float64), GRID_W)
    colv = np.tile(np.arange(GRID_W, dtype=np.float64), rows)
    nf = QK_ROPE // 4
    inv = ROPE_THETA ** (-np.arange(nf, dtype=np.float64) / nf)
    ang_r = row[:, None] * inv
    ang_c = colv[:, None] * inv
    cos32 = np.concatenate([np.cos(ang_r), np.cos(ang_r), np.cos(ang_c), np.cos(ang_c)], axis=-1)
    sin32 = np.concatenate([-np.sin(ang_r), np.sin(ang_r), -np.sin(ang_c), np.sin(ang_c)], axis=-1)
    ones = np.ones((n_tokens, QK_NOPE))
    tail = np.zeros((n_tokens, HEAD_PAD - MLA_QK))
    cos = np.concatenate([ones, cos32, tail], axis=-1)
    sin = np.concatenate([np.zeros((n_tokens, QK_NOPE)), sin32, tail], axis=-1)
    return jnp.asarray(cos, F32), jnp.asarray(sin, F32)


def _expand_matrix():
    e = np.zeros((N_PAIRS, LANES, 4 * LANES), np.float32)
    for p in range(N_PAIRS):
        for blk in range(4):
            for hh in range(2):
                src = blk * GDN_HEADS + 2 * p + hh
                e[p, src, blk * LANES + hh * GDN_DK: blk * LANES + (hh + 1) * GDN_DK] = 1.0
    return jnp.asarray(e)


def _prepare_weights(w_in, q_norm_g, kv_norm_g, w_uq, w_ukv, w_oa, w_ob, w_o, g_post_mix, g_pre_ffn,
                     g_post_ffn, gdn_norm_g, w_router, e_bias, w_gate, w_up, w_down, ws_gate, ws_up, ws_down):
    offs = np.cumsum((Q_LORA, KV_LORA, QK_ROPE, 3 * GDN_W, GDN_W, 2 * GDN_HEADS, 2 * GDN_HEADS,
                      D_MODEL, D_MODEL))[:-1].tolist()
    cq, ckv, kr, qkv, z, a, b, ga, gb = jnp.split(w_in, offs, axis=-1)
    ab = jnp.concatenate([a, b, jnp.zeros((D_MODEL, LANES - 4 * GDN_HEADS), F32)], axis=-1)
    small = jnp.concatenate([_rope_block(kr), _rope_block(_rope_swap(kr)), ab], axis=-1)
    lo0, mid0, hi0 = 0, offs[2], offs[6]
    uq = w_uq.reshape(Q_LORA, MLA_HEADS, MLA_QK)
    ukv = w_ukv.reshape(KV_LORA, MLA_HEADS, QK_NOPE + V_HEAD)
    return {
        "w_lo": w_in[:, lo0:lo0 + Q_LORA + KV_LORA].astype(BF16),
        "w_small": small.astype(BF16),
        "w_mid": w_in[:, mid0:mid0 + 4 * GDN_W].astype(BF16),
        "w_hi": w_in[:, hi0:hi0 + 2 * D_MODEL].astype(BF16),
        "qg": q_norm_g.reshape(1, Q_LORA), "kvg": kv_norm_g.reshape(1, KV_LORA),
        "wuq": _head_block(uq[..., :QK_NOPE], uq[..., QK_NOPE:]).astype(BF16),
        "wuqs": _head_block(None, _rope_swap(uq[..., QK_NOPE:])).astype(BF16),
        "wuk": _head_block(ukv[..., :QK_NOPE], None).astype(BF16),
        "wuv": ukv[..., QK_NOPE:].reshape(KV_LORA, MLA_HEADS * V_HEAD).astype(BF16),
        "woa": w_oa.astype(BF16), "wob": w_ob.astype(BF16), "wo": w_o.astype(BF16),
        "gpost": g_post_mix.reshape(1, D_MODEL), "gpre2": g_pre_ffn.reshape(1, D_MODEL),
        "gpost2": g_post_ffn.reshape(1, D_MODEL),
        "gdng": jnp.tile(gdn_norm_g.reshape(1, GDN_DV), (1, 2)),
        "wr_t": w_router.T, "eb": e_bias.reshape(N_EXPERTS, 1),
        "w_gate": w_gate, "w_up": w_up, "w_down": w_down,
        "wsg": ws_gate.astype(BF16), "wsu": ws_up.astype(BF16), "wsd": ws_down.astype(BF16),
    }


def _pad_lanes(v):
    v = v.reshape(1, -1)
    return jnp.concatenate([v, jnp.zeros((1, LANES - v.shape[1]), F32)], axis=-1)


def _layer_group(x, n_seq, seq_len, mods, mod_index, wts, gpre, conv_w, alog128, dtb128,
                 expand, rope_tabs, extra_kv, s0, tile0, into):
    pr = _proj(x, mods, mod_index, gpre, wts, rope_tabs)
    kvs = list(extra_kv) + [(pr["k"], pr["v"], seq_len)]
    omla = _attention(pr["q"], kvs, n_seq, seq_len, "attn_%d" % seq_len)
    prep = _gdn_prep(pr["qkv"], pr["ab"], conv_w, alog128, dtb128, expand, n_seq, seq_len)
    o_f, o_b, s_fin = _gdn_scan(prep, s0, n_seq, seq_len)
    t = n_seq * seq_len
    merged = _merge(x, mods, lambda i: mod_index(i * (MERGE_TILE // ROW_TILE)), omla,
                    o_f.reshape(t, GDN_W), o_b.reshape(t, GDN_W), pr, wts, tile0, N_TOKENS, into)
    return merged, pr, s_fin


def _state_to_pairs(s):
    b = s.shape[0]
    s = s.reshape(b, 2, N_PAIRS, 2, GDN_DK, GDN_DV)
    return jnp.transpose(s, (0, 1, 2, 4, 3, 5)).reshape(b, 2, N_PAIRS, GDN_DK, 2 * GDN_DV)


def kernel(x_prompt, x_sample, cache_ckv, cache_krope, state_delta, c, c_ctx, w_ada, b_ada, g_pre_mix,
           g_post_mix, g_pre_ffn, g_post_ffn, w_in, q_norm_g, kv_norm_g, w_uq, w_ukv, conv_w, a_log,
           dt_bias, gdn_norm_g, w_oa, w_ob, w_o, w_router, e_bias, w_gate, w_up, w_down, ws_gate, ws_up,
           ws_down):
    batch, seq, _ = x_prompt.shape
    dec_batch, dec_seq, _ = x_sample.shape
    past = cache_ckv.shape[2]
    assert batch * seq + dec_batch * dec_seq == N_TOKENS, "dispatch buffers are sized for N_TOKENS"
    y_p = x_prompt.reshape(batch * seq, D_MODEL)
    y_s = x_sample.reshape(dec_batch * dec_seq, D_MODEL)
    expand = _expand_matrix()
    rope_tabs = _rope_tables(dec_seq)
    cond8 = jnp.concatenate([c_ctx[None], c, jnp.zeros((8 - 1 - dec_batch, D_MODEL), F32)], axis=0)
    ckv_out, krope_out, state_out = [], [], []
    for l in range(DEPTH):
        wts = _prepare_weights(w_in[l], q_norm_g[l], kv_norm_g[l], w_uq[l], w_ukv[l], w_oa[l], w_ob[l],
                               w_o[l], g_post_mix[l], g_pre_ffn[l], g_post_ffn[l], gdn_norm_g[l],
                               w_router[l], e_bias[l], w_gate[l], w_up[l], w_down[l], ws_gate[l],
                               ws_up[l], ws_down[l])
        gpre = g_pre_mix[l].reshape(1, D_MODEL)
        alog128 = _pad_lanes(a_log[l])
        dtb128 = _pad_lanes(dt_bias[l])
        mods = _mods(cond8, w_ada[l], b_ada[l].reshape(1, -1)).reshape(8, 6, D_MODEL)

        zero_state = jnp.zeros((batch, 2, N_PAIRS, GDN_DK, LANES), F32)
        merged_p, pr_p, s_fin = _layer_group(
            y_p, batch, seq, mods, lambda i: 0, wts, gpre, conv_w[l], alog128, dtb128,
            expand, None, [], zero_state, 0, None)
        ckv_out.append(pr_p["ckv"].reshape(batch, seq, KV_LORA))
        krope_out.append(pr_p["kr"][:, QK_NOPE:MLA_QK].reshape(batch, seq, QK_ROPE))
        state_out.append(s_fin)

        kr_ctx = _rope_block(cache_krope[:, l].reshape(dec_batch * past, QK_ROPE))
        k_ctx, v_ctx = _cache_kv(cache_ckv[:, l].reshape(dec_batch * past, KV_LORA), kr_ctx, wts)
        tiles_per_seq = dec_seq // ROW_TILE
        merged_s, _, _ = _layer_group(
            y_s, dec_batch, dec_seq, mods, lambda i: 1 + i // tiles_per_seq,
            wts, gpre, conv_w[l], alog128, dtb128, expand,
            rope_tabs, [(k_ctx, v_ctx, past)], _state_to_pairs(state_delta[:, l]),
            batch * seq // MERGE_TILE, merged_p)

        x1, h2, gates_t, cnt = merged_s
        ctx_blocks = batch * seq // MOE_BLOCK
        blocks_per_seq = dec_seq // MOE_BLOCK
        y_p, y_s = _moe(h2, gates_t, cnt, x1, mods,
                        lambda b: jnp.where(b < ctx_blocks, 0, 1 + (b - ctx_blocks) // blocks_per_seq),
                        wts, batch * seq)
    new_ckv = jnp.stack(ckv_out, axis=1)
    new_krope = jnp.stack(krope_out, axis=1)
    new_state = jnp.stack(state_out, axis=1)
    return (y_p.reshape(batch, seq, D_MODEL), y_s.reshape(dec_batch, dec_seq, D_MODEL),
            new_ckv, new_krope, new_state)
```

```python
import functools

import numpy as np
import jax
import jax.numpy as jnp
from jax import lax
from jax.experimental import pallas as pl
from jax.experimental.pallas import tpu as pltpu

F32 = jnp.float32
BF16 = jnp.bfloat16

D_MODEL = 1024
DEPTH = 1
GRID_W = 64
MLA_HEADS = 8
QK_NOPE = 64
QK_ROPE = 32
V_HEAD = 64
Q_LORA = 256
KV_LORA = 256
ROPE_THETA = 10000.0
GDN_HEADS = 8
GDN_DK = 64
GDN_DV = 64
CONV_K = 5
CHUNK = 64
N_EXPERTS = 64
TOP_K = 8
N_GROUPS = 8
TOPK_GROUPS = 4
D_EXPERT = 256
D_SHARED = 256
ROUTED_SCALE = 2.5
EPS = 1e-6

LANES = 128
LOG2_E = 1.4426950408889634
MLA_QK = QK_NOPE + QK_ROPE
GDN_W = GDN_HEADS * GDN_DK
N_PAIRS = GDN_HEADS // 2
HEAD_PAD = LANES

_SEG = {"cq": ("w_lo", 0, Q_LORA), "ckv": ("w_lo", Q_LORA, Q_LORA + KV_LORA),
        "kr": ("w_small", 0, LANES), "krs": ("w_small", LANES, 2 * LANES), "ab": ("w_small", 2 * LANES, 3 * LANES),
        "qkv": ("w_mid", 0, 3 * GDN_W), "z": ("w_mid", 3 * GDN_W, 4 * GDN_W),
        "ga": ("w_hi", 0, D_MODEL), "gb": ("w_hi", D_MODEL, 2 * D_MODEL)}
_PROJ_W = ("w_lo", "w_small", "w_mid", "w_hi")

ROW_TILE = 512
ATTN_TILE = 512
N_TOKENS = 8192
MOE_BLOCK = 256
MERGE_TILE = 512
PIECE = 16
GROUP_ROWS = 256
GROUP_PIECES = GROUP_ROWS // PIECE
LOOP_GROUPS = 2
LOOP_PIECES = LOOP_GROUPS * GROUP_PIECES
EXP_TILE = 1024
TILE_PIECES = EXP_TILE // PIECE
N_BLOCKS = N_TOKENS // MOE_BLOCK
STAGE_PIECES = -(-((MOE_BLOCK * TOP_K + N_EXPERTS * (PIECE - 1)) // PIECE) // LOOP_PIECES) * LOOP_PIECES
N_EXP_TILES = -(-((N_TOKENS * TOP_K + N_BLOCKS * N_EXPERTS * (PIECE - 1)) // PIECE
                  + N_EXPERTS * (TILE_PIECES - 1)) // TILE_PIECES)
DUMP_PIECE0 = N_EXP_TILES * TILE_PIECES
DISPATCH_ROWS = (DUMP_PIECE0 + max(N_BLOCKS * LOOP_PIECES, TILE_PIECES)) * PIECE
DISPATCH_W = D_MODEL + LANES
PREP_GROUP = 8
PREP_ROWS = 1024
SCAN_SEQS = 4
SCAN_CHUNKS = 4
VMEM_LIMIT = 56 * 1024 * 1024


def _dot(a, b):
    return jnp.dot(a, b, preferred_element_type=F32)


def _dot_nt(a, b):
    return lax.dot_general(a, b, (((1,), (1,)), ((), ())), preferred_element_type=F32)


def _dot_tn(a, b):
    return lax.dot_general(a, b, (((0,), (0,)), ((), ())), preferred_element_type=F32)


def _rms(x, g):
    return x * lax.rsqrt(jnp.mean(x * x, axis=-1, keepdims=True) + EPS) * g


def _silu(x):
    return x * jax.nn.sigmoid(x)


def _cparams(sem):
    return pltpu.CompilerParams(dimension_semantics=sem, vmem_limit_bytes=VMEM_LIMIT)


def _mods_kernel(c_ref, w_ref, b_ref, o_ref):
    s = _silu(c_ref[...]).astype(BF16)
    o_ref[...] = _dot(s, w_ref[...].astype(BF16)) + b_ref[...]


def _mods(cond8, w_ada, b_ada):
    n = w_ada.shape[1]
    bn = 1024
    return pl.pallas_call(
        _mods_kernel,
        out_shape=jax.ShapeDtypeStruct((8, n), F32),
        grid=(n // bn,),
        in_specs=[pl.BlockSpec((8, D_MODEL), lambda j: (0, 0)),
                  pl.BlockSpec((D_MODEL, bn), lambda j: (0, j)),
                  pl.BlockSpec((1, bn), lambda j: (0, j))],
        out_specs=pl.BlockSpec((8, bn), lambda j: (0, j)),
        compiler_params=_cparams(("parallel",)),
        name="mods",
    )(cond8, w_ada, b_ada)


def _proj_kernel(rope, x_ref, m_ref, gpre_ref, wlo_ref, wsmall_ref, wmid_ref, whi_ref, qg_ref, kvg_ref,
                 wuq_ref, wuqs_ref, wuk_ref, wuv_ref, cos_ref, sin_ref,
                 q_ref, k_ref, v_ref, ckv_ref, kr_ref, qkv_ref, z_ref, ab_ref, ga_ref, gb_ref):
    m = m_ref[0]
    h = (_rms(x_ref[...], gpre_ref[...]) * (1.0 + m[1:2]) + m[0:1]).astype(BF16)
    w_refs = dict(zip(_PROJ_W, (wlo_ref, wsmall_ref, wmid_ref, whi_ref)))

    def seg(name):
        op, a, b = _SEG[name]
        return _dot(h, w_refs[op][:, a:b])

    qkv_ref[...] = seg("qkv")
    z_ref[...] = seg("z")
    ab_ref[...] = seg("ab")
    ga_ref[...] = seg("ga")
    gb_ref[...] = seg("gb")

    qn = _rms(seg("cq"), qg_ref[...]).astype(BF16)
    ckv = _rms(seg("ckv"), kvg_ref[...])
    ckv_ref[...] = ckv
    ckv_b = ckv.astype(BF16)
    kr = seg("kr")
    kr_ref[...] = kr
    qm = _dot(qn, wuq_ref[...])
    kk = _dot(ckv_b, wuk_ref[...])
    v_ref[...] = _dot(ckv_b, wuv_ref[...]).astype(BF16)
    scale = MLA_QK ** -0.5 * LOG2_E
    if rope:
        cos = cos_ref[...]
        sin = sin_ref[...]
        qs = _dot(qn, wuqs_ref[...])
        kr = kr * cos + seg("krs") * sin
    for hd in range(MLA_HEADS):
        sl = slice(hd * HEAD_PAD, (hd + 1) * HEAD_PAD)
        qh = qm[:, sl]
        if rope:
            qh = qh * cos + qs[:, sl] * sin
        q_ref[:, sl] = (qh * scale).astype(BF16)
        k_ref[:, sl] = (kk[:, sl] + kr).astype(BF16)


def _proj(x, mods, mod_index, gpre, wts, rope_tabs):
    t = x.shape[0]
    tm = ROW_TILE
    rope = rope_tabs is not None
    if rope:
        cos, sin = rope_tabs
        n_rope_blocks = cos.shape[0] // tm
        rope_spec = pl.BlockSpec((tm, LANES), lambda i: (i % n_rope_blocks, 0))
    else:
        cos = sin = jnp.zeros((8, LANES), F32)
        rope_spec = pl.BlockSpec((8, LANES), lambda i: (0, 0))

    def full(a):
        return pl.BlockSpec(a.shape, lambda i: (0,) * a.ndim)

    def rows(w):
        return pl.BlockSpec((tm, w), lambda i: (i, 0))

    out_widths = (("q", MLA_HEADS * HEAD_PAD, BF16), ("k", MLA_HEADS * HEAD_PAD, BF16),
                  ("v", MLA_HEADS * V_HEAD, BF16), ("ckv", KV_LORA, F32), ("kr", LANES, F32),
                  ("qkv", 3 * GDN_W, F32), ("z", GDN_W, F32), ("ab", LANES, F32),
                  ("ga", D_MODEL, F32), ("gb", D_MODEL, F32))
    outs = pl.pallas_call(
        functools.partial(_proj_kernel, rope),
        out_shape=[jax.ShapeDtypeStruct((t, w), dt) for _, w, dt in out_widths],
        grid=(t // tm,),
        in_specs=[rows(D_MODEL),
                  pl.BlockSpec((1, 6, D_MODEL), lambda i: (mod_index(i), 0, 0)),
                  full(gpre)] + [full(wts[n]) for n in _PROJ_W] + [full(wts["qg"]), full(wts["kvg"]),
                  full(wts["wuq"]), full(wts["wuqs"]), full(wts["wuk"]), full(wts["wuv"]),
                  rope_spec, rope_spec],
        out_specs=[rows(w) for _, w, _ in out_widths],
        compiler_params=_cparams(("parallel",)),
        name="proj_rope" if rope else "proj",
    )(x, mods, gpre, *[wts[n] for n in _PROJ_W], wts["qg"], wts["kvg"], wts["wuq"], wts["wuqs"],
      wts["wuk"], wts["wuv"], cos, sin)
    return {name: o for (name, _, _), o in zip(out_widths, outs)}


def _cache_kv_kernel(ckv_ref, kr_ref, wuk_ref, wuv_ref, k_ref, v_ref):
    c = ckv_ref[...].astype(BF16)
    kk = _dot(c, wuk_ref[...])
    v_ref[...] = _dot(c, wuv_ref[...]).astype(BF16)
    kr = kr_ref[...]
    for hd in range(MLA_HEADS):
        sl = slice(hd * HEAD_PAD, (hd + 1) * HEAD_PAD)
        k_ref[:, sl] = (kk[:, sl] + kr).astype(BF16)


def _cache_kv(ckv, kr128, wts):
    t = ckv.shape[0]
    tm = 512
    return pl.pallas_call(
        _cache_kv_kernel,
        out_shape=[jax.ShapeDtypeStruct((t, MLA_HEADS * HEAD_PAD), BF16),
                   jax.ShapeDtypeStruct((t, MLA_HEADS * V_HEAD), BF16)],
        grid=(t // tm,),
        in_specs=[pl.BlockSpec((tm, KV_LORA), lambda i: (i, 0)),
                  pl.BlockSpec((tm, LANES), lambda i: (i, 0)),
                  pl.BlockSpec(wts["wuk"].shape, lambda i: (0, 0)),
                  pl.BlockSpec(wts["wuv"].shape, lambda i: (0, 0))],
        out_specs=[pl.BlockSpec((tm, MLA_HEADS * HEAD_PAD), lambda i: (i, 0)),
                   pl.BlockSpec((tm, MLA_HEADS * V_HEAD), lambda i: (i, 0))],
        compiler_params=_cparams(("parallel",)),
        name="cache_kv",
    )(ckv, kr128, wts["wuk"], wts["wuv"])


def _attn_kernel(n_kv, q_ref, *refs):
    k_refs = refs[:n_kv]
    v_refs = refs[n_kv:2 * n_kv]
    o_ref = refs[2 * n_kv]
    lane = lax.broadcasted_iota(jnp.int32, (1, LANES), 1)
    low = lane < V_HEAD
    for pr in range(MLA_HEADS // 2):
        halves = []
        for hd in (2 * pr, 2 * pr + 1):
            sl = slice(hd * HEAD_PAD, (hd + 1) * HEAD_PAD)
            qh = q_ref[:, sl]
            scores = [_dot_nt(qh, kr[:, sl]) for kr in k_refs]
            mx = functools.reduce(jnp.maximum, [jnp.max(s, axis=-1, keepdims=True) for s in scores])
            ps = [jnp.exp2(s - mx) for s in scores]
            den = functools.reduce(jnp.add, [jnp.sum(p, axis=-1, keepdims=True) for p in ps])
            vsl = slice(pr * LANES, (pr + 1) * LANES)
            acc = functools.reduce(jnp.add, [_dot(p.astype(BF16), vr[:, vsl]) for p, vr in zip(ps, v_refs)])
            halves.append(acc / den)
        o_ref[:, pr * LANES:(pr + 1) * LANES] = jnp.where(low, halves[0], halves[1]).astype(BF16)


def _attention(q, kvs, n_seq, seq_len, name):
    tq = min(seq_len, ATTN_TILE)
    nq = seq_len // tq
    n_kv = len(kvs)
    in_specs = [pl.BlockSpec((tq, MLA_HEADS * HEAD_PAD), lambda b, j: (b * nq + j, 0))]
    in_specs += [pl.BlockSpec((rows, MLA_HEADS * HEAD_PAD), lambda b, j: (b, 0)) for _, _, rows in kvs]
    in_specs += [pl.BlockSpec((rows, MLA_HEADS * V_HEAD), lambda b, j: (b, 0)) for _, _, rows in kvs]
    return pl.pallas_call(
        functools.partial(_attn_kernel, n_kv),
        out_shape=jax.ShapeDtypeStruct((n_seq * seq_len, MLA_HEADS * V_HEAD), BF16),
        grid=(n_seq, nq),
        in_specs=in_specs,
        out_specs=pl.BlockSpec((tq, MLA_HEADS * V_HEAD), lambda b, j: (b * nq + j, 0)),
        compiler_params=_cparams(("parallel", "parallel")),
        name=name,
    )(q, *[k for k, _, _ in kvs], *[v for _, v, _ in kvs])


def _pair_masks():
    lane = lax.broadcasted_iota(jnp.int32, (1, LANES), 1)
    return lane < GDN_DK


def _stack(x, low):
    zero = jnp.zeros_like(x)
    return jnp.concatenate([jnp.where(low, x, zero), jnp.where(low, zero, x)], axis=0)


def _split3(x):
    hi = x.astype(BF16)
    r = x - hi.astype(F32)
    mid = r.astype(BF16)
    lo = (r - mid.astype(F32)).astype(BF16)
    return hi, mid, lo


def _gdn_prep_kernel(seq_len, q_ref, k_ref, v_ref, cwq_ref, cwk_ref, cwv_ref, ab_ref, alog_ref, dtb_ref, e_ref,
                     uf_ref, ub_ref, wf_ref, wb_ref, af_ref, abk_ref, qdf_ref, qdb_ref, kdf_ref, kdb_ref,
                     glf_ref, glb_ref,
                     qn_s, kn_s, vn_s, gcb_s, gf_s):
    seq = q_ref.shape[0]
    n_chunks = seq // CHUNK
    low = _pair_masks()
    row = lax.broadcasted_iota(jnp.int32, (seq, 1), 0) % seq_len
    lane = lax.broadcasted_iota(jnp.int32, (1, LANES), 1)

    def conv(x_ref, cw_ref):
        x = x_ref[...]
        acc = jnp.zeros_like(x)
        for j in range(CONV_K):
            sh = CONV_K // 2 - j
            xs = x if sh == 0 else pltpu.roll(x, sh % seq, axis=0)
            src = row - sh
            valid = (src >= 0) & (src < seq_len)
            acc = acc + jnp.where(valid, xs, 0.0) * cw_ref[j:j + 1, :]
        return _silu(acc)

    def l2n(x):
        sq = x * x
        s0 = jnp.sum(jnp.where(low, sq, 0.0), axis=-1, keepdims=True)
        s1 = jnp.sum(jnp.where(low, 0.0, sq), axis=-1, keepdims=True)
        return x * lax.rsqrt(jnp.where(low, s0, s1) + EPS)

    qn_s[...] = l2n(conv(q_ref, cwq_ref)) * (GDN_DK ** -0.5)
    kn_s[...] = l2n(conv(k_ref, cwk_ref))
    vn_s[...] = conv(v_ref, cwv_ref)

    a = ab_ref[...]
    xg = a + dtb_ref[...]
    softplus = jnp.maximum(xg, 0.0) + jnp.log(1.0 + jnp.exp(-jnp.abs(xg)))
    act = jnp.where(lane < 2 * GDN_HEADS, -jnp.exp(alog_ref[...]) * softplus, jax.nn.sigmoid(a))

    ti = lax.broadcasted_iota(jnp.int32, (CHUNK, CHUNK), 0)
    tj = lax.broadcasted_iota(jnp.int32, (CHUNK, CHUNK), 1)
    tri_lo = (tj <= ti).astype(BF16)
    tri_up = (tj >= ti).astype(BF16)
    for c in range(n_chunks):
        ac = act[c * CHUNK:(c + 1) * CHUNK]
        pieces = _split3(ac)
        lo = functools.reduce(jnp.add, [_dot(tri_lo, pc) for pc in pieces])
        up = functools.reduce(jnp.add, [_dot(tri_up, pc) for pc in pieces])
        gcb_s[c * CHUNK:(c + 1) * CHUNK, :] = jnp.where(lane < GDN_HEADS, lo,
                                                        jnp.where(lane < 2 * GDN_HEADS, up, ac))
    expand = e_ref[0].astype(BF16)
    gf_s[...] = functools.reduce(jnp.add, [_dot(pc, expand) for pc in _split3(gcb_s[...])])

    ri = lax.broadcasted_iota(jnp.int32, (CHUNK, LANES), 0)
    cj = lax.broadcasted_iota(jnp.int32, (CHUNK, LANES), 1) % CHUNK
    eye = (ri == cj).astype(F32)

    def pmm(x, y):
        return _dot(x.astype(BF16), _stack(y, low).astype(BF16))

    def row_form(g):
        gt = jnp.concatenate([g, jnp.zeros_like(g)], axis=0).T
        r0 = jnp.broadcast_to(gt[0:1, :], (CHUNK, LANES))
        r1 = jnp.broadcast_to(gt[GDN_DK:GDN_DK + 1, :], (CHUNK, LANES))
        return jnp.where(low, r0, pltpu.roll(r1, GDN_DK, axis=1))

    out_refs = ((uf_ref, wf_ref, af_ref, qdf_ref, kdf_ref, glf_ref),
                (ub_ref, wb_ref, abk_ref, qdb_ref, kdb_ref, glb_ref))
    incl = (ri >= cj, ri <= cj)
    strict = (ri > cj, ri < cj)
    diag8 = (ri // 8) == (cj // 8)
    merge_masks = [((ri // (2 * s)) == (cj // (2 * s))) & ((ri // s) != (cj // s)) for s in (8, 16, 32)]

    def group(it, carry):
        cs = [it * PREP_GROUP + cc for cc in range(PREP_GROUP)]
        rows = [pl.ds(pl.multiple_of(c * CHUNK, CHUNK), CHUNK) for c in cs]
        qc = [qn_s[r, :] for r in rows]
        kc = [kn_s[r, :] for r in rows]
        vc = [vn_s[r, :] for r in rows]
        kst = [_stack(k, low).astype(BF16) for k in kc]
        kk = [_dot_nt(k.astype(BF16), ks) for k, ks in zip(kc, kst)]
        qk = [_dot_nt(q.astype(BF16), ks) for q, ks in zip(qc, kst)]
        chains = [(ci, d) for ci in range(PREP_GROUP) for d in range(2)]
        gc = [gf_s[rows[ci], d * LANES:(d + 1) * LANES] for ci, d in chains]
        beta = [gf_s[rows[ci], (2 + d) * LANES:(3 + d) * LANES] for ci, d in chains]
        gr = [row_form(g) for g in gc]
        dm = [jnp.exp(jnp.where(incl[d], g - r, -jnp.inf)) for (ci, d), g, r in zip(chains, gc, gr)]
        lm = [jnp.where(strict[d], b * kk[ci] * m, 0.0) for (ci, d), b, m in zip(chains, beta, dm)]
        aint = [(qk[ci] * m).astype(BF16) for (ci, d), m in zip(chains, dm)]
        x = [-jnp.where(diag8, l, 0.0) for l in lm]
        t = [eye + xx for xx in x]
        for _ in range(2):
            x = [pmm(xx, xx) for xx in x]
            t = [tt + pmm(tt, xx) for tt, xx in zip(t, x)]
        for off in merge_masks:
            tc = [pmm(tt, jnp.where(off, l, 0.0)) for tt, l in zip(t, lm)]
            t = [tt - pmm(a, tt) for tt, a in zip(t, tc)]
        egc = [jnp.exp(g) for g in gc]
        u = [pmm(tt, vc[ci] * b) for (ci, d), tt, b in zip(chains, t, beta)]
        w = [pmm(tt, kc[ci] * b * e).astype(BF16) for (ci, d), tt, b, e in zip(chains, t, beta, egc)]
        qd = [(qc[ci] * e).astype(BF16) for (ci, d), e in zip(chains, egc)]
        gtot = [g[CHUNK - 1:CHUNK, :] if d == 0 else g[0:1, :] for (ci, d), g in zip(chains, gc)]
        kd = [(kc[ci] * jnp.exp(gt - g)).astype(BF16) for (ci, d), gt, g in zip(chains, gtot, gc)]
        for n, (ci, d) in enumerate(chains):
            u_ref, w_ref, a_ref, qd_ref, kd_ref, gl_ref = out_refs[d]
            u_ref[0, rows[ci], :] = u[n]
            w_ref[0, rows[ci], :] = w[n]
            a_ref[0, rows[ci], :] = aint[n]
            qd_ref[0, rows[ci], :] = qd[n]
            kd_ref[0, rows[ci], :] = kd[n]
            gl_ref[0, pl.ds(cs[ci], 1), :, :] = jnp.broadcast_to(jnp.exp(gtot[n]), (1, 8, LANES))
        return carry

    lax.fori_loop(0, n_chunks // PREP_GROUP, group, 0)


def _gdn_prep(qkv, ab, conv_w, alog128, dtb128, expand, n_seq, seq_len):
    rb = max(seq_len, PREP_ROWS)
    nb = n_seq * seq_len // rb
    n_chunks = rb // CHUNK
    col = lambda off: pl.BlockSpec((rb, LANES), lambda s, p: (s, off + p))
    cw = lambda off: pl.BlockSpec((CONV_K, LANES), lambda s, p: (0, off + p))
    vec = pl.BlockSpec((1, LANES), lambda s, p: (0, 0))
    big = lambda: pl.BlockSpec((1, rb, LANES), lambda s, p: (s, 0, p))
    glspec = lambda: pl.BlockSpec((1, n_chunks, 8, LANES), lambda s, p: (s, 0, 0, p))
    shp = lambda dt: jax.ShapeDtypeStruct((nb, rb, GDN_W), dt)
    glshp = jax.ShapeDtypeStruct((nb, n_chunks, 8, GDN_W), F32)
    outs = pl.pallas_call(
        functools.partial(_gdn_prep_kernel, seq_len),
        out_shape=[shp(F32), shp(F32)] + [shp(BF16)] * 8 + [glshp, glshp],
        grid=(nb, N_PAIRS),
        in_specs=[col(0), col(N_PAIRS), col(2 * N_PAIRS), cw(0), cw(N_PAIRS), cw(2 * N_PAIRS),
                  pl.BlockSpec((rb, LANES), lambda s, p: (s, 0)), vec, vec,
                  pl.BlockSpec((1, LANES, 4 * LANES), lambda s, p: (p, 0, 0))],
        out_specs=[big() for _ in range(10)] + [glspec(), glspec()],
        scratch_shapes=[pltpu.VMEM((rb, LANES), F32)] * 4 + [pltpu.VMEM((rb, 4 * LANES), F32)],
        compiler_params=_cparams(("parallel", "parallel")),
        name="gdn_prep_%d" % seq_len,
    )(qkv, qkv, qkv, conv_w, conv_w, conv_w, ab, alog128, dtb128, expand)
    per_seq = [o.reshape(n_seq, seq_len, GDN_W) for o in outs[:10]]
    return per_seq + [o.reshape(n_seq, seq_len // CHUNK, 8, GDN_W) for o in outs[10:]]


def _gdn_scan_kernel(uf_ref, ub_ref, wf_ref, wb_ref, af_ref, abk_ref, qdf_ref, qdb_ref, kdf_ref, kdb_ref,
                     glf_ref, glb_ref, s0_ref, of_ref, ob_ref, sfin_ref, state):
    step = pl.program_id(1)
    n_steps = pl.num_programs(1)
    low = _pair_masks()
    chains = [(d, j, p) for d in range(2) for j in range(uf_ref.shape[0]) for p in range(N_PAIRS)]

    first = step == 0
    per_dir = ((uf_ref, wf_ref, af_ref, qdf_ref, kdf_ref, glf_ref, of_ref),
               (ub_ref, wb_ref, abk_ref, qdb_ref, kdb_ref, glb_ref, ob_ref))

    s = [jnp.where(first, _stack(s0_ref[j, d, p], low), state[idx]) for idx, (d, j, p) in enumerate(chains)]
    for sub in range(SCAN_CHUNKS):
        at = (sub, SCAN_CHUNKS - 1 - sub)

        def rd(k, d, j, p):
            return per_dir[d][k][j, at[d] * CHUNK:(at[d] + 1) * CHUNK, p * LANES:(p + 1) * LANES]

        sb = [x.astype(BF16) for x in s]
        ws = [_dot(rd(1, *c), b) for c, b in zip(chains, sb)]
        qs = [_dot(rd(3, *c), b) for c, b in zip(chains, sb)]
        vst = [_stack(rd(0, *c) - w, low).astype(BF16) for c, w in zip(chains, ws)]
        upd = [_dot_tn(_stack(rd(4, *c), low), v) for c, v in zip(chains, vst)]
        intra = [_dot(rd(2, *c), v) for c, v in zip(chains, vst)]
        nxt = []
        for idx, (d, j, p) in enumerate(chains):
            sl = slice(p * LANES, (p + 1) * LANES)
            nxt.append(s[idx] * per_dir[d][5][j, at[d], 0:1, sl] + upd[idx])
            per_dir[d][6][j, at[d] * CHUNK:(at[d] + 1) * CHUNK, sl] = qs[idx] + intra[idx]
        s = nxt
    for idx in range(len(chains)):
        state[idx] = s[idx]

    @pl.when(step == n_steps - 1)
    def _():
        for idx, (d, j, p) in enumerate(chains):
            fin = state[idx]
            sfin_ref[j, d, 2 * p] = fin[:GDN_DK, :GDN_DV]
            sfin_ref[j, d, 2 * p + 1] = pltpu.roll(fin[GDN_DK:], GDN_DV, axis=1)[:, :GDN_DV]


def _gdn_scan(prep, s0, n_seq, seq_len):
    n_steps = seq_len // (CHUNK * SCAN_CHUNKS)
    ns = min(SCAN_SEQS, n_seq)
    rows = CHUNK * SCAN_CHUNKS
    fwd = lambda: pl.BlockSpec((ns, rows, GDN_W), lambda g, i: (g, i, 0))
    bwd = lambda: pl.BlockSpec((ns, rows, GDN_W), lambda g, i: (g, n_steps - 1 - i, 0))
    glf = pl.BlockSpec((ns, SCAN_CHUNKS, 8, GDN_W), lambda g, i: (g, i, 0, 0))
    glb = pl.BlockSpec((ns, SCAN_CHUNKS, 8, GDN_W), lambda g, i: (g, n_steps - 1 - i, 0, 0))
    st = pl.BlockSpec((ns, 2, N_PAIRS, GDN_DK, LANES), lambda g, i: (g, 0, 0, 0, 0))
    st_out = pl.BlockSpec((ns, 2, GDN_HEADS, GDN_DK, GDN_DV), lambda g, i: (g, 0, 0, 0, 0))
    oshape = jax.ShapeDtypeStruct((n_seq, seq_len, GDN_W), F32)
    return pl.pallas_call(
        _gdn_scan_kernel,
        out_shape=[oshape, oshape, jax.ShapeDtypeStruct((n_seq, 2, GDN_HEADS, GDN_DK, GDN_DV), F32)],
        grid=(n_seq // ns, n_steps),
        in_specs=[fwd(), bwd()] * 5 + [glf, glb, st],
        out_specs=[fwd(), bwd(), st_out],
        scratch_shapes=[pltpu.VMEM((2 * ns * N_PAIRS, LANES, LANES), F32)],
        compiler_params=_cparams(("parallel", "arbitrary")),
        name="gdn_scan_%d" % seq_len,
    )(*prep, s0)


def _route(sel, s):
    per_group = N_EXPERTS // N_GROUPS
    ninf = -jnp.inf
    sub = lax.broadcasted_iota(jnp.int32, sel.shape, 1).astype(F32)
    gid = lax.broadcasted_iota(jnp.int32, (N_GROUPS, 1, sel.shape[2]), 0).astype(F32)
    m1 = jnp.max(sel, axis=1, keepdims=True)
    i1 = jnp.min(jnp.where(sel == m1, sub, float(per_group)), axis=1, keepdims=True)
    m2 = jnp.max(jnp.where(sub == i1, ninf, sel), axis=1, keepdims=True)
    work = m1 + m2
    gmask = jnp.zeros(work.shape, jnp.bool_)
    for _ in range(TOPK_GROUPS):
        m = jnp.max(work, axis=0, keepdims=True)
        idx = jnp.min(jnp.where(work == m, gid, float(N_GROUPS)), axis=0, keepdims=True)
        pick = gid == idx
        gmask = gmask | pick
        work = jnp.where(pick, ninf, work)
    work = jnp.where(gmask, sel, ninf)
    eid = gid * per_group + sub
    chosen = jnp.zeros(sel.shape, jnp.bool_)
    for _ in range(TOP_K):
        m = jnp.max(jnp.max(work, axis=1, keepdims=True), axis=0, keepdims=True)
        idx = jnp.min(jnp.min(jnp.where(work == m, eid, float(N_EXPERTS)), axis=1, keepdims=True),
                      axis=0, keepdims=True)
        pick = eid == idx
        chosen = chosen | pick
        work = jnp.where(pick, ninf, work)
    wk = jnp.where(chosen, s, 0.0)
    den = jnp.sum(jnp.sum(wk, axis=1, keepdims=True), axis=0, keepdims=True)
    return wk / den * ROUTED_SCALE


def _merge_kernel(x_ref, m_ref, omla_ref, of_ref, ob_ref, z_ref, ga_ref, gb_ref,
                  woa_ref, wob_ref, wo_ref, gpost_ref, gpre_ref, gdng_ref, wr_ref, eb_ref,
                  x1_ref, h2_ref, gates_ref, cnt_ref):
    m = m_ref[0]
    low = _pair_masks()
    n_blk = x_ref.shape[0] // MOE_BLOCK

    def mix_and_norm(blk):
        rows = slice(blk * MOE_BLOCK, (blk + 1) * MOE_BLOCK)
        parts = []
        for p in range(N_PAIRS):
            sl = slice(p * LANES, (p + 1) * LANES)
            op = of_ref[rows, sl] + ob_ref[rows, sl]
            sq = op * op
            s0 = jnp.sum(jnp.where(low, sq, 0.0), axis=-1, keepdims=True)
            s1 = jnp.sum(jnp.where(low, 0.0, sq), axis=-1, keepdims=True)
            ms = jnp.where(low, s0, s1) * (1.0 / GDN_DV)
            parts.append(op * lax.rsqrt(ms + EPS) * gdng_ref[...] * _silu(z_ref[rows, sl]))
        og = jnp.concatenate(parts, axis=1).astype(BF16)
        ya = _dot(omla_ref[rows, :], woa_ref[...])
        yb = _dot(og, wob_ref[...])
        mix = (jax.nn.sigmoid(ga_ref[rows, :]) * ya + jax.nn.sigmoid(gb_ref[rows, :]) * yb).astype(BF16)
        y = _dot(mix, wo_ref[...])
        x1 = x_ref[rows, :] + m[2:3] * _rms(y, gpost_ref[...])
        x1_ref[rows, :] = x1
        h2 = _rms(x1, gpre_ref[...]) * (1.0 + m[4:5]) + m[3:4]
        h2_ref[rows, :] = h2.astype(BF16)
        return h2

    def route(blk, h2):
        wh, wl, _ = _split3(wr_ref[...])
        hh, hl, _ = _split3(h2)
        logits = _dot_nt(wh, hh) + (_dot_nt(wh, hl) + _dot_nt(wl, hh))
        s = jax.nn.sigmoid(logits)
        sel = s + eb_ref[...]
        shape3 = (N_GROUPS, N_EXPERTS // N_GROUPS, MOE_BLOCK)
        gates_t = _route(sel.reshape(shape3), s.reshape(shape3)).reshape(N_EXPERTS, MOE_BLOCK)
        gates_ref[:, blk * MOE_BLOCK:(blk + 1) * MOE_BLOCK] = gates_t
        cnt_ref[blk] = jnp.sum((gates_t > 0.0).astype(F32), axis=1, keepdims=True)

    h2s = [mix_and_norm(blk) for blk in range(n_blk)]
    for blk in range(n_blk):
        route(blk, h2s[blk])


def _merge_kernel_into(*refs):
    n_in = 16
    _merge_kernel(*refs[:n_in], *refs[n_in + 4:])


def _merge(x, mods, mod_index, omla, o_f, o_b, pr, wts, tile0, total, into=None):
    t = x.shape[0]
    tm = MERGE_TILE
    bpt = tm // MOE_BLOCK

    def full(a):
        return pl.BlockSpec(a.shape, lambda i: (0,) * a.ndim)

    def rows(w):
        return pl.BlockSpec((tm, w), lambda i: (i, 0))

    def out_rows(w):
        return pl.BlockSpec((tm, w), lambda i: (tile0 + i, 0))

    names = ("woa", "wob", "wo", "gpost", "gpre2", "gdng", "wr_t", "eb")
    args = [x, mods, omla, o_f, o_b, pr["z"], pr["ga"], pr["gb"]] + [wts[n] for n in names]
    in_specs = [rows(D_MODEL), pl.BlockSpec((1, 6, D_MODEL), lambda i: (mod_index(i), 0, 0)),
                rows(MLA_HEADS * V_HEAD), rows(GDN_W), rows(GDN_W), rows(GDN_W),
                rows(D_MODEL), rows(D_MODEL)] + [full(wts[n]) for n in names]
    aliases = {}
    if into is not None:
        aliases = {len(args) + k: k for k in range(4)}
        in_specs = in_specs + [pl.BlockSpec(memory_space=pl.ANY)] * 4
        args = args + list(into)
    return pl.pallas_call(
        _merge_kernel if into is None else _merge_kernel_into,
        out_shape=[jax.ShapeDtypeStruct((total, D_MODEL), F32), jax.ShapeDtypeStruct((total, D_MODEL), BF16),
                   jax.ShapeDtypeStruct((N_EXPERTS, total), F32),
                   jax.ShapeDtypeStruct((total // MOE_BLOCK, N_EXPERTS, 1), F32)],
        grid=(t // tm,),
        in_specs=in_specs,
        out_specs=[out_rows(D_MODEL), out_rows(D_MODEL),
                   pl.BlockSpec((N_EXPERTS, tm), lambda i: (0, tile0 + i)),
                   pl.BlockSpec((bpt, N_EXPERTS, 1), lambda i: (tile0 + i, 0, 0))],
        input_output_aliases=aliases,
        compiler_params=_cparams(("parallel",)),
        name="merge",
    )(*args)


TABLE_W = 256
TILE_TABLE_W = 512


def _ceil_div(x, d):
    return jnp.floor((x + (d - 1)) * (1.0 / d))


def _moe_tables_kernel(cnt_ref, cnt_t_ref, ce_ref, cb_ref, dst_ref, src_ref, ng_ref, tile_ref):
    nb = cnt_ref.shape[0]
    ppt = float(TILE_PIECES)
    ppg = float(LOOP_PIECES)
    ei = lax.broadcasted_iota(jnp.int32, (N_EXPERTS, N_EXPERTS), 0)
    ej = lax.broadcasted_iota(jnp.int32, (N_EXPERTS, N_EXPERTS), 1)
    tri = (ej <= ei).astype(BF16)

    def cumsum_experts(col):
        wide = jnp.broadcast_to(col, (N_EXPERTS, LANES))
        return functools.reduce(jnp.add, [_dot(tri, pc) for pc in _split3(wide)])[:, 0:1]

    eid = lax.broadcasted_iota(jnp.int32, (N_EXPERTS, 1), 0).astype(F32)
    pc_t = _ceil_div(cnt_t_ref[...], PIECE)
    tp = jnp.sum(pc_t, axis=1, keepdims=True)
    rp = _ceil_div(tp, TILE_PIECES) * ppt
    gs_end = cumsum_experts(rp)
    gs = gs_end - rp
    blk = lax.broadcasted_iota(jnp.int32, (1, nb), 1)
    c = lax.broadcasted_iota(jnp.int32, (1, TABLE_W), 1).astype(F32)
    for b in range(nb):
        pc = _ceil_div(cnt_ref[b], PIECE)
        seg_end = cumsum_experts(pc)
        seg = seg_end - pc
        blk_off = jnp.sum(jnp.where(blk < b, pc_t, 0.0), axis=1, keepdims=True)
        nvalid = seg_end[N_EXPERTS - 1:N_EXPERTS, :]
        ce = jnp.minimum(jnp.sum((seg_end <= c).astype(F32), axis=0, keepdims=True), N_EXPERTS - 1.0)
        onehot = eid == ce
        seg_sel = jnp.sum(jnp.where(onehot, seg, 0.0), axis=0, keepdims=True)
        base_sel = jnp.sum(jnp.where(onehot, gs + blk_off - seg, 0.0), axis=0, keepdims=True)
        valid = c < nvalid
        dump = DUMP_PIECE0 + b * LOOP_PIECES + (c - ppg * jnp.floor(c * (1.0 / ppg)))
        dst = jnp.where(valid, base_sel + c, dump)
        row = slice(b, b + 1)
        ce_ref[row, :] = ce.astype(jnp.int32)
        cb_ref[row, :] = jnp.where(valid, (c - seg_sel) * PIECE, -float(1 << 20)).astype(jnp.int32)
        dst_ref[row, :] = dst.astype(jnp.int32)
        src_ref[row, :] = jnp.where(valid, dst, dst[:, 0:1]).astype(jnp.int32)
        ng_ref[row, :] = jnp.broadcast_to(_ceil_div(nvalid, LOOP_PIECES), (1, LANES)).astype(jnp.int32)
    j = lax.broadcasted_iota(jnp.int32, (1, TILE_TABLE_W), 1).astype(F32)
    start = j * ppt
    te = jnp.minimum(jnp.sum((gs_end <= start).astype(F32), axis=0, keepdims=True), N_EXPERTS - 1.0)
    onehot = eid == te
    tp_sel = jnp.sum(jnp.where(onehot, tp, 0.0), axis=0, keepdims=True)
    gs_sel = jnp.sum(jnp.where(onehot, gs, 0.0), axis=0, keepdims=True)
    n_used = gs_end[N_EXPERTS - 1:N_EXPERTS, :] * (1.0 / ppt)
    used = j < n_used
    tv = jnp.where(used, jnp.clip((tp_sel - (start - gs_sel)) * PIECE, 0.0, float(EXP_TILE)), 0.0)
    tin = jnp.where(used, j, n_used - 1.0)
    tout = jnp.where(used, j, float(N_EXP_TILES))
    tile_ref[...] = jnp.zeros(tile_ref.shape, jnp.int32)
    for r, v in enumerate((te, tv, tin, tout)):
        tile_ref[r:r + 1, :] = v.astype(jnp.int32)


def _dispatch_tables(cnt):
    nb = cnt.shape[0]
    tab = jax.ShapeDtypeStruct((nb, TABLE_W), jnp.int32)
    ce, cb, dst, src, ng, tile = pl.pallas_call(
        _moe_tables_kernel,
        out_shape=[tab, tab, tab, tab, jax.ShapeDtypeStruct((nb, LANES), jnp.int32),
                   jax.ShapeDtypeStruct((8, TILE_TABLE_W), jnp.int32)],
        name="moe_tables",
    )(cnt, cnt[:, :, 0].T)
    return {"ce": ce, "cb": cb, "dst": dst, "src": src, "ngroups": ng, "tile": tile}


def _piece_onehot(ce_ref, cb_ref, rank_s, b, g, extra=None):
    sub = lax.broadcasted_iota(jnp.int32, (PIECE, 1), 0).astype(F32)
    ps, ex = [], []
    for cc in range(GROUP_PIECES):
        c = g * GROUP_PIECES + cc
        e = ce_ref[b, c]
        base = cb_ref[b, c].astype(F32)
        hit = rank_s[pl.ds(e, 1), :] == base + sub
        ps.append(jnp.where(hit, 1.0, 0.0).astype(BF16))
        if extra is not None:
            ex.append(jnp.sum(jnp.where(hit, extra[pl.ds(e, 1), :], 0.0), axis=-1, keepdims=True))
    p = jnp.concatenate(ps, axis=0)
    return (p, jnp.concatenate(ex, axis=0)) if extra is not None else p


def _block_ranks(gt):
    n = gt.shape[1]
    ti = lax.broadcasted_iota(jnp.int32, (n, n), 0)
    tj = lax.broadcasted_iota(jnp.int32, (n, n), 1)
    before = (ti < tj).astype(BF16)
    member = gt > 0.0
    rank = _dot(member.astype(BF16), before)
    return jnp.where(member, rank, -1.0)


def _moe_sort_kernel(ce_ref, cb_ref, dst_ref, ng_ref, h_ref, gt_ref, xg_ref, stage, rank_s, gate_s, sem):
    b = pl.program_id(0)
    slot = b % 2
    gt = gt_ref[...]
    rank_s[...] = _block_ranks(gt)
    gate_s[...] = gt
    lane = lax.broadcasted_iota(jnp.int32, (1, LANES), 1)

    def piece_copy(blk, sl, c):
        r0 = pl.multiple_of(c * PIECE, PIECE)
        d0 = pl.multiple_of(dst_ref[blk, c] * PIECE, PIECE)
        return pltpu.make_async_copy(stage.at[sl, pl.ds(r0, PIECE)], xg_ref.at[pl.ds(d0, PIECE)], sem.at[sl])

    def groups(it, carry):
        gs = [it * LOOP_GROUPS + k for k in range(LOOP_GROUPS)]
        sel = [_piece_onehot(ce_ref, cb_ref, rank_s, b, g, gate_s) for g in gs]
        xs = [_dot(p, h_ref[...]).astype(BF16) for p, _ in sel]
        for g, x, (_, gcol) in zip(gs, xs, sel):
            hi, mid, lo = (t.astype(F32) for t in _split3(gcol))
            gblk = jnp.where(lane == 0, hi, jnp.where(lane == 1, mid, jnp.where(lane == 2, lo, 0.0)))
            r0 = pl.multiple_of(g * GROUP_ROWS, GROUP_ROWS)
            stage[slot, pl.ds(r0, GROUP_ROWS), :] = jnp.concatenate([x, gblk.astype(BF16)], axis=1)
        for cc in range(LOOP_PIECES):
            piece_copy(b, slot, it * LOOP_PIECES + cc).start()
        return carry

    lax.fori_loop(0, ng_ref[b, 0], groups, 0)

    def drain(blk, sl):
        def wait_some(it, carry):
            for cc in range(LOOP_PIECES):
                piece_copy(blk, sl, it * LOOP_PIECES + cc).wait()
            return carry
        lax.fori_loop(0, ng_ref[blk, 0], wait_some, 0)

    @pl.when(b > 0)
    def _():
        drain(b - 1, 1 - slot)

    @pl.when(b == pl.num_programs(0) - 1)
    def _():
        drain(b, slot)


def _moe_sort(h2, gates_t, tabs):
    nb = h2.shape[0] // MOE_BLOCK
    grid_spec = pltpu.PrefetchScalarGridSpec(
        num_scalar_prefetch=4, grid=(nb,),
        in_specs=[pl.BlockSpec((MOE_BLOCK, D_MODEL), lambda b, *_: (b, 0)),
                  pl.BlockSpec((N_EXPERTS, MOE_BLOCK), lambda b, *_: (0, b))],
        out_specs=pl.BlockSpec(memory_space=pl.ANY),
        scratch_shapes=[pltpu.VMEM((2, STAGE_PIECES * PIECE, DISPATCH_W), BF16),
                        pltpu.VMEM((N_EXPERTS, MOE_BLOCK), F32), pltpu.VMEM((N_EXPERTS, MOE_BLOCK), F32),
                        pltpu.SemaphoreType.DMA((2,))])
    return pl.pallas_call(
        _moe_sort_kernel,
        out_shape=jax.ShapeDtypeStruct((DISPATCH_ROWS, DISPATCH_W), BF16),
        grid_spec=grid_spec,
        compiler_params=_cparams(("arbitrary",)),
        name="moe_sort",
    )(tabs["ce"], tabs["cb"], tabs["dst"], tabs["ngroups"], h2, gates_t)


def _moe_expert_kernel(tile_ref, x_ref, wg_ref, wu_ref, wd_ref, y_ref):
    valid = tile_ref[1, pl.program_id(0)]

    @pl.when(valid > 0)
    def _():
        keep = lax.broadcasted_iota(jnp.int32, (EXP_TILE, 1), 0) < valid
        xrow = x_ref[...]
        x = jnp.where(keep, xrow[:, :D_MODEL], jnp.zeros((), BF16))
        g = jnp.sum(jnp.where(keep, xrow[:, D_MODEL:].astype(F32), 0.0), axis=-1, keepdims=True)
        hg = _dot(x, wg_ref[0].astype(BF16))
        hu = _dot(x, wu_ref[0].astype(BF16))
        act = (_silu(hg) * hu * g).astype(BF16)
        y_ref[...] = _dot(act, wd_ref[0].astype(BF16)).astype(BF16)


def _moe_expert(xg, tabs, wts):
    grid_spec = pltpu.PrefetchScalarGridSpec(
        num_scalar_prefetch=1, grid=(N_EXP_TILES,),
        in_specs=[pl.BlockSpec((EXP_TILE, DISPATCH_W), lambda j, tt: (tt[2, j], 0)),
                  pl.BlockSpec((1, D_MODEL, D_EXPERT), lambda j, tt: (tt[0, j], 0, 0)),
                  pl.BlockSpec((1, D_MODEL, D_EXPERT), lambda j, tt: (tt[0, j], 0, 0)),
                  pl.BlockSpec((1, D_EXPERT, D_MODEL), lambda j, tt: (tt[0, j], 0, 0))],
        out_specs=pl.BlockSpec((EXP_TILE, D_MODEL), lambda j, tt: (tt[3, j], 0)))
    return pl.pallas_call(
        _moe_expert_kernel,
        out_shape=jax.ShapeDtypeStruct((DISPATCH_ROWS, D_MODEL), BF16),
        grid_spec=grid_spec,
        compiler_params=_cparams(("arbitrary",)),
        name="moe_expert",
    )(tabs["tile"], xg, wts["w_gate"], wts["w_up"], wts["w_down"])


def _moe_combine_kernel(ctx_blocks, ce_ref, cb_ref, src_ref, ng_ref, yg_ref, gt_ref, h_ref, x1_ref, m_ref,
                        gpost_ref, wsg_ref, wsu_ref, wsd_ref, outp_ref, outs_ref, stage, rank_s, acc_s, sem):
    b = pl.program_id(0)
    slot = b % 2

    def piece_copy(blk, sl, c):
        r0 = pl.multiple_of(c * PIECE, PIECE)
        s0 = pl.multiple_of(src_ref[blk, c] * PIECE, PIECE)
        return pltpu.make_async_copy(yg_ref.at[pl.ds(s0, PIECE)], stage.at[sl, pl.ds(r0, PIECE)], sem.at[sl])

    def start_pieces(blk, sl, it):
        for cc in range(LOOP_PIECES):
            piece_copy(blk, sl, it * LOOP_PIECES + cc).start()

    def fetch(blk, sl):
        def start_some(it, carry):
            start_pieces(blk, sl, it)
            return carry
        lax.fori_loop(0, ng_ref[blk, 0], start_some, 0)

    ng_cur = ng_ref[b, 0]

    @pl.when(b == 0)
    def _():
        fetch(0, 0)

    @pl.when(b + 1 < pl.num_programs(0))
    def _():
        fetch(b + 1, 1 - slot)

    rank_s[...] = _block_ranks(gt_ref[...])
    h = h_ref[...]
    sh = (_silu(_dot(h, wsg_ref[...])) * _dot(h, wsu_ref[...])).astype(BF16)
    acc_s[...] = _dot(sh, wsd_ref[...])

    def wait_some(it, carry):
        for cc in range(LOOP_PIECES):
            piece_copy(b, slot, it * LOOP_PIECES + cc).wait()
        return carry

    lax.fori_loop(0, ng_cur, wait_some, 0)

    def groups(it, carry):
        gs = [it * LOOP_GROUPS + k for k in range(LOOP_GROUPS)]
        ps = [_piece_onehot(ce_ref, cb_ref, rank_s, b, g) for g in gs]
        ys = [stage[slot, pl.ds(pl.multiple_of(g * GROUP_ROWS, GROUP_ROWS), GROUP_ROWS), :] for g in gs]
        acc_s[...] += functools.reduce(jnp.add, [_dot_tn(p, y) for p, y in zip(ps, ys)])
        return carry

    lax.fori_loop(0, ng_cur, groups, 0)
    m = m_ref[0]
    y = x1_ref[...] + m[5:6] * _rms(acc_s[...], gpost_ref[...])

    @pl.when(b < ctx_blocks)
    def _():
        outp_ref[...] = y

    @pl.when(b >= ctx_blocks)
    def _():
        outs_ref[...] = y


def _moe_combine(yg, gates_t, h2, x1, mods, mod_index, tabs, wts, ctx_tokens):
    t = h2.shape[0]
    nb = t // MOE_BLOCK
    ctx_blocks = ctx_tokens // MOE_BLOCK

    def full(a):
        return pl.BlockSpec(a.shape, lambda b, *_: (0,) * a.ndim)

    grid_spec = pltpu.PrefetchScalarGridSpec(
        num_scalar_prefetch=4, grid=(nb,),
        in_specs=[pl.BlockSpec(memory_space=pl.ANY),
                  pl.BlockSpec((N_EXPERTS, MOE_BLOCK), lambda b, *_: (0, b)),
                  pl.BlockSpec((MOE_BLOCK, D_MODEL), lambda b, *_: (b, 0)),
                  pl.BlockSpec((MOE_BLOCK, D_MODEL), lambda b, *_: (b, 0)),
                  pl.BlockSpec((1, 6, D_MODEL), lambda b, *_: (mod_index(b), 0, 0)),
                  full(wts["gpost2"]), full(wts["wsg"]), full(wts["wsu"]), full(wts["wsd"])],
        out_specs=[pl.BlockSpec((MOE_BLOCK, D_MODEL), lambda b, *_: (jnp.minimum(b, ctx_blocks - 1), 0)),
                   pl.BlockSpec((MOE_BLOCK, D_MODEL), lambda b, *_: (jnp.maximum(b - ctx_blocks, 0), 0))],
        scratch_shapes=[pltpu.VMEM((2, STAGE_PIECES * PIECE, D_MODEL), BF16),
                        pltpu.VMEM((N_EXPERTS, MOE_BLOCK), F32), pltpu.VMEM((MOE_BLOCK, D_MODEL), F32),
                        pltpu.SemaphoreType.DMA((2,))])
    return pl.pallas_call(
        functools.partial(_moe_combine_kernel, ctx_blocks),
        out_shape=[jax.ShapeDtypeStruct((ctx_tokens, D_MODEL), F32),
                   jax.ShapeDtypeStruct((t - ctx_tokens, D_MODEL), F32)],
        grid_spec=grid_spec,
        compiler_params=_cparams(("arbitrary",)),
        name="moe_combine",
    )(tabs["ce"], tabs["cb"], tabs["src"], tabs["ngroups"], yg, gates_t, h2, x1, mods,
      wts["gpost2"], wts["wsg"], wts["wsu"], wts["wsd"])


def _moe(h2, gates_t, cnt, x1, mods, mod_index, wts, ctx_tokens):
    tabs = _dispatch_tables(cnt)
    xg = _moe_sort(h2, gates_t, tabs)
    yg = _moe_expert(xg, tabs, wts)
    return _moe_combine(yg, gates_t, h2, x1, mods, mod_index, tabs, wts, ctx_tokens)


def _rope_swap(w):
    nf = QK_ROPE // 4
    parts = [w[..., i * nf:(i + 1) * nf] for i in range(4)]
    return jnp.concatenate([parts[1], parts[0], parts[3], parts[2]], axis=-1)


def _head_block(nope, rope):
    lead = nope.shape[:-2] if nope is not None else rope.shape[:-2]
    nope = jnp.zeros(lead + (MLA_HEADS, QK_NOPE), F32) if nope is None else nope
    rope = jnp.zeros(lead + (MLA_HEADS, QK_ROPE), F32) if rope is None else rope
    pad = jnp.zeros(lead + (MLA_HEADS, HEAD_PAD - MLA_QK), F32)
    return jnp.concatenate([nope, rope, pad], axis=-1).reshape(lead + (MLA_HEADS * HEAD_PAD,))


def _rope_block(w):
    lead = w.shape[:-1]
    return jnp.concatenate([jnp.zeros(lead + (QK_NOPE,), F32), w,
                            jnp.zeros(lead + (HEAD_PAD - MLA_QK,), F32)], axis=-1)


def _rope_tables(n_tokens):
    rows = n_tokens // GRID_W
    row = np.repeat(np.arange(rows, dtype=np.float64), GRID_W)
    colv = np.tile(np.arange(GRID_W, dtype=np.float64), rows)
    nf = QK_ROPE // 4
    inv = ROPE_THETA ** (-np.arange(nf, dtype=np.float64) / nf)
    ang_r = row[:, None] * inv
    ang_c = colv[:, None] * inv
    cos32 = np.concatenate([np.cos(ang_r), np.cos(ang_r), np.cos(ang_c), np.cos(ang_c)], axis=-1)
    sin32 = np.concatenate([-np.sin(ang_r), np.sin(ang_r), -np.sin(ang_c), np.sin(ang_c)], axis=-1)
    ones = np.ones((n_tokens, QK_NOPE))
    tail = np.zeros((n_tokens, HEAD_PAD - MLA_QK))
    cos = np.concatenate([ones, cos32, tail], axis=-1)
    sin = np.concatenate([np.zeros((n_tokens, QK_NOPE)), sin32, tail], axis=-1)
    return jnp.asarray(cos, F32), jnp.asarray(sin, F32)


def _expand_matrix():
    e = np.zeros((N_PAIRS, LANES, 4 * LANES), np.float32)
    for p in range(N_PAIRS):
        for blk in range(4):
            for hh in range(2):
                src = blk * GDN_HEADS + 2 * p + hh
                e[p, src, blk * LANES + hh * GDN_DK: blk * LANES + (hh + 1) * GDN_DK] = 1.0
    return jnp.asarray(e)


def _prepare_weights(w_in, q_norm_g, kv_norm_g, w_uq, w_ukv, w_oa, w_ob, w_o, g_post_mix, g_pre_ffn,
                     g_post_ffn, gdn_norm_g, w_router, e_bias, w_gate, w_up, w_down, ws_gate, ws_up, ws_down):
    offs = np.cumsum((Q_LORA, KV_LORA, QK_ROPE, 3 * GDN_W, GDN_W, 2 * GDN_HEADS, 2 * GDN_HEADS,
                      D_MODEL, D_MODEL))[:-1].tolist()
    cq, ckv, kr, qkv, z, a, b, ga, gb = jnp.split(w_in, offs, axis=-1)
    ab = jnp.concatenate([a, b, jnp.zeros((D_MODEL, LANES - 4 * GDN_HEADS), F32)], axis=-1)
    small = jnp.concatenate([_rope_block(kr), _rope_block(_rope_swap(kr)), ab], axis=-1)
    lo0, mid0, hi0 = 0, offs[2], offs[6]
    uq = w_uq.reshape(Q_LORA, MLA_HEADS, MLA_QK)
    ukv = w_ukv.reshape(KV_LORA, MLA_HEADS, QK_NOPE + V_HEAD)
    return {
        "w_lo": w_in[:, lo0:lo0 + Q_LORA + KV_LORA].astype(BF16),
        "w_small": small.astype(BF16),
        "w_mid": w_in[:, mid0:mid0 + 4 * GDN_W].astype(BF16),
        "w_hi": w_in[:, hi0:hi0 + 2 * D_MODEL].astype(BF16),
        "qg": q_norm_g.reshape(1, Q_LORA), "kvg": kv_norm_g.reshape(1, KV_LORA),
        "wuq": _head_block(uq[..., :QK_NOPE], uq[..., QK_NOPE:]).astype(BF16),
        "wuqs": _head_block(None, _rope_swap(uq[..., QK_NOPE:])).astype(BF16),
        "wuk": _head_block(ukv[..., :QK_NOPE], None).astype(BF16),
        "wuv": ukv[..., QK_NOPE:].reshape(KV_LORA, MLA_HEADS * V_HEAD).astype(BF16),
        "woa": w_oa.astype(BF16), "wob": w_ob.astype(BF16), "wo": w_o.astype(BF16),
        "gpost": g_post_mix.reshape(1, D_MODEL), "gpre2": g_pre_ffn.reshape(1, D_MODEL),
        "gpost2": g_post_ffn.reshape(1, D_MODEL),
        "gdng": jnp.tile(gdn_norm_g.reshape(1, GDN_DV), (1, 2)),
        "wr_t": w_router.T, "eb": e_bias.reshape(N_EXPERTS, 1),
        "w_gate": w_gate, "w_up": w_up, "w_down": w_down,
        "wsg": ws_gate.astype(BF16), "wsu": ws_up.astype(BF16), "wsd": ws_down.astype(BF16),
    }


def _pad_lanes(v):
    v = v.reshape(1, -1)
    return jnp.concatenate([v, jnp.zeros((1, LANES - v.shape[1]), F32)], axis=-1)


def _layer_group(x, n_seq, seq_len, mods, mod_index, wts, gpre, conv_w, alog128, dtb128,
                 expand, rope_tabs, extra_kv, s0, tile0, into):
    pr = _proj(x, mods, mod_index, gpre, wts, rope_tabs)
    kvs = list(extra_kv) + [(pr["k"], pr["v"], seq_len)]
    omla = _attention(pr["q"], kvs, n_seq, seq_len, "attn_%d" % seq_len)
    prep = _gdn_prep(pr["qkv"], pr["ab"], conv_w, alog128, dtb128, expand, n_seq, seq_len)
    o_f, o_b, s_fin = _gdn_scan(prep, s0, n_seq, seq_len)
    t = n_seq * seq_len
    merged = _merge(x, mods, lambda i: mod_index(i * (MERGE_TILE // ROW_TILE)), omla,
                    o_f.reshape(t, GDN_W), o_b.reshape(t, GDN_W), pr, wts, tile0, N_TOKENS, into)
    return merged, pr, s_fin


def _state_to_pairs(s):
    b = s.shape[0]
    s = s.reshape(b, 2, N_PAIRS, 2, GDN_DK, GDN_DV)
    return jnp.transpose(s, (0, 1, 2, 4, 3, 5)).reshape(b, 2, N_PAIRS, GDN_DK, 2 * GDN_DV)


def kernel(x_prompt, x_sample, cache_ckv, cache_krope, state_delta, c, c_ctx, w_ada, b_ada, g_pre_mix,
           g_post_mix, g_pre_ffn, g_post_ffn, w_in, q_norm_g, kv_norm_g, w_uq, w_ukv, conv_w, a_log,
           dt_bias, gdn_norm_g, w_oa, w_ob, w_o, w_router, e_bias, w_gate, w_up, w_down, ws_gate, ws_up,
           ws_down):
    batch, seq, _ = x_prompt.shape
    dec_batch, dec_seq, _ = x_sample.shape
    past = cache_ckv.shape[2]
    assert batch * seq + dec_batch * dec_seq == N_TOKENS, "dispatch buffers are sized for N_TOKENS"
    y_p = x_prompt.reshape(batch * seq, D_MODEL)
    y_s = x_sample.reshape(dec_batch * dec_seq, D_MODEL)
    expand = _expand_matrix()
    rope_tabs = _rope_tables(dec_seq)
    cond8 = jnp.concatenate([c_ctx[None], c, jnp.zeros((8 - 1 - dec_batch, D_MODEL), F32)], axis=0)
    ckv_out, krope_out, state_out = [], [], []
    for l in range(DEPTH):
        wts = _prepare_weights(w_in[l], q_norm_g[l], kv_norm_g[l], w_uq[l], w_ukv[l], w_oa[l], w_ob[l],
                               w_o[l], g_post_mix[l], g_pre_ffn[l], g_post_ffn[l], gdn_norm_g[l],
                               w_router[l], e_bias[l], w_gate[l], w_up[l], w_down[l], ws_gate[l],
                               ws_up[l], ws_down[l])
        gpre = g_pre_mix[l].reshape(1, D_MODEL)
        alog128 = _pad_lanes(a_log[l])
        dtb128 = _pad_lanes(dt_bias[l])
        mods = _mods(cond8, w_ada[l], b_ada[l].reshape(1, -1)).reshape(8, 6, D_MODEL)

        zero_state = jnp.zeros((batch, 2, N_PAIRS, GDN_DK, LANES), F32)
        merged_p, pr_p, s_fin = _layer_group(
            y_p, batch, seq, mods, lambda i: 0, wts, gpre, conv_w[l], alog128, dtb128,
            expand, None, [], zero_state, 0, None)
        ckv_out.append(pr_p["ckv"].reshape(batch, seq, KV_LORA))
        krope_out.append(pr_p["kr"][:, QK_NOPE:MLA_QK].reshape(batch, seq, QK_ROPE))
        state_out.append(s_fin)

        kr_ctx = _rope_block(cache_krope[:, l].reshape(dec_batch * past, QK_ROPE))
        k_ctx, v_ctx = _cache_kv(cache_ckv[:, l].reshape(dec_batch * past, KV_LORA), kr_ctx, wts)
        tiles_per_seq = dec_seq // ROW_TILE
        merged_s, _, _ = _layer_group(
            y_s, dec_batch, dec_seq, mods, lambda i: 1 + i // tiles_per_seq,
            wts, gpre, conv_w[l], alog128, dtb128, expand,
            rope_tabs, [(k_ctx, v_ctx, past)], _state_to_pairs(state_delta[:, l]),
            batch * seq // MERGE_TILE, merged_p)

        x1, h2, gates_t, cnt = merged_s
        ctx_blocks = batch * seq // MOE_BLOCK
        blocks_per_seq = dec_seq // MOE_BLOCK
        y_p, y_s = _moe(h2, gates_t, cnt, x1, mods,
                        lambda b: jnp.where(b < ctx_blocks, 0, 1 + (b - ctx_blocks) // blocks_per_seq),
                        wts, batch * seq)
    new_ckv = jnp.stack(ckv_out, axis=1)
    new_krope = jnp.stack(krope_out, axis=1)
    new_state = jnp.stack(state_out, axis=1)
    return (y_p.reshape(batch, seq, D_MODEL), y_s.reshape(dec_batch, dec_seq, D_MODEL),
            new_ckv, new_krope, new_state)
```

```python
import functools

import numpy as np
import jax
import jax.numpy as jnp
from jax import lax
from jax.experimental import pallas as pl
from jax.experimental.pallas import tpu as pltpu

F32 = jnp.float32
BF16 = jnp.bfloat16

D_MODEL = 1024
DEPTH = 1
GRID_W = 64
MLA_HEADS = 8
QK_NOPE = 64
QK_ROPE = 32
V_HEAD = 64
Q_LORA = 256
KV_LORA = 256
ROPE_THETA = 10000.0
GDN_HEADS = 8
GDN_DK = 64
GDN_DV = 64
CONV_K = 5
CHUNK = 64
N_EXPERTS = 64
TOP_K = 8
N_GROUPS = 8
TOPK_GROUPS = 4
D_EXPERT = 256
D_SHARED = 256
ROUTED_SCALE = 2.5
EPS = 1e-6

LANES = 128
LOG2_E = 1.4426950408889634
MLA_QK = QK_NOPE + QK_ROPE
GDN_W = GDN_HEADS * GDN_DK
N_PAIRS = GDN_HEADS // 2
HEAD_PAD = LANES

_SEG = {"cq": ("w_lo", 0, Q_LORA), "ckv": ("w_lo", Q_LORA, Q_LORA + KV_LORA),
        "kr": ("w_small", 0, LANES), "krs": ("w_small", LANES, 2 * LANES), "ab": ("w_small", 2 * LANES, 3 * LANES),
        "qkv": ("w_mid", 0, 3 * GDN_W), "z": ("w_mid", 3 * GDN_W, 4 * GDN_W),
        "ga": ("w_hi", 0, D_MODEL), "gb": ("w_hi", D_MODEL, 2 * D_MODEL)}
_PROJ_W = ("w_lo", "w_small", "w_mid", "w_hi")

ROW_TILE = 512
ATTN_TILE = 512
N_TOKENS = 8192
MOE_BLOCK = 256
MERGE_TILE = 512
PIECE = 16
GROUP_ROWS = 256
GROUP_PIECES = GROUP_ROWS // PIECE
LOOP_GROUPS = 2
LOOP_PIECES = LOOP_GROUPS * GROUP_PIECES
EXP_TILE = 1024
TILE_PIECES = EXP_TILE // PIECE
N_BLOCKS = N_TOKENS // MOE_BLOCK
STAGE_PIECES = -(-((MOE_BLOCK * TOP_K + N_EXPERTS * (PIECE - 1)) // PIECE) // LOOP_PIECES) * LOOP_PIECES
N_EXP_TILES = -(-((N_TOKENS * TOP_K + N_BLOCKS * N_EXPERTS * (PIECE - 1)) // PIECE
                  + N_EXPERTS * (TILE_PIECES - 1)) // TILE_PIECES)
DUMP_PIECE0 = N_EXP_TILES * TILE_PIECES
DISPATCH_ROWS = (DUMP_PIECE0 + max(N_BLOCKS * LOOP_PIECES, TILE_PIECES)) * PIECE
DISPATCH_W = D_MODEL + LANES
PREP_GROUP = 8
PREP_ROWS = 1024
SCAN_SEQS = 4
SCAN_CHUNKS = 4
VMEM_LIMIT = 56 * 1024 * 1024


def _dot(a, b):
    return jnp.dot(a, b, preferred_element_type=F32)


def _dot_nt(a, b):
    return lax.dot_general(a, b, (((1,), (1,)), ((), ())), preferred_element_type=F32)


def _dot_tn(a, b):
    return lax.dot_general(a, b, (((0,), (0,)), ((), ())), preferred_element_type=F32)


def _rms(x, g):
    return x * lax.rsqrt(jnp.mean(x * x, axis=-1, keepdims=True) + EPS) * g


def _silu(x):
    return x * jax.nn.sigmoid(x)


def _cparams(sem):
    return pltpu.CompilerParams(dimension_semantics=sem, vmem_limit_bytes=VMEM_LIMIT)


def _mods_kernel(c_ref, w_ref, b_ref, o_ref):
    s = _silu(c_ref[...]).astype(BF16)
    o_ref[...] = _dot(s, w_ref[...].astype(BF16)) + b_ref[...]


def _mods(cond8, w_ada, b_ada):
    n = w_ada.shape[1]
    bn = 1024
    return pl.pallas_call(
        _mods_kernel,
        out_shape=jax.ShapeDtypeStruct((8, n), F32),
        grid=(n // bn,),
        in_specs=[pl.BlockSpec((8, D_MODEL), lambda j: (0, 0)),
                  pl.BlockSpec((D_MODEL, bn), lambda j: (0, j)),
                  pl.BlockSpec((1, bn), lambda j: (0, j))],
        out_specs=pl.BlockSpec((8, bn), lambda j: (0, j)),
        compiler_params=_cparams(("parallel",)),
        name="mods",
    )(cond8, w_ada, b_ada)


def _proj_kernel(rope, x_ref, m_ref, gpre_ref, wlo_ref, wsmall_ref, wmid_ref, whi_ref, qg_ref, kvg_ref,
                 wuq_ref, wuqs_ref, wuk_ref, wuv_ref, cos_ref, sin_ref,
                 q_ref, k_ref, v_ref, ckv_ref, kr_ref, qkv_ref, z_ref, ab_ref, ga_ref, gb_ref):
    m = m_ref[0]
    h = (_rms(x_ref[...], gpre_ref[...]) * (1.0 + m[1:2]) + m[0:1]).astype(BF16)
    w_refs = dict(zip(_PROJ_W, (wlo_ref, wsmall_ref, wmid_ref, whi_ref)))

    def seg(name):
        op, a, b = _SEG[name]
        return _dot(h, w_refs[op][:, a:b])

    qkv_ref[...] = seg("qkv")
    z_ref[...] = seg("z")
    ab_ref[...] = seg("ab")
    ga_ref[...] = seg("ga")
    gb_ref[...] = seg("gb")

    qn = _rms(seg("cq"), qg_ref[...]).astype(BF16)
    ckv = _rms(seg("ckv"), kvg_ref[...])
    ckv_ref[...] = ckv
    ckv_b = ckv.astype(BF16)
    kr = seg("kr")
    kr_ref[...] = kr
    qm = _dot(qn, wuq_ref[...])
    kk = _dot(ckv_b, wuk_ref[...])
    v_ref[...] = _dot(ckv_b, wuv_ref[...]).astype(BF16)
    scale = MLA_QK ** -0.5 * LOG2_E
    if rope:
        cos = cos_ref[...]
        sin = sin_ref[...]
        qs = _dot(qn, wuqs_ref[...])
        kr = kr * cos + seg("krs") * sin
    for hd in range(MLA_HEADS):
        sl = slice(hd * HEAD_PAD, (hd + 1) * HEAD_PAD)
        qh = qm[:, sl]
        if rope:
            qh = qh * cos + qs[:, sl] * sin
        q_ref[:, sl] = (qh * scale).astype(BF16)
        k_ref[:, sl] = (kk[:, sl] + kr).astype(BF16)


def _proj(x, mods, mod_index, gpre, wts, rope_tabs):
    t = x.shape[0]
    tm = ROW_TILE
    rope = rope_tabs is not None
    if rope:
        cos, sin = rope_tabs
        n_rope_blocks = cos.shape[0] // tm
        rope_spec = pl.BlockSpec((tm, LANES), lambda i: (i % n_rope_blocks, 0))
    else:
        cos = sin = jnp.zeros((8, LANES), F32)
        rope_spec = pl.BlockSpec((8, LANES), lambda i: (0, 0))

    def full(a):
        return pl.BlockSpec(a.shape, lambda i: (0,) * a.ndim)

    def rows(w):
        return pl.BlockSpec((tm, w), lambda i: (i, 0))

    out_widths = (("q", MLA_HEADS * HEAD_PAD, BF16), ("k", MLA_HEADS * HEAD_PAD, BF16),
                  ("v", MLA_HEADS * V_HEAD, BF16), ("ckv", KV_LORA, F32), ("kr", LANES, F32),
                  ("qkv", 3 * GDN_W, F32), ("z", GDN_W, F32), ("ab", LANES, F32),
                  ("ga", D_MODEL, F32), ("gb", D_MODEL, F32))
    outs = pl.pallas_call(
        functools.partial(_proj_kernel, rope),
        out_shape=[jax.ShapeDtypeStruct((t, w), dt) for _, w, dt in out_widths],
        grid=(t // tm,),
        in_specs=[rows(D_MODEL),
                  pl.BlockSpec((1, 6, D_MODEL), lambda i: (mod_index(i), 0, 0)),
                  full(gpre)] + [full(wts[n]) for n in _PROJ_W] + [full(wts["qg"]), full(wts["kvg"]),
                  full(wts["wuq"]), full(wts["wuqs"]), full(wts["wuk"]), full(wts["wuv"]),
                  rope_spec, rope_spec],
        out_specs=[rows(w) for _, w, _ in out_widths],
        compiler_params=_cparams(("parallel",)),
        name="proj_rope" if rope else "proj",
    )(x, mods, gpre, *[wts[n] for n in _PROJ_W], wts["qg"], wts["kvg"], wts["wuq"], wts["wuqs"],
      wts["wuk"], wts["wuv"], cos, sin)
    return {name: o for (name, _, _), o in zip(out_widths, outs)}


def _cache_kv_kernel(ckv_ref, kr_ref, wuk_ref, wuv_ref, k_ref, v_ref):
    c = ckv_ref[...].astype(BF16)
    kk = _dot(c, wuk_ref[...])
    v_ref[...] = _dot(c, wuv_ref[...]).astype(BF16)
    kr = kr_ref[...]
    for hd in range(MLA_HEADS):
        sl = slice(hd * HEAD_PAD, (hd + 1) * HEAD_PAD)
        k_ref[:, sl] = (kk[:, sl] + kr).astype(BF16)


def _cache_kv(ckv, kr128, wts):
    t = ckv.shape[0]
    tm = 512
    return pl.pallas_call(
        _cache_kv_kernel,
        out_shape=[jax.ShapeDtypeStruct((t, MLA_HEADS * HEAD_PAD), BF16),
                   jax.ShapeDtypeStruct((t, MLA_HEADS * V_HEAD), BF16)],
        grid=(t // tm,),
        in_specs=[pl.BlockSpec((tm, KV_LORA), lambda i: (i, 0)),
                  pl.BlockSpec((tm, LANES), lambda i: (i, 0)),
                  pl.BlockSpec(wts["wuk"].shape, lambda i: (0, 0)),
                  pl.BlockSpec(wts["wuv"].shape, lambda i: (0, 0))],
        out_specs=[pl.BlockSpec((tm, MLA_HEADS * HEAD_PAD), lambda i: (i, 0)),
                   pl.BlockSpec((tm, MLA_HEADS * V_HEAD), lambda i: (i, 0))],
        compiler_params=_cparams(("parallel",)),
        name="cache_kv",
    )(ckv, kr128, wts["wuk"], wts["wuv"])


def _attn_kernel(n_kv, q_ref, *refs):
    k_refs = refs[:n_kv]
    v_refs = refs[n_kv:2 * n_kv]
    o_ref = refs[2 * n_kv]
    lane = lax.broadcasted_iota(jnp.int32, (1, LANES), 1)
    low = lane < V_HEAD
    for pr in range(MLA_HEADS // 2):
        halves = []
        for hd in (2 * pr, 2 * pr + 1):
            sl = slice(hd * HEAD_PAD, (hd + 1) * HEAD_PAD)
            qh = q_ref[:, sl]
            scores = [_dot_nt(qh, kr[:, sl]) for kr in k_refs]
            mx = functools.reduce(jnp.maximum, [jnp.max(s, axis=-1, keepdims=True) for s in scores])
            ps = [jnp.exp2(s - mx) for s in scores]
            den = functools.reduce(jnp.add, [jnp.sum(p, axis=-1, keepdims=True) for p in ps])
            vsl = slice(pr * LANES, (pr + 1) * LANES)
            acc = functools.reduce(jnp.add, [_dot(p.astype(BF16), vr[:, vsl]) for p, vr in zip(ps, v_refs)])
            halves.append(acc / den)
        o_ref[:, pr * LANES:(pr + 1) * LANES] = jnp.where(low, halves[0], halves[1]).astype(BF16)


def _attention(q, kvs, n_seq, seq_len, name):
    tq = min(seq_len, ATTN_TILE)
    nq = seq_len // tq
    n_kv = len(kvs)
    in_specs = [pl.BlockSpec((tq, MLA_HEADS * HEAD_PAD), lambda b, j: (b * nq + j, 0))]
    in_specs += [pl.BlockSpec((rows, MLA_HEADS * HEAD_PAD), lambda b, j: (b, 0)) for _, _, rows in kvs]
    in_specs += [pl.BlockSpec((rows, MLA_HEADS * V_HEAD), lambda b, j: (b, 0)) for _, _, rows in kvs]
    return pl.pallas_call(
        functools.partial(_attn_kernel, n_kv),
        out_shape=jax.ShapeDtypeStruct((n_seq * seq_len, MLA_HEADS * V_HEAD), BF16),
        grid=(n_seq, nq),
        in_specs=in_specs,
        out_specs=pl.BlockSpec((tq, MLA_HEADS * V_HEAD), lambda b, j: (b * nq + j, 0)),
        compiler_params=_cparams(("parallel", "parallel")),
        name=name,
    )(q, *[k for k, _, _ in kvs], *[v for _, v, _ in kvs])


def _pair_masks():
    lane = lax.broadcasted_iota(jnp.int32, (1, LANES), 1)
    return lane < GDN_DK


def _stack(x, low):
    zero = jnp.zeros_like(x)
    return jnp.concatenate([jnp.where(low, x, zero), jnp.where(low, zero, x)], axis=0)


def _split3(x):
    hi = x.astype(BF16)
    r = x - hi.astype(F32)
    mid = r.astype(BF16)
    lo = (r - mid.astype(F32)).astype(BF16)
    return hi, mid, lo


def _gdn_prep_kernel(seq_len, q_ref, k_ref, v_ref, cwq_ref, cwk_ref, cwv_ref, ab_ref, alog_ref, dtb_ref, e_ref,
                     uf_ref, ub_ref, wf_ref, wb_ref, af_ref, abk_ref, qdf_ref, qdb_ref, kdf_ref, kdb_ref,
                     glf_ref, glb_ref,
                     qn_s, kn_s, vn_s, gcb_s, gf_s):
    seq = q_ref.shape[0]
    n_chunks = seq // CHUNK
    low = _pair_masks()
    row = lax.broadcasted_iota(jnp.int32, (seq, 1), 0) % seq_len
    lane = lax.broadcasted_iota(jnp.int32, (1, LANES), 1)

    def conv(x_ref, cw_ref):
        x = x_ref[...]
        acc = jnp.zeros_like(x)
        for j in range(CONV_K):
            sh = CONV_K // 2 - j
            xs = x if sh == 0 else pltpu.roll(x, sh % seq, axis=0)
            src = row - sh
            valid = (src >= 0) & (src < seq_len)
            acc = acc + jnp.where(valid, xs, 0.0) * cw_ref[j:j + 1, :]
        return _silu(acc)

    def l2n(x):
        sq = x * x
        s0 = jnp.sum(jnp.where(low, sq, 0.0), axis=-1, keepdims=True)
        s1 = jnp.sum(jnp.where(low, 0.0, sq), axis=-1, keepdims=True)
        return x * lax.rsqrt(jnp.where(low, s0, s1) + EPS)

    qn_s[...] = l2n(conv(q_ref, cwq_ref)) * (GDN_DK ** -0.5)
    kn_s[...] = l2n(conv(k_ref, cwk_ref))
    vn_s[...] = conv(v_ref, cwv_ref)

    a = ab_ref[...]
    xg = a + dtb_ref[...]
    softplus = jnp.maximum(xg, 0.0) + jnp.log(1.0 + jnp.exp(-jnp.abs(xg)))
    act = jnp.where(lane < 2 * GDN_HEADS, -jnp.exp(alog_ref[...]) * softplus, jax.nn.sigmoid(a))

    ti = lax.broadcasted_iota(jnp.int32, (CHUNK, CHUNK), 0)
    tj = lax.broadcasted_iota(jnp.int32, (CHUNK, CHUNK), 1)
    tri_lo = (tj <= ti).astype(BF16)
    tri_up = (tj >= ti).astype(BF16)
    for c in range(n_chunks):
        ac = act[c * CHUNK:(c + 1) * CHUNK]
        pieces = _split3(ac)
        lo = functools.reduce(jnp.add, [_dot(tri_lo, pc) for pc in pieces])
        up = functools.reduce(jnp.add, [_dot(tri_up, pc) for pc in pieces])
        gcb_s[c * CHUNK:(c + 1) * CHUNK, :] = jnp.where(lane < GDN_HEADS, lo,
                                                        jnp.where(lane < 2 * GDN_HEADS, up, ac))
    expand = e_ref[0].astype(BF16)
    gf_s[...] = functools.reduce(jnp.add, [_dot(pc, expand) for pc in _split3(gcb_s[...])])

    ri = lax.broadcasted_iota(jnp.int32, (CHUNK, LANES), 0)
    cj = lax.broadcasted_iota(jnp.int32, (CHUNK, LANES), 1) % CHUNK
    eye = (ri == cj).astype(F32)

    def pmm(x, y):
        return _dot(x.astype(BF16), _stack(y, low).astype(BF16))

    def row_form(g):
        gt = jnp.concatenate([g, jnp.zeros_like(g)], axis=0).T
        r0 = jnp.broadcast_to(gt[0:1, :], (CHUNK, LANES))
        r1 = jnp.broadcast_to(gt[GDN_DK:GDN_DK + 1, :], (CHUNK, LANES))
        return jnp.where(low, r0, pltpu.roll(r1, GDN_DK, axis=1))

    out_refs = ((uf_ref, wf_ref, af_ref, qdf_ref, kdf_ref, glf_ref),
                (ub_ref, wb_ref, abk_ref, qdb_ref, kdb_ref, glb_ref))
    incl = (ri >= cj, ri <= cj)
    strict = (ri > cj, ri < cj)
    diag8 = (ri // 8) == (cj // 8)
    merge_masks = [((ri // (2 * s)) == (cj // (2 * s))) & ((ri // s) != (cj // s)) for s in (8, 16, 32)]

    def group(it, carry):
        cs = [it * PREP_GROUP + cc for cc in range(PREP_GROUP)]
        rows = [pl.ds(pl.multiple_of(c * CHUNK, CHUNK), CHUNK) for c in cs]
        qc = [qn_s[r, :] for r in rows]
        kc = [kn_s[r, :] for r in rows]
        vc = [vn_s[r, :] for r in rows]
        kst = [_stack(k, low).astype(BF16) for k in kc]
        kq = [_dot_nt(jnp.concatenate([k, q], axis=0).astype(BF16), ks) for k, q, ks in zip(kc, qc, kst)]
        kk = [x[:CHUNK] for x in kq]
        qk = [x[CHUNK:] for x in kq]
        chains = [(ci, d) for ci in range(PREP_GROUP) for d in range(2)]
        gc = [gf_s[rows[ci], d * LANES:(d + 1) * LANES] for ci, d in chains]
        beta = [gf_s[rows[ci], (2 + d) * LANES:(3 + d) * LANES] for ci, d in chains]
        gr = [row_form(g) for g in gc]
        dm = [jnp.exp(jnp.where(incl[d], g - r, -jnp.inf)) for (ci, d), g, r in zip(chains, gc, gr)]
        lm = [jnp.where(strict[d], b * kk[ci] * m, 0.0) for (ci, d), b, m in zip(chains, beta, dm)]
        aint = [(qk[ci] * m).astype(BF16) for (ci, d), m in zip(chains, dm)]
        x = [-jnp.where(diag8, l, 0.0) for l in lm]
        t = [eye + xx for xx in x]
        for _ in range(2):
            x = [pmm(xx, xx) for xx in x]
            t = [tt + pmm(tt, xx) for tt, xx in zip(t, x)]
        for off in merge_masks:
            tc = [pmm(tt, jnp.where(off, l, 0.0)) for tt, l in zip(t, lm)]
            t = [tt - pmm(a, tt) for tt, a in zip(t, tc)]
        egc = [jnp.exp(g) for g in gc]
        rhs = [jnp.concatenate([_stack(vc[ci] * b, low), _stack(kc[ci] * b * e, low)], axis=1).astype(BF16)
               for (ci, d), b, e in zip(chains, beta, egc)]
        uw = [_dot(tt.astype(BF16), r) for tt, r in zip(t, rhs)]
        u = [x[:, :LANES] for x in uw]
        w = [x[:, LANES:].astype(BF16) for x in uw]
        qd = [(qc[ci] * e).astype(BF16) for (ci, d), e in zip(chains, egc)]
        gtot = [g[CHUNK - 1:CHUNK, :] if d == 0 else g[0:1, :] for (ci, d), g in zip(chains, gc)]
        kd = [(kc[ci] * jnp.exp(gt - g)).astype(BF16) for (ci, d), gt, g in zip(chains, gtot, gc)]
        for n, (ci, d) in enumerate(chains):
            u_ref, w_ref, a_ref, qd_ref, kd_ref, gl_ref = out_refs[d]
            u_ref[0, rows[ci], :] = u[n]
            w_ref[0, rows[ci], :] = w[n]
            a_ref[0, rows[ci], :] = aint[n]
            qd_ref[0, rows[ci], :] = qd[n]
            kd_ref[0, rows[ci], :] = kd[n]
            gl_ref[0, pl.ds(cs[ci], 1), :, :] = jnp.broadcast_to(jnp.exp(gtot[n]), (1, 8, LANES))
        return carry

    lax.fori_loop(0, n_chunks // PREP_GROUP, group, 0)


def _gdn_prep(qkv, ab, conv_w, alog128, dtb128, expand, n_seq, seq_len):
    rb = max(seq_len, PREP_ROWS)
    nb = n_seq * seq_len // rb
    n_chunks = rb // CHUNK
    col = lambda off: pl.BlockSpec((rb, LANES), lambda s, p: (s, off + p))
    cw = lambda off: pl.BlockSpec((CONV_K, LANES), lambda s, p: (0, off + p))
    vec = pl.BlockSpec((1, LANES), lambda s, p: (0, 0))
    big = lambda: pl.BlockSpec((1, rb, LANES), lambda s, p: (s, 0, p))
    glspec = lambda: pl.BlockSpec((1, n_chunks, 8, LANES), lambda s, p: (s, 0, 0, p))
    shp = lambda dt: jax.ShapeDtypeStruct((nb, rb, GDN_W), dt)
    glshp = jax.ShapeDtypeStruct((nb, n_chunks, 8, GDN_W), F32)
    outs = pl.pallas_call(
        functools.partial(_gdn_prep_kernel, seq_len),
        out_shape=[shp(F32), shp(F32)] + [shp(BF16)] * 8 + [glshp, glshp],
        grid=(nb, N_PAIRS),
        in_specs=[col(0), col(N_PAIRS), col(2 * N_PAIRS), cw(0), cw(N_PAIRS), cw(2 * N_PAIRS),
                  pl.BlockSpec((rb, LANES), lambda s, p: (s, 0)), vec, vec,
                  pl.BlockSpec((1, LANES, 4 * LANES), lambda s, p: (p, 0, 0))],
        out_specs=[big() for _ in range(10)] + [glspec(), glspec()],
        scratch_shapes=[pltpu.VMEM((rb, LANES), F32)] * 4 + [pltpu.VMEM((rb, 4 * LANES), F32)],
        compiler_params=_cparams(("parallel", "parallel")),
        name="gdn_prep_%d" % seq_len,
    )(qkv, qkv, qkv, conv_w, conv_w, conv_w, ab, alog128, dtb128, expand)
    per_seq = [o.reshape(n_seq, seq_len, GDN_W) for o in outs[:10]]
    return per_seq + [o.reshape(n_seq, seq_len // CHUNK, 8, GDN_W) for o in outs[10:]]


def _gdn_scan_kernel(uf_ref, ub_ref, wf_ref, wb_ref, af_ref, abk_ref, qdf_ref, qdb_ref, kdf_ref, kdb_ref,
                     glf_ref, glb_ref, s0_ref, of_ref, ob_ref, sfin_ref, state):
    step = pl.program_id(1)
    n_steps = pl.num_programs(1)
    low = _pair_masks()
    chains = [(d, j, p) for d in range(2) for j in range(uf_ref.shape[0]) for p in range(N_PAIRS)]

    first = step == 0
    per_dir = ((uf_ref, wf_ref, af_ref, qdf_ref, kdf_ref, glf_ref, of_ref),
               (ub_ref, wb_ref, abk_ref, qdb_ref, kdb_ref, glb_ref, ob_ref))

    s = [jnp.where(first, _stack(s0_ref[j, d, p], low), state[idx]) for idx, (d, j, p) in enumerate(chains)]
    for sub in range(SCAN_CHUNKS):
        at = (sub, SCAN_CHUNKS - 1 - sub)

        def rd(k, d, j, p):
            return per_dir[d][k][j, at[d] * CHUNK:(at[d] + 1) * CHUNK, p * LANES:(p + 1) * LANES]

        sb = [x.astype(BF16) for x in s]
        ws = [_dot(rd(1, *c), b) for c, b in zip(chains, sb)]
        qs = [_dot(rd(3, *c), b) for c, b in zip(chains, sb)]
        vst = [_stack(rd(0, *c) - w, low).astype(BF16) for c, w in zip(chains, ws)]
        upd = [_dot_tn(_stack(rd(4, *c), low), v) for c, v in zip(chains, vst)]
        intra = [_dot(rd(2, *c), v) for c, v in zip(chains, vst)]
        nxt = []
        for idx, (d, j, p) in enumerate(chains):
            sl = slice(p * LANES, (p + 1) * LANES)
            nxt.append(s[idx] * per_dir[d][5][j, at[d], 0:1, sl] + upd[idx])
            per_dir[d][6][j, at[d] * CHUNK:(at[d] + 1) * CHUNK, sl] = qs[idx] + intra[idx]
        s = nxt
    for idx in range(len(chains)):
        state[idx] = s[idx]

    @pl.when(step == n_steps - 1)
    def _():
        for idx, (d, j, p) in enumerate(chains):
            fin = state[idx]
            sfin_ref[j, d, 2 * p] = fin[:GDN_DK, :GDN_DV]
            sfin_ref[j, d, 2 * p + 1] = pltpu.roll(fin[GDN_DK:], GDN_DV, axis=1)[:, :GDN_DV]


def _gdn_scan(prep, s0, n_seq, seq_len):
    n_steps = seq_len // (CHUNK * SCAN_CHUNKS)
    ns = min(SCAN_SEQS, n_seq)
    rows = CHUNK * SCAN_CHUNKS
    fwd = lambda: pl.BlockSpec((ns, rows, GDN_W), lambda g, i: (g, i, 0))
    bwd = lambda: pl.BlockSpec((ns, rows, GDN_W), lambda g, i: (g, n_steps - 1 - i, 0))
    glf = pl.BlockSpec((ns, SCAN_CHUNKS, 8, GDN_W), lambda g, i: (g, i, 0, 0))
    glb = pl.BlockSpec((ns, SCAN_CHUNKS, 8, GDN_W), lambda g, i: (g, n_steps - 1 - i, 0, 0))
    st = pl.BlockSpec((ns, 2, N_PAIRS, GDN_DK, LANES), lambda g, i: (g, 0, 0, 0, 0))
    st_out = pl.BlockSpec((ns, 2, GDN_HEADS, GDN_DK, GDN_DV), lambda g, i: (g, 0, 0, 0, 0))
    oshape = jax.ShapeDtypeStruct((n_seq, seq_len, GDN_W), F32)
    return pl.pallas_call(
        _gdn_scan_kernel,
        out_shape=[oshape, oshape, jax.ShapeDtypeStruct((n_seq, 2, GDN_HEADS, GDN_DK, GDN_DV), F32)],
        grid=(n_seq // ns, n_steps),
        in_specs=[fwd(), bwd()] * 5 + [glf, glb, st],
        out_specs=[fwd(), bwd(), st_out],
        scratch_shapes=[pltpu.VMEM((2 * ns * N_PAIRS, LANES, LANES), F32)],
        compiler_params=_cparams(("parallel", "arbitrary")),
        name="gdn_scan_%d" % seq_len,
    )(*prep, s0)


def _route(sel, s):
    per_group = N_EXPERTS // N_GROUPS
    ninf = -jnp.inf
    sub = lax.broadcasted_iota(jnp.int32, sel.shape, 1).astype(F32)
    gid = lax.broadcasted_iota(jnp.int32, (N_GROUPS, 1, sel.shape[2]), 0).astype(F32)
    m1 = jnp.max(sel, axis=1, keepdims=True)
    i1 = jnp.min(jnp.where(sel == m1, sub, float(per_group)), axis=1, keepdims=True)
    m2 = jnp.max(jnp.where(sub == i1, ninf, sel), axis=1, keepdims=True)
    work = m1 + m2
    gmask = jnp.zeros(work.shape, jnp.bool_)
    for _ in range(TOPK_GROUPS):
        m = jnp.max(work, axis=0, keepdims=True)
        idx = jnp.min(jnp.where(work == m, gid, float(N_GROUPS)), axis=0, keepdims=True)
        pick = gid == idx
        gmask = gmask | pick
        work = jnp.where(pick, ninf, work)
    work = jnp.where(gmask, sel, ninf)
    eid = gid * per_group + sub
    chosen = jnp.zeros(sel.shape, jnp.bool_)
    for _ in range(TOP_K):
        m = jnp.max(jnp.max(work, axis=1, keepdims=True), axis=0, keepdims=True)
        idx = jnp.min(jnp.min(jnp.where(work == m, eid, float(N_EXPERTS)), axis=1, keepdims=True),
                      axis=0, keepdims=True)
        pick = eid == idx
        chosen = chosen | pick
        work = jnp.where(pick, ninf, work)
    wk = jnp.where(chosen, s, 0.0)
    den = jnp.sum(jnp.sum(wk, axis=1, keepdims=True), axis=0, keepdims=True)
    return wk / den * ROUTED_SCALE


def _merge_kernel(x_ref, m_ref, omla_ref, of_ref, ob_ref, z_ref, ga_ref, gb_ref,
                  woa_ref, wob_ref, wo_ref, gpost_ref, gpre_ref, gdng_ref, wr_ref, eb_ref,
                  x1_ref, h2_ref, gates_ref, cnt_ref):
    m = m_ref[0]
    low = _pair_masks()
    n_blk = x_ref.shape[0] // MOE_BLOCK

    def mix_and_norm(blk):
        rows = slice(blk * MOE_BLOCK, (blk + 1) * MOE_BLOCK)
        parts = []
        for p in range(N_PAIRS):
            sl = slice(p * LANES, (p + 1) * LANES)
            op = of_ref[rows, sl] + ob_ref[rows, sl]
            sq = op * op
            s0 = jnp.sum(jnp.where(low, sq, 0.0), axis=-1, keepdims=True)
            s1 = jnp.sum(jnp.where(low, 0.0, sq), axis=-1, keepdims=True)
            ms = jnp.where(low, s0, s1) * (1.0 / GDN_DV)
            parts.append(op * lax.rsqrt(ms + EPS) * gdng_ref[...] * _silu(z_ref[rows, sl]))
        og = jnp.concatenate(parts, axis=1).astype(BF16)
        ya = _dot(omla_ref[rows, :], woa_ref[...])
        yb = _dot(og, wob_ref[...])
        mix = (jax.nn.sigmoid(ga_ref[rows, :]) * ya + jax.nn.sigmoid(gb_ref[rows, :]) * yb).astype(BF16)
        y = _dot(mix, wo_ref[...])
        x1 = x_ref[rows, :] + m[2:3] * _rms(y, gpost_ref[...])
        x1_ref[rows, :] = x1
        h2 = _rms(x1, gpre_ref[...]) * (1.0 + m[4:5]) + m[3:4]
        h2_ref[rows, :] = h2.astype(BF16)
        return h2

    def route(blk, h2):
        wh, wl, _ = _split3(wr_ref[...])
        hh, hl, _ = _split3(h2)
        logits = _dot_nt(wh, hh) + (_dot_nt(wh, hl) + _dot_nt(wl, hh))
        s = jax.nn.sigmoid(logits)
        sel = s + eb_ref[...]
        shape3 = (N_GROUPS, N_EXPERTS // N_GROUPS, MOE_BLOCK)
        gates_t = _route(sel.reshape(shape3), s.reshape(shape3)).reshape(N_EXPERTS, MOE_BLOCK)
        gates_ref[:, blk * MOE_BLOCK:(blk + 1) * MOE_BLOCK] = gates_t
        cnt_ref[blk] = jnp.sum((gates_t > 0.0).astype(F32), axis=1, keepdims=True)

    h2s = [mix_and_norm(blk) for blk in range(n_blk)]
    for blk in range(n_blk):
        route(blk, h2s[blk])


def _merge_kernel_into(*refs):
    n_in = 16
    _merge_kernel(*refs[:n_in], *refs[n_in + 4:])


def _merge(x, mods, mod_index, omla, o_f, o_b, pr, wts, tile0, total, into=None):
    t = x.shape[0]
    tm = MERGE_TILE
    bpt = tm // MOE_BLOCK

    def full(a):
        return pl.BlockSpec(a.shape, lambda i: (0,) * a.ndim)

    def rows(w):
        return pl.BlockSpec((tm, w), lambda i: (i, 0))

    def out_rows(w):
        return pl.BlockSpec((tm, w), lambda i: (tile0 + i, 0))

    names = ("woa", "wob", "wo", "gpost", "gpre2", "gdng", "wr_t", "eb")
    args = [x, mods, omla, o_f, o_b, pr["z"], pr["ga"], pr["gb"]] + [wts[n] for n in names]
    in_specs = [rows(D_MODEL), pl.BlockSpec((1, 6, D_MODEL), lambda i: (mod_index(i), 0, 0)),
                rows(MLA_HEADS * V_HEAD), rows(GDN_W), rows(GDN_W), rows(GDN_W),
                rows(D_MODEL), rows(D_MODEL)] + [full(wts[n]) for n in names]
    aliases = {}
    if into is not None:
        aliases = {len(args) + k: k for k in range(4)}
        in_specs = in_specs + [pl.BlockSpec(memory_space=pl.ANY)] * 4
        args = args + list(into)
    return pl.pallas_call(
        _merge_kernel if into is None else _merge_kernel_into,
        out_shape=[jax.ShapeDtypeStruct((total, D_MODEL), F32), jax.ShapeDtypeStruct((total, D_MODEL), BF16),
                   jax.ShapeDtypeStruct((N_EXPERTS, total), F32),
                   jax.ShapeDtypeStruct((total // MOE_BLOCK, N_EXPERTS, 1), F32)],
        grid=(t // tm,),
        in_specs=in_specs,
        out_specs=[out_rows(D_MODEL), out_rows(D_MODEL),
                   pl.BlockSpec((N_EXPERTS, tm), lambda i: (0, tile0 + i)),
                   pl.BlockSpec((bpt, N_EXPERTS, 1), lambda i: (tile0 + i, 0, 0))],
        input_output_aliases=aliases,
        compiler_params=_cparams(("parallel",)),
        name="merge",
    )(*args)


TABLE_W = 256
TILE_TABLE_W = 512


def _ceil_div(x, d):
    return jnp.floor((x + (d - 1)) * (1.0 / d))


def _moe_tables_kernel(cnt_ref, cnt_t_ref, ce_ref, cb_ref, dst_ref, src_ref, ng_ref, tile_ref):
    nb = cnt_ref.shape[0]
    ppt = float(TILE_PIECES)
    ppg = float(LOOP_PIECES)
    ei = lax.broadcasted_iota(jnp.int32, (N_EXPERTS, N_EXPERTS), 0)
    ej = lax.broadcasted_iota(jnp.int32, (N_EXPERTS, N_EXPERTS), 1)
    tri = (ej <= ei).astype(BF16)

    def cumsum_experts(col):
        wide = jnp.broadcast_to(col, (N_EXPERTS, LANES))
        return functools.reduce(jnp.add, [_dot(tri, pc) for pc in _split3(wide)])[:, 0:1]

    eid = lax.broadcasted_iota(jnp.int32, (N_EXPERTS, 1), 0).astype(F32)
    pc_t = _ceil_div(cnt_t_ref[...], PIECE)
    tp = jnp.sum(pc_t, axis=1, keepdims=True)
    rp = _ceil_div(tp, TILE_PIECES) * ppt
    gs_end = cumsum_experts(rp)
    gs = gs_end - rp
    blk = lax.broadcasted_iota(jnp.int32, (1, nb), 1)
    c = lax.broadcasted_iota(jnp.int32, (1, TABLE_W), 1).astype(F32)
    for b in range(nb):
        pc = _ceil_div(cnt_ref[b], PIECE)
        seg_end = cumsum_experts(pc)
        seg = seg_end - pc
        blk_off = jnp.sum(jnp.where(blk < b, pc_t, 0.0), axis=1, keepdims=True)
        nvalid = seg_end[N_EXPERTS - 1:N_EXPERTS, :]
        ce = jnp.minimum(jnp.sum((seg_end <= c).astype(F32), axis=0, keepdims=True), N_EXPERTS - 1.0)
        onehot = eid == ce
        seg_sel = jnp.sum(jnp.where(onehot, seg, 0.0), axis=0, keepdims=True)
        base_sel = jnp.sum(jnp.where(onehot, gs + blk_off - seg, 0.0), axis=0, keepdims=True)
        valid = c < nvalid
        dump = DUMP_PIECE0 + b * LOOP_PIECES + (c - ppg * jnp.floor(c * (1.0 / ppg)))
        dst = jnp.where(valid, base_sel + c, dump)
        row = slice(b, b + 1)
        ce_ref[row, :] = ce.astype(jnp.int32)
        cb_ref[row, :] = jnp.where(valid, (c - seg_sel) * PIECE, -float(1 << 20)).astype(jnp.int32)
        dst_ref[row, :] = dst.astype(jnp.int32)
        src_ref[row, :] = jnp.where(valid, dst, dst[:, 0:1]).astype(jnp.int32)
        ng_ref[row, :] = jnp.broadcast_to(_ceil_div(nvalid, LOOP_PIECES), (1, LANES)).astype(jnp.int32)
    j = lax.broadcasted_iota(jnp.int32, (1, TILE_TABLE_W), 1).astype(F32)
    start = j * ppt
    te = jnp.minimum(jnp.sum((gs_end <= start).astype(F32), axis=0, keepdims=True), N_EXPERTS - 1.0)
    onehot = eid == te
    tp_sel = jnp.sum(jnp.where(onehot, tp, 0.0), axis=0, keepdims=True)
    gs_sel = jnp.sum(jnp.where(onehot, gs, 0.0), axis=0, keepdims=True)
    n_used = gs_end[N_EXPERTS - 1:N_EXPERTS, :] * (1.0 / ppt)
    used = j < n_used
    tv = jnp.where(used, jnp.clip((tp_sel - (start - gs_sel)) * PIECE, 0.0, float(EXP_TILE)), 0.0)
    tin = jnp.where(used, j, n_used - 1.0)
    tout = jnp.where(used, j, float(N_EXP_TILES))
    tile_ref[...] = jnp.zeros(tile_ref.shape, jnp.int32)
    for r, v in enumerate((te, tv, tin, tout)):
        tile_ref[r:r + 1, :] = v.astype(jnp.int32)


def _dispatch_tables(cnt):
    nb = cnt.shape[0]
    tab = jax.ShapeDtypeStruct((nb, TABLE_W), jnp.int32)
    ce, cb, dst, src, ng, tile = pl.pallas_call(
        _moe_tables_kernel,
        out_shape=[tab, tab, tab, tab, jax.ShapeDtypeStruct((nb, LANES), jnp.int32),
                   jax.ShapeDtypeStruct((8, TILE_TABLE_W), jnp.int32)],
        name="moe_tables",
    )(cnt, cnt[:, :, 0].T)
    return {"ce": ce, "cb": cb, "dst": dst, "src": src, "ngroups": ng, "tile": tile}


def _piece_onehot(ce_ref, cb_ref, rank_s, b, g, extra=None):
    sub = lax.broadcasted_iota(jnp.int32, (PIECE, 1), 0).astype(F32)
    ps, ex = [], []
    for cc in range(GROUP_PIECES):
        c = g * GROUP_PIECES + cc
        e = ce_ref[b, c]
        base = cb_ref[b, c].astype(F32)
        hit = rank_s[pl.ds(e, 1), :] == base + sub
        ps.append(jnp.where(hit, 1.0, 0.0).astype(BF16))
        if extra is not None:
            ex.append(jnp.sum(jnp.where(hit, extra[pl.ds(e, 1), :], 0.0), axis=-1, keepdims=True))
    p = jnp.concatenate(ps, axis=0)
    return (p, jnp.concatenate(ex, axis=0)) if extra is not None else p


def _block_ranks(gt):
    n = gt.shape[1]
    ti = lax.broadcasted_iota(jnp.int32, (n, n), 0)
    tj = lax.broadcasted_iota(jnp.int32, (n, n), 1)
    before = (ti < tj).astype(BF16)
    member = gt > 0.0
    rank = _dot(member.astype(BF16), before)
    return jnp.where(member, rank, -1.0)


def _moe_sort_kernel(ce_ref, cb_ref, dst_ref, ng_ref, h_ref, gt_ref, xg_ref, stage, rank_s, gate_s, sem):
    b = pl.program_id(0)
    slot = b % 2
    gt = gt_ref[...]
    rank_s[...] = _block_ranks(gt)
    gate_s[...] = gt
    lane = lax.broadcasted_iota(jnp.int32, (1, LANES), 1)

    def piece_copy(blk, sl, c):
        r0 = pl.multiple_of(c * PIECE, PIECE)
        d0 = pl.multiple_of(dst_ref[blk, c] * PIECE, PIECE)
        return pltpu.make_async_copy(stage.at[sl, pl.ds(r0, PIECE)], xg_ref.at[pl.ds(d0, PIECE)], sem.at[sl])

    def groups(it, carry):
        gs = [it * LOOP_GROUPS + k for k in range(LOOP_GROUPS)]
        sel = [_piece_onehot(ce_ref, cb_ref, rank_s, b, g, gate_s) for g in gs]
        xs = [_dot(p, h_ref[...]).astype(BF16) for p, _ in sel]
        for g, x, (_, gcol) in zip(gs, xs, sel):
            hi, mid, lo = (t.astype(F32) for t in _split3(gcol))
            gblk = jnp.where(lane == 0, hi, jnp.where(lane == 1, mid, jnp.where(lane == 2, lo, 0.0)))
            r0 = pl.multiple_of(g * GROUP_ROWS, GROUP_ROWS)
            stage[slot, pl.ds(r0, GROUP_ROWS), :] = jnp.concatenate([x, gblk.astype(BF16)], axis=1)
        for cc in range(LOOP_PIECES):
            piece_copy(b, slot, it * LOOP_PIECES + cc).start()
        return carry

    lax.fori_loop(0, ng_ref[b, 0], groups, 0)

    def drain(blk, sl):
        def wait_some(it, carry):
            for cc in range(LOOP_PIECES):
                piece_copy(blk, sl, it * LOOP_PIECES + cc).wait()
            return carry
        lax.fori_loop(0, ng_ref[blk, 0], wait_some, 0)

    @pl.when(b > 0)
    def _():
        drain(b - 1, 1 - slot)

    @pl.when(b == pl.num_programs(0) - 1)
    def _():
        drain(b, slot)


def _moe_sort(h2, gates_t, tabs):
    nb = h2.shape[0] // MOE_BLOCK
    grid_spec = pltpu.PrefetchScalarGridSpec(
        num_scalar_prefetch=4, grid=(nb,),
        in_specs=[pl.BlockSpec((MOE_BLOCK, D_MODEL), lambda b, *_: (b, 0)),
                  pl.BlockSpec((N_EXPERTS, MOE_BLOCK), lambda b, *_: (0, b))],
        out_specs=pl.BlockSpec(memory_space=pl.ANY),
        scratch_shapes=[pltpu.VMEM((2, STAGE_PIECES * PIECE, DISPATCH_W), BF16),
                        pltpu.VMEM((N_EXPERTS, MOE_BLOCK), F32), pltpu.VMEM((N_EXPERTS, MOE_BLOCK), F32),
                        pltpu.SemaphoreType.DMA((2,))])
    return pl.pallas_call(
        _moe_sort_kernel,
        out_shape=jax.ShapeDtypeStruct((DISPATCH_ROWS, DISPATCH_W), BF16),
        grid_spec=grid_spec,
        compiler_params=_cparams(("arbitrary",)),
        name="moe_sort",
    )(tabs["ce"], tabs["cb"], tabs["dst"], tabs["ngroups"], h2, gates_t)


def _moe_expert_kernel(tile_ref, x_ref, wg_ref, wu_ref, wd_ref, y_ref):
    valid = tile_ref[1, pl.program_id(0)]

    @pl.when(valid > 0)
    def _():
        keep = lax.broadcasted_iota(jnp.int32, (EXP_TILE, 1), 0) < valid
        xrow = x_ref[...]
        x = jnp.where(keep, xrow[:, :D_MODEL], jnp.zeros((), BF16))
        g = jnp.sum(jnp.where(keep, xrow[:, D_MODEL:].astype(F32), 0.0), axis=-1, keepdims=True)
        hg = _dot(x, wg_ref[0].astype(BF16))
        hu = _dot(x, wu_ref[0].astype(BF16))
        act = (_silu(hg) * hu * g).astype(BF16)
        y_ref[...] = _dot(act, wd_ref[0].astype(BF16)).astype(BF16)


def _moe_expert(xg, tabs, wts):
    grid_spec = pltpu.PrefetchScalarGridSpec(
        num_scalar_prefetch=1, grid=(N_EXP_TILES,),
        in_specs=[pl.BlockSpec((EXP_TILE, DISPATCH_W), lambda j, tt: (tt[2, j], 0)),
                  pl.BlockSpec((1, D_MODEL, D_EXPERT), lambda j, tt: (tt[0, j], 0, 0)),
                  pl.BlockSpec((1, D_MODEL, D_EXPERT), lambda j, tt: (tt[0, j], 0, 0)),
                  pl.BlockSpec((1, D_EXPERT, D_MODEL), lambda j, tt: (tt[0, j], 0, 0))],
        out_specs=pl.BlockSpec((EXP_TILE, D_MODEL), lambda j, tt: (tt[3, j], 0)))
    return pl.pallas_call(
        _moe_expert_kernel,
        out_shape=jax.ShapeDtypeStruct((DISPATCH_ROWS, D_MODEL), BF16),
        grid_spec=grid_spec,
        compiler_params=_cparams(("arbitrary",)),
        name="moe_expert",
    )(tabs["tile"], xg, wts["w_gate"], wts["w_up"], wts["w_down"])


def _moe_combine_kernel(ctx_blocks, ce_ref, cb_ref, src_ref, ng_ref, yg_ref, gt_ref, h_ref, x1_ref, m_ref,
                        gpost_ref, wsg_ref, wsu_ref, wsd_ref, outp_ref, outs_ref, stage, rank_s, acc_s, sem):
    b = pl.program_id(0)
    slot = b % 2

    def piece_copy(blk, sl, c):
        r0 = pl.multiple_of(c * PIECE, PIECE)
        s0 = pl.multiple_of(src_ref[blk, c] * PIECE, PIECE)
        return pltpu.make_async_copy(yg_ref.at[pl.ds(s0, PIECE)], stage.at[sl, pl.ds(r0, PIECE)], sem.at[sl])

    def start_pieces(blk, sl, it):
        for cc in range(LOOP_PIECES):
            piece_copy(blk, sl, it * LOOP_PIECES + cc).start()

    def fetch(blk, sl):
        def start_some(it, carry):
            start_pieces(blk, sl, it)
            return carry
        lax.fori_loop(0, ng_ref[blk, 0], start_some, 0)

    ng_cur = ng_ref[b, 0]

    @pl.when(b == 0)
    def _():
        fetch(0, 0)

    @pl.when(b + 1 < pl.num_programs(0))
    def _():
        fetch(b + 1, 1 - slot)

    rank_s[...] = _block_ranks(gt_ref[...])
    h = h_ref[...]
    sh = (_silu(_dot(h, wsg_ref[...])) * _dot(h, wsu_ref[...])).astype(BF16)
    acc_s[...] = _dot(sh, wsd_ref[...])

    def wait_some(it, carry):
        for cc in range(LOOP_PIECES):
            piece_copy(b, slot, it * LOOP_PIECES + cc).wait()
        return carry

    lax.fori_loop(0, ng_cur, wait_some, 0)

    def groups(it, carry):
        gs = [it * LOOP_GROUPS + k for k in range(LOOP_GROUPS)]
        ps = [_piece_onehot(ce_ref, cb_ref, rank_s, b, g) for g in gs]
        ys = [stage[slot, pl.ds(pl.multiple_of(g * GROUP_ROWS, GROUP_ROWS), GROUP_ROWS), :] for g in gs]
        acc_s[...] += functools.reduce(jnp.add, [_dot_tn(p, y) for p, y in zip(ps, ys)])
        return carry

    lax.fori_loop(0, ng_cur, groups, 0)
    m = m_ref[0]
    y = x1_ref[...] + m[5:6] * _rms(acc_s[...], gpost_ref[...])

    @pl.when(b < ctx_blocks)
    def _():
        outp_ref[...] = y

    @pl.when(b >= ctx_blocks)
    def _():
        outs_ref[...] = y


def _moe_combine(yg, gates_t, h2, x1, mods, mod_index, tabs, wts, ctx_tokens):
    t = h2.shape[0]
    nb = t // MOE_BLOCK
    ctx_blocks = ctx_tokens // MOE_BLOCK

    def full(a):
        return pl.BlockSpec(a.shape, lambda b, *_: (0,) * a.ndim)

    grid_spec = pltpu.PrefetchScalarGridSpec(
        num_scalar_prefetch=4, grid=(nb,),
        in_specs=[pl.BlockSpec(memory_space=pl.ANY),
                  pl.BlockSpec((N_EXPERTS, MOE_BLOCK), lambda b, *_: (0, b)),
                  pl.BlockSpec((MOE_BLOCK, D_MODEL), lambda b, *_: (b, 0)),
                  pl.BlockSpec((MOE_BLOCK, D_MODEL), lambda b, *_: (b, 0)),
                  pl.BlockSpec((1, 6, D_MODEL), lambda b, *_: (mod_index(b), 0, 0)),
                  full(wts["gpost2"]), full(wts["wsg"]), full(wts["wsu"]), full(wts["wsd"])],
        out_specs=[pl.BlockSpec((MOE_BLOCK, D_MODEL), lambda b, *_: (jnp.minimum(b, ctx_blocks - 1), 0)),
                   pl.BlockSpec((MOE_BLOCK, D_MODEL), lambda b, *_: (jnp.maximum(b - ctx_blocks, 0), 0))],
        scratch_shapes=[pltpu.VMEM((2, STAGE_PIECES * PIECE, D_MODEL), BF16),
                        pltpu.VMEM((N_EXPERTS, MOE_BLOCK), F32), pltpu.VMEM((MOE_BLOCK, D_MODEL), F32),
                        pltpu.SemaphoreType.DMA((2,))])
    return pl.pallas_call(
        functools.partial(_moe_combine_kernel, ctx_blocks),
        out_shape=[jax.ShapeDtypeStruct((ctx_tokens, D_MODEL), F32),
                   jax.ShapeDtypeStruct((t - ctx_tokens, D_MODEL), F32)],
        grid_spec=grid_spec,
        compiler_params=_cparams(("arbitrary",)),
        name="moe_combine",
    )(tabs["ce"], tabs["cb"], tabs["src"], tabs["ngroups"], yg, gates_t, h2, x1, mods,
      wts["gpost2"], wts["wsg"], wts["wsu"], wts["wsd"])


def _moe(h2, gates_t, cnt, x1, mods, mod_index, wts, ctx_tokens):
    tabs = _dispatch_tables(cnt)
    xg = _moe_sort(h2, gates_t, tabs)
    yg = _moe_expert(xg, tabs, wts)
    return _moe_combine(yg, gates_t, h2, x1, mods, mod_index, tabs, wts, ctx_tokens)


def _rope_swap(w):
    nf = QK_ROPE // 4
    parts = [w[..., i * nf:(i + 1) * nf] for i in range(4)]
    return jnp.concatenate([parts[1], parts[0], parts[3], parts[2]], axis=-1)


def _head_block(nope, rope):
    lead = nope.shape[:-2] if nope is not None else rope.shape[:-2]
    nope = jnp.zeros(lead + (MLA_HEADS, QK_NOPE), F32) if nope is None else nope
    rope = jnp.zeros(lead + (MLA_HEADS, QK_ROPE), F32) if rope is None else rope
    pad = jnp.zeros(lead + (MLA_HEADS, HEAD_PAD - MLA_QK), F32)
    return jnp.concatenate([nope, rope, pad], axis=-1).reshape(lead + (MLA_HEADS * HEAD_PAD,))


def _rope_block(w):
    lead = w.shape[:-1]
    return jnp.concatenate([jnp.zeros(lead + (QK_NOPE,), F32), w,
                            jnp.zeros(lead + (HEAD_PAD - MLA_QK,), F32)], axis=-1)


def _rope_tables(n_tokens):
    rows = n_tokens // GRID_W
    row = np.repeat(np.arange(rows, dtype=np.float64), GRID_W)
    colv = np.tile(np.arange(GRID_W, dtype=np.float64), rows)
    nf = QK_ROPE // 4
    inv = ROPE_THETA ** (-np.arange(nf, dtype=np.float64) / nf)
    ang_r = row[:, None] * inv
    ang_c = colv[:, None] * inv
    cos32 = np.concatenate([np.cos(ang_r), np.cos(ang_r), np.cos(ang_c), np.cos(ang_c)], axis=-1)
    sin32 = np.concatenate([-np.sin(ang_r), np.sin(ang_r), -np.sin(ang_c), np.sin(ang_c)], axis=-1)
    ones = np.ones((n_tokens, QK_NOPE))
    tail = np.zeros((n_tokens, HEAD_PAD - MLA_QK))
    cos = np.concatenate([ones, cos32, tail], axis=-1)
    sin = np.concatenate([np.zeros((n_tokens, QK_NOPE)), sin32, tail], axis=-1)
    return jnp.asarray(cos, F32), jnp.asarray(sin, F32)


def _expand_matrix():
    e = np.zeros((N_PAIRS, LANES, 4 * LANES), np.float32)
    for p in range(N_PAIRS):
        for blk in range(4):
            for hh in range(2):
                src = blk * GDN_HEADS + 2 * p + hh
                e[p, src, blk * LANES + hh * GDN_DK: blk * LANES + (hh + 1) * GDN_DK] = 1.0
    return jnp.asarray(e)


def _prepare_weights(w_in, q_norm_g, kv_norm_g, w_uq, w_ukv, w_oa, w_ob, w_o, g_post_mix, g_pre_ffn,
                     g_post_ffn, gdn_norm_g, w_router, e_bias, w_gate, w_up, w_down, ws_gate, ws_up, ws_down):
    offs = np.cumsum((Q_LORA, KV_LORA, QK_ROPE, 3 * GDN_W, GDN_W, 2 * GDN_HEADS, 2 * GDN_HEADS,
                      D_MODEL, D_MODEL))[:-1].tolist()
    cq, ckv, kr, qkv, z, a, b, ga, gb = jnp.split(w_in, offs, axis=-1)
    ab = jnp.concatenate([a, b, jnp.zeros((D_MODEL, LANES - 4 * GDN_HEADS), F32)], axis=-1)
    small = jnp.concatenate([_rope_block(kr), _rope_block(_rope_swap(kr)), ab], axis=-1)
    lo0, mid0, hi0 = 0, offs[2], offs[6]
    uq = w_uq.reshape(Q_LORA, MLA_HEADS, MLA_QK)
    ukv = w_ukv.reshape(KV_LORA, MLA_HEADS, QK_NOPE + V_HEAD)
    return {
        "w_lo": w_in[:, lo0:lo0 + Q_LORA + KV_LORA].astype(BF16),
        "w_small": small.astype(BF16),
        "w_mid": w_in[:, mid0:mid0 + 4 * GDN_W].astype(BF16),
        "w_hi": w_in[:, hi0:hi0 + 2 * D_MODEL].astype(BF16),
        "qg": q_norm_g.reshape(1, Q_LORA), "kvg": kv_norm_g.reshape(1, KV_LORA),
        "wuq": _head_block(uq[..., :QK_NOPE], uq[..., QK_NOPE:]).astype(BF16),
        "wuqs": _head_block(None, _rope_swap(uq[..., QK_NOPE:])).astype(BF16),
        "wuk": _head_block(ukv[..., :QK_NOPE], None).astype(BF16),
        "wuv": ukv[..., QK_NOPE:].reshape(KV_LORA, MLA_HEADS * V_HEAD).astype(BF16),
        "woa": w_oa.astype(BF16), "wob": w_ob.astype(BF16), "wo": w_o.astype(BF16),
        "gpost": g_post_mix.reshape(1, D_MODEL), "gpre2": g_pre_ffn.reshape(1, D_MODEL),
        "gpost2": g_post_ffn.reshape(1, D_MODEL),
        "gdng": jnp.tile(gdn_norm_g.reshape(1, GDN_DV), (1, 2)),
        "wr_t": w_router.T, "eb": e_bias.reshape(N_EXPERTS, 1),
        "w_gate": w_gate, "w_up": w_up, "w_down": w_down,
        "wsg": ws_gate.astype(BF16), "wsu": ws_up.astype(BF16), "wsd": ws_down.astype(BF16),
    }


def _pad_lanes(v):
    v = v.reshape(1, -1)
    return jnp.concatenate([v, jnp.zeros((1, LANES - v.shape[1]), F32)], axis=-1)


def _layer_group(x, n_seq, seq_len, mods, mod_index, wts, gpre, conv_w, alog128, dtb128,
                 expand, rope_tabs, extra_kv, s0, tile0, into):
    pr = _proj(x, mods, mod_index, gpre, wts, rope_tabs)
    kvs = list(extra_kv) + [(pr["k"], pr["v"], seq_len)]
    omla = _attention(pr["q"], kvs, n_seq, seq_len, "attn_%d" % seq_len)
    prep = _gdn_prep(pr["qkv"], pr["ab"], conv_w, alog128, dtb128, expand, n_seq, seq_len)
    o_f, o_b, s_fin = _gdn_scan(prep, s0, n_seq, seq_len)
    t = n_seq * seq_len
    merged = _merge(x, mods, lambda i: mod_index(i * (MERGE_TILE // ROW_TILE)), omla,
                    o_f.reshape(t, GDN_W), o_b.reshape(t, GDN_W), pr, wts, tile0, N_TOKENS, into)
    return merged, pr, s_fin


def _state_to_pairs(s):
    b = s.shape[0]
    s = s.reshape(b, 2, N_PAIRS, 2, GDN_DK, GDN_DV)
    return jnp.transpose(s, (0, 1, 2, 4, 3, 5)).reshape(b, 2, N_PAIRS, GDN_DK, 2 * GDN_DV)


def kernel(x_prompt, x_sample, cache_ckv, cache_krope, state_delta, c, c_ctx, w_ada, b_ada, g_pre_mix,
           g_post_mix, g_pre_ffn, g_post_ffn, w_in, q_norm_g, kv_norm_g, w_uq, w_ukv, conv_w, a_log,
           dt_bias, gdn_norm_g, w_oa, w_ob, w_o, w_router, e_bias, w_gate, w_up, w_down, ws_gate, ws_up,
           ws_down):
    batch, seq, _ = x_prompt.shape
    dec_batch, dec_seq, _ = x_sample.shape
    past = cache_ckv.shape[2]
    assert batch * seq + dec_batch * dec_seq == N_TOKENS, "dispatch buffers are sized for N_TOKENS"
    y_p = x_prompt.reshape(batch * seq, D_MODEL)
    y_s = x_sample.reshape(dec_batch * dec_seq, D_MODEL)
    expand = _expand_matrix()
    rope_tabs = _rope_tables(dec_seq)
    cond8 = jnp.concatenate([c_ctx[None], c, jnp.zeros((8 - 1 - dec_batch, D_MODEL), F32)], axis=0)
    ckv_out, krope_out, state_out = [], [], []
    for l in range(DEPTH):
        wts = _prepare_weights(w_in[l], q_norm_g[l], kv_norm_g[l], w_uq[l], w_ukv[l], w_oa[l], w_ob[l],
                               w_o[l], g_post_mix[l], g_pre_ffn[l], g_post_ffn[l], gdn_norm_g[l],
                               w_router[l], e_bias[l], w_gate[l], w_up[l], w_down[l], ws_gate[l],
                               ws_up[l], ws_down[l])
        gpre = g_pre_mix[l].reshape(1, D_MODEL)
        alog128 = _pad_lanes(a_log[l])
        dtb128 = _pad_lanes(dt_bias[l])
        mods = _mods(cond8, w_ada[l], b_ada[l].reshape(1, -1)).reshape(8, 6, D_MODEL)

        zero_state = jnp.zeros((batch, 2, N_PAIRS, GDN_DK, LANES), F32)
        merged_p, pr_p, s_fin = _layer_group(
            y_p, batch, seq, mods, lambda i: 0, wts, gpre, conv_w[l], alog128, dtb128,
            expand, None, [], zero_state, 0, None)
        ckv_out.append(pr_p["ckv"].reshape(batch, seq, KV_LORA))
        krope_out.append(pr_p["kr"][:, QK_NOPE:MLA_QK].reshape(batch, seq, QK_ROPE))
        state_out.append(s_fin)

        kr_ctx = _rope_block(cache_krope[:, l].reshape(dec_batch * past, QK_ROPE))
        k_ctx, v_ctx = _cache_kv(cache_ckv[:, l].reshape(dec_batch * past, KV_LORA), kr_ctx, wts)
        tiles_per_seq = dec_seq // ROW_TILE
        merged_s, _, _ = _layer_group(
            y_s, dec_batch, dec_seq, mods, lambda i: 1 + i // tiles_per_seq,
            wts, gpre, conv_w[l], alog128, dtb128, expand,
            rope_tabs, [(k_ctx, v_ctx, past)], _state_to_pairs(state_delta[:, l]),
            batch * seq // MERGE_TILE, merged_p)

        x1, h2, gates_t, cnt = merged_s
        ctx_blocks = batch * seq // MOE_BLOCK
        blocks_per_seq = dec_seq // MOE_BLOCK
        y_p, y_s = _moe(h2, gates_t, cnt, x1, mods,
                        lambda b: jnp.where(b < ctx_blocks, 0, 1 + (b - ctx_blocks) // blocks_per_seq),
                        wts, batch * seq)
    new_ckv = jnp.stack(ckv_out, axis=1)
    new_krope = jnp.stack(krope_out, axis=1)
    new_state = jnp.stack(state_out, axis=1)
    return (y_p.reshape(batch, seq, D_MODEL), y_s.reshape(dec_batch, dec_seq, D_MODEL),
            new_ckv, new_krope, new_state)
```

```python
import functools

import numpy as np
import jax
import jax.numpy as jnp
from jax import lax
from jax.experimental import pallas as pl
from jax.experimental.pallas import tpu as pltpu

F32 = jnp.float32
BF16 = jnp.bfloat16

D_MODEL = 1024
DEPTH = 1
GRID_W = 64
MLA_HEADS = 8
QK_NOPE = 64
QK_ROPE = 32
V_HEAD = 64
Q_LORA = 256
KV_LORA = 256
ROPE_THETA = 10000.0
GDN_HEADS = 8
GDN_DK = 64
GDN_DV = 64
CONV_K = 5
CHUNK = 64
N_EXPERTS = 64
TOP_K = 8
N_GROUPS = 8
TOPK_GROUPS = 4
D_EXPERT = 256
D_SHARED = 256
ROUTED_SCALE = 2.5
EPS = 1e-6

LANES = 128
LOG2_E = 1.4426950408889634
MLA_QK = QK_NOPE + QK_ROPE
GDN_W = GDN_HEADS * GDN_DK
N_PAIRS = GDN_HEADS // 2
HEAD_PAD = LANES

_SEG = {"cq": ("w_lo", 0, Q_LORA), "ckv": ("w_lo", Q_LORA, Q_LORA + KV_LORA),
        "kr": ("w_small", 0, LANES), "krs": ("w_small", LANES, 2 * LANES), "ab": ("w_small", 2 * LANES, 3 * LANES),
        "qkv": ("w_mid", 0, 3 * GDN_W), "z": ("w_mid", 3 * GDN_W, 4 * GDN_W),
        "ga": ("w_hi", 0, D_MODEL), "gb": ("w_hi", D_MODEL, 2 * D_MODEL)}
_PROJ_W = ("w_lo", "w_small", "w_mid", "w_hi")

ROW_TILE = 512
ATTN_TILE = 512
N_TOKENS = 8192
MOE_BLOCK = 256
MERGE_TILE = 512
PIECE = 16
GROUP_ROWS = 256
GROUP_PIECES = GROUP_ROWS // PIECE
LOOP_GROUPS = 2
LOOP_PIECES = LOOP_GROUPS * GROUP_PIECES
EXP_TILE = 1024
TILE_PIECES = EXP_TILE // PIECE
N_BLOCKS = N_TOKENS // MOE_BLOCK
STAGE_PIECES = -(-((MOE_BLOCK * TOP_K + N_EXPERTS * (PIECE - 1)) // PIECE) // LOOP_PIECES) * LOOP_PIECES
N_EXP_TILES = -(-((N_TOKENS * TOP_K + N_BLOCKS * N_EXPERTS * (PIECE - 1)) // PIECE
                  + N_EXPERTS * (TILE_PIECES - 1)) // TILE_PIECES)
DUMP_PIECE0 = N_EXP_TILES * TILE_PIECES
DISPATCH_ROWS = (DUMP_PIECE0 + max(N_BLOCKS * LOOP_PIECES, TILE_PIECES)) * PIECE
DISPATCH_W = D_MODEL + LANES
PREP_GROUP = 8
PREP_ROWS = 1024
SCAN_SEQS = 4
SCAN_CHUNKS = 4
VMEM_LIMIT = 56 * 1024 * 1024


def _dot(a, b):
    return jnp.dot(a, b, preferred_element_type=F32)


def _dot_nt(a, b):
    return lax.dot_general(a, b, (((1,), (1,)), ((), ())), preferred_element_type=F32)


def _dot_tn(a, b):
    return lax.dot_general(a, b, (((0,), (0,)), ((), ())), preferred_element_type=F32)


def _rms(x, g):
    return x * lax.rsqrt(jnp.mean(x * x, axis=-1, keepdims=True) + EPS) * g


def _silu(x):
    return x * jax.nn.sigmoid(x)


def _cparams(sem):
    return pltpu.CompilerParams(dimension_semantics=sem, vmem_limit_bytes=VMEM_LIMIT)


def _mods_kernel(c_ref, w_ref, b_ref, o_ref):
    s = _silu(c_ref[...]).astype(BF16)
    o_ref[...] = _dot(s, w_ref[...].astype(BF16)) + b_ref[...]


def _mods(cond8, w_ada, b_ada):
    n = w_ada.shape[1]
    bn = 1024
    return pl.pallas_call(
        _mods_kernel,
        out_shape=jax.ShapeDtypeStruct((8, n), F32),
        grid=(n // bn,),
        in_specs=[pl.BlockSpec((8, D_MODEL), lambda j: (0, 0)),
                  pl.BlockSpec((D_MODEL, bn), lambda j: (0, j)),
                  pl.BlockSpec((1, bn), lambda j: (0, j))],
        out_specs=pl.BlockSpec((8, bn), lambda j: (0, j)),
        compiler_params=_cparams(("parallel",)),
        name="mods",
    )(cond8, w_ada, b_ada)


def _proj_kernel(rope, x_ref, m_ref, gpre_ref, wlo_ref, wsmall_ref, wmid_ref, whi_ref, qg_ref, kvg_ref,
                 wuq_ref, wuqs_ref, wuk_ref, wuv_ref, cos_ref, sin_ref,
                 q_ref, k_ref, v_ref, ckv_ref, kr_ref, qkv_ref, z_ref, ab_ref, ga_ref, gb_ref):
    m = m_ref[0]
    h = (_rms(x_ref[...], gpre_ref[...]) * (1.0 + m[1:2]) + m[0:1]).astype(BF16)
    w_refs = dict(zip(_PROJ_W, (wlo_ref, wsmall_ref, wmid_ref, whi_ref)))

    def seg(name):
        op, a, b = _SEG[name]
        return _dot(h, w_refs[op][:, a:b])

    qkv_ref[...] = seg("qkv")
    z_ref[...] = seg("z")
    ab_ref[...] = seg("ab")
    ga_ref[...] = seg("ga")
    gb_ref[...] = seg("gb")

    qn = _rms(seg("cq"), qg_ref[...]).astype(BF16)
    ckv = _rms(seg("ckv"), kvg_ref[...])
    ckv_ref[...] = ckv
    ckv_b = ckv.astype(BF16)
    kr = seg("kr")
    kr_ref[...] = kr
    qm = _dot(qn, wuq_ref[...])
    kk = _dot(ckv_b, wuk_ref[...])
    v_ref[...] = _dot(ckv_b, wuv_ref[...]).astype(BF16)
    scale = MLA_QK ** -0.5 * LOG2_E
    if rope:
        cos = cos_ref[...]
        sin = sin_ref[...]
        qs = _dot(qn, wuqs_ref[...])
        kr = kr * cos + seg("krs") * sin
    for hd in range(MLA_HEADS):
        sl = slice(hd * HEAD_PAD, (hd + 1) * HEAD_PAD)
        qh = qm[:, sl]
        if rope:
            qh = qh * cos + qs[:, sl] * sin
        q_ref[:, sl] = (qh * scale).astype(BF16)
        k_ref[:, sl] = (kk[:, sl] + kr).astype(BF16)


def _proj(x, mods, mod_index, gpre, wts, rope_tabs):
    t = x.shape[0]
    tm = ROW_TILE
    rope = rope_tabs is not None
    if rope:
        cos, sin = rope_tabs
        n_rope_blocks = cos.shape[0] // tm
        rope_spec = pl.BlockSpec((tm, LANES), lambda i: (i % n_rope_blocks, 0))
    else:
        cos = sin = jnp.zeros((8, LANES), F32)
        rope_spec = pl.BlockSpec((8, LANES), lambda i: (0, 0))

    def full(a):
        return pl.BlockSpec(a.shape, lambda i: (0,) * a.ndim)

    def rows(w):
        return pl.BlockSpec((tm, w), lambda i: (i, 0))

    out_widths = (("q", MLA_HEADS * HEAD_PAD, BF16), ("k", MLA_HEADS * HEAD_PAD, BF16),
                  ("v", MLA_HEADS * V_HEAD, BF16), ("ckv", KV_LORA, F32), ("kr", LANES, F32),
                  ("qkv", 3 * GDN_W, F32), ("z", GDN_W, F32), ("ab", LANES, F32),
                  ("ga", D_MODEL, F32), ("gb", D_MODEL, F32))
    outs = pl.pallas_call(
        functools.partial(_proj_kernel, rope),
        out_shape=[jax.ShapeDtypeStruct((t, w), dt) for _, w, dt in out_widths],
        grid=(t // tm,),
        in_specs=[rows(D_MODEL),
                  pl.BlockSpec((1, 6, D_MODEL), lambda i: (mod_index(i), 0, 0)),
                  full(gpre)] + [full(wts[n]) for n in _PROJ_W] + [full(wts["qg"]), full(wts["kvg"]),
                  full(wts["wuq"]), full(wts["wuqs"]), full(wts["wuk"]), full(wts["wuv"]),
                  rope_spec, rope_spec],
        out_specs=[rows(w) for _, w, _ in out_widths],
        compiler_params=_cparams(("parallel",)),
        name="proj_rope" if rope else "proj",
    )(x, mods, gpre, *[wts[n] for n in _PROJ_W], wts["qg"], wts["kvg"], wts["wuq"], wts["wuqs"],
      wts["wuk"], wts["wuv"], cos, sin)
    return {name: o for (name, _, _), o in zip(out_widths, outs)}


def _cache_kv_kernel(ckv_ref, kr_ref, wuk_ref, wuv_ref, k_ref, v_ref):
    c = ckv_ref[...].astype(BF16)
    kk = _dot(c, wuk_ref[...])
    v_ref[...] = _dot(c, wuv_ref[...]).astype(BF16)
    kr = kr_ref[...]
    for hd in range(MLA_HEADS):
        sl = slice(hd * HEAD_PAD, (hd + 1) * HEAD_PAD)
        k_ref[:, sl] = (kk[:, sl] + kr).astype(BF16)


def _cache_kv(ckv, kr128, wts):
    t = ckv.shape[0]
    tm = 512
    return pl.pallas_call(
        _cache_kv_kernel,
        out_shape=[jax.ShapeDtypeStruct((t, MLA_HEADS * HEAD_PAD), BF16),
                   jax.ShapeDtypeStruct((t, MLA_HEADS * V_HEAD), BF16)],
        grid=(t // tm,),
        in_specs=[pl.BlockSpec((tm, KV_LORA), lambda i: (i, 0)),
                  pl.BlockSpec((tm, LANES), lambda i: (i, 0)),
                  pl.BlockSpec(wts["wuk"].shape, lambda i: (0, 0)),
                  pl.BlockSpec(wts["wuv"].shape, lambda i: (0, 0))],
        out_specs=[pl.BlockSpec((tm, MLA_HEADS * HEAD_PAD), lambda i: (i, 0)),
                   pl.BlockSpec((tm, MLA_HEADS * V_HEAD), lambda i: (i, 0))],
        compiler_params=_cparams(("parallel",)),
        name="cache_kv",
    )(ckv, kr128, wts["wuk"], wts["wuv"])


def _attn_kernel(n_kv, q_ref, *refs):
    k_refs = refs[:n_kv]
    v_refs = refs[n_kv:2 * n_kv]
    o_ref = refs[2 * n_kv]
    lane = lax.broadcasted_iota(jnp.int32, (1, LANES), 1)
    low = lane < V_HEAD
    for pr in range(MLA_HEADS // 2):
        halves = []
        for hd in (2 * pr, 2 * pr + 1):
            sl = slice(hd * HEAD_PAD, (hd + 1) * HEAD_PAD)
            qh = q_ref[:, sl]
            scores = [_dot_nt(qh, kr[:, sl]) for kr in k_refs]
            mx = functools.reduce(jnp.maximum, [jnp.max(s, axis=-1, keepdims=True) for s in scores])
            ps = [jnp.exp2(s - mx) for s in scores]
            den = functools.reduce(jnp.add, [jnp.sum(p, axis=-1, keepdims=True) for p in ps])
            vsl = slice(pr * LANES, (pr + 1) * LANES)
            acc = functools.reduce(jnp.add, [_dot(p.astype(BF16), vr[:, vsl]) for p, vr in zip(ps, v_refs)])
            halves.append(acc / den)
        o_ref[:, pr * LANES:(pr + 1) * LANES] = jnp.where(low, halves[0], halves[1]).astype(BF16)


def _attention(q, kvs, n_seq, seq_len, name):
    tq = min(seq_len, ATTN_TILE)
    nq = seq_len // tq
    n_kv = len(kvs)
    in_specs = [pl.BlockSpec((tq, MLA_HEADS * HEAD_PAD), lambda b, j: (b * nq + j, 0))]
    in_specs += [pl.BlockSpec((rows, MLA_HEADS * HEAD_PAD), lambda b, j: (b, 0)) for _, _, rows in kvs]
    in_specs += [pl.BlockSpec((rows, MLA_HEADS * V_HEAD), lambda b, j: (b, 0)) for _, _, rows in kvs]
    return pl.pallas_call(
        functools.partial(_attn_kernel, n_kv),
        out_shape=jax.ShapeDtypeStruct((n_seq * seq_len, MLA_HEADS * V_HEAD), BF16),
        grid=(n_seq, nq),
        in_specs=in_specs,
        out_specs=pl.BlockSpec((tq, MLA_HEADS * V_HEAD), lambda b, j: (b * nq + j, 0)),
        compiler_params=_cparams(("parallel", "parallel")),
        name=name,
    )(q, *[k for k, _, _ in kvs], *[v for _, v, _ in kvs])


def _pair_masks():
    lane = lax.broadcasted_iota(jnp.int32, (1, LANES), 1)
    return lane < GDN_DK


def _stack(x, low):
    zero = jnp.zeros_like(x)
    return jnp.concatenate([jnp.where(low, x, zero), jnp.where(low, zero, x)], axis=0)


def _split3(x):
    hi = x.astype(BF16)
    r = x - hi.astype(F32)
    mid = r.astype(BF16)
    lo = (r - mid.astype(F32)).astype(BF16)
    return hi, mid, lo


def _gdn_prep_kernel(seq_len, q_ref, k_ref, v_ref, cwq_ref, cwk_ref, cwv_ref, ab_ref, alog_ref, dtb_ref, e_ref,
                     uf_ref, ub_ref, wf_ref, wb_ref, af_ref, abk_ref, qdf_ref, qdb_ref, kdf_ref, kdb_ref,
                     glf_ref, glb_ref,
                     qn_s, kn_s, vn_s, gcb_s, gf_s):
    seq = q_ref.shape[0]
    n_chunks = seq // CHUNK
    low = _pair_masks()
    row = lax.broadcasted_iota(jnp.int32, (seq, 1), 0) % seq_len
    lane = lax.broadcasted_iota(jnp.int32, (1, LANES), 1)

    def conv(x_ref, cw_ref):
        x = x_ref[...]
        acc = jnp.zeros_like(x)
        for j in range(CONV_K):
            sh = CONV_K // 2 - j
            xs = x if sh == 0 else pltpu.roll(x, sh % seq, axis=0)
            src = row - sh
            valid = (src >= 0) & (src < seq_len)
            acc = acc + jnp.where(valid, xs, 0.0) * cw_ref[j:j + 1, :]
        return _silu(acc)

    def l2n(x):
        sq = x * x
        s0 = jnp.sum(jnp.where(low, sq, 0.0), axis=-1, keepdims=True)
        s1 = jnp.sum(jnp.where(low, 0.0, sq), axis=-1, keepdims=True)
        return x * lax.rsqrt(jnp.where(low, s0, s1) + EPS)

    qn_s[...] = l2n(conv(q_ref, cwq_ref)) * (GDN_DK ** -0.5)
    kn_s[...] = l2n(conv(k_ref, cwk_ref))
    vn_s[...] = conv(v_ref, cwv_ref)

    a = ab_ref[...]
    xg = a + dtb_ref[...]
    softplus = jnp.maximum(xg, 0.0) + jnp.log(1.0 + jnp.exp(-jnp.abs(xg)))
    act = jnp.where(lane < 2 * GDN_HEADS, -jnp.exp(alog_ref[...]) * softplus, jax.nn.sigmoid(a))

    ti = lax.broadcasted_iota(jnp.int32, (CHUNK, CHUNK), 0)
    tj = lax.broadcasted_iota(jnp.int32, (CHUNK, CHUNK), 1)
    tri_lo = (tj <= ti).astype(BF16)
    tri_up = (tj >= ti).astype(BF16)
    for c in range(n_chunks):
        ac = act[c * CHUNK:(c + 1) * CHUNK]
        pieces = _split3(ac)
        lo = functools.reduce(jnp.add, [_dot(tri_lo, pc) for pc in pieces])
        up = functools.reduce(jnp.add, [_dot(tri_up, pc) for pc in pieces])
        gcb_s[c * CHUNK:(c + 1) * CHUNK, :] = jnp.where(lane < GDN_HEADS, lo,
                                                        jnp.where(lane < 2 * GDN_HEADS, up, ac))
    expand = e_ref[0].astype(BF16)
    gf_s[...] = functools.reduce(jnp.add, [_dot(pc, expand) for pc in _split3(gcb_s[...])])

    ri = lax.broadcasted_iota(jnp.int32, (CHUNK, LANES), 0)
    cj = lax.broadcasted_iota(jnp.int32, (CHUNK, LANES), 1) % CHUNK
    eye = (ri == cj).astype(F32)

    def pmm(x, y):
        return _dot(x.astype(BF16), _stack(y, low).astype(BF16))

    def row_form(g):
        gt = jnp.concatenate([g, jnp.zeros_like(g)], axis=0).T
        r0 = jnp.broadcast_to(gt[0:1, :], (CHUNK, LANES))
        r1 = jnp.broadcast_to(gt[GDN_DK:GDN_DK + 1, :], (CHUNK, LANES))
        return jnp.where(low, r0, pltpu.roll(r1, GDN_DK, axis=1))

    out_refs = ((uf_ref, wf_ref, af_ref, qdf_ref, kdf_ref, glf_ref),
                (ub_ref, wb_ref, abk_ref, qdb_ref, kdb_ref, glb_ref))
    incl = (ri >= cj, ri <= cj)
    strict = (ri > cj, ri < cj)
    diag8 = (ri // 8) == (cj // 8)
    merge_masks = [((ri // (2 * s)) == (cj // (2 * s))) & ((ri // s) != (cj // s)) for s in (8, 16, 32)]

    def group(it, carry):
        cs = [it * PREP_GROUP + cc for cc in range(PREP_GROUP)]
        rows = [pl.ds(pl.multiple_of(c * CHUNK, CHUNK), CHUNK) for c in cs]
        qc = [qn_s[r, :] for r in rows]
        kc = [kn_s[r, :] for r in rows]
        vc = [vn_s[r, :] for r in rows]
        kst = [_stack(k, low).astype(BF16) for k in kc]
        kq = [_dot_nt(jnp.concatenate([k, q], axis=0).astype(BF16), ks) for k, q, ks in zip(kc, qc, kst)]
        kk = [x[:CHUNK] for x in kq]
        qk = [x[CHUNK:] for x in kq]
        chains = [(ci, d) for ci in range(PREP_GROUP) for d in range(2)]
        gc = [gf_s[rows[ci], d * LANES:(d + 1) * LANES] for ci, d in chains]
        beta = [gf_s[rows[ci], (2 + d) * LANES:(3 + d) * LANES] for ci, d in chains]
        gr = [row_form(g) for g in gc]
        dm = [jnp.exp(jnp.where(incl[d], g - r, -jnp.inf)) for (ci, d), g, r in zip(chains, gc, gr)]
        lm = [jnp.where(strict[d], b * kk[ci] * m, 0.0) for (ci, d), b, m in zip(chains, beta, dm)]
        aint = [(qk[ci] * m).astype(BF16) for (ci, d), m in zip(chains, dm)]
        x = [-jnp.where(diag8, l, 0.0) for l in lm]
        t = [eye + xx for xx in x]
        for _ in range(2):
            x = [pmm(xx, xx) for xx in x]
            t = [tt + pmm(tt, xx) for tt, xx in zip(t, x)]
        for off in merge_masks:
            tc = [pmm(tt, jnp.where(off, l, 0.0)) for tt, l in zip(t, lm)]
            t = [tt - pmm(a, tt) for tt, a in zip(t, tc)]
        egc = [jnp.exp(g) for g in gc]
        rhs = [jnp.concatenate([_stack(vc[ci] * b, low), _stack(kc[ci] * b * e, low)], axis=1).astype(BF16)
               for (ci, d), b, e in zip(chains, beta, egc)]
        uw = [_dot(tt.astype(BF16), r) for tt, r in zip(t, rhs)]
        u = [x[:, :LANES] for x in uw]
        w = [x[:, LANES:].astype(BF16) for x in uw]
        qd = [(qc[ci] * e).astype(BF16) for (ci, d), e in zip(chains, egc)]
        gtot = [g[CHUNK - 1:CHUNK, :] if d == 0 else g[0:1, :] for (ci, d), g in zip(chains, gc)]
        kd = [(kc[ci] * jnp.exp(gt - g)).astype(BF16) for (ci, d), gt, g in zip(chains, gtot, gc)]
        for n, (ci, d) in enumerate(chains):
            u_ref, w_ref, a_ref, qd_ref, kd_ref, gl_ref = out_refs[d]
            u_ref[0, rows[ci], :] = u[n]
            w_ref[0, rows[ci], :] = w[n]
            a_ref[0, rows[ci], :] = aint[n]
            qd_ref[0, rows[ci], :] = qd[n]
            kd_ref[0, rows[ci], :] = kd[n]
            gl_ref[0, pl.ds(cs[ci], 1), :, :] = jnp.broadcast_to(jnp.exp(gtot[n]), (1, 8, LANES))
        return carry

    lax.fori_loop(0, n_chunks // PREP_GROUP, group, 0)


def _gdn_prep(qkv, ab, conv_w, alog128, dtb128, expand, n_seq, seq_len):
    rb = max(seq_len, PREP_ROWS)
    nb = n_seq * seq_len // rb
    n_chunks = rb // CHUNK
    col = lambda off: pl.BlockSpec((rb, LANES), lambda s, p: (s, off + p))
    cw = lambda off: pl.BlockSpec((CONV_K, LANES), lambda s, p: (0, off + p))
    vec = pl.BlockSpec((1, LANES), lambda s, p: (0, 0))
    big = lambda: pl.BlockSpec((1, rb, LANES), lambda s, p: (s, 0, p))
    glspec = lambda: pl.BlockSpec((1, n_chunks, 8, LANES), lambda s, p: (s, 0, 0, p))
    shp = lambda dt: jax.ShapeDtypeStruct((nb, rb, GDN_W), dt)
    glshp = jax.ShapeDtypeStruct((nb, n_chunks, 8, GDN_W), F32)
    outs = pl.pallas_call(
        functools.partial(_gdn_prep_kernel, seq_len),
        out_shape=[shp(F32), shp(F32)] + [shp(BF16)] * 8 + [glshp, glshp],
        grid=(nb, N_PAIRS),
        in_specs=[col(0), col(N_PAIRS), col(2 * N_PAIRS), cw(0), cw(N_PAIRS), cw(2 * N_PAIRS),
                  pl.BlockSpec((rb, LANES), lambda s, p: (s, 0)), vec, vec,
                  pl.BlockSpec((1, LANES, 4 * LANES), lambda s, p: (p, 0, 0))],
        out_specs=[big() for _ in range(10)] + [glspec(), glspec()],
        scratch_shapes=[pltpu.VMEM((rb, LANES), F32)] * 4 + [pltpu.VMEM((rb, 4 * LANES), F32)],
        compiler_params=_cparams(("parallel", "parallel")),
        name="gdn_prep_%d" % seq_len,
    )(qkv, qkv, qkv, conv_w, conv_w, conv_w, ab, alog128, dtb128, expand)
    per_seq = [o.reshape(n_seq, seq_len, GDN_W) for o in outs[:10]]
    return per_seq + [o.reshape(n_seq, seq_len // CHUNK, 8, GDN_W) for o in outs[10:]]


def _gdn_scan_kernel(uf_ref, ub_ref, wf_ref, wb_ref, af_ref, abk_ref, qdf_ref, qdb_ref, kdf_ref, kdb_ref,
                     glf_ref, glb_ref, s0_ref, of_ref, ob_ref, sfin_ref, state):
    step = pl.program_id(1)
    n_steps = pl.num_programs(1)
    low = _pair_masks()
    chains = [(d, j, p) for d in range(2) for j in range(uf_ref.shape[0]) for p in range(N_PAIRS)]

    first = step == 0
    per_dir = ((uf_ref, wf_ref, af_ref, qdf_ref, kdf_ref, glf_ref, of_ref),
               (ub_ref, wb_ref, abk_ref, qdb_ref, kdb_ref, glb_ref, ob_ref))

    s = [jnp.where(first, _stack(s0_ref[j, d, p], low), state[idx]) for idx, (d, j, p) in enumerate(chains)]
    for sub in range(SCAN_CHUNKS):
        at = (sub, SCAN_CHUNKS - 1 - sub)

        def rd(k, d, j, p):
            return per_dir[d][k][j, at[d] * CHUNK:(at[d] + 1) * CHUNK, p * LANES:(p + 1) * LANES]

        sb = [x.astype(BF16) for x in s]
        wq = [_dot(jnp.concatenate([rd(1, *c), rd(3, *c)], axis=0), b) for c, b in zip(chains, sb)]
        ws = [x[:CHUNK] for x in wq]
        qs = [x[CHUNK:] for x in wq]
        vst = [_stack(rd(0, *c) - w, low).astype(BF16) for c, w in zip(chains, ws)]
        upd = [_dot_tn(_stack(rd(4, *c), low), v) for c, v in zip(chains, vst)]
        intra = [_dot(rd(2, *c), v) for c, v in zip(chains, vst)]
        nxt = []
        for idx, (d, j, p) in enumerate(chains):
            sl = slice(p * LANES, (p + 1) * LANES)
            nxt.append(s[idx] * per_dir[d][5][j, at[d], 0:1, sl] + upd[idx])
            per_dir[d][6][j, at[d] * CHUNK:(at[d] + 1) * CHUNK, sl] = qs[idx] + intra[idx]
        s = nxt
    for idx in range(len(chains)):
        state[idx] = s[idx]

    @pl.when(step == n_steps - 1)
    def _():
        for idx, (d, j, p) in enumerate(chains):
            fin = state[idx]
            sfin_ref[j, d, 2 * p] = fin[:GDN_DK, :GDN_DV]
            sfin_ref[j, d, 2 * p + 1] = pltpu.roll(fin[GDN_DK:], GDN_DV, axis=1)[:, :GDN_DV]


def _gdn_scan(prep, s0, n_seq, seq_len):
    n_steps = seq_len // (CHUNK * SCAN_CHUNKS)
    ns = min(SCAN_SEQS, n_seq)
    rows = CHUNK * SCAN_CHUNKS
    fwd = lambda: pl.BlockSpec((ns, rows, GDN_W), lambda g, i: (g, i, 0))
    bwd = lambda: pl.BlockSpec((ns, rows, GDN_W), lambda g, i: (g, n_steps - 1 - i, 0))
    glf = pl.BlockSpec((ns, SCAN_CHUNKS, 8, GDN_W), lambda g, i: (g, i, 0, 0))
    glb = pl.BlockSpec((ns, SCAN_CHUNKS, 8, GDN_W), lambda g, i: (g, n_steps - 1 - i, 0, 0))
    st = pl.BlockSpec((ns, 2, N_PAIRS, GDN_DK, LANES), lambda g, i: (g, 0, 0, 0, 0))
    st_out = pl.BlockSpec((ns, 2, GDN_HEADS, GDN_DK, GDN_DV), lambda g, i: (g, 0, 0, 0, 0))
    oshape = jax.ShapeDtypeStruct((n_seq, seq_len, GDN_W), F32)
    return pl.pallas_call(
        _gdn_scan_kernel,
        out_shape=[oshape, oshape, jax.ShapeDtypeStruct((n_seq, 2, GDN_HEADS, GDN_DK, GDN_DV), F32)],
        grid=(n_seq // ns, n_steps),
        in_specs=[fwd(), bwd()] * 5 + [glf, glb, st],
        out_specs=[fwd(), bwd(), st_out],
        scratch_shapes=[pltpu.VMEM((2 * ns * N_PAIRS, LANES, LANES), F32)],
        compiler_params=_cparams(("parallel", "arbitrary")),
        name="gdn_scan_%d" % seq_len,
    )(*prep, s0)


def _route(sel, s):
    per_group = N_EXPERTS // N_GROUPS
    ninf = -jnp.inf
    sub = lax.broadcasted_iota(jnp.int32, sel.shape, 1).astype(F32)
    gid = lax.broadcasted_iota(jnp.int32, (N_GROUPS, 1, sel.shape[2]), 0).astype(F32)
    m1 = jnp.max(sel, axis=1, keepdims=True)
    i1 = jnp.min(jnp.where(sel == m1, sub, float(per_group)), axis=1, keepdims=True)
    m2 = jnp.max(jnp.where(sub == i1, ninf, sel), axis=1, keepdims=True)
    work = m1 + m2
    gmask = jnp.zeros(work.shape, jnp.bool_)
    for _ in range(TOPK_GROUPS):
        m = jnp.max(work, axis=0, keepdims=True)
        idx = jnp.min(jnp.where(work == m, gid, float(N_GROUPS)), axis=0, keepdims=True)
        pick = gid == idx
        gmask = gmask | pick
        work = jnp.where(pick, ninf, work)
    work = jnp.where(gmask, sel, ninf)
    eid = gid * per_group + sub
    chosen = jnp.zeros(sel.shape, jnp.bool_)
    for _ in range(TOP_K):
        m = jnp.max(jnp.max(work, axis=1, keepdims=True), axis=0, keepdims=True)
        idx = jnp.min(jnp.min(jnp.where(work == m, eid, float(N_EXPERTS)), axis=1, keepdims=True),
                      axis=0, keepdims=True)
        pick = eid == idx
        chosen = chosen | pick
        work = jnp.where(pick, ninf, work)
    wk = jnp.where(chosen, s, 0.0)
    den = jnp.sum(jnp.sum(wk, axis=1, keepdims=True), axis=0, keepdims=True)
    return wk / den * ROUTED_SCALE


def _merge_kernel(x_ref, m_ref, omla_ref, of_ref, ob_ref, z_ref, ga_ref, gb_ref,
                  woa_ref, wob_ref, wo_ref, gpost_ref, gpre_ref, gdng_ref, wr_ref, eb_ref,
                  x1_ref, h2_ref, gates_ref, cnt_ref):
    m = m_ref[0]
    low = _pair_masks()
    n_blk = x_ref.shape[0] // MOE_BLOCK

    def mix_and_norm(blk):
        rows = slice(blk * MOE_BLOCK, (blk + 1) * MOE_BLOCK)
        parts = []
        for p in range(N_PAIRS):
            sl = slice(p * LANES, (p + 1) * LANES)
            op = of_ref[rows, sl] + ob_ref[rows, sl]
            sq = op * op
            s0 = jnp.sum(jnp.where(low, sq, 0.0), axis=-1, keepdims=True)
            s1 = jnp.sum(jnp.where(low, 0.0, sq), axis=-1, keepdims=True)
            ms = jnp.where(low, s0, s1) * (1.0 / GDN_DV)
            parts.append(op * lax.rsqrt(ms + EPS) * gdng_ref[...] * _silu(z_ref[rows, sl]))
        og = jnp.concatenate(parts, axis=1).astype(BF16)
        ya = _dot(omla_ref[rows, :], woa_ref[...])
        yb = _dot(og, wob_ref[...])
        mix = (jax.nn.sigmoid(ga_ref[rows, :]) * ya + jax.nn.sigmoid(gb_ref[rows, :]) * yb).astype(BF16)
        y = _dot(mix, wo_ref[...])
        x1 = x_ref[rows, :] + m[2:3] * _rms(y, gpost_ref[...])
        x1_ref[rows, :] = x1
        h2 = _rms(x1, gpre_ref[...]) * (1.0 + m[4:5]) + m[3:4]
        h2_ref[rows, :] = h2.astype(BF16)
        return h2

    def route(blk, h2):
        wh, wl, _ = _split3(wr_ref[...])
        hh, hl, _ = _split3(h2)
        logits = _dot_nt(wh, hh) + (_dot_nt(wh, hl) + _dot_nt(wl, hh))
        s = jax.nn.sigmoid(logits)
        sel = s + eb_ref[...]
        shape3 = (N_GROUPS, N_EXPERTS // N_GROUPS, MOE_BLOCK)
        gates_t = _route(sel.reshape(shape3), s.reshape(shape3)).reshape(N_EXPERTS, MOE_BLOCK)
        gates_ref[:, blk * MOE_BLOCK:(blk + 1) * MOE_BLOCK] = gates_t
        cnt_ref[blk] = jnp.sum((gates_t > 0.0).astype(F32), axis=1, keepdims=True)

    h2s = [mix_and_norm(blk) for blk in range(n_blk)]
    for blk in range(n_blk):
        route(blk, h2s[blk])


def _merge_kernel_into(*refs):
    n_in = 16
    _merge_kernel(*refs[:n_in], *refs[n_in + 4:])


def _merge(x, mods, mod_index, omla, o_f, o_b, pr, wts, tile0, total, into=None):
    t = x.shape[0]
    tm = MERGE_TILE
    bpt = tm // MOE_BLOCK

    def full(a):
        return pl.BlockSpec(a.shape, lambda i: (0,) * a.ndim)

    def rows(w):
        return pl.BlockSpec((tm, w), lambda i: (i, 0))

    def out_rows(w):
        return pl.BlockSpec((tm, w), lambda i: (tile0 + i, 0))

    names = ("woa", "wob", "wo", "gpost", "gpre2", "gdng", "wr_t", "eb")
    args = [x, mods, omla, o_f, o_b, pr["z"], pr["ga"], pr["gb"]] + [wts[n] for n in names]
    in_specs = [rows(D_MODEL), pl.BlockSpec((1, 6, D_MODEL), lambda i: (mod_index(i), 0, 0)),
                rows(MLA_HEADS * V_HEAD), rows(GDN_W), rows(GDN_W), rows(GDN_W),
                rows(D_MODEL), rows(D_MODEL)] + [full(wts[n]) for n in names]
    aliases = {}
    if into is not None:
        aliases = {len(args) + k: k for k in range(4)}
        in_specs = in_specs + [pl.BlockSpec(memory_space=pl.ANY)] * 4
        args = args + list(into)
    return pl.pallas_call(
        _merge_kernel if into is None else _merge_kernel_into,
        out_shape=[jax.ShapeDtypeStruct((total, D_MODEL), F32), jax.ShapeDtypeStruct((total, D_MODEL), BF16),
                   jax.ShapeDtypeStruct((N_EXPERTS, total), F32),
                   jax.ShapeDtypeStruct((total // MOE_BLOCK, N_EXPERTS, 1), F32)],
        grid=(t // tm,),
        in_specs=in_specs,
        out_specs=[out_rows(D_MODEL), out_rows(D_MODEL),
                   pl.BlockSpec((N_EXPERTS, tm), lambda i: (0, tile0 + i)),
                   pl.BlockSpec((bpt, N_EXPERTS, 1), lambda i: (tile0 + i, 0, 0))],
        input_output_aliases=aliases,
        compiler_params=_cparams(("parallel",)),
        name="merge",
    )(*args)


TABLE_W = 256
TILE_TABLE_W = 512


def _ceil_div(x, d):
    return jnp.floor((x + (d - 1)) * (1.0 / d))


def _moe_tables_kernel(cnt_ref, cnt_t_ref, ce_ref, cb_ref, dst_ref, src_ref, ng_ref, tile_ref):
    nb = cnt_ref.shape[0]
    ppt = float(TILE_PIECES)
    ppg = float(LOOP_PIECES)
    ei = lax.broadcasted_iota(jnp.int32, (N_EXPERTS, N_EXPERTS), 0)
    ej = lax.broadcasted_iota(jnp.int32, (N_EXPERTS, N_EXPERTS), 1)
    tri = (ej <= ei).astype(BF16)

    def cumsum_experts(col):
        wide = jnp.broadcast_to(col, (N_EXPERTS, LANES))
        return functools.reduce(jnp.add, [_dot(tri, pc) for pc in _split3(wide)])[:, 0:1]

    eid = lax.broadcasted_iota(jnp.int32, (N_EXPERTS, 1), 0).astype(F32)
    pc_t = _ceil_div(cnt_t_ref[...], PIECE)
    tp = jnp.sum(pc_t, axis=1, keepdims=True)
    rp = _ceil_div(tp, TILE_PIECES) * ppt
    gs_end = cumsum_experts(rp)
    gs = gs_end - rp
    blk = lax.broadcasted_iota(jnp.int32, (1, nb), 1)
    c = lax.broadcasted_iota(jnp.int32, (1, TABLE_W), 1).astype(F32)
    for b in range(nb):
        pc = _ceil_div(cnt_ref[b], PIECE)
        seg_end = cumsum_experts(pc)
        seg = seg_end - pc
        blk_off = jnp.sum(jnp.where(blk < b, pc_t, 0.0), axis=1, keepdims=True)
        nvalid = seg_end[N_EXPERTS - 1:N_EXPERTS, :]
        ce = jnp.minimum(jnp.sum((seg_end <= c).astype(F32), axis=0, keepdims=True), N_EXPERTS - 1.0)
        onehot = eid == ce
        seg_sel = jnp.sum(jnp.where(onehot, seg, 0.0), axis=0, keepdims=True)
        base_sel = jnp.sum(jnp.where(onehot, gs + blk_off - seg, 0.0), axis=0, keepdims=True)
        valid = c < nvalid
        dump = DUMP_PIECE0 + b * LOOP_PIECES + (c - ppg * jnp.floor(c * (1.0 / ppg)))
        dst = jnp.where(valid, base_sel + c, dump)
        row = slice(b, b + 1)
        ce_ref[row, :] = ce.astype(jnp.int32)
        cb_ref[row, :] = jnp.where(valid, (c - seg_sel) * PIECE, -float(1 << 20)).astype(jnp.int32)
        dst_ref[row, :] = dst.astype(jnp.int32)
        src_ref[row, :] = jnp.where(valid, dst, dst[:, 0:1]).astype(jnp.int32)
        ng_ref[row, :] = jnp.broadcast_to(_ceil_div(nvalid, LOOP_PIECES), (1, LANES)).astype(jnp.int32)
    j = lax.broadcasted_iota(jnp.int32, (1, TILE_TABLE_W), 1).astype(F32)
    start = j * ppt
    te = jnp.minimum(jnp.sum((gs_end <= start).astype(F32), axis=0, keepdims=True), N_EXPERTS - 1.0)
    onehot = eid == te
    tp_sel = jnp.sum(jnp.where(onehot, tp, 0.0), axis=0, keepdims=True)
    gs_sel = jnp.sum(jnp.where(onehot, gs, 0.0), axis=0, keepdims=True)
    n_used = gs_end[N_EXPERTS - 1:N_EXPERTS, :] * (1.0 / ppt)
    used = j < n_used
    tv = jnp.where(used, jnp.clip((tp_sel - (start - gs_sel)) * PIECE, 0.0, float(EXP_TILE)), 0.0)
    tin = jnp.where(used, j, n_used - 1.0)
    tout = jnp.where(used, j, float(N_EXP_TILES))
    tile_ref[...] = jnp.zeros(tile_ref.shape, jnp.int32)
    for r, v in enumerate((te, tv, tin, tout)):
        tile_ref[r:r + 1, :] = v.astype(jnp.int32)


def _dispatch_tables(cnt):
    nb = cnt.shape[0]
    tab = jax.ShapeDtypeStruct((nb, TABLE_W), jnp.int32)
    ce, cb, dst, src, ng, tile = pl.pallas_call(
        _moe_tables_kernel,
        out_shape=[tab, tab, tab, tab, jax.ShapeDtypeStruct((nb, LANES), jnp.int32),
                   jax.ShapeDtypeStruct((8, TILE_TABLE_W), jnp.int32)],
        name="moe_tables",
    )(cnt, cnt[:, :, 0].T)
    return {"ce": ce, "cb": cb, "dst": dst, "src": src, "ngroups": ng, "tile": tile}


def _piece_onehot(ce_ref, cb_ref, rank_s, b, g, extra=None):
    sub = lax.broadcasted_iota(jnp.int32, (PIECE, 1), 0).astype(F32)
    ps, ex = [], []
    for cc in range(GROUP_PIECES):
        c = g * GROUP_PIECES + cc
        e = ce_ref[b, c]
        base = cb_ref[b, c].astype(F32)
        hit = rank_s[pl.ds(e, 1), :] == base + sub
        ps.append(jnp.where(hit, 1.0, 0.0).astype(BF16))
        if extra is not None:
            ex.append(jnp.sum(jnp.where(hit, extra[pl.ds(e, 1), :], 0.0), axis=-1, keepdims=True))
    p = jnp.concatenate(ps, axis=0)
    return (p, jnp.concatenate(ex, axis=0)) if extra is not None else p


def _block_ranks(gt):
    n = gt.shape[1]
    ti = lax.broadcasted_iota(jnp.int32, (n, n), 0)
    tj = lax.broadcasted_iota(jnp.int32, (n, n), 1)
    before = (ti < tj).astype(BF16)
    member = gt > 0.0
    rank = _dot(member.astype(BF16), before)
    return jnp.where(member, rank, -1.0)


def _moe_sort_kernel(ce_ref, cb_ref, dst_ref, ng_ref, h_ref, gt_ref, xg_ref, stage, rank_s, gate_s, sem):
    b = pl.program_id(0)
    slot = b % 2
    gt = gt_ref[...]
    rank_s[...] = _block_ranks(gt)
    gate_s[...] = gt
    lane = lax.broadcasted_iota(jnp.int32, (1, LANES), 1)

    def piece_copy(blk, sl, c):
        r0 = pl.multiple_of(c * PIECE, PIECE)
        d0 = pl.multiple_of(dst_ref[blk, c] * PIECE, PIECE)
        return pltpu.make_async_copy(stage.at[sl, pl.ds(r0, PIECE)], xg_ref.at[pl.ds(d0, PIECE)], sem.at[sl])

    def groups(it, carry):
        gs = [it * LOOP_GROUPS + k for k in range(LOOP_GROUPS)]
        sel = [_piece_onehot(ce_ref, cb_ref, rank_s, b, g, gate_s) for g in gs]
        xs = [_dot(p, h_ref[...]).astype(BF16) for p, _ in sel]
        for g, x, (_, gcol) in zip(gs, xs, sel):
            hi, mid, lo = (t.astype(F32) for t in _split3(gcol))
            gblk = jnp.where(lane == 0, hi, jnp.where(lane == 1, mid, jnp.where(lane == 2, lo, 0.0)))
            r0 = pl.multiple_of(g * GROUP_ROWS, GROUP_ROWS)
            stage[slot, pl.ds(r0, GROUP_ROWS), :] = jnp.concatenate([x, gblk.astype(BF16)], axis=1)
        for cc in range(LOOP_PIECES):
            piece_copy(b, slot, it * LOOP_PIECES + cc).start()
        return carry

    lax.fori_loop(0, ng_ref[b, 0], groups, 0)

    def drain(blk, sl):
        def wait_some(it, carry):
            for cc in range(LOOP_PIECES):
                piece_copy(blk, sl, it * LOOP_PIECES + cc).wait()
            return carry
        lax.fori_loop(0, ng_ref[blk, 0], wait_some, 0)

    @pl.when(b > 0)
    def _():
        drain(b - 1, 1 - slot)

    @pl.when(b == pl.num_programs(0) - 1)
    def _():
        drain(b, slot)


def _moe_sort(h2, gates_t, tabs):
    nb = h2.shape[0] // MOE_BLOCK
    grid_spec = pltpu.PrefetchScalarGridSpec(
        num_scalar_prefetch=4, grid=(nb,),
        in_specs=[pl.BlockSpec((MOE_BLOCK, D_MODEL), lambda b, *_: (b, 0)),
                  pl.BlockSpec((N_EXPERTS, MOE_BLOCK), lambda b, *_: (0, b))],
        out_specs=pl.BlockSpec(memory_space=pl.ANY),
        scratch_shapes=[pltpu.VMEM((2, STAGE_PIECES * PIECE, DISPATCH_W), BF16),
                        pltpu.VMEM((N_EXPERTS, MOE_BLOCK), F32), pltpu.VMEM((N_EXPERTS, MOE_BLOCK), F32),
                        pltpu.SemaphoreType.DMA((2,))])
    return pl.pallas_call(
        _moe_sort_kernel,
        out_shape=jax.ShapeDtypeStruct((DISPATCH_ROWS, DISPATCH_W), BF16),
        grid_spec=grid_spec,
        compiler_params=_cparams(("arbitrary",)),
        name="moe_sort",
    )(tabs["ce"], tabs["cb"], tabs["dst"], tabs["ngroups"], h2, gates_t)


def _moe_expert_kernel(tile_ref, x_ref, wg_ref, wu_ref, wd_ref, y_ref):
    valid = tile_ref[1, pl.program_id(0)]

    @pl.when(valid > 0)
    def _():
        keep = lax.broadcasted_iota(jnp.int32, (EXP_TILE, 1), 0) < valid
        xrow = x_ref[...]
        x = jnp.where(keep, xrow[:, :D_MODEL], jnp.zeros((), BF16))
        g = jnp.sum(jnp.where(keep, xrow[:, D_MODEL:].astype(F32), 0.0), axis=-1, keepdims=True)
        hg = _dot(x, wg_ref[0].astype(BF16))
        hu = _dot(x, wu_ref[0].astype(BF16))
        act = (_silu(hg) * hu * g).astype(BF16)
        y_ref[...] = _dot(act, wd_ref[0].astype(BF16)).astype(BF16)


def _moe_expert(xg, tabs, wts):
    grid_spec = pltpu.PrefetchScalarGridSpec(
        num_scalar_prefetch=1, grid=(N_EXP_TILES,),
        in_specs=[pl.BlockSpec((EXP_TILE, DISPATCH_W), lambda j, tt: (tt[2, j], 0)),
                  pl.BlockSpec((1, D_MODEL, D_EXPERT), lambda j, tt: (tt[0, j], 0, 0)),
                  pl.BlockSpec((1, D_MODEL, D_EXPERT), lambda j, tt: (tt[0, j], 0, 0)),
                  pl.BlockSpec((1, D_EXPERT, D_MODEL), lambda j, tt: (tt[0, j], 0, 0))],
        out_specs=pl.BlockSpec((EXP_TILE, D_MODEL), lambda j, tt: (tt[3, j], 0)))
    return pl.pallas_call(
        _moe_expert_kernel,
        out_shape=jax.ShapeDtypeStruct((DISPATCH_ROWS, D_MODEL), BF16),
        grid_spec=grid_spec,
        compiler_params=_cparams(("arbitrary",)),
        name="moe_expert",
    )(tabs["tile"], xg, wts["w_gate"], wts["w_up"], wts["w_down"])


def _moe_combine_kernel(ctx_blocks, ce_ref, cb_ref, src_ref, ng_ref, yg_ref, gt_ref, h_ref, x1_ref, m_ref,
                        gpost_ref, wsg_ref, wsu_ref, wsd_ref, outp_ref, outs_ref, stage, rank_s, acc_s, sem):
    b = pl.program_id(0)
    slot = b % 2

    def piece_copy(blk, sl, c):
        r0 = pl.multiple_of(c * PIECE, PIECE)
        s0 = pl.multiple_of(src_ref[blk, c] * PIECE, PIECE)
        return pltpu.make_async_copy(yg_ref.at[pl.ds(s0, PIECE)], stage.at[sl, pl.ds(r0, PIECE)], sem.at[sl])

    def start_pieces(blk, sl, it):
        for cc in range(LOOP_PIECES):
            piece_copy(blk, sl, it * LOOP_PIECES + cc).start()

    def fetch(blk, sl):
        def start_some(it, carry):
            start_pieces(blk, sl, it)
            return carry
        lax.fori_loop(0, ng_ref[blk, 0], start_some, 0)

    ng_cur = ng_ref[b, 0]

    @pl.when(b == 0)
    def _():
        fetch(0, 0)

    @pl.when(b + 1 < pl.num_programs(0))
    def _():
        fetch(b + 1, 1 - slot)

    rank_s[...] = _block_ranks(gt_ref[...])
    h = h_ref[...]
    sh = (_silu(_dot(h, wsg_ref[...])) * _dot(h, wsu_ref[...])).astype(BF16)
    acc_s[...] = _dot(sh, wsd_ref[...])

    def wait_some(it, carry):
        for cc in range(LOOP_PIECES):
            piece_copy(b, slot, it * LOOP_PIECES + cc).wait()
        return carry

    lax.fori_loop(0, ng_cur, wait_some, 0)

    def groups(it, carry):
        gs = [it * LOOP_GROUPS + k for k in range(LOOP_GROUPS)]
        ps = [_piece_onehot(ce_ref, cb_ref, rank_s, b, g) for g in gs]
        ys = [stage[slot, pl.ds(pl.multiple_of(g * GROUP_ROWS, GROUP_ROWS), GROUP_ROWS), :] for g in gs]
        acc_s[...] += functools.reduce(jnp.add, [_dot_tn(p, y) for p, y in zip(ps, ys)])
        return carry

    lax.fori_loop(0, ng_cur, groups, 0)
    m = m_ref[0]
    y = x1_ref[...] + m[5:6] * _rms(acc_s[...], gpost_ref[...])

    @pl.when(b < ctx_blocks)
    def _():
        outp_ref[...] = y

    @pl.when(b >= ctx_blocks)
    def _():
        outs_ref[...] = y


def _moe_combine(yg, gates_t, h2, x1, mods, mod_index, tabs, wts, ctx_tokens):
    t = h2.shape[0]
    nb = t // MOE_BLOCK
    ctx_blocks = ctx_tokens // MOE_BLOCK

    def full(a):
        return pl.BlockSpec(a.shape, lambda b, *_: (0,) * a.ndim)

    grid_spec = pltpu.PrefetchScalarGridSpec(
        num_scalar_prefetch=4, grid=(nb,),
        in_specs=[pl.BlockSpec(memory_space=pl.ANY),
                  pl.BlockSpec((N_EXPERTS, MOE_BLOCK), lambda b, *_: (0, b)),
                  pl.BlockSpec((MOE_BLOCK, D_MODEL), lambda b, *_: (b, 0)),
                  pl.BlockSpec((MOE_BLOCK, D_MODEL), lambda b, *_: (b, 0)),
                  pl.BlockSpec((1, 6, D_MODEL), lambda b, *_: (mod_index(b), 0, 0)),
                  full(wts["gpost2"]), full(wts["wsg"]), full(wts["wsu"]), full(wts["wsd"])],
        out_specs=[pl.BlockSpec((MOE_BLOCK, D_MODEL), lambda b, *_: (jnp.minimum(b, ctx_blocks - 1), 0)),
                   pl.BlockSpec((MOE_BLOCK, D_MODEL), lambda b, *_: (jnp.maximum(b - ctx_blocks, 0), 0))],
        scratch_shapes=[pltpu.VMEM((2, STAGE_PIECES * PIECE, D_MODEL), BF16),
                        pltpu.VMEM((N_EXPERTS, MOE_BLOCK), F32), pltpu.VMEM((MOE_BLOCK, D_MODEL), F32),
                        pltpu.SemaphoreType.DMA((2,))])
    return pl.pallas_call(
        functools.partial(_moe_combine_kernel, ctx_blocks),
        out_shape=[jax.ShapeDtypeStruct((ctx_tokens, D_MODEL), F32),
                   jax.ShapeDtypeStruct((t - ctx_tokens, D_MODEL), F32)],
        grid_spec=grid_spec,
        compiler_params=_cparams(("arbitrary",)),
        name="moe_combine",
    )(tabs["ce"], tabs["cb"], tabs["src"], tabs["ngroups"], yg, gates_t, h2, x1, mods,
      wts["gpost2"], wts["wsg"], wts["wsu"], wts["wsd"])


def _moe(h2, gates_t, cnt, x1, mods, mod_index, wts, ctx_tokens):
    tabs = _dispatch_tables(cnt)
    xg = _moe_sort(h2, gates_t, tabs)
    yg = _moe_expert(xg, tabs, wts)
    return _moe_combine(yg, gates_t, h2, x1, mods, mod_index, tabs, wts, ctx_tokens)


def _rope_swap(w):
    nf = QK_ROPE // 4
    parts = [w[..., i * nf:(i + 1) * nf] for i in range(4)]
    return jnp.concatenate([parts[1], parts[0], parts[3], parts[2]], axis=-1)


def _head_block(nope, rope):
    lead = nope.shape[:-2] if nope is not None else rope.shape[:-2]
    nope = jnp.zeros(lead + (MLA_HEADS, QK_NOPE), F32) if nope is None else nope
    rope = jnp.zeros(lead + (MLA_HEADS, QK_ROPE), F32) if rope is None else rope
    pad = jnp.zeros(lead + (MLA_HEADS, HEAD_PAD - MLA_QK), F32)
    return jnp.concatenate([nope, rope, pad], axis=-1).reshape(lead + (MLA_HEADS * HEAD_PAD,))


def _rope_block(w):
    lead = w.shape[:-1]
    return jnp.concatenate([jnp.zeros(lead + (QK_NOPE,), F32), w,
                            jnp.zeros(lead + (HEAD_PAD - MLA_QK,), F32)], axis=-1)


def _rope_tables(n_tokens):
    rows = n_tokens // GRID_W
    row = np.repeat(np.arange(rows, dtype=np.float64), GRID_W)
    colv = np.tile(np.arange(GRID_W, dtype=np.float64), rows)
    nf = QK_ROPE // 4
    inv = ROPE_THETA ** (-np.arange(nf, dtype=np.float64) / nf)
    ang_r = row[:, None] * inv
    ang_c = colv[:, None] * inv
    cos32 = np.concatenate([np.cos(ang_r), np.cos(ang_r), np.cos(ang_c), np.cos(ang_c)], axis=-1)
    sin32 = np.concatenate([-np.sin(ang_r), np.sin(ang_r), -np.sin(ang_c), np.sin(ang_c)], axis=-1)
    ones = np.ones((n_tokens, QK_NOPE))
    tail = np.zeros((n_tokens, HEAD_PAD - MLA_QK))
    cos = np.concatenate([ones, cos32, tail], axis=-1)
    sin = np.concatenate([np.zeros((n_tokens, QK_NOPE)), sin32, tail], axis=-1)
    return jnp.asarray(cos, F32), jnp.asarray(sin, F32)


def _expand_matrix():
    e = np.zeros((N_PAIRS, LANES, 4 * LANES), np.float32)
    for p in range(N_PAIRS):
        for blk in range(4):
            for hh in range(2):
                src = blk * GDN_HEADS + 2 * p + hh
                e[p, src, blk * LANES + hh * GDN_DK: blk * LANES + (hh + 1) * GDN_DK] = 1.0
    return jnp.asarray(e)


def _prepare_weights(w_in, q_norm_g, kv_norm_g, w_uq, w_ukv, w_oa, w_ob, w_o, g_post_mix, g_pre_ffn,
                     g_post_ffn, gdn_norm_g, w_router, e_bias, w_gate, w_up, w_down, ws_gate, ws_up, ws_down):
    offs = np.cumsum((Q_LORA, KV_LORA, QK_ROPE, 3 * GDN_W, GDN_W, 2 * GDN_HEADS, 2 * GDN_HEADS,
                      D_MODEL, D_MODEL))[:-1].tolist()
    cq, ckv, kr, qkv, z, a, b, ga, gb = jnp.split(w_in, offs, axis=-1)
    ab = jnp.concatenate([a, b, jnp.zeros((D_MODEL, LANES - 4 * GDN_HEADS), F32)], axis=-1)
    small = jnp.concatenate([_rope_block(kr), _rope_block(_rope_swap(kr)), ab], axis=-1)
    lo0, mid0, hi0 = 0, offs[2], offs[6]
    uq = w_uq.reshape(Q_LORA, MLA_HEADS, MLA_QK)
    ukv = w_ukv.reshape(KV_LORA, MLA_HEADS, QK_NOPE + V_HEAD)
    return {
        "w_lo": w_in[:, lo0:lo0 + Q_LORA + KV_LORA].astype(BF16),
        "w_small": small.astype(BF16),
        "w_mid": w_in[:, mid0:mid0 + 4 * GDN_W].astype(BF16),
        "w_hi": w_in[:, hi0:hi0 + 2 * D_MODEL].astype(BF16),
        "qg": q_norm_g.reshape(1, Q_LORA), "kvg": kv_norm_g.reshape(1, KV_LORA),
        "wuq": _head_block(uq[..., :QK_NOPE], uq[..., QK_NOPE:]).astype(BF16),
        "wuqs": _head_block(None, _rope_swap(uq[..., QK_NOPE:])).astype(BF16),
        "wuk": _head_block(ukv[..., :QK_NOPE], None).astype(BF16),
        "wuv": ukv[..., QK_NOPE:].reshape(KV_LORA, MLA_HEADS * V_HEAD).astype(BF16),
        "woa": w_oa.astype(BF16), "wob": w_ob.astype(BF16), "wo": w_o.astype(BF16),
        "gpost": g_post_mix.reshape(1, D_MODEL), "gpre2": g_pre_ffn.reshape(1, D_MODEL),
        "gpost2": g_post_ffn.reshape(1, D_MODEL),
        "gdng": jnp.tile(gdn_norm_g.reshape(1, GDN_DV), (1, 2)),
        "wr_t": w_router.T, "eb": e_bias.reshape(N_EXPERTS, 1),
        "w_gate": w_gate, "w_up": w_up, "w_down": w_down,
        "wsg": ws_gate.astype(BF16), "wsu": ws_up.astype(BF16), "wsd": ws_down.astype(BF16),
    }


def _pad_lanes(v):
    v = v.reshape(1, -1)
    return jnp.concatenate([v, jnp.zeros((1, LANES - v.shape[1]), F32)], axis=-1)


def _layer_group(x, n_seq, seq_len, mods, mod_index, wts, gpre, conv_w, alog128, dtb128,
                 expand, rope_tabs, extra_kv, s0, tile0, into):
    pr = _proj(x, mods, mod_index, gpre, wts, rope_tabs)
    kvs = list(extra_kv) + [(pr["k"], pr["v"], seq_len)]
    omla = _attention(pr["q"], kvs, n_seq, seq_len, "attn_%d" % seq_len)
    prep = _gdn_prep(pr["qkv"], pr["ab"], conv_w, alog128, dtb128, expand, n_seq, seq_len)
    o_f, o_b, s_fin = _gdn_scan(prep, s0, n_seq, seq_len)
    t = n_seq * seq_len
    merged = _merge(x, mods, lambda i: mod_index(i * (MERGE_TILE // ROW_TILE)), omla,
                    o_f.reshape(t, GDN_W), o_b.reshape(t, GDN_W), pr, wts, tile0, N_TOKENS, into)
    return merged, pr, s_fin


def _state_to_pairs(s):
    b = s.shape[0]
    s = s.reshape(b, 2, N_PAIRS, 2, GDN_DK, GDN_DV)
    return jnp.transpose(s, (0, 1, 2, 4, 3, 5)).reshape(b, 2, N_PAIRS, GDN_DK, 2 * GDN_DV)


def kernel(x_prompt, x_sample, cache_ckv, cache_krope, state_delta, c, c_ctx, w_ada, b_ada, g_pre_mix,
           g_post_mix, g_pre_ffn, g_post_ffn, w_in, q_norm_g, kv_norm_g, w_uq, w_ukv, conv_w, a_log,
           dt_bias, gdn_norm_g, w_oa, w_ob, w_o, w_router, e_bias, w_gate, w_up, w_down, ws_gate, ws_up,
           ws_down):
    batch, seq, _ = x_prompt.shape
    dec_batch, dec_seq, _ = x_sample.shape
    past = cache_ckv.shape[2]
    assert batch * seq + dec_batch * dec_seq == N_TOKENS, "dispatch buffers are sized for N_TOKENS"
    y_p = x_prompt.reshape(batch * seq, D_MODEL)
    y_s = x_sample.reshape(dec_batch * dec_seq, D_MODEL)
    expand = _expand_matrix()
    rope_tabs = _rope_tables(dec_seq)
    cond8 = jnp.concatenate([c_ctx[None], c, jnp.zeros((8 - 1 - dec_batch, D_MODEL), F32)], axis=0)
    ckv_out, krope_out, state_out = [], [], []
    for l in range(DEPTH):
        wts = _prepare_weights(w_in[l], q_norm_g[l], kv_norm_g[l], w_uq[l], w_ukv[l], w_oa[l], w_ob[l],
                               w_o[l], g_post_mix[l], g_pre_ffn[l], g_post_ffn[l], gdn_norm_g[l],
                               w_router[l], e_bias[l], w_gate[l], w_up[l], w_down[l], ws_gate[l],
                               ws_up[l], ws_down[l])
        gpre = g_pre_mix[l].reshape(1, D_MODEL)
        alog128 = _pad_lanes(a_log[l])
        dtb128 = _pad_lanes(dt_bias[l])
        mods = _mods(cond8, w_ada[l], b_ada[l].reshape(1, -1)).reshape(8, 6, D_MODEL)

        zero_state = jnp.zeros((batch, 2, N_PAIRS, GDN_DK, LANES), F32)
        merged_p, pr_p, s_fin = _layer_group(
            y_p, batch, seq, mods, lambda i: 0, wts, gpre, conv_w[l], alog128, dtb128,
            expand, None, [], zero_state, 0, None)
        ckv_out.append(pr_p["ckv"].reshape(batch, seq, KV_LORA))
        krope_out.append(pr_p["kr"][:, QK_NOPE:MLA_QK].reshape(batch, seq, QK_ROPE))
        state_out.append(s_fin)

        kr_ctx = _rope_block(cache_krope[:, l].reshape(dec_batch * past, QK_ROPE))
        k_ctx, v_ctx = _cache_kv(cache_ckv[:, l].reshape(dec_batch * past, KV_LORA), kr_ctx, wts)
        tiles_per_seq = dec_seq // ROW_TILE
        merged_s, _, _ = _layer_group(
            y_s, dec_batch, dec_seq, mods, lambda i: 1 + i // tiles_per_seq,
            wts, gpre, conv_w[l], alog128, dtb128, expand,
            rope_tabs, [(k_ctx, v_ctx, past)], _state_to_pairs(state_delta[:, l]),
            batch * seq // MERGE_TILE, merged_p)

        x1, h2, gates_t, cnt = merged_s
        ctx_blocks = batch * seq // MOE_BLOCK
        blocks_per_seq = dec_seq // MOE_BLOCK
        y_p, y_s = _moe(h2, gates_t, cnt, x1, mods,
                        lambda b: jnp.where(b < ctx_blocks, 0, 1 + (b - ctx_blocks) // blocks_per_seq),
                        wts, batch * seq)
    new_ckv = jnp.stack(ckv_out, axis=1)
    new_krope = jnp.stack(krope_out, axis=1)
    new_state = jnp.stack(state_out, axis=1)
    return (y_p.reshape(batch, seq, D_MODEL), y_s.reshape(dec_batch, dec_seq, D_MODEL),
            new_ckv, new_krope, new_state)
```

```python
import functools

import numpy as np
import jax
import jax.numpy as jnp
from jax import lax
from jax.experimental import pallas as pl
from jax.experimental.pallas import tpu as pltpu

F32 = jnp.float32
BF16 = jnp.bfloat16

D_MODEL = 1024
DEPTH = 1
GRID_W = 64
MLA_HEADS = 8
QK_NOPE = 64
QK_ROPE = 32
V_HEAD = 64
Q_LORA = 256
KV_LORA = 256
ROPE_THETA = 10000.0
GDN_HEADS = 8
GDN_DK = 64
GDN_DV = 64
CONV_K = 5
CHUNK = 64
N_EXPERTS = 64
TOP_K = 8
N_GROUPS = 8
TOPK_GROUPS = 4
D_EXPERT = 256
D_SHARED = 256
ROUTED_SCALE = 2.5
EPS = 1e-6

LANES = 128
LOG2_E = 1.4426950408889634
MLA_QK = QK_NOPE + QK_ROPE
GDN_W = GDN_HEADS * GDN_DK
N_PAIRS = GDN_HEADS // 2
HEAD_PAD = LANES

_SEG = {"cq": ("w_lo", 0, Q_LORA), "ckv": ("w_lo", Q_LORA, Q_LORA + KV_LORA),
        "kr": ("w_small", 0, LANES), "krs": ("w_small", LANES, 2 * LANES), "ab": ("w_small", 2 * LANES, 3 * LANES),
        "qkv": ("w_mid", 0, 3 * GDN_W), "z": ("w_mid", 3 * GDN_W, 4 * GDN_W),
        "ga": ("w_hi", 0, D_MODEL), "gb": ("w_hi", D_MODEL, 2 * D_MODEL)}
_PROJ_W = ("w_lo", "w_small", "w_mid", "w_hi")

ROW_TILE = 512
ATTN_TILE = 512
N_TOKENS = 8192
MOE_BLOCK = 256
MERGE_TILE = 512
PIECE = 16
GROUP_ROWS = 256
GROUP_PIECES = GROUP_ROWS // PIECE
LOOP_GROUPS = 2
LOOP_PIECES = LOOP_GROUPS * GROUP_PIECES
EXP_TILE = 1024
TILE_PIECES = EXP_TILE // PIECE
N_BLOCKS = N_TOKENS // MOE_BLOCK
STAGE_PIECES = -(-((MOE_BLOCK * TOP_K + N_EXPERTS * (PIECE - 1)) // PIECE) // LOOP_PIECES) * LOOP_PIECES
N_EXP_TILES = -(-((N_TOKENS * TOP_K + N_BLOCKS * N_EXPERTS * (PIECE - 1)) // PIECE
                  + N_EXPERTS * (TILE_PIECES - 1)) // TILE_PIECES)
DUMP_PIECE0 = N_EXP_TILES * TILE_PIECES
DISPATCH_ROWS = (DUMP_PIECE0 + max(N_BLOCKS * LOOP_PIECES, TILE_PIECES)) * PIECE
DISPATCH_W = D_MODEL + LANES
PREP_GROUP = 8
PREP_ROWS = 1024
SCAN_SEQS = 4
SCAN_CHUNKS = 4
VMEM_LIMIT = 56 * 1024 * 1024


def _dot(a, b):
    return jnp.dot(a, b, preferred_element_type=F32)


def _dot_nt(a, b):
    return lax.dot_general(a, b, (((1,), (1,)), ((), ())), preferred_element_type=F32)


def _dot_tn(a, b):
    return lax.dot_general(a, b, (((0,), (0,)), ((), ())), preferred_element_type=F32)


def _rms(x, g):
    return x * lax.rsqrt(jnp.mean(x * x, axis=-1, keepdims=True) + EPS) * g


def _silu(x):
    return x * jax.nn.sigmoid(x)


def _cparams(sem):
    return pltpu.CompilerParams(dimension_semantics=sem, vmem_limit_bytes=VMEM_LIMIT)


def _mods_kernel(c_ref, w_ref, b_ref, o_ref):
    s = _silu(c_ref[...]).astype(BF16)
    o_ref[...] = _dot(s, w_ref[...].astype(BF16)) + b_ref[...]


def _mods(cond8, w_ada, b_ada):
    n = w_ada.shape[1]
    bn = 1024
    return pl.pallas_call(
        _mods_kernel,
        out_shape=jax.ShapeDtypeStruct((8, n), F32),
        grid=(n // bn,),
        in_specs=[pl.BlockSpec((8, D_MODEL), lambda j: (0, 0)),
                  pl.BlockSpec((D_MODEL, bn), lambda j: (0, j)),
                  pl.BlockSpec((1, bn), lambda j: (0, j))],
        out_specs=pl.BlockSpec((8, bn), lambda j: (0, j)),
        compiler_params=_cparams(("parallel",)),
        name="mods",
    )(cond8, w_ada, b_ada)


def _proj_kernel(rope, x_ref, m_ref, gpre_ref, wlo_ref, wsmall_ref, wmid_ref, whi_ref, qg_ref, kvg_ref,
                 wuq_ref, wuqs_ref, wuk_ref, wuv_ref, cos_ref, sin_ref,
                 q_ref, k_ref, v_ref, ckv_ref, kr_ref, qkv_ref, z_ref, ab_ref, ga_ref, gb_ref):
    m = m_ref[0]
    h = (_rms(x_ref[...], gpre_ref[...]) * (1.0 + m[1:2]) + m[0:1]).astype(BF16)
    w_refs = dict(zip(_PROJ_W, (wlo_ref, wsmall_ref, wmid_ref, whi_ref)))

    def seg(name):
        op, a, b = _SEG[name]
        return _dot(h, w_refs[op][:, a:b])

    qkv_ref[...] = seg("qkv")
    z_ref[...] = seg("z")
    ab_ref[...] = seg("ab")
    ga_ref[...] = seg("ga")
    gb_ref[...] = seg("gb")

    qn = _rms(seg("cq"), qg_ref[...]).astype(BF16)
    ckv = _rms(seg("ckv"), kvg_ref[...])
    ckv_ref[...] = ckv
    ckv_b = ckv.astype(BF16)
    kr = seg("kr")
    kr_ref[...] = kr
    qm = _dot(qn, wuq_ref[...])
    kk = _dot(ckv_b, wuk_ref[...])
    v_ref[...] = _dot(ckv_b, wuv_ref[...]).astype(BF16)
    scale = MLA_QK ** -0.5 * LOG2_E
    if rope:
        cos = cos_ref[...]
        sin = sin_ref[...]
        qs = _dot(qn, wuqs_ref[...])
        kr = kr * cos + seg("krs") * sin
    for hd in range(MLA_HEADS):
        sl = slice(hd * HEAD_PAD, (hd + 1) * HEAD_PAD)
        qh = qm[:, sl]
        if rope:
            qh = qh * cos + qs[:, sl] * sin
        q_ref[:, sl] = (qh * scale).astype(BF16)
        k_ref[:, sl] = (kk[:, sl] + kr).astype(BF16)


def _proj(x, mods, mod_index, gpre, wts, rope_tabs):
    t = x.shape[0]
    tm = ROW_TILE
    rope = rope_tabs is not None
    if rope:
        cos, sin = rope_tabs
        n_rope_blocks = cos.shape[0] // tm
        rope_spec = pl.BlockSpec((tm, LANES), lambda i: (i % n_rope_blocks, 0))
    else:
        cos = sin = jnp.zeros((8, LANES), F32)
        rope_spec = pl.BlockSpec((8, LANES), lambda i: (0, 0))

    def full(a):
        return pl.BlockSpec(a.shape, lambda i: (0,) * a.ndim)

    def rows(w):
        return pl.BlockSpec((tm, w), lambda i: (i, 0))

    out_widths = (("q", MLA_HEADS * HEAD_PAD, BF16), ("k", MLA_HEADS * HEAD_PAD, BF16),
                  ("v", MLA_HEADS * V_HEAD, BF16), ("ckv", KV_LORA, F32), ("kr", LANES, F32),
                  ("qkv", 3 * GDN_W, F32), ("z", GDN_W, F32), ("ab", LANES, F32),
                  ("ga", D_MODEL, F32), ("gb", D_MODEL, F32))
    outs = pl.pallas_call(
        functools.partial(_proj_kernel, rope),
        out_shape=[jax.ShapeDtypeStruct((t, w), dt) for _, w, dt in out_widths],
        grid=(t // tm,),
        in_specs=[rows(D_MODEL),
                  pl.BlockSpec((1, 6, D_MODEL), lambda i: (mod_index(i), 0, 0)),
                  full(gpre)] + [full(wts[n]) for n in _PROJ_W] + [full(wts["qg"]), full(wts["kvg"]),
                  full(wts["wuq"]), full(wts["wuqs"]), full(wts["wuk"]), full(wts["wuv"]),
                  rope_spec, rope_spec],
        out_specs=[rows(w) for _, w, _ in out_widths],
        compiler_params=_cparams(("parallel",)),
        name="proj_rope" if rope else "proj",
    )(x, mods, gpre, *[wts[n] for n in _PROJ_W], wts["qg"], wts["kvg"], wts["wuq"], wts["wuqs"],
      wts["wuk"], wts["wuv"], cos, sin)
    return {name: o for (name, _, _), o in zip(out_widths, outs)}


def _cache_kv_kernel(ckv_ref, kr_ref, wuk_ref, wuv_ref, k_ref, v_ref):
    c = ckv_ref[...].astype(BF16)
    kk = _dot(c, wuk_ref[...])
    v_ref[...] = _dot(c, wuv_ref[...]).astype(BF16)
    kr = kr_ref[...]
    for hd in range(MLA_HEADS):
        sl = slice(hd * HEAD_PAD, (hd + 1) * HEAD_PAD)
        k_ref[:, sl] = (kk[:, sl] + kr).astype(BF16)


def _cache_kv(ckv, kr128, wts):
    t = ckv.shape[0]
    tm = 512
    return pl.pallas_call(
        _cache_kv_kernel,
        out_shape=[jax.ShapeDtypeStruct((t, MLA_HEADS * HEAD_PAD), BF16),
                   jax.ShapeDtypeStruct((t, MLA_HEADS * V_HEAD), BF16)],
        grid=(t // tm,),
        in_specs=[pl.BlockSpec((tm, KV_LORA), lambda i: (i, 0)),
                  pl.BlockSpec((tm, LANES), lambda i: (i, 0)),
                  pl.BlockSpec(wts["wuk"].shape, lambda i: (0, 0)),
                  pl.BlockSpec(wts["wuv"].shape, lambda i: (0, 0))],
        out_specs=[pl.BlockSpec((tm, MLA_HEADS * HEAD_PAD), lambda i: (i, 0)),
                   pl.BlockSpec((tm, MLA_HEADS * V_HEAD), lambda i: (i, 0))],
        compiler_params=_cparams(("parallel",)),
        name="cache_kv",
    )(ckv, kr128, wts["wuk"], wts["wuv"])


def _attn_kernel(n_kv, q_ref, *refs):
    k_refs = refs[:n_kv]
    v_refs = refs[n_kv:2 * n_kv]
    o_ref = refs[2 * n_kv]
    lane = lax.broadcasted_iota(jnp.int32, (1, LANES), 1)
    low = lane < V_HEAD
    for pr in range(MLA_HEADS // 2):
        halves = []
        for hd in (2 * pr, 2 * pr + 1):
            sl = slice(hd * HEAD_PAD, (hd + 1) * HEAD_PAD)
            qh = q_ref[:, sl]
            scores = [_dot_nt(qh, kr[:, sl]) for kr in k_refs]
            mx = functools.reduce(jnp.maximum, [jnp.max(s, axis=-1, keepdims=True) for s in scores])
            ps = [jnp.exp2(s - mx) for s in scores]
            den = functools.reduce(jnp.add, [jnp.sum(p, axis=-1, keepdims=True) for p in ps])
            vsl = slice(pr * LANES, (pr + 1) * LANES)
            acc = functools.reduce(jnp.add, [_dot(p.astype(BF16), vr[:, vsl]) for p, vr in zip(ps, v_refs)])
            halves.append(acc / den)
        o_ref[:, pr * LANES:(pr + 1) * LANES] = jnp.where(low, halves[0], halves[1]).astype(BF16)


def _attention(q, kvs, n_seq, seq_len, name):
    tq = min(seq_len, ATTN_TILE)
    nq = seq_len // tq
    n_kv = len(kvs)
    in_specs = [pl.BlockSpec((tq, MLA_HEADS * HEAD_PAD), lambda b, j: (b * nq + j, 0))]
    in_specs += [pl.BlockSpec((rows, MLA_HEADS * HEAD_PAD), lambda b, j: (b, 0)) for _, _, rows in kvs]
    in_specs += [pl.BlockSpec((rows, MLA_HEADS * V_HEAD), lambda b, j: (b, 0)) for _, _, rows in kvs]
    return pl.pallas_call(
        functools.partial(_attn_kernel, n_kv),
        out_shape=jax.ShapeDtypeStruct((n_seq * seq_len, MLA_HEADS * V_HEAD), BF16),
        grid=(n_seq, nq),
        in_specs=in_specs,
        out_specs=pl.BlockSpec((tq, MLA_HEADS * V_HEAD), lambda b, j: (b * nq + j, 0)),
        compiler_params=_cparams(("parallel", "parallel")),
        name=name,
    )(q, *[k for k, _, _ in kvs], *[v for _, v, _ in kvs])


def _pair_masks():
    lane = lax.broadcasted_iota(jnp.int32, (1, LANES), 1)
    return lane < GDN_DK


def _stack(x, low):
    zero = jnp.zeros_like(x)
    return jnp.concatenate([jnp.where(low, x, zero), jnp.where(low, zero, x)], axis=0)


def _split3(x):
    hi = x.astype(BF16)
    r = x - hi.astype(F32)
    mid = r.astype(BF16)
    lo = (r - mid.astype(F32)).astype(BF16)
    return hi, mid, lo


def _gdn_prep_kernel(seq_len, q_ref, k_ref, v_ref, cwq_ref, cwk_ref, cwv_ref, ab_ref, alog_ref, dtb_ref, e_ref,
                     uf_ref, ub_ref, wf_ref, wb_ref, af_ref, abk_ref, qdf_ref, qdb_ref, kdf_ref, kdb_ref,
                     glf_ref, glb_ref,
                     qn_s, kn_s, vn_s, gcb_s, gf_s):
    seq = q_ref.shape[0]
    n_chunks = seq // CHUNK
    low = _pair_masks()
    row = lax.broadcasted_iota(jnp.int32, (seq, 1), 0) % seq_len
    lane = lax.broadcasted_iota(jnp.int32, (1, LANES), 1)

    def conv(x_ref, cw_ref):
        x = x_ref[...]
        acc = jnp.zeros_like(x)
        for j in range(CONV_K):
            sh = CONV_K // 2 - j
            xs = x if sh == 0 else pltpu.roll(x, sh % seq, axis=0)
            src = row - sh
            valid = (src >= 0) & (src < seq_len)
            acc = acc + jnp.where(valid, xs, 0.0) * cw_ref[j:j + 1, :]
        return _silu(acc)

    def l2n(x):
        sq = x * x
        s0 = jnp.sum(jnp.where(low, sq, 0.0), axis=-1, keepdims=True)
        s1 = jnp.sum(jnp.where(low, 0.0, sq), axis=-1, keepdims=True)
        return x * lax.rsqrt(jnp.where(low, s0, s1) + EPS)

    qn_s[...] = l2n(conv(q_ref, cwq_ref)) * (GDN_DK ** -0.5)
    kn_s[...] = l2n(conv(k_ref, cwk_ref))
    vn_s[...] = conv(v_ref, cwv_ref)

    a = ab_ref[...]
    xg = a + dtb_ref[...]
    softplus = jnp.maximum(xg, 0.0) + jnp.log(1.0 + jnp.exp(-jnp.abs(xg)))
    act = jnp.where(lane < 2 * GDN_HEADS, -jnp.exp(alog_ref[...]) * softplus, jax.nn.sigmoid(a))

    ti = lax.broadcasted_iota(jnp.int32, (CHUNK, CHUNK), 0)
    tj = lax.broadcasted_iota(jnp.int32, (CHUNK, CHUNK), 1)
    tri_lo = (tj <= ti).astype(BF16)
    tri_up = (tj >= ti).astype(BF16)
    for c in range(n_chunks):
        ac = act[c * CHUNK:(c + 1) * CHUNK]
        pieces = _split3(ac)
        lo = functools.reduce(jnp.add, [_dot(tri_lo, pc) for pc in pieces])
        up = functools.reduce(jnp.add, [_dot(tri_up, pc) for pc in pieces])
        gcb_s[c * CHUNK:(c + 1) * CHUNK, :] = jnp.where(lane < GDN_HEADS, lo,
                                                        jnp.where(lane < 2 * GDN_HEADS, up, ac))
    expand = e_ref[0].astype(BF16)
    gf_s[...] = functools.reduce(jnp.add, [_dot(pc, expand) for pc in _split3(gcb_s[...])])

    ri = lax.broadcasted_iota(jnp.int32, (CHUNK, LANES), 0)
    cj = lax.broadcasted_iota(jnp.int32, (CHUNK, LANES), 1) % CHUNK
    eye = (ri == cj).astype(F32)

    def pmm(x, y):
        return _dot(x.astype(BF16), _stack(y, low).astype(BF16))

    def row_form(g):
        gt = jnp.concatenate([g, jnp.zeros_like(g)], axis=0).T
        r0 = jnp.broadcast_to(gt[0:1, :], (CHUNK, LANES))
        r1 = jnp.broadcast_to(gt[GDN_DK:GDN_DK + 1, :], (CHUNK, LANES))
        return jnp.where(low, r0, pltpu.roll(r1, GDN_DK, axis=1))

    out_refs = ((uf_ref, wf_ref, af_ref, qdf_ref, kdf_ref, glf_ref),
                (ub_ref, wb_ref, abk_ref, qdb_ref, kdb_ref, glb_ref))
    incl = (ri >= cj, ri <= cj)
    strict = (ri > cj, ri < cj)
    diag8 = (ri // 8) == (cj // 8)
    merge_masks = [((ri // (2 * s)) == (cj // (2 * s))) & ((ri // s) != (cj // s)) for s in (8, 16, 32)]

    def group(it, carry):
        cs = [it * PREP_GROUP + cc for cc in range(PREP_GROUP)]
        rows = [pl.ds(pl.multiple_of(c * CHUNK, CHUNK), CHUNK) for c in cs]
        qc = [qn_s[r, :] for r in rows]
        kc = [kn_s[r, :] for r in rows]
        vc = [vn_s[r, :] for r in rows]
        kst = [_stack(k, low).astype(BF16) for k in kc]
        kq = [_dot_nt(jnp.concatenate([k, q], axis=0).astype(BF16), ks) for k, q, ks in zip(kc, qc, kst)]
        kk = [x[:CHUNK] for x in kq]
        qk = [x[CHUNK:] for x in kq]
        chains = [(ci, d) for ci in range(PREP_GROUP) for d in range(2)]
        gc = [gf_s[rows[ci], d * LANES:(d + 1) * LANES] for ci, d in chains]
        beta = [gf_s[rows[ci], (2 + d) * LANES:(3 + d) * LANES] for ci, d in chains]
        gr = [row_form(g) for g in gc]
        dm = [jnp.exp(jnp.where(incl[d], g - r, -jnp.inf)) for (ci, d), g, r in zip(chains, gc, gr)]
        lm = [jnp.where(strict[d], b * kk[ci] * m, 0.0) for (ci, d), b, m in zip(chains, beta, dm)]
        aint = [(qk[ci] * m).astype(BF16) for (ci, d), m in zip(chains, dm)]
        x = [-jnp.where(diag8, l, 0.0) for l in lm]
        t = [eye + xx for xx in x]
        for _ in range(2):
            x = [pmm(xx, xx) for xx in x]
            t = [tt + pmm(tt, xx) for tt, xx in zip(t, x)]
        for off in merge_masks:
            tc = [pmm(tt, jnp.where(off, l, 0.0)) for tt, l in zip(t, lm)]
            t = [tt - pmm(a, tt) for tt, a in zip(t, tc)]
        egc = [jnp.exp(g) for g in gc]
        rhs = [jnp.concatenate([_stack(vc[ci] * b, low), _stack(kc[ci] * b * e, low)], axis=1).astype(BF16)
               for (ci, d), b, e in zip(chains, beta, egc)]
        uw = [_dot(tt.astype(BF16), r) for tt, r in zip(t, rhs)]
        u = [x[:, :LANES] for x in uw]
        w = [x[:, LANES:].astype(BF16) for x in uw]
        qd = [(qc[ci] * e).astype(BF16) for (ci, d), e in zip(chains, egc)]
        gtot = [g[CHUNK - 1:CHUNK, :] if d == 0 else g[0:1, :] for (ci, d), g in zip(chains, gc)]
        kd = [(kc[ci] * jnp.exp(gt - g)).astype(BF16) for (ci, d), gt, g in zip(chains, gtot, gc)]
        for n, (ci, d) in enumerate(chains):
            u_ref, w_ref, a_ref, qd_ref, kd_ref, gl_ref = out_refs[d]
            u_ref[0, rows[ci], :] = u[n]
            w_ref[0, rows[ci], :] = w[n]
            a_ref[0, rows[ci], :] = aint[n]
            qd_ref[0, rows[ci], :] = qd[n]
            kd_ref[0, rows[ci], :] = kd[n]
            gl_ref[0, pl.ds(cs[ci], 1), :, :] = jnp.broadcast_to(jnp.exp(gtot[n]), (1, 8, LANES))
        return carry

    lax.fori_loop(0, n_chunks // PREP_GROUP, group, 0)


def _gdn_prep(qkv, ab, conv_w, alog128, dtb128, expand, n_seq, seq_len):
    rb = max(seq_len, PREP_ROWS)
    nb = n_seq * seq_len // rb
    n_chunks = rb // CHUNK
    col = lambda off: pl.BlockSpec((rb, LANES), lambda s, p: (s, off + p))
    cw = lambda off: pl.BlockSpec((CONV_K, LANES), lambda s, p: (0, off + p))
    vec = pl.BlockSpec((1, LANES), lambda s, p: (0, 0))
    big = lambda: pl.BlockSpec((1, rb, LANES), lambda s, p: (s, 0, p))
    glspec = lambda: pl.BlockSpec((1, n_chunks, 8, LANES), lambda s, p: (s, 0, 0, p))
    shp = lambda dt: jax.ShapeDtypeStruct((nb, rb, GDN_W), dt)
    glshp = jax.ShapeDtypeStruct((nb, n_chunks, 8, GDN_W), F32)
    outs = pl.pallas_call(
        functools.partial(_gdn_prep_kernel, seq_len),
        out_shape=[shp(F32), shp(F32)] + [shp(BF16)] * 8 + [glshp, glshp],
        grid=(nb, N_PAIRS),
        in_specs=[col(0), col(N_PAIRS), col(2 * N_PAIRS), cw(0), cw(N_PAIRS), cw(2 * N_PAIRS),
                  pl.BlockSpec((rb, LANES), lambda s, p: (s, 0)), vec, vec,
                  pl.BlockSpec((1, LANES, 4 * LANES), lambda s, p: (p, 0, 0))],
        out_specs=[big() for _ in range(10)] + [glspec(), glspec()],
        scratch_shapes=[pltpu.VMEM((rb, LANES), F32)] * 4 + [pltpu.VMEM((rb, 4 * LANES), F32)],
        compiler_params=_cparams(("parallel", "parallel")),
        name="gdn_prep_%d" % seq_len,
    )(qkv, qkv, qkv, conv_w, conv_w, conv_w, ab, alog128, dtb128, expand)
    per_seq = [o.reshape(n_seq, seq_len, GDN_W) for o in outs[:10]]
    return per_seq + [o.reshape(n_seq, seq_len // CHUNK, 8, GDN_W) for o in outs[10:]]


def _gdn_scan_kernel(uf_ref, ub_ref, wf_ref, wb_ref, af_ref, abk_ref, qdf_ref, qdb_ref, kdf_ref, kdb_ref,
                     glf_ref, glb_ref, s0_ref, of_ref, ob_ref, sfin_ref, state):
    step = pl.program_id(1)
    n_steps = pl.num_programs(1)
    low = _pair_masks()
    chains = [(d, j, p) for d in range(2) for j in range(uf_ref.shape[0]) for p in range(N_PAIRS)]

    first = step == 0
    per_dir = ((uf_ref, wf_ref, af_ref, qdf_ref, kdf_ref, glf_ref, of_ref),
               (ub_ref, wb_ref, abk_ref, qdb_ref, kdb_ref, glb_ref, ob_ref))

    s = [jnp.where(first, _stack(s0_ref[j, d, p], low), state[idx]) for idx, (d, j, p) in enumerate(chains)]
    for sub in range(SCAN_CHUNKS):
        at = (sub, SCAN_CHUNKS - 1 - sub)

        def rd(k, d, j, p):
            return per_dir[d][k][j, at[d] * CHUNK:(at[d] + 1) * CHUNK, p * LANES:(p + 1) * LANES]

        sb = [x.astype(BF16) for x in s]
        wq = [_dot(jnp.concatenate([rd(1, *c), rd(3, *c)], axis=0), b) for c, b in zip(chains, sb)]
        ws = [x[:CHUNK] for x in wq]
        qs = [x[CHUNK:] for x in wq]
        vst = [_stack(rd(0, *c) - w, low).astype(BF16) for c, w in zip(chains, ws)]
        upd = [_dot_tn(_stack(rd(4, *c), low), v) for c, v in zip(chains, vst)]
        intra = [_dot(rd(2, *c), v) for c, v in zip(chains, vst)]
        nxt = []
        for idx, (d, j, p) in enumerate(chains):
            sl = slice(p * LANES, (p + 1) * LANES)
            nxt.append(s[idx] * per_dir[d][5][j, at[d], 0:1, sl] + upd[idx])
            per_dir[d][6][j, at[d] * CHUNK:(at[d] + 1) * CHUNK, sl] = qs[idx] + intra[idx]
        s = nxt
    for idx in range(len(chains)):
        state[idx] = s[idx]

    @pl.when(step == n_steps - 1)
    def _():
        for idx, (d, j, p) in enumerate(chains):
            fin = state[idx]
            sfin_ref[j, d, 2 * p] = fin[:GDN_DK, :GDN_DV]
            sfin_ref[j, d, 2 * p + 1] = pltpu.roll(fin[GDN_DK:], GDN_DV, axis=1)[:, :GDN_DV]


def _gdn_scan(prep, s0, n_seq, seq_len):
    n_steps = seq_len // (CHUNK * SCAN_CHUNKS)
    ns = min(SCAN_SEQS, n_seq)
    rows = CHUNK * SCAN_CHUNKS
    fwd = lambda: pl.BlockSpec((ns, rows, GDN_W), lambda g, i: (g, i, 0))
    bwd = lambda: pl.BlockSpec((ns, rows, GDN_W), lambda g, i: (g, n_steps - 1 - i, 0))
    glf = pl.BlockSpec((ns, SCAN_CHUNKS, 8, GDN_W), lambda g, i: (g, i, 0, 0))
    glb = pl.BlockSpec((ns, SCAN_CHUNKS, 8, GDN_W), lambda g, i: (g, n_steps - 1 - i, 0, 0))
    st = pl.BlockSpec((ns, 2, N_PAIRS, GDN_DK, LANES), lambda g, i: (g, 0, 0, 0, 0))
    st_out = pl.BlockSpec((ns, 2, GDN_HEADS, GDN_DK, GDN_DV), lambda g, i: (g, 0, 0, 0, 0))
    oshape = jax.ShapeDtypeStruct((n_seq, seq_len, GDN_W), F32)
    return pl.pallas_call(
        _gdn_scan_kernel,
        out_shape=[oshape, oshape, jax.ShapeDtypeStruct((n_seq, 2, GDN_HEADS, GDN_DK, GDN_DV), F32)],
        grid=(n_seq // ns, n_steps),
        in_specs=[fwd(), bwd()] * 5 + [glf, glb, st],
        out_specs=[fwd(), bwd(), st_out],
        scratch_shapes=[pltpu.VMEM((2 * ns * N_PAIRS, LANES, LANES), F32)],
        compiler_params=_cparams(("parallel", "arbitrary")),
        name="gdn_scan_%d" % seq_len,
    )(*prep, s0)


def _route(sel, s):
    per_group = N_EXPERTS // N_GROUPS
    ninf = -jnp.inf
    sub = lax.broadcasted_iota(jnp.int32, sel.shape, 1).astype(F32)
    gid = lax.broadcasted_iota(jnp.int32, (N_GROUPS, 1, sel.shape[2]), 0).astype(F32)
    m1 = jnp.max(sel, axis=1, keepdims=True)
    i1 = jnp.min(jnp.where(sel == m1, sub, float(per_group)), axis=1, keepdims=True)
    m2 = jnp.max(jnp.where(sub == i1, ninf, sel), axis=1, keepdims=True)
    work = m1 + m2
    gmask = jnp.zeros(work.shape, jnp.bool_)
    for _ in range(TOPK_GROUPS):
        m = jnp.max(work, axis=0, keepdims=True)
        idx = jnp.min(jnp.where(work == m, gid, float(N_GROUPS)), axis=0, keepdims=True)
        pick = gid == idx
        gmask = gmask | pick
        work = jnp.where(pick, ninf, work)
    work = jnp.where(gmask, sel, ninf)
    eid = gid * per_group + sub
    chosen = jnp.zeros(sel.shape, jnp.bool_)
    for _ in range(TOP_K):
        m = jnp.max(jnp.max(work, axis=1, keepdims=True), axis=0, keepdims=True)
        idx = jnp.min(jnp.min(jnp.where(work == m, eid, float(N_EXPERTS)), axis=1, keepdims=True),
                      axis=0, keepdims=True)
        pick = eid == idx
        chosen = chosen | pick
        work = jnp.where(pick, ninf, work)
    wk = jnp.where(chosen, s, 0.0)
    den = jnp.sum(jnp.sum(wk, axis=1, keepdims=True), axis=0, keepdims=True)
    return wk / den * ROUTED_SCALE


def _merge_kernel(x_ref, m_ref, omla_ref, of_ref, ob_ref, z_ref, ga_ref, gb_ref,
                  woa_ref, wob_ref, wo_ref, gpost_ref, gpre_ref, gdng_ref, wr_ref, eb_ref,
                  x1_ref, h2_ref, gates_ref, cnt_ref):
    m = m_ref[0]
    low = _pair_masks()
    n_blk = x_ref.shape[0] // MOE_BLOCK

    def mix_and_norm(blk):
        rows = slice(blk * MOE_BLOCK, (blk + 1) * MOE_BLOCK)
        parts = []
        for p in range(N_PAIRS):
            sl = slice(p * LANES, (p + 1) * LANES)
            op = of_ref[rows, sl] + ob_ref[rows, sl]
            sq = op * op
            s0 = jnp.sum(jnp.where(low, sq, 0.0), axis=-1, keepdims=True)
            s1 = jnp.sum(jnp.where(low, 0.0, sq), axis=-1, keepdims=True)
            ms = jnp.where(low, s0, s1) * (1.0 / GDN_DV)
            parts.append(op * lax.rsqrt(ms + EPS) * gdng_ref[...] * _silu(z_ref[rows, sl]))
        og = jnp.concatenate(parts, axis=1).astype(BF16)
        ya = _dot(omla_ref[rows, :], woa_ref[...])
        yb = _dot(og, wob_ref[...])
        mix = (jax.nn.sigmoid(ga_ref[rows, :]) * ya + jax.nn.sigmoid(gb_ref[rows, :]) * yb).astype(BF16)
        y = _dot(mix, wo_ref[...])
        x1 = x_ref[rows, :] + m[2:3] * _rms(y, gpost_ref[...])
        x1_ref[rows, :] = x1
        h2 = _rms(x1, gpre_ref[...]) * (1.0 + m[4:5]) + m[3:4]
        h2_ref[rows, :] = h2.astype(BF16)
        return h2

    def route(blk, h2):
        wh, wl, _ = _split3(wr_ref[...])
        hh, hl, _ = _split3(h2)
        logits = _dot_nt(wh, hh) + (_dot_nt(wh, hl) + _dot_nt(wl, hh))
        s = jax.nn.sigmoid(logits)
        sel = s + eb_ref[...]
        shape3 = (N_GROUPS, N_EXPERTS // N_GROUPS, MOE_BLOCK)
        gates_t = _route(sel.reshape(shape3), s.reshape(shape3)).reshape(N_EXPERTS, MOE_BLOCK)
        gates_ref[:, blk * MOE_BLOCK:(blk + 1) * MOE_BLOCK] = gates_t
        cnt_ref[blk] = jnp.sum((gates_t > 0.0).astype(F32), axis=1, keepdims=True)

    h2s = [mix_and_norm(blk) for blk in range(n_blk)]
    for blk in range(n_blk):
        route(blk, h2s[blk])


def _merge_kernel_into(*refs):
    n_in = 16
    _merge_kernel(*refs[:n_in], *refs[n_in + 4:])


def _merge(x, mods, mod_index, omla, o_f, o_b, pr, wts, tile0, total, into=None):
    t = x.shape[0]
    tm = MERGE_TILE
    bpt = tm // MOE_BLOCK

    def full(a):
        return pl.BlockSpec(a.shape, lambda i: (0,) * a.ndim)

    def rows(w):
        return pl.BlockSpec((tm, w), lambda i: (i, 0))

    def out_rows(w):
        return pl.BlockSpec((tm, w), lambda i: (tile0 + i, 0))

    names = ("woa", "wob", "wo", "gpost", "gpre2", "gdng", "wr_t", "eb")
    args = [x, mods, omla, o_f, o_b, pr["z"], pr["ga"], pr["gb"]] + [wts[n] for n in names]
    in_specs = [rows(D_MODEL), pl.BlockSpec((1, 6, D_MODEL), lambda i: (mod_index(i), 0, 0)),
                rows(MLA_HEADS * V_HEAD), rows(GDN_W), rows(GDN_W), rows(GDN_W),
                rows(D_MODEL), rows(D_MODEL)] + [full(wts[n]) for n in names]
    aliases = {}
    if into is not None:
        aliases = {len(args) + k: k for k in range(4)}
        in_specs = in_specs + [pl.BlockSpec(memory_space=pl.ANY)] * 4
        args = args + list(into)
    return pl.pallas_call(
        _merge_kernel if into is None else _merge_kernel_into,
        out_shape=[jax.ShapeDtypeStruct((total, D_MODEL), F32), jax.ShapeDtypeStruct((total, D_MODEL), BF16),
                   jax.ShapeDtypeStruct((N_EXPERTS, total), F32),
                   jax.ShapeDtypeStruct((total // MOE_BLOCK, N_EXPERTS, 1), F32)],
        grid=(t // tm,),
        in_specs=in_specs,
        out_specs=[out_rows(D_MODEL), out_rows(D_MODEL),
                   pl.BlockSpec((N_EXPERTS, tm), lambda i: (0, tile0 + i)),
                   pl.BlockSpec((bpt, N_EXPERTS, 1), lambda i: (tile0 + i, 0, 0))],
        input_output_aliases=aliases,
        compiler_params=_cparams(("parallel",)),
        name="merge",
    )(*args)


TABLE_W = 256
TILE_TABLE_W = 512


def _ceil_div(x, d):
    return jnp.floor((x + (d - 1)) * (1.0 / d))


def _moe_tables_kernel(cnt_ref, cnt_t_ref, ce_ref, cb_ref, dst_ref, src_ref, ng_ref, tile_ref):
    nb = cnt_ref.shape[0]
    ppt = float(TILE_PIECES)
    ppg = float(LOOP_PIECES)
    ei = lax.broadcasted_iota(jnp.int32, (N_EXPERTS, N_EXPERTS), 0)
    ej = lax.broadcasted_iota(jnp.int32, (N_EXPERTS, N_EXPERTS), 1)
    tri = (ej <= ei).astype(BF16)

    def cumsum_experts(col):
        wide = jnp.broadcast_to(col, (N_EXPERTS, LANES))
        return functools.reduce(jnp.add, [_dot(tri, pc) for pc in _split3(wide)])[:, 0:1]

    eid = lax.broadcasted_iota(jnp.int32, (N_EXPERTS, 1), 0).astype(F32)
    pc_t = _ceil_div(cnt_t_ref[...], PIECE)
    tp = jnp.sum(pc_t, axis=1, keepdims=True)
    rp = _ceil_div(tp, TILE_PIECES) * ppt
    gs_end = cumsum_experts(rp)
    gs = gs_end - rp
    blk = lax.broadcasted_iota(jnp.int32, (1, nb), 1)
    c = lax.broadcasted_iota(jnp.int32, (1, TABLE_W), 1).astype(F32)
    for b in range(nb):
        pc = _ceil_div(cnt_ref[b], PIECE)
        seg_end = cumsum_experts(pc)
        seg = seg_end - pc
        blk_off = jnp.sum(jnp.where(blk < b, pc_t, 0.0), axis=1, keepdims=True)
        nvalid = seg_end[N_EXPERTS - 1:N_EXPERTS, :]
        ce = jnp.minimum(jnp.sum((seg_end <= c).astype(F32), axis=0, keepdims=True), N_EXPERTS - 1.0)
        onehot = eid == ce
        seg_sel = jnp.sum(jnp.where(onehot, seg, 0.0), axis=0, keepdims=True)
        base_sel = jnp.sum(jnp.where(onehot, gs + blk_off - seg, 0.0), axis=0, keepdims=True)
        valid = c < nvalid
        dump = DUMP_PIECE0 + b * LOOP_PIECES + (c - ppg * jnp.floor(c * (1.0 / ppg)))
        dst = jnp.where(valid, base_sel + c, dump)
        row = slice(b, b + 1)
        ce_ref[row, :] = ce.astype(jnp.int32)
        cb_ref[row, :] = jnp.where(valid, (c - seg_sel) * PIECE, -float(1 << 20)).astype(jnp.int32)
        dst_ref[row, :] = (dst * PIECE).astype(jnp.int32)
        src_ref[row, :] = (jnp.where(valid, dst, dst[:, 0:1]) * PIECE).astype(jnp.int32)
        ng_ref[row, :] = jnp.broadcast_to(_ceil_div(nvalid, LOOP_PIECES), (1, LANES)).astype(jnp.int32)
    j = lax.broadcasted_iota(jnp.int32, (1, TILE_TABLE_W), 1).astype(F32)
    start = j * ppt
    te = jnp.minimum(jnp.sum((gs_end <= start).astype(F32), axis=0, keepdims=True), N_EXPERTS - 1.0)
    onehot = eid == te
    tp_sel = jnp.sum(jnp.where(onehot, tp, 0.0), axis=0, keepdims=True)
    gs_sel = jnp.sum(jnp.where(onehot, gs, 0.0), axis=0, keepdims=True)
    n_used = gs_end[N_EXPERTS - 1:N_EXPERTS, :] * (1.0 / ppt)
    used = j < n_used
    tv = jnp.where(used, jnp.clip((tp_sel - (start - gs_sel)) * PIECE, 0.0, float(EXP_TILE)), 0.0)
    tin = jnp.where(used, j, n_used - 1.0)
    tout = jnp.where(used, j, float(N_EXP_TILES))
    tile_ref[...] = jnp.zeros(tile_ref.shape, jnp.int32)
    for r, v in enumerate((te, tv, tin, tout)):
        tile_ref[r:r + 1, :] = v.astype(jnp.int32)


def _dispatch_tables(cnt):
    nb = cnt.shape[0]
    tab = jax.ShapeDtypeStruct((nb, TABLE_W), jnp.int32)
    ce, cb, dst, src, ng, tile = pl.pallas_call(
        _moe_tables_kernel,
        out_shape=[tab, tab, tab, tab, jax.ShapeDtypeStruct((nb, LANES), jnp.int32),
                   jax.ShapeDtypeStruct((8, TILE_TABLE_W), jnp.int32)],
        name="moe_tables",
    )(cnt, cnt[:, :, 0].T)
    return {"ce": ce, "cb": cb, "dst": dst, "src": src, "ngroups": ng, "tile": tile}


def _piece_onehot(ce_ref, cb_ref, rank_s, b, g, extra=None):
    sub = lax.broadcasted_iota(jnp.int32, (PIECE, 1), 0).astype(F32)
    ps, ex = [], []
    for cc in range(GROUP_PIECES):
        c = g * GROUP_PIECES + cc
        e = ce_ref[b, c]
        base = cb_ref[b, c].astype(F32)
        hit = rank_s[pl.ds(e, 1), :] == base + sub
        ps.append(jnp.where(hit, 1.0, 0.0).astype(BF16))
        if extra is not None:
            ex.append(jnp.sum(jnp.where(hit, extra[pl.ds(e, 1), :], 0.0), axis=-1, keepdims=True))
    p = jnp.concatenate(ps, axis=0)
    return (p, jnp.concatenate(ex, axis=0)) if extra is not None else p


def _block_ranks(gt):
    n = gt.shape[1]
    ti = lax.broadcasted_iota(jnp.int32, (n, n), 0)
    tj = lax.broadcasted_iota(jnp.int32, (n, n), 1)
    before = (ti < tj).astype(BF16)
    member = gt > 0.0
    rank = _dot(member.astype(BF16), before)
    return jnp.where(member, rank, -1.0)


def _moe_sort_kernel(ce_ref, cb_ref, dst_ref, ng_ref, h_ref, gt_ref, xg_ref, stage, rank_s, gate_s, sem):
    b = pl.program_id(0)
    slot = b % 2
    gt = gt_ref[...]
    rank_s[...] = _block_ranks(gt)
    gate_s[...] = gt
    lane = lax.broadcasted_iota(jnp.int32, (1, LANES), 1)

    def piece_copy(blk, sl, c):
        r0 = pl.multiple_of(c * PIECE, PIECE)
        d0 = pl.multiple_of(dst_ref[blk, c], PIECE)
        return pltpu.make_async_copy(stage.at[sl, pl.ds(r0, PIECE)], xg_ref.at[pl.ds(d0, PIECE)], sem.at[sl])

    def groups(it, carry):
        gs = [it * LOOP_GROUPS + k for k in range(LOOP_GROUPS)]
        sel = [_piece_onehot(ce_ref, cb_ref, rank_s, b, g, gate_s) for g in gs]
        xs = [_dot(p, h_ref[...]).astype(BF16) for p, _ in sel]
        for g, x, (_, gcol) in zip(gs, xs, sel):
            hi, mid, lo = (t.astype(F32) for t in _split3(gcol))
            gblk = jnp.where(lane == 0, hi, jnp.where(lane == 1, mid, jnp.where(lane == 2, lo, 0.0)))
            r0 = pl.multiple_of(g * GROUP_ROWS, GROUP_ROWS)
            stage[slot, pl.ds(r0, GROUP_ROWS), :] = jnp.concatenate([x, gblk.astype(BF16)], axis=1)
        for cc in range(LOOP_PIECES):
            piece_copy(b, slot, it * LOOP_PIECES + cc).start()
        return carry

    lax.fori_loop(0, ng_ref[b, 0], groups, 0)

    def drain(blk, sl):
        def wait_some(it, carry):
            for cc in range(LOOP_PIECES):
                piece_copy(blk, sl, it * LOOP_PIECES + cc).wait()
            return carry
        lax.fori_loop(0, ng_ref[blk, 0], wait_some, 0)

    @pl.when(b > 0)
    def _():
        drain(b - 1, 1 - slot)

    @pl.when(b == pl.num_programs(0) - 1)
    def _():
        drain(b, slot)


def _moe_sort(h2, gates_t, tabs):
    nb = h2.shape[0] // MOE_BLOCK
    grid_spec = pltpu.PrefetchScalarGridSpec(
        num_scalar_prefetch=4, grid=(nb,),
        in_specs=[pl.BlockSpec((MOE_BLOCK, D_MODEL), lambda b, *_: (b, 0)),
                  pl.BlockSpec((N_EXPERTS, MOE_BLOCK), lambda b, *_: (0, b))],
        out_specs=pl.BlockSpec(memory_space=pl.ANY),
        scratch_shapes=[pltpu.VMEM((2, STAGE_PIECES * PIECE, DISPATCH_W), BF16),
                        pltpu.VMEM((N_EXPERTS, MOE_BLOCK), F32), pltpu.VMEM((N_EXPERTS, MOE_BLOCK), F32),
                        pltpu.SemaphoreType.DMA((2,))])
    return pl.pallas_call(
        _moe_sort_kernel,
        out_shape=jax.ShapeDtypeStruct((DISPATCH_ROWS, DISPATCH_W), BF16),
        grid_spec=grid_spec,
        compiler_params=_cparams(("arbitrary",)),
        name="moe_sort",
    )(tabs["ce"], tabs["cb"], tabs["dst"], tabs["ngroups"], h2, gates_t)


def _moe_expert_kernel(tile_ref, x_ref, wg_ref, wu_ref, wd_ref, y_ref):
    valid = tile_ref[1, pl.program_id(0)]

    @pl.when(valid > 0)
    def _():
        keep = lax.broadcasted_iota(jnp.int32, (EXP_TILE, 1), 0) < valid
        xrow = x_ref[...]
        x = jnp.where(keep, xrow[:, :D_MODEL], jnp.zeros((), BF16))
        g = jnp.sum(jnp.where(keep, xrow[:, D_MODEL:].astype(F32), 0.0), axis=-1, keepdims=True)
        hg = _dot(x, wg_ref[0].astype(BF16))
        hu = _dot(x, wu_ref[0].astype(BF16))
        act = (_silu(hg) * hu * g).astype(BF16)
        y_ref[...] = _dot(act, wd_ref[0].astype(BF16)).astype(BF16)


def _moe_expert(xg, tabs, wts):
    grid_spec = pltpu.PrefetchScalarGridSpec(
        num_scalar_prefetch=1, grid=(N_EXP_TILES,),
        in_specs=[pl.BlockSpec((EXP_TILE, DISPATCH_W), lambda j, tt: (tt[2, j], 0)),
                  pl.BlockSpec((1, D_MODEL, D_EXPERT), lambda j, tt: (tt[0, j], 0, 0)),
                  pl.BlockSpec((1, D_MODEL, D_EXPERT), lambda j, tt: (tt[0, j], 0, 0)),
                  pl.BlockSpec((1, D_EXPERT, D_MODEL), lambda j, tt: (tt[0, j], 0, 0))],
        out_specs=pl.BlockSpec((EXP_TILE, D_MODEL), lambda j, tt: (tt[3, j], 0)))
    return pl.pallas_call(
        _moe_expert_kernel,
        out_shape=jax.ShapeDtypeStruct((DISPATCH_ROWS, D_MODEL), BF16),
        grid_spec=grid_spec,
        compiler_params=_cparams(("arbitrary",)),
        name="moe_expert",
    )(tabs["tile"], xg, wts["w_gate"], wts["w_up"], wts["w_down"])


def _moe_combine_kernel(ctx_blocks, ce_ref, cb_ref, src_ref, ng_ref, yg_ref, gt_ref, h_ref, x1_ref, m_ref,
                        gpost_ref, wsg_ref, wsu_ref, wsd_ref, outp_ref, outs_ref, stage, rank_s, acc_s, sem):
    b = pl.program_id(0)
    slot = b % 2

    def piece_copy(blk, sl, c):
        r0 = pl.multiple_of(c * PIECE, PIECE)
        s0 = pl.multiple_of(src_ref[blk, c], PIECE)
        return pltpu.make_async_copy(yg_ref.at[pl.ds(s0, PIECE)], stage.at[sl, pl.ds(r0, PIECE)], sem.at[sl])

    def start_pieces(blk, sl, it):
        for cc in range(LOOP_PIECES):
            piece_copy(blk, sl, it * LOOP_PIECES + cc).start()

    def fetch(blk, sl):
        def start_some(it, carry):
            start_pieces(blk, sl, it)
            return carry
        lax.fori_loop(0, ng_ref[blk, 0], start_some, 0)

    ng_cur = ng_ref[b, 0]

    @pl.when(b == 0)
    def _():
        fetch(0, 0)

    @pl.when(b + 1 < pl.num_programs(0))
    def _():
        fetch(b + 1, 1 - slot)

    rank_s[...] = _block_ranks(gt_ref[...])
    h = h_ref[...]
    sh = (_silu(_dot(h, wsg_ref[...])) * _dot(h, wsu_ref[...])).astype(BF16)
    acc_s[...] = _dot(sh, wsd_ref[...])

    def wait_some(it, carry):
        for cc in range(LOOP_PIECES):
            piece_copy(b, slot, it * LOOP_PIECES + cc).wait()
        return carry

    lax.fori_loop(0, ng_cur, wait_some, 0)

    def groups(it, carry):
        gs = [it * LOOP_GROUPS + k for k in range(LOOP_GROUPS)]
        ps = [_piece_onehot(ce_ref, cb_ref, rank_s, b, g) for g in gs]
        ys = [stage[slot, pl.ds(pl.multiple_of(g * GROUP_ROWS, GROUP_ROWS), GROUP_ROWS), :] for g in gs]
        acc_s[...] += functools.reduce(jnp.add, [_dot_tn(p, y) for p, y in zip(ps, ys)])
        return carry

    lax.fori_loop(0, ng_cur, groups, 0)
    m = m_ref[0]
    y = x1_ref[...] + m[5:6] * _rms(acc_s[...], gpost_ref[...])

    @pl.when(b < ctx_blocks)
    def _():
        outp_ref[...] = y

    @pl.when(b >= ctx_blocks)
    def _():
        outs_ref[...] = y


def _moe_combine(yg, gates_t, h2, x1, mods, mod_index, tabs, wts, ctx_tokens):
    t = h2.shape[0]
    nb = t // MOE_BLOCK
    ctx_blocks = ctx_tokens // MOE_BLOCK

    def full(a):
        return pl.BlockSpec(a.shape, lambda b, *_: (0,) * a.ndim)

    grid_spec = pltpu.PrefetchScalarGridSpec(
        num_scalar_prefetch=4, grid=(nb,),
        in_specs=[pl.BlockSpec(memory_space=pl.ANY),
                  pl.BlockSpec((N_EXPERTS, MOE_BLOCK), lambda b, *_: (0, b)),
                  pl.BlockSpec((MOE_BLOCK, D_MODEL), lambda b, *_: (b, 0)),
                  pl.BlockSpec((MOE_BLOCK, D_MODEL), lambda b, *_: (b, 0)),
                  pl.BlockSpec((1, 6, D_MODEL), lambda b, *_: (mod_index(b), 0, 0)),
                  full(wts["gpost2"]), full(wts["wsg"]), full(wts["wsu"]), full(wts["wsd"])],
        out_specs=[pl.BlockSpec((MOE_BLOCK, D_MODEL), lambda b, *_: (jnp.minimum(b, ctx_blocks - 1), 0)),
                   pl.BlockSpec((MOE_BLOCK, D_MODEL), lambda b, *_: (jnp.maximum(b - ctx_blocks, 0), 0))],
        scratch_shapes=[pltpu.VMEM((2, STAGE_PIECES * PIECE, D_MODEL), BF16),
                        pltpu.VMEM((N_EXPERTS, MOE_BLOCK), F32), pltpu.VMEM((MOE_BLOCK, D_MODEL), F32),
                        pltpu.SemaphoreType.DMA((2,))])
    return pl.pallas_call(
        functools.partial(_moe_combine_kernel, ctx_blocks),
        out_shape=[jax.ShapeDtypeStruct((ctx_tokens, D_MODEL), F32),
                   jax.ShapeDtypeStruct((t - ctx_tokens, D_MODEL), F32)],
        grid_spec=grid_spec,
        compiler_params=_cparams(("arbitrary",)),
        name="moe_combine",
    )(tabs["ce"], tabs["cb"], tabs["src"], tabs["ngroups"], yg, gates_t, h2, x1, mods,
      wts["gpost2"], wts["wsg"], wts["wsu"], wts["wsd"])


def _moe(h2, gates_t, cnt, x1, mods, mod_index, wts, ctx_tokens):
    tabs = _dispatch_tables(cnt)
    xg = _moe_sort(h2, gates_t, tabs)
    yg = _moe_expert(xg, tabs, wts)
    return _moe_combine(yg, gates_t, h2, x1, mods, mod_index, tabs, wts, ctx_tokens)


def _rope_swap(w):
    nf = QK_ROPE // 4
    parts = [w[..., i * nf:(i + 1) * nf] for i in range(4)]
    return jnp.concatenate([parts[1], parts[0], parts[3], parts[2]], axis=-1)


def _head_block(nope, rope):
    lead = nope.shape[:-2] if nope is not None else rope.shape[:-2]
    nope = jnp.zeros(lead + (MLA_HEADS, QK_NOPE), F32) if nope is None else nope
    rope = jnp.zeros(lead + (MLA_HEADS, QK_ROPE), F32) if rope is None else rope
    pad = jnp.zeros(lead + (MLA_HEADS, HEAD_PAD - MLA_QK), F32)
    return jnp.concatenate([nope, rope, pad], axis=-1).reshape(lead + (MLA_HEADS * HEAD_PAD,))


def _rope_block(w):
    lead = w.shape[:-1]
    return jnp.concatenate([jnp.zeros(lead + (QK_NOPE,), F32), w,
                            jnp.zeros(lead + (HEAD_PAD - MLA_QK,), F32)], axis=-1)


def _rope_tables(n_tokens):
    rows = n_tokens // GRID_W
    row = np.repeat(np.arange(rows, dtype=np.float64), GRID_W)
    colv = np.tile(np.arange(GRID_W, dtype=np.float64), rows)
    nf = QK_ROPE // 4
    inv = ROPE_THETA ** (-np.arange(nf, dtype=np.float64) / nf)
    ang_r = row[:, None] * inv
    ang_c = colv[:, None] * inv
    cos32 = np.concatenate([np.cos(ang_r), np.cos(ang_r), np.cos(ang_c), np.cos(ang_c)], axis=-1)
    sin32 = np.concatenate([-np.sin(ang_r), np.sin(ang_r), -np.sin(ang_c), np.sin(ang_c)], axis=-1)
    ones = np.ones((n_tokens, QK_NOPE))
    tail = np.zeros((n_tokens, HEAD_PAD - MLA_QK))
    cos = np.concatenate([ones, cos32, tail], axis=-1)
    sin = np.concatenate([np.zeros((n_tokens, QK_NOPE)), sin32, tail], axis=-1)
    return jnp.asarray(cos, F32), jnp.asarray(sin, F32)


def _expand_matrix():
    e = np.zeros((N_PAIRS, LANES, 4 * LANES), np.float32)
    for p in range(N_PAIRS):
        for blk in range(4):
            for hh in range(2):
                src = blk * GDN_HEADS + 2 * p + hh
                e[p, src, blk * LANES + hh * GDN_DK: blk * LANES + (hh + 1) * GDN_DK] = 1.0
    return jnp.asarray(e)


def _prepare_weights(w_in, q_norm_g, kv_norm_g, w_uq, w_ukv, w_oa, w_ob, w_o, g_post_mix, g_pre_ffn,
                     g_post_ffn, gdn_norm_g, w_router, e_bias, w_gate, w_up, w_down, ws_gate, ws_up, ws_down):
    offs = np.cumsum((Q_LORA, KV_LORA, QK_ROPE, 3 * GDN_W, GDN_W, 2 * GDN_HEADS, 2 * GDN_HEADS,
                      D_MODEL, D_MODEL))[:-1].tolist()
    cq, ckv, kr, qkv, z, a, b, ga, gb = jnp.split(w_in, offs, axis=-1)
    ab = jnp.concatenate([a, b, jnp.zeros((D_MODEL, LANES - 4 * GDN_HEADS), F32)], axis=-1)
    small = jnp.concatenate([_rope_block(kr), _rope_block(_rope_swap(kr)), ab], axis=-1)
    lo0, mid0, hi0 = 0, offs[2], offs[6]
    uq = w_uq.reshape(Q_LORA, MLA_HEADS, MLA_QK)
    ukv = w_ukv.reshape(KV_LORA, MLA_HEADS, QK_NOPE + V_HEAD)
    return {
        "w_lo": w_in[:, lo0:lo0 + Q_LORA + KV_LORA].astype(BF16),
        "w_small": small.astype(BF16),
        "w_mid": w_in[:, mid0:mid0 + 4 * GDN_W].astype(BF16),
        "w_hi": w_in[:, hi0:hi0 + 2 * D_MODEL].astype(BF16),
        "qg": q_norm_g.reshape(1, Q_LORA), "kvg": kv_norm_g.reshape(1, KV_LORA),
        "wuq": _head_block(uq[..., :QK_NOPE], uq[..., QK_NOPE:]).astype(BF16),
        "wuqs": _head_block(None, _rope_swap(uq[..., QK_NOPE:])).astype(BF16),
        "wuk": _head_block(ukv[..., :QK_NOPE], None).astype(BF16),
        "wuv": ukv[..., QK_NOPE:].reshape(KV_LORA, MLA_HEADS * V_HEAD).astype(BF16),
        "woa": w_oa.astype(BF16), "wob": w_ob.astype(BF16), "wo": w_o.astype(BF16),
        "gpost": g_post_mix.reshape(1, D_MODEL), "gpre2": g_pre_ffn.reshape(1, D_MODEL),
        "gpost2": g_post_ffn.reshape(1, D_MODEL),
        "gdng": jnp.tile(gdn_norm_g.reshape(1, GDN_DV), (1, 2)),
        "wr_t": w_router.T, "eb": e_bias.reshape(N_EXPERTS, 1),
        "w_gate": w_gate, "w_up": w_up, "w_down": w_down,
        "wsg": ws_gate.astype(BF16), "wsu": ws_up.astype(BF16), "wsd": ws_down.astype(BF16),
    }


def _pad_lanes(v):
    v = v.reshape(1, -1)
    return jnp.concatenate([v, jnp.zeros((1, LANES - v.shape[1]), F32)], axis=-1)


def _layer_group(x, n_seq, seq_len, mods, mod_index, wts, gpre, conv_w, alog128, dtb128,
                 expand, rope_tabs, extra_kv, s0, tile0, into):
    pr = _proj(x, mods, mod_index, gpre, wts, rope_tabs)
    kvs = list(extra_kv) + [(pr["k"], pr["v"], seq_len)]
    omla = _attention(pr["q"], kvs, n_seq, seq_len, "attn_%d" % seq_len)
    prep = _gdn_prep(pr["qkv"], pr["ab"], conv_w, alog128, dtb128, expand, n_seq, seq_len)
    o_f, o_b, s_fin = _gdn_scan(prep, s0, n_seq, seq_len)
    t = n_seq * seq_len
    merged = _merge(x, mods, lambda i: mod_index(i * (MERGE_TILE // ROW_TILE)), omla,
                    o_f.reshape(t, GDN_W), o_b.reshape(t, GDN_W), pr, wts, tile0, N_TOKENS, into)
    return merged, pr, s_fin


def _state_to_pairs(s):
    b = s.shape[0]
    s = s.reshape(b, 2, N_PAIRS, 2, GDN_DK, GDN_DV)
    return jnp.transpose(s, (0, 1, 2, 4, 3, 5)).reshape(b, 2, N_PAIRS, GDN_DK, 2 * GDN_DV)


def kernel(x_prompt, x_sample, cache_ckv, cache_krope, state_delta, c, c_ctx, w_ada, b_ada, g_pre_mix,
           g_post_mix, g_pre_ffn, g_post_ffn, w_in, q_norm_g, kv_norm_g, w_uq, w_ukv, conv_w, a_log,
           dt_bias, gdn_norm_g, w_oa, w_ob, w_o, w_router, e_bias, w_gate, w_up, w_down, ws_gate, ws_up,
           ws_down):
    batch, seq, _ = x_prompt.shape
    dec_batch, dec_seq, _ = x_sample.shape
    past = cache_ckv.shape[2]
    assert batch * seq + dec_batch * dec_seq == N_TOKENS, "dispatch buffers are sized for N_TOKENS"
    y_p = x_prompt.reshape(batch * seq, D_MODEL)
    y_s = x_sample.reshape(dec_batch * dec_seq, D_MODEL)
    expand = _expand_matrix()
    rope_tabs = _rope_tables(dec_seq)
    cond8 = jnp.concatenate([c_ctx[None], c, jnp.zeros((8 - 1 - dec_batch, D_MODEL), F32)], axis=0)
    ckv_out, krope_out, state_out = [], [], []
    for l in range(DEPTH):
        wts = _prepare_weights(w_in[l], q_norm_g[l], kv_norm_g[l], w_uq[l], w_ukv[l], w_oa[l], w_ob[l],
                               w_o[l], g_post_mix[l], g_pre_ffn[l], g_post_ffn[l], gdn_norm_g[l],
                               w_router[l], e_bias[l], w_gate[l], w_up[l], w_down[l], ws_gate[l],
                               ws_up[l], ws_down[l])
        gpre = g_pre_mix[l].reshape(1, D_MODEL)
        alog128 = _pad_lanes(a_log[l])
        dtb128 = _pad_lanes(dt_bias[l])
        mods = _mods(cond8, w_ada[l], b_ada[l].reshape(1, -1)).reshape(8, 6, D_MODEL)

        zero_state = jnp.zeros((batch, 2, N_PAIRS, GDN_DK, LANES), F32)
        merged_p, pr_p, s_fin = _layer_group(
            y_p, batch, seq, mods, lambda i: 0, wts, gpre, conv_w[l], alog128, dtb128,
            expand, None, [], zero_state, 0, None)
        ckv_out.append(pr_p["ckv"].reshape(batch, seq, KV_LORA))
        krope_out.append(pr_p["kr"][:, QK_NOPE:MLA_QK].reshape(batch, seq, QK_ROPE))
        state_out.append(s_fin)

        kr_ctx = _rope_block(cache_krope[:, l].reshape(dec_batch * past, QK_ROPE))
        k_ctx, v_ctx = _cache_kv(cache_ckv[:, l].reshape(dec_batch * past, KV_LORA), kr_ctx, wts)
        tiles_per_seq = dec_seq // ROW_TILE
        merged_s, _, _ = _layer_group(
            y_s, dec_batch, dec_seq, mods, lambda i: 1 + i // tiles_per_seq,
            wts, gpre, conv_w[l], alog128, dtb128, expand,
            rope_tabs, [(k_ctx, v_ctx, past)], _state_to_pairs(state_delta[:, l]),
            batch * seq // MERGE_TILE, merged_p)

        x1, h2, gates_t, cnt = merged_s
        ctx_blocks = batch * seq // MOE_BLOCK
        blocks_per_seq = dec_seq // MOE_BLOCK
        y_p, y_s = _moe(h2, gates_t, cnt, x1, mods,
                        lambda b: jnp.where(b < ctx_blocks, 0, 1 + (b - ctx_blocks) // blocks_per_seq),
                        wts, batch * seq)
    new_ckv = jnp.stack(ckv_out, axis=1)
    new_krope = jnp.stack(krope_out, axis=1)
    new_state = jnp.stack(state_out, axis=1)
    return (y_p.reshape(batch, seq, D_MODEL), y_s.reshape(dec_batch, dec_seq, D_MODEL),
            new_ckv, new_krope, new_state)
```
